```python
import math
import jax, jax.numpy as jnp
from jax import lax
import numpy as np

D_MODEL = 2048
BATCH = 4
SEQ = 2048
DEPTH = 1

CHUNK = 64
Q_BLOCK = 128
N_MEM = 256
EPS = 1e-6
ROPE_THETA = 500000.0

FOX_HEADS = 8
FOX_DIM = 128
FOX_W = FOX_HEADS * FOX_DIM

DIFF_HEADS = 4
DIFF_QK_DIM = 64
DIFF_V_DIM = 2 * DIFF_QK_DIM
DIFF_QK_W = DIFF_HEADS * 2 * DIFF_QK_DIM
DIFF_V_W = DIFF_HEADS * DIFF_V_DIM
ROPE_DIM = DIFF_QK_DIM // 4

MEM_HEADS = 4
MEM_DIM = 128
MEM_W = MEM_HEADS * MEM_DIM

D_FF = 5632

IN_SIZES = (FOX_W, FOX_W, FOX_W, FOX_HEADS, DIFF_QK_W, DIFF_QK_W, DIFF_V_W, MEM_W)
D_IN = sum(IN_SIZES)

kernel_name = 'hybrid_gated_fox_diffattn_macaron_block'


def rmsnorm(x, g):
    xf = x.astype(jnp.float32)
    y = xf * lax.rsqrt(jnp.mean(xf * xf, axis=-1, keepdims=True) + EPS)
    return (y * g.astype(jnp.float32)).astype(x.dtype)


def swiglu(h, w_gate, w_up, w_down):
    return (jax.nn.silu(h @ w_gate) * (h @ w_up)) @ w_down


def rope_tables(seq):
    pos = jnp.arange(seq, dtype=jnp.float32)
    inv_freq = ROPE_THETA ** (-jnp.arange(0, ROPE_DIM, 2, dtype=jnp.float32) / ROPE_DIM)
    ang = pos[:, None] * inv_freq[None, :]
    return jnp.cos(ang), jnp.sin(ang)


def partial_rope(x, cos, sin):
    half = ROPE_DIM // 2
    c = cos[None, :, None, :].astype(x.dtype)
    s = sin[None, :, None, :].astype(x.dtype)
    x1 = x[..., :half]
    x2 = x[..., half:ROPE_DIM]
    return jnp.concatenate([x1 * c - x2 * s, x2 * c + x1 * s, x[..., ROPE_DIM:]], axis=-1)


def forgetting_attention(q, k, v, log_f):
    seq = q.shape[1]
    F = jnp.cumsum(log_f, axis=1).transpose(0, 2, 1)
    scale = FOX_DIM ** -0.5
    outs = []
    for i in range(seq // Q_BLOCK):
        lo, hi = i * Q_BLOCK, (i + 1) * Q_BLOCK
        logits = jnp.einsum('bqhd,bkhd->bhqk', q[:, lo:hi], k[:, :hi],
                            preferred_element_type=jnp.float32) * scale
        logits = logits + F[:, :, lo:hi, None] - F[:, :, None, :hi]
        t_idx = jnp.arange(lo, hi)[:, None]
        s_idx = jnp.arange(hi)[None, :]
        logits = jnp.where(s_idx <= t_idx, logits, -jnp.inf)
        p = jax.nn.softmax(logits, axis=-1).astype(v.dtype)
        outs.append(jnp.einsum('bhqk,bkhd->bqhd', p, v[:, :hi]))
    return jnp.concatenate(outs, axis=1)


def differential_attention(q, k, v, lam):
    seq = q.shape[1]
    scale = DIFF_QK_DIM ** -0.5
    outs = []
    for i in range(seq // Q_BLOCK):
        lo, hi = i * Q_BLOCK, (i + 1) * Q_BLOCK
        logits = jnp.einsum('bqhmd,bkhmd->bhmqk', q[:, lo:hi], k[:, :hi],
                            preferred_element_type=jnp.float32) * scale
        q_chunk = jnp.arange(lo, hi)[:, None] // CHUNK
        k_chunk = jnp.arange(hi)[None, :] // CHUNK
        logits = jnp.where(k_chunk <= q_chunk, logits, -jnp.inf)
        p = jax.nn.softmax(logits, axis=-1)
        p_diff = (p[:, :, 0] - lam * p[:, :, 1]).astype(v.dtype)
        outs.append(jnp.einsum('bhqk,bkhe->bqhe', p_diff, v[:, :hi]))
    return jnp.concatenate(outs, axis=1)


def memory_attention(q, k, v):
    logits = jnp.einsum('bqhd,bkhd->bhqk', q, k,
                        preferred_element_type=jnp.float32) * (MEM_DIM ** -0.5)
    p = jax.nn.softmax(logits, axis=-1).astype(v.dtype)
    return jnp.einsum('bhqk,bkhd->bqhd', p, v)


def setup_inputs(seed: int = 0) -> dict:
    key = jax.random.key(seed)
    ks = iter(jax.random.split(key, 48))
    L = DEPTH

    def nrm(shape, fan_in):
        return jax.random.normal(next(ks), shape, jnp.float32) * fan_in ** -0.5

    def gain(n):
        return 1.0 + 0.05 * jax.random.normal(next(ks), (L, n), jnp.float32)

    return {
        'x': jax.random.normal(next(ks), (BATCH, SEQ, D_MODEL), jnp.float32),
        'mem': jax.random.normal(next(ks), (BATCH, N_MEM, D_MODEL), jnp.float32),
        'ffn1_pre_g': gain(D_MODEL),
        'ffn1_w_gate': nrm((L, D_MODEL, D_FF), D_MODEL),
        'ffn1_w_up': nrm((L, D_MODEL, D_FF), D_MODEL),
        'ffn1_w_down': nrm((L, D_FF, D_MODEL), D_FF),
        'ffn1_post_g': gain(D_MODEL),
        'mix_pre_g': gain(D_MODEL),
        'w_in': nrm((L, D_MODEL, D_IN), D_MODEL),
        'fox_f_bias': jax.random.uniform(next(ks), (L, FOX_HEADS), jnp.float32, 1.0, 4.0),
        'diff_lambda_q1': 0.1 * jax.random.normal(next(ks), (L, DIFF_QK_DIM), jnp.float32),
        'diff_lambda_k1': 0.1 * jax.random.normal(next(ks), (L, DIFF_QK_DIM), jnp.float32),
        'diff_lambda_q2': 0.1 * jax.random.normal(next(ks), (L, DIFF_QK_DIM), jnp.float32),
        'diff_lambda_k2': 0.1 * jax.random.normal(next(ks), (L, DIFF_QK_DIM), jnp.float32),
        'diff_head_g': gain(DIFF_V_DIM),
        'mem_norm_g': gain(D_MODEL),
        'w_mem_kv': nrm((L, D_MODEL, 2 * MEM_W), D_MODEL),
        'w_branch_fox': nrm((L, FOX_W, D_MODEL), FOX_W),
        'w_branch_diff': nrm((L, DIFF_V_W, D_MODEL), DIFF_V_W),
        'w_branch_mem': nrm((L, MEM_W, D_MODEL), MEM_W),
        'w_merge_gate': nrm((L, D_MODEL, 3 * D_MODEL), D_MODEL),
        'b_merge_gate': 0.02 * jax.random.normal(next(ks), (L, 3 * D_MODEL), jnp.float32),
        'w_out': nrm((L, D_MODEL, D_MODEL), D_MODEL),
        'mix_post_g': gain(D_MODEL),
        'ffn2_pre_g': gain(D_MODEL),
        'ffn2_w_gate': nrm((L, D_MODEL, D_FF), D_MODEL),
        'ffn2_w_up': nrm((L, D_MODEL, D_FF), D_MODEL),
        'ffn2_w_down': nrm((L, D_FF, D_MODEL), D_FF),
        'ffn2_post_g': gain(D_MODEL),
    }


def reference(x, mem, ffn1_pre_g, ffn1_w_gate, ffn1_w_up, ffn1_w_down, ffn1_post_g,
              mix_pre_g, w_in, fox_f_bias, diff_lambda_q1, diff_lambda_k1, diff_lambda_q2,
              diff_lambda_k2, diff_head_g, mem_norm_g, w_mem_kv, w_branch_fox, w_branch_diff,
              w_branch_mem, w_merge_gate, b_merge_gate, w_out, mix_post_g,
              ffn2_pre_g, ffn2_w_gate, ffn2_w_up, ffn2_w_down, ffn2_post_g):
    b, seq, _ = x.shape
    n_mem = mem.shape[1]
    cos, sin = rope_tables(seq)
    split_points = [int(p) for p in np.cumsum(IN_SIZES)[:-1]]

    for l in range(DEPTH):
        h = rmsnorm(x, ffn1_pre_g[l])
        x = x + 0.5 * rmsnorm(swiglu(h, ffn1_w_gate[l], ffn1_w_up[l], ffn1_w_down[l]), ffn1_post_g[l])

        h = rmsnorm(x, mix_pre_g[l])
        proj = h @ w_in[l]
        fq, fk, fv, ff, dq, dk, dv, mq = jnp.split(proj, split_points, axis=-1)

        fq = fq.reshape(b, seq, FOX_HEADS, FOX_DIM)
        fk = fk.reshape(b, seq, FOX_HEADS, FOX_DIM)
        fv = fv.reshape(b, seq, FOX_HEADS, FOX_DIM)
        log_f = jax.nn.log_sigmoid(ff.astype(jnp.float32) + fox_f_bias[l].astype(jnp.float32))
        y_fox = forgetting_attention(fq, fk, fv, log_f).reshape(b, seq, FOX_W)

        dq = partial_rope(dq.reshape(b, seq, 2 * DIFF_HEADS, DIFF_QK_DIM), cos, sin)
        dk = partial_rope(dk.reshape(b, seq, 2 * DIFF_HEADS, DIFF_QK_DIM), cos, sin)
        dq = dq.reshape(b, seq, DIFF_HEADS, 2, DIFF_QK_DIM)
        dk = dk.reshape(b, seq, DIFF_HEADS, 2, DIFF_QK_DIM)
        dv = dv.reshape(b, seq, DIFF_HEADS, DIFF_V_DIM)
        lam_init = 0.8 - 0.6 * math.exp(-0.3 * l)
        lam = (jnp.exp(jnp.sum(diff_lambda_q1[l].astype(jnp.float32) * diff_lambda_k1[l].astype(jnp.float32)))
               - jnp.exp(jnp.sum(diff_lambda_q2[l].astype(jnp.float32) * diff_lambda_k2[l].astype(jnp.float32)))
               + lam_init)
        yd = differential_attention(dq, dk, dv, lam)
        yd = rmsnorm(yd, diff_head_g[l]) * (1.0 - lam_init)
        y_diff = yd.reshape(b, seq, DIFF_V_W)

        mh = rmsnorm(mem, mem_norm_g[l])
        mk, mv = jnp.split(mh @ w_mem_kv[l], 2, axis=-1)
        mk = mk.reshape(b, n_mem, MEM_HEADS, MEM_DIM)
        mv = mv.reshape(b, n_mem, MEM_HEADS, MEM_DIM)
        mq = mq.reshape(b, seq, MEM_HEADS, MEM_DIM)
        y_mem = memory_attention(mq, mk, mv).reshape(b, seq, MEM_W)

        gates = jax.nn.sigmoid(h @ w_merge_gate[l] + b_merge_gate[l])
        g_fox, g_diff, g_mem = jnp.split(gates, 3, axis=-1)
        merged = (g_fox * (y_fox @ w_branch_fox[l])
                  + g_diff * (y_diff @ w_branch_diff[l])
                  + g_mem * (y_mem @ w_branch_mem[l]))
        x = x + rmsnorm(merged @ w_out[l], mix_post_g[l])

        h = rmsnorm(x, ffn2_pre_g[l])
        x = x + 0.5 * rmsnorm(swiglu(h, ffn2_w_gate[l], ffn2_w_up[l], ffn2_w_down[l]), ffn2_post_g[l])

    return x
```

```python
import functools
import math

import jax
import jax.numpy as jnp
from jax import lax
from jax.experimental import pallas as pl
from jax.experimental.pallas import tpu as pltpu

D_MODEL = 2048
CHUNK = 64
EPS = 1e-6
ROPE_THETA = 500000.0

FOX_HEADS = 8
FOX_DIM = 128
FOX_W = FOX_HEADS * FOX_DIM

DIFF_HEADS = 4
DIFF_QK_DIM = 64
DIFF_V_DIM = 2 * DIFF_QK_DIM
DIFF_QK_W = DIFF_HEADS * 2 * DIFF_QK_DIM
DIFF_V_W = DIFF_HEADS * DIFF_V_DIM
ROPE_DIM = DIFF_QK_DIM // 4

MEM_HEADS = 4
MEM_DIM = 128
MEM_W = MEM_HEADS * MEM_DIM

LANES = 128
PROJ_W = 3 * FOX_W + 2 * DIFF_QK_W + DIFF_V_W + MEM_W
FQ_BLK, FK_BLK, FV_BLK = 0, FOX_HEADS, 2 * FOX_HEADS
DQ_BLK = 3 * FOX_HEADS
DK_BLK = DQ_BLK + DIFF_HEADS
DV_BLK = DK_BLK + DIFF_HEADS
MQ_COL = 3 * FOX_W + 2 * DIFF_QK_W + DIFF_V_W

VMEM_LIMIT = 56 * 1024 * 1024

BF16 = jnp.bfloat16
F32 = jnp.float32
NEG_INF = float("-inf")


def _dot(a, b):
    return jnp.dot(a, b, preferred_element_type=F32)


def _dot_nt(a, b):
    return lax.dot_general(a, b, (((1,), (1,)), ((), ())), preferred_element_type=F32)


def _rms(x, g):
    return x * lax.rsqrt(jnp.mean(x * x, axis=-1, keepdims=True) + EPS) * g


def _params(*sem):
    return pltpu.CompilerParams(dimension_semantics=sem, vmem_limit_bytes=VMEM_LIMIT)


def _ffn_kernel(x_ref, gpre_ref, wg_ref, wu_ref, wd_ref, gpost_ref, o_ref, h_ref, acc_ref):
    f = pl.program_id(1)

    @pl.when(f == 0)
    def _():
        h_ref[...] = _rms(x_ref[...], gpre_ref[...]).astype(BF16)
        acc_ref[...] = jnp.zeros_like(acc_ref)

    h = h_ref[...]
    g = _dot(h, wg_ref[...])
    u = _dot(h, wu_ref[...])
    a = (g * (1.0 / (1.0 + jnp.exp(-g)))) * u
    acc_ref[...] += _dot(a.astype(BF16), wd_ref[...])

    @pl.when(f == pl.num_programs(1) - 1)
    def _():
        o_ref[...] = x_ref[...] + 0.5 * _rms(acc_ref[...], gpost_ref[...])


def _ffn(x, g_pre, w_gate, w_up, w_down, g_post, *, tm=512, tf=512):
    t, d = x.shape
    d_ff = w_gate.shape[1]
    return pl.pallas_call(
        _ffn_kernel,
        out_shape=jax.ShapeDtypeStruct((t, d), F32),
        grid=(t // tm, d_ff // tf),
        in_specs=[
            pl.BlockSpec((tm, d), lambda m, f: (m, 0)),
            pl.BlockSpec((1, d), lambda m, f: (0, 0)),
            pl.BlockSpec((d, tf), lambda m, f: (0, f)),
            pl.BlockSpec((d, tf), lambda m, f: (0, f)),
            pl.BlockSpec((tf, d), lambda m, f: (f, 0)),
            pl.BlockSpec((1, d), lambda m, f: (0, 0)),
        ],
        out_specs=pl.BlockSpec((tm, d), lambda m, f: (m, 0)),
        scratch_shapes=[pltpu.VMEM((tm, d), BF16), pltpu.VMEM((tm, d), F32)],
        compiler_params=_params("parallel", "arbitrary"),
        name="ffn",
    )(x, g_pre, w_gate, w_up, w_down, g_post)


def _proj_kernel(x_ref, g_ref, w_ref, wff_ref, cos_ref, sa_ref, sb_ref, o_ref, ff_ref, h_ref,
                 *, tn):
    n = pl.program_id(1)
    dq_tile = (DQ_BLK * LANES) // tn
    dk_tile = (DK_BLK * LANES) // tn

    @pl.when(n == 0)
    def _():
        h = _rms(x_ref[...], g_ref[...]).astype(BF16)
        h_ref[...] = h
        ff_ref[...] = _dot(h, wff_ref[...])

    y = _dot(h_ref[...], w_ref[...])
    is_rope = jnp.logical_or(n == dq_tile, n == dk_tile)

    @pl.when(is_rope)
    def _():
        qscale = jnp.where(n == dq_tile, DIFF_QK_DIM ** -0.5, 1.0).astype(F32)
        c, sa, sb = cos_ref[...], sa_ref[...], sb_ref[...]
        for j in range(tn // LANES):
            blk = y[:, j * LANES:(j + 1) * LANES]
            half = ROPE_DIM // 2
            rot = (blk * c + pltpu.roll(blk, LANES - half, axis=1) * sa
                   + pltpu.roll(blk, half, axis=1) * sb)
            o_ref[:, j * LANES:(j + 1) * LANES] = (rot * qscale).astype(BF16)

    @pl.when(jnp.logical_not(is_rope))
    def _():
        o_ref[...] = y.astype(BF16)


def _mix_proj(x, g, w, wff, cos_t, sa_t, sb_t, *, seq, tm=512, tn=512):
    t, d = x.shape
    n_out = w.shape[1]
    assert (DQ_BLK * LANES) % tn == 0 and DIFF_QK_W == tn
    s_tiles = seq // tm
    return pl.pallas_call(
        functools.partial(_proj_kernel, tn=tn),
        out_shape=(jax.ShapeDtypeStruct((t, n_out), BF16),
                   jax.ShapeDtypeStruct((t, LANES), F32)),
        grid=(t // tm, n_out // tn),
        in_specs=[
            pl.BlockSpec((tm, d), lambda m, n: (m, 0)),
            pl.BlockSpec((1, d), lambda m, n: (0, 0)),
            pl.BlockSpec((d, tn), lambda m, n: (0, n)),
            pl.BlockSpec((d, LANES), lambda m, n: (0, 0)),
            pl.BlockSpec((tm, LANES), lambda m, n: (m % s_tiles, 0)),
            pl.BlockSpec((tm, LANES), lambda m, n: (m % s_tiles, 0)),
            pl.BlockSpec((tm, LANES), lambda m, n: (m % s_tiles, 0)),
        ],
        out_specs=(pl.BlockSpec((tm, tn), lambda m, n: (m, n)),
                   pl.BlockSpec((tm, LANES), lambda m, n: (m, 0))),
        scratch_shapes=[pltpu.VMEM((tm, d), BF16)],
        compiler_params=_params("parallel", "arbitrary"),
        name="mix_proj",
    )(x, g, w, wff, cos_t, sa_t, sb_t)


def _rope_tables(seq):
    half = ROPE_DIM // 2
    pos = jnp.arange(seq, dtype=F32)
    inv_freq = ROPE_THETA ** (-jnp.arange(0, ROPE_DIM, 2, dtype=F32) / ROPE_DIM)
    ang = pos[:, None] * inv_freq[None, :]
    cos, sin = jnp.cos(ang), jnp.sin(ang)
    ones = jnp.ones((seq, DIFF_QK_DIM - ROPE_DIM), F32)
    zeros_h = jnp.zeros((seq, half), F32)
    zeros_r = jnp.zeros((seq, DIFF_QK_DIM - ROPE_DIM), F32)
    cos_m = jnp.concatenate([cos, cos, ones], axis=1)
    sa_m = jnp.concatenate([-sin, zeros_h, zeros_r], axis=1)
    sb_m = jnp.concatenate([zeros_h, sin, zeros_r], axis=1)
    rep = LANES // DIFF_QK_DIM
    return (jnp.tile(cos_m, (1, rep)), jnp.tile(sa_m, (1, rep)), jnp.tile(sb_m, (1, rep)))


def _fgate_kernel(ff_ref, bias_ref, fcol_ref, frow_ref, *, cb):
    seq = ff_ref.shape[0]
    z = ff_ref[...] + bias_ref[...]
    lf = jnp.minimum(z, 0.0) - jnp.log1p(jnp.exp(-jnp.abs(z)))
    r = lax.broadcasted_iota(jnp.int32, (cb, cb), 0)
    c = lax.broadcasted_iota(jnp.int32, (cb, cb), 1)
    tri = (r >= c).astype(F32)
    carry = jnp.zeros((1, LANES), F32)
    for i in range(seq // cb):
        cs = jnp.dot(tri, lf[i * cb:(i + 1) * cb], precision=lax.Precision.HIGHEST,
                     preferred_element_type=F32) + carry
        fcol_ref[i * cb:(i + 1) * cb, :] = cs
        carry = cs[cb - 1:cb, :]
    frow_ref[...] = fcol_ref[...].T[:FOX_HEADS]


def _fox_gate(ff, bias, *, batch, seq, cb=256):
    ff = ff.reshape(batch, seq, LANES)
    return pl.pallas_call(
        functools.partial(_fgate_kernel, cb=cb),
        out_shape=(jax.ShapeDtypeStruct((batch, seq, LANES), F32),
                   jax.ShapeDtypeStruct((batch, FOX_HEADS, seq), F32)),
        grid=(batch,),
        in_specs=[pl.BlockSpec((None, seq, LANES), lambda b: (b, 0, 0)),
                  pl.BlockSpec((1, LANES), lambda b: (0, 0))],
        out_specs=(pl.BlockSpec((None, seq, LANES), lambda b: (b, 0, 0)),
                   pl.BlockSpec((None, FOX_HEADS, seq), lambda b: (b, 0, 0))),
        compiler_params=_params("parallel"),
        name="fox_gate",
    )(ff, bias)


def _flash_step(carry, s, v):
    m, l, acc = carry
    m_new = jnp.maximum(m, jnp.max(s, axis=1, keepdims=True))
    alpha = jnp.exp(m - m_new)
    p = jnp.exp(s - m_new)
    l = alpha * l + jnp.sum(p, axis=1, keepdims=True)
    acc = alpha * acc + _dot(p.astype(BF16), v)
    return m_new, l, acc


def _flash_init(rows, width):
    return (jnp.full((rows, 1), NEG_INF, F32), jnp.zeros((rows, 1), F32),
            jnp.zeros((rows, width), F32))


def _fox_kernel(q_ref, k_ref, v_ref, fcol_ref, frow_ref, o_ref, *, tq):
    h = pl.program_id(1)
    i = pl.program_id(2)
    q = q_ref[...]
    lane = lax.broadcasted_iota(jnp.int32, (tq, LANES), 1)
    ft = jnp.sum(jnp.where(lane == h, fcol_ref[...], 0.0), axis=1, keepdims=True)
    scale = FOX_DIM ** -0.5

    def logits(j):
        start = pl.multiple_of(j * tq, tq)
        s = _dot_nt(q, k_ref[pl.ds(start, tq), :])
        fs = frow_ref[pl.ds(j, 1), :]
        return s * scale + (ft - fs), v_ref[pl.ds(start, tq), :]

    def body(j, carry):
        s, v = logits(j)
        return _flash_step(carry, s, v)

    carry = lax.fori_loop(0, i, body, _flash_init(tq, FOX_DIM))
    s, v = logits(i)
    row = lax.broadcasted_iota(jnp.int32, (tq, tq), 0)
    col = lax.broadcasted_iota(jnp.int32, (tq, tq), 1)
    s = jnp.where(col <= row, s, NEG_INF)
    _, l, acc = _flash_step(carry, s, v)
    o_ref[...] = (acc / l).astype(BF16)


def _fox_attn(proj, fcol, frow, *, batch, seq, tq=256):
    proj3 = proj.reshape(batch, seq, PROJ_W)
    frow4 = frow.reshape(batch, FOX_HEADS, seq // tq, tq)
    out = pl.pallas_call(
        functools.partial(_fox_kernel, tq=tq),
        out_shape=jax.ShapeDtypeStruct((batch, seq, FOX_W), BF16),
        grid=(batch, FOX_HEADS, seq // tq),
        in_specs=[
            pl.BlockSpec((None, tq, FOX_DIM), lambda b, h, i: (b, i, FQ_BLK + h)),
            pl.BlockSpec((None, seq, FOX_DIM), lambda b, h, i: (b, 0, FK_BLK + h)),
            pl.BlockSpec((None, seq, FOX_DIM), lambda b, h, i: (b, 0, FV_BLK + h)),
            pl.BlockSpec((None, tq, LANES), lambda b, h, i: (b, i, 0)),
            pl.BlockSpec((None, None, seq // tq, tq), lambda b, h, i: (b, h, 0, 0)),
        ],
        out_specs=pl.BlockSpec((None, tq, FOX_DIM), lambda b, h, i: (b, i, h)),
        compiler_params=_params("parallel", "parallel", "arbitrary"),
        name="fox_attn",
    )(proj3, proj3, proj3, fcol, frow4)
    return out.reshape(batch * seq, FOX_W)


def _diff_kernel(lamv_ref, g_ref, q_ref, k_ref, v_ref, o_ref, *, tq, lam_init):
    i = pl.program_id(2)
    lv = lamv_ref[...]
    lam = (jnp.exp(jnp.sum(lv[0:1] * lv[1:2], axis=1, keepdims=True))
           - jnp.exp(jnp.sum(lv[2:3] * lv[3:4], axis=1, keepdims=True)) + lam_init)

    q = q_ref[...].astype(F32)
    lane = lax.broadcasted_iota(jnp.int32, (tq, LANES), 1)
    qs = jnp.concatenate([jnp.where(lane < DIFF_QK_DIM, q, 0.0),
                          jnp.where(lane >= DIFF_QK_DIM, q, 0.0)], axis=0).astype(BF16)

    def logits(j):
        start = pl.multiple_of(j * tq, tq)
        return _dot_nt(qs, k_ref[pl.ds(start, tq), :]), v_ref[pl.ds(start, tq), :]

    def body(j, carry):
        s, v = logits(j)
        return _flash_step(carry, s, v)

    carry = lax.fori_loop(0, i, body, _flash_init(2 * tq, DIFF_V_DIM))
    s, v = logits(i)
    row = lax.broadcasted_iota(jnp.int32, (2 * tq, tq), 0)
    row = jnp.where(row >= tq, row - tq, row)
    col = lax.broadcasted_iota(jnp.int32, (2 * tq, tq), 1)
    s = jnp.where(col // CHUNK <= row // CHUNK, s, NEG_INF)
    _, l, acc = _flash_step(carry, s, v)
    o = acc / l
    yd = o[:tq] - lam * o[tq:]
    o_ref[...] = (_rms(yd, g_ref[...]) * (1.0 - lam_init)).astype(BF16)


def _diff_attn(proj, lamv, g, *, batch, seq, lam_init, tq=256):
    proj3 = proj.reshape(batch, seq, PROJ_W)
    out = pl.pallas_call(
        functools.partial(_diff_kernel, tq=tq, lam_init=lam_init),
        out_shape=jax.ShapeDtypeStruct((batch, seq, DIFF_V_W), BF16),
        grid=(batch, DIFF_HEADS, seq // tq),
        in_specs=[
            pl.BlockSpec((4, DIFF_QK_DIM), lambda b, h, i: (0, 0)),
            pl.BlockSpec((1, DIFF_V_DIM), lambda b, h, i: (0, 0)),
            pl.BlockSpec((None, tq, LANES), lambda b, h, i: (b, i, DQ_BLK + h)),
            pl.BlockSpec((None, seq, LANES), lambda b, h, i: (b, 0, DK_BLK + h)),
            pl.BlockSpec((None, seq, DIFF_V_DIM), lambda b, h, i: (b, 0, DV_BLK + h)),
        ],
        out_specs=pl.BlockSpec((None, tq, DIFF_V_DIM), lambda b, h, i: (b, i, h)),
        compiler_params=_params("parallel", "parallel", "arbitrary"),
        name="diff_attn",
    )(lamv, g, proj3, proj3, proj3)
    return out.reshape(batch * seq, DIFF_V_W)


def _mem_kv_kernel(mem_ref, g_ref, w_ref, o_ref):
    o_ref[...] = _dot(_rms(mem_ref[...], g_ref[...]).astype(BF16), w_ref[...]).astype(BF16)


def _mem_kv(mem, g, w):
    batch, n_mem, d = mem.shape
    return pl.pallas_call(
        _mem_kv_kernel,
        out_shape=jax.ShapeDtypeStruct((batch, n_mem, 2 * MEM_W), BF16),
        grid=(batch,),
        in_specs=[pl.BlockSpec((None, n_mem, d), lambda b: (b, 0, 0)),
                  pl.BlockSpec((1, d), lambda b: (0, 0)),
                  pl.BlockSpec((d, 2 * MEM_W), lambda b: (0, 0))],
        out_specs=pl.BlockSpec((None, n_mem, 2 * MEM_W), lambda b: (b, 0, 0)),
        compiler_params=_params("parallel"),
        name="mem_kv",
    )(mem, g, w)


def _mem_attn_kernel(q_ref, kv_ref, o_ref):
    scale = MEM_DIM ** -0.5
    for h in range(MEM_HEADS):
        q = q_ref[:, h * MEM_DIM:(h + 1) * MEM_DIM]
        k = kv_ref[:, h * MEM_DIM:(h + 1) * MEM_DIM]
        v = kv_ref[:, MEM_W + h * MEM_DIM:MEM_W + (h + 1) * MEM_DIM]
        s = _dot_nt(q, k) * scale
        p = jnp.exp(s - jnp.max(s, axis=1, keepdims=True))
        l = jnp.sum(p, axis=1, keepdims=True)
        o_ref[:, h * MEM_DIM:(h + 1) * MEM_DIM] = (_dot(p.astype(BF16), v) / l).astype(BF16)


def _mem_attn(proj, mkv, *, batch, seq, tq=512):
    proj3 = proj.reshape(batch, seq, PROJ_W)
    n_mem = mkv.shape[1]
    out = pl.pallas_call(
        _mem_attn_kernel,
        out_shape=jax.ShapeDtypeStruct((batch, seq, MEM_W), BF16),
        grid=(batch, seq // tq),
        in_specs=[pl.BlockSpec((None, tq, MEM_W), lambda b, i: (b, i, MQ_COL // MEM_W)),
                  pl.BlockSpec((None, n_mem, 2 * MEM_W), lambda b, i: (b, 0, 0))],
        out_specs=pl.BlockSpec((None, tq, MEM_W), lambda b, i: (b, i, 0)),
        compiler_params=_params("parallel", "arbitrary"),
        name="mem_attn",
    )(proj3, mkv)
    return out.reshape(batch * seq, MEM_W)


def _merge_kernel(x_ref, gpre_ref, yf_ref, yd_ref, ym_ref, wgf_ref, wgd_ref, wgm_ref,
                  bf_ref, bd_ref, bm_ref, wf_ref, wd_ref, wm_ref, wo_ref, gpost_ref,
                  o_ref, h_ref, acc_ref):
    n = pl.program_id(1)

    @pl.when(n == 0)
    def _():
        h_ref[...] = _rms(x_ref[...], gpre_ref[...]).astype(BF16)
        acc_ref[...] = jnp.zeros_like(acc_ref)

    h = h_ref[...]

    def gated(wg_ref, b_ref, y_ref, w_ref):
        z = _dot(h, wg_ref[...]) + b_ref[...]
        return (1.0 / (1.0 + jnp.exp(-z))) * _dot(y_ref[...], w_ref[...])

    merged = (gated(wgf_ref, bf_ref, yf_ref, wf_ref) + gated(wgd_ref, bd_ref, yd_ref, wd_ref)
              + gated(wgm_ref, bm_ref, ym_ref, wm_ref))
    acc_ref[...] += _dot(merged.astype(BF16), wo_ref[...])

    @pl.when(n == pl.num_programs(1) - 1)
    def _():
        o_ref[...] = x_ref[...] + _rms(acc_ref[...], gpost_ref[...])


def _merge(x, g_pre, y_fox, y_diff, y_mem, w_gate, b_gate, w_fox, w_diff, w_mem, w_out, g_post,
           *, tm=512, tn=512):
    t, d = x.shape
    nt = d // tn
    row = lambda m, n: (m, 0)
    return pl.pallas_call(
        _merge_kernel,
        out_shape=jax.ShapeDtypeStruct((t, d), F32),
        grid=(t // tm, nt),
        in_specs=[
            pl.BlockSpec((tm, d), row),
            pl.BlockSpec((1, d), lambda m, n: (0, 0)),
            pl.BlockSpec((tm, FOX_W), row),
            pl.BlockSpec((tm, DIFF_V_W), row),
            pl.BlockSpec((tm, MEM_W), row),
            pl.BlockSpec((d, tn), lambda m, n: (0, n)),
            pl.BlockSpec((d, tn), lambda m, n: (0, nt + n)),
            pl.BlockSpec((d, tn), lambda m, n: (0, 2 * nt + n)),
            pl.BlockSpec((1, tn), lambda m, n: (0, n)),
            pl.BlockSpec((1, tn), lambda m, n: (0, nt + n)),
            pl.BlockSpec((1, tn), lambda m, n: (0, 2 * nt + n)),
            pl.BlockSpec((FOX_W, tn), lambda m, n: (0, n)),
            pl.BlockSpec((DIFF_V_W, tn), lambda m, n: (0, n)),
            pl.BlockSpec((MEM_W, tn), lambda m, n: (0, n)),
            pl.BlockSpec((tn, d), lambda m, n: (n, 0)),
            pl.BlockSpec((1, d), lambda m, n: (0, 0)),
        ],
        out_specs=pl.BlockSpec((tm, d), row),
        scratch_shapes=[pltpu.VMEM((tm, d), BF16), pltpu.VMEM((tm, d), F32)],
        compiler_params=_params("parallel", "arbitrary"),
        name="merge",
    )(x, g_pre, y_fox, y_diff, y_mem, w_gate, w_gate, w_gate, b_gate, b_gate, b_gate,
      w_fox, w_diff, w_mem, w_out, g_post)


def kernel(x, mem, ffn1_pre_g, ffn1_w_gate, ffn1_w_up, ffn1_w_down, ffn1_post_g, mix_pre_g, w_in, fox_f_bias, diff_lambda_q1, diff_lambda_k1, diff_lambda_q2, diff_lambda_k2, diff_head_g, mem_norm_g, w_mem_kv, w_branch_fox, w_branch_diff, w_branch_mem, w_merge_gate, b_merge_gate, w_out, mix_post_g, ffn2_pre_g, ffn2_w_gate, ffn2_w_up, ffn2_w_down, ffn2_post_g):
    batch, seq, d = x.shape
    depth = w_in.shape[0]
    xt = x.reshape(batch * seq, d)
    cos_t, sa_t, sb_t = _rope_tables(seq)
    ff_lo = 3 * FOX_W
    ff_hi = ff_lo + FOX_HEADS

    def row(v):
        return v.reshape(1, -1).astype(F32)

    for l in range(depth):
        bf = lambda w: w[l].astype(BF16)
        xt = _ffn(xt, row(ffn1_pre_g[l]), bf(ffn1_w_gate), bf(ffn1_w_up), bf(ffn1_w_down),
                  row(ffn1_post_g[l]))

        w_main = jnp.concatenate([w_in[l][:, :ff_lo], w_in[l][:, ff_hi:]], axis=1).astype(BF16)
        w_ff = jnp.pad(w_in[l][:, ff_lo:ff_hi], ((0, 0), (0, LANES - FOX_HEADS))).astype(BF16)
        proj, ff = _mix_proj(xt, row(mix_pre_g[l]), w_main, w_ff, cos_t, sa_t, sb_t, seq=seq)

        bias = jnp.pad(fox_f_bias[l].astype(F32), (0, LANES - FOX_HEADS)).reshape(1, LANES)
        fcol, frow = _fox_gate(ff, bias, batch=batch, seq=seq)
        y_fox = _fox_attn(proj, fcol, frow, batch=batch, seq=seq)

        lam_init = 0.8 - 0.6 * math.exp(-0.3 * l)
        lamv = jnp.stack([diff_lambda_q1[l], diff_lambda_k1[l], diff_lambda_q2[l],
                          diff_lambda_k2[l]]).astype(F32)
        y_diff = _diff_attn(proj, lamv, row(diff_head_g[l]), batch=batch, seq=seq,
                            lam_init=lam_init)

        mkv = _mem_kv(mem, row(mem_norm_g[l]), bf(w_mem_kv))
        y_mem = _mem_attn(proj, mkv, batch=batch, seq=seq)

        xt = _merge(xt, row(mix_pre_g[l]), y_fox, y_diff, y_mem, bf(w_merge_gate),
                    row(b_merge_gate[l]), bf(w_branch_fox), bf(w_branch_diff), bf(w_branch_mem),
                    bf(w_out), row(mix_post_g[l]))

        xt = _ffn(xt, row(ffn2_pre_g[l]), bf(ffn2_w_gate), bf(ffn2_w_up), bf(ffn2_w_down),
                  row(ffn2_post_g[l]))

    return xt.reshape(batch, seq, d)
```

```python
import functools
import math

import jax
import jax.numpy as jnp
from jax import lax
from jax.experimental import pallas as pl
from jax.experimental.pallas import tpu as pltpu

D_MODEL = 2048
CHUNK = 64
EPS = 1e-6
ROPE_THETA = 500000.0

FOX_HEADS = 8
FOX_DIM = 128
FOX_W = FOX_HEADS * FOX_DIM

DIFF_HEADS = 4
DIFF_QK_DIM = 64
DIFF_V_DIM = 2 * DIFF_QK_DIM
DIFF_QK_W = DIFF_HEADS * 2 * DIFF_QK_DIM
DIFF_V_W = DIFF_HEADS * DIFF_V_DIM
ROPE_DIM = DIFF_QK_DIM // 4

MEM_HEADS = 4
MEM_DIM = 128
MEM_W = MEM_HEADS * MEM_DIM

LANES = 128
PROJ_W = 3 * FOX_W + 2 * DIFF_QK_W + DIFF_V_W + MEM_W
FQ_BLK, FK_BLK, FV_BLK = 0, FOX_HEADS, 2 * FOX_HEADS
DQ_BLK = 3 * FOX_HEADS
DK_BLK = DQ_BLK + DIFF_HEADS
DV_BLK = DK_BLK + DIFF_HEADS
MQ_COL = 3 * FOX_W + 2 * DIFF_QK_W + DIFF_V_W

VMEM_LIMIT = 56 * 1024 * 1024

BF16 = jnp.bfloat16
F32 = jnp.float32
NEG_INF = float("-inf")
LOG2E = math.log2(math.e)


def _dot(a, b):
    return jnp.dot(a, b, preferred_element_type=F32)


def _dot_nt(a, b):
    return lax.dot_general(a, b, (((1,), (1,)), ((), ())), preferred_element_type=F32)


def _rms(x, g):
    return x * lax.rsqrt(jnp.mean(x * x, axis=-1, keepdims=True) + EPS) * g


def _params(*sem):
    return pltpu.CompilerParams(dimension_semantics=sem, vmem_limit_bytes=VMEM_LIMIT)


def _ffn_kernel(x_ref, gpre_ref, wg_ref, wu_ref, wd_ref, gpost_ref, o_ref, h_ref, acc_ref):
    f = pl.program_id(1)

    @pl.when(f == 0)
    def _():
        h_ref[...] = _rms(x_ref[...], gpre_ref[...]).astype(BF16)
        acc_ref[...] = jnp.zeros_like(acc_ref)

    h = h_ref[...]
    g = _dot(h, wg_ref[...])
    u = _dot(h, wu_ref[...])
    a = (g * (1.0 / (1.0 + jnp.exp(-g)))) * u
    acc_ref[...] += _dot(a.astype(BF16), wd_ref[...])

    @pl.when(f == pl.num_programs(1) - 1)
    def _():
        o_ref[...] = x_ref[...] + 0.5 * _rms(acc_ref[...], gpost_ref[...])


def _ffn(x, g_pre, w_gate, w_up, w_down, g_post, *, tm=512, tf=512):
    t, d = x.shape
    d_ff = w_gate.shape[1]
    return pl.pallas_call(
        _ffn_kernel,
        out_shape=jax.ShapeDtypeStruct((t, d), F32),
        grid=(t // tm, d_ff // tf),
        in_specs=[
            pl.BlockSpec((tm, d), lambda m, f: (m, 0)),
            pl.BlockSpec((1, d), lambda m, f: (0, 0)),
            pl.BlockSpec((d, tf), lambda m, f: (0, f)),
            pl.BlockSpec((d, tf), lambda m, f: (0, f)),
            pl.BlockSpec((tf, d), lambda m, f: (f, 0)),
            pl.BlockSpec((1, d), lambda m, f: (0, 0)),
        ],
        out_specs=pl.BlockSpec((tm, d), lambda m, f: (m, 0)),
        scratch_shapes=[pltpu.VMEM((tm, d), BF16), pltpu.VMEM((tm, d), F32)],
        compiler_params=_params("parallel", "arbitrary"),
        name="ffn",
    )(x, g_pre, w_gate, w_up, w_down, g_post)


def _proj_kernel(x_ref, g_ref, w_ref, wff_ref, cos_ref, sa_ref, sb_ref, o_ref, ff_ref, h_ref,
                 *, tn):
    n = pl.program_id(1)
    dq_tile = (DQ_BLK * LANES) // tn
    dk_tile = (DK_BLK * LANES) // tn

    @pl.when(n == 0)
    def _():
        h = _rms(x_ref[...], g_ref[...]).astype(BF16)
        h_ref[...] = h
        ff_ref[...] = _dot(h, wff_ref[...])

    y = _dot(h_ref[...], w_ref[...])
    is_rope = jnp.logical_or(n == dq_tile, n == dk_tile)

    @pl.when(is_rope)
    def _():
        qscale = jnp.where(n == dq_tile, DIFF_QK_DIM ** -0.5, 1.0).astype(F32)
        c, sa, sb = cos_ref[...], sa_ref[...], sb_ref[...]
        for j in range(tn // LANES):
            blk = y[:, j * LANES:(j + 1) * LANES]
            half = ROPE_DIM // 2
            rot = (blk * c + pltpu.roll(blk, LANES - half, axis=1) * sa
                   + pltpu.roll(blk, half, axis=1) * sb)
            o_ref[:, j * LANES:(j + 1) * LANES] = (rot * qscale).astype(BF16)

    @pl.when(jnp.logical_not(is_rope))
    def _():
        o_ref[...] = y.astype(BF16)


def _mix_proj(x, g, w, wff, cos_t, sa_t, sb_t, *, seq, tm=512, tn=512):
    t, d = x.shape
    n_out = w.shape[1]
    assert (DQ_BLK * LANES) % tn == 0 and DIFF_QK_W == tn
    s_tiles = seq // tm
    return pl.pallas_call(
        functools.partial(_proj_kernel, tn=tn),
        out_shape=(jax.ShapeDtypeStruct((t, n_out), BF16),
                   jax.ShapeDtypeStruct((t, LANES), F32)),
        grid=(t // tm, n_out // tn),
        in_specs=[
            pl.BlockSpec((tm, d), lambda m, n: (m, 0)),
            pl.BlockSpec((1, d), lambda m, n: (0, 0)),
            pl.BlockSpec((d, tn), lambda m, n: (0, n)),
            pl.BlockSpec((d, LANES), lambda m, n: (0, 0)),
            pl.BlockSpec((tm, LANES), lambda m, n: (m % s_tiles, 0)),
            pl.BlockSpec((tm, LANES), lambda m, n: (m % s_tiles, 0)),
            pl.BlockSpec((tm, LANES), lambda m, n: (m % s_tiles, 0)),
        ],
        out_specs=(pl.BlockSpec((tm, tn), lambda m, n: (m, n)),
                   pl.BlockSpec((tm, LANES), lambda m, n: (m, 0))),
        scratch_shapes=[pltpu.VMEM((tm, d), BF16)],
        compiler_params=_params("parallel", "arbitrary"),
        name="mix_proj",
    )(x, g, w, wff, cos_t, sa_t, sb_t)


def _rope_tables(seq):
    half = ROPE_DIM // 2
    pos = jnp.arange(seq, dtype=F32)
    inv_freq = ROPE_THETA ** (-jnp.arange(0, ROPE_DIM, 2, dtype=F32) / ROPE_DIM)
    ang = pos[:, None] * inv_freq[None, :]
    cos, sin = jnp.cos(ang), jnp.sin(ang)
    ones = jnp.ones((seq, DIFF_QK_DIM - ROPE_DIM), F32)
    zeros_h = jnp.zeros((seq, half), F32)
    zeros_r = jnp.zeros((seq, DIFF_QK_DIM - ROPE_DIM), F32)
    cos_m = jnp.concatenate([cos, cos, ones], axis=1)
    sa_m = jnp.concatenate([-sin, zeros_h, zeros_r], axis=1)
    sb_m = jnp.concatenate([zeros_h, sin, zeros_r], axis=1)
    rep = LANES // DIFF_QK_DIM
    return (jnp.tile(cos_m, (1, rep)), jnp.tile(sa_m, (1, rep)), jnp.tile(sb_m, (1, rep)))


def _fgate_kernel(ff_ref, bias_ref, fb_ref, frow_ref, fcol_ref, *, cb):
    seq = ff_ref.shape[0]
    z = ff_ref[...] + bias_ref[...]
    lf = jnp.minimum(z, 0.0) - jnp.log1p(jnp.exp(-jnp.abs(z)))
    r = lax.broadcasted_iota(jnp.int32, (cb, cb), 0)
    c = lax.broadcasted_iota(jnp.int32, (cb, cb), 1)
    tri = (r >= c).astype(F32)
    carry = jnp.zeros((1, LANES), F32)
    for i in range(seq // cb):
        cs = jnp.dot(tri, lf[i * cb:(i + 1) * cb], precision=lax.Precision.HIGHEST,
                     preferred_element_type=F32) + carry
        carry = cs[cb - 1:cb, :]
        cs2 = cs * LOG2E
        fcol_ref[i * cb:(i + 1) * cb, :] = cs2
        for h in range(FOX_HEADS):
            fb_ref[h, i * cb:(i + 1) * cb, :] = jnp.broadcast_to(cs2[:, h:h + 1], (cb, LANES))
    frow_ref[...] = fcol_ref[...].T[:FOX_HEADS]


def _fox_gate(ff, bias, *, batch, seq, cb=256):
    ff = ff.reshape(batch, seq, LANES)
    return pl.pallas_call(
        functools.partial(_fgate_kernel, cb=cb),
        out_shape=(jax.ShapeDtypeStruct((batch, FOX_HEADS, seq, LANES), F32),
                   jax.ShapeDtypeStruct((batch, FOX_HEADS, seq), F32)),
        grid=(batch,),
        in_specs=[pl.BlockSpec((None, seq, LANES), lambda b: (b, 0, 0)),
                  pl.BlockSpec((1, LANES), lambda b: (0, 0))],
        out_specs=(pl.BlockSpec((None, FOX_HEADS, seq, LANES), lambda b: (b, 0, 0, 0)),
                   pl.BlockSpec((None, FOX_HEADS, seq), lambda b: (b, 0, 0))),
        scratch_shapes=[pltpu.VMEM((seq, LANES), F32)],
        compiler_params=_params("parallel"),
        name="fox_gate",
    )(ff, bias)


def _dot_tn(a, b):
    return lax.dot_general(a, b, (((0,), (0,)), ((), ())), preferred_element_type=F32)


def _flash_step(carry, a2, v, ft2):
    m, l, acc = carry
    m_new = jnp.maximum(m, jnp.max(a2, axis=0, keepdims=True) + ft2)
    alpha = jnp.exp2(m - m_new)
    p = jnp.exp2(a2 - (m_new - ft2))
    l = alpha * l + jnp.sum(p, axis=0, keepdims=True)
    acc = alpha * acc + _dot_tn(v, p.astype(BF16))
    return m_new, l, acc


def _flash_init(width, cols):
    return (jnp.full((1, cols), NEG_INF, F32), jnp.zeros((1, cols), F32),
            jnp.zeros((width, cols), F32))


def _fox_kernel(q_ref, k_ref, v_ref, fb_ref, fr_ref, o_ref, *, tq, hb):
    i = pl.program_id(2)
    c2 = FOX_DIM ** -0.5 * LOG2E
    heads = range(hb)
    ft2 = [fr_ref[h, pl.ds(i, 1), :] for h in heads]

    def logits(h, j):
        start = pl.multiple_of(j * tq, tq)
        hs = slice(h * FOX_DIM, (h + 1) * FOX_DIM)
        st = _dot_nt(k_ref[pl.ds(start, tq), hs], q_ref[:, hs])
        fb = fb_ref[h, pl.ds(start, tq), :]
        return st * c2 - jnp.concatenate([fb] * (tq // LANES), axis=1)

    def values(h, j):
        return v_ref[pl.ds(pl.multiple_of(j * tq, tq), tq), h * FOX_DIM:(h + 1) * FOX_DIM]

    def body(j, carry):
        cur, state = carry
        nxt, out = [], []
        for h in heads:
            nxt.append(logits(h, j + 1))
            out.append(_flash_step(state[h], cur[h], values(h, j), ft2[h]))
        return tuple(nxt), tuple(out)

    cur, state = lax.fori_loop(
        0, i, body, (tuple(logits(h, 0) for h in heads),
                     tuple(_flash_init(FOX_DIM, tq) for _ in heads)))
    krow = lax.broadcasted_iota(jnp.int32, (tq, tq), 0)
    qcol = lax.broadcasted_iota(jnp.int32, (tq, tq), 1)
    for h in heads:
        a2 = jnp.where(krow <= qcol, cur[h], NEG_INF)
        _, l, acc = _flash_step(state[h], a2, values(h, i), ft2[h])
        o_ref[:, h * FOX_DIM:(h + 1) * FOX_DIM] = (acc / l).T.astype(BF16)


def _fox_attn(proj, fb, frow, *, batch, seq, tq=256, hb=4):
    proj3 = proj.reshape(batch, seq, PROJ_W)
    frow4 = frow.reshape(batch, FOX_HEADS, seq // tq, tq)
    w = hb * FOX_DIM
    groups = FOX_HEADS // hb
    out = pl.pallas_call(
        functools.partial(_fox_kernel, tq=tq, hb=hb),
        out_shape=jax.ShapeDtypeStruct((batch, seq, FOX_W), BF16),
        grid=(batch, groups, seq // tq),
        in_specs=[
            pl.BlockSpec((None, tq, w), lambda b, g, i: (b, i, g)),
            pl.BlockSpec((None, seq, w), lambda b, g, i: (b, 0, groups + g)),
            pl.BlockSpec((None, seq, w), lambda b, g, i: (b, 0, 2 * groups + g)),
            pl.BlockSpec((None, hb, seq, LANES), lambda b, g, i: (b, g, 0, 0)),
            pl.BlockSpec((None, hb, seq // tq, tq), lambda b, g, i: (b, g, 0, 0)),
        ],
        out_specs=pl.BlockSpec((None, tq, w), lambda b, g, i: (b, i, g)),
        compiler_params=_params("parallel", "parallel", "arbitrary"),
        name="fox_attn",
    )(proj3, proj3, proj3, fb, frow4)
    return out.reshape(batch * seq, FOX_W)


def _diff_kernel(lamv_ref, g_ref, q_ref, k_ref, v_ref, o_ref, *, tq, hb, lam_init):
    i = pl.program_id(2)
    heads = range(hb)
    lv = lamv_ref[...]
    lam = (jnp.exp(jnp.sum(lv[0:1] * lv[1:2], axis=1, keepdims=True))
           - jnp.exp(jnp.sum(lv[2:3] * lv[3:4], axis=1, keepdims=True)) + lam_init)

    lane = lax.broadcasted_iota(jnp.int32, (tq, LANES), 1)
    qs = []
    for h in heads:
        q = q_ref[:, h * LANES:(h + 1) * LANES].astype(F32)
        qs.append(jnp.concatenate([jnp.where(lane < DIFF_QK_DIM, q, 0.0),
                                   jnp.where(lane >= DIFF_QK_DIM, q, 0.0)], axis=0).astype(BF16))
    zero = jnp.zeros((1, 2 * tq), F32)

    def logits(h, j):
        start = pl.multiple_of(j * tq, tq)
        st = _dot_nt(k_ref[pl.ds(start, tq), h * LANES:(h + 1) * LANES], qs[h])
        return st * LOG2E

    def values(h, j):
        return v_ref[pl.ds(pl.multiple_of(j * tq, tq), tq), h * LANES:(h + 1) * LANES]

    def body(j, carry):
        cur, state = carry
        nxt, out = [], []
        for h in heads:
            nxt.append(logits(h, j + 1))
            out.append(_flash_step(state[h], cur[h], values(h, j), zero))
        return tuple(nxt), tuple(out)

    cur, state = lax.fori_loop(
        0, i, body, (tuple(logits(h, 0) for h in heads),
                     tuple(_flash_init(DIFF_V_DIM, 2 * tq) for _ in heads)))
    krow = lax.broadcasted_iota(jnp.int32, (tq, 2 * tq), 0)
    qcol = lax.broadcasted_iota(jnp.int32, (tq, 2 * tq), 1)
    qcol = jnp.where(qcol >= tq, qcol - tq, qcol)
    visible = krow // CHUNK <= qcol // CHUNK
    for h in heads:
        a2 = jnp.where(visible, cur[h], NEG_INF)
        _, l, acc = _flash_step(state[h], a2, values(h, i), zero)
        o = acc / l
        yd = (o[:, :tq] - lam * o[:, tq:]).T
        o_ref[:, h * LANES:(h + 1) * LANES] = (
            _rms(yd, g_ref[...]) * (1.0 - lam_init)).astype(BF16)


def _diff_attn(proj, lamv, g, *, batch, seq, lam_init, tq=256, hb=4):
    proj3 = proj.reshape(batch, seq, PROJ_W)
    w = hb * LANES
    groups = DIFF_HEADS // hb
    dq, dk, dv = (DQ_BLK * LANES) // w, (DK_BLK * LANES) // w, (DV_BLK * LANES) // w
    out = pl.pallas_call(
        functools.partial(_diff_kernel, tq=tq, hb=hb, lam_init=lam_init),
        out_shape=jax.ShapeDtypeStruct((batch, seq, DIFF_V_W), BF16),
        grid=(batch, groups, seq // tq),
        in_specs=[
            pl.BlockSpec((4, DIFF_QK_DIM), lambda b, g, i: (0, 0)),
            pl.BlockSpec((1, DIFF_V_DIM), lambda b, g, i: (0, 0)),
            pl.BlockSpec((None, tq, w), lambda b, g, i: (b, i, dq + g)),
            pl.BlockSpec((None, seq, w), lambda b, g, i: (b, 0, dk + g)),
            pl.BlockSpec((None, seq, w), lambda b, g, i: (b, 0, dv + g)),
        ],
        out_specs=pl.BlockSpec((None, tq, w), lambda b, g, i: (b, i, g)),
        compiler_params=_params("parallel", "parallel", "arbitrary"),
        name="diff_attn",
    )(lamv, g, proj3, proj3, proj3)
    return out.reshape(batch * seq, DIFF_V_W)


def _mem_kv_kernel(mem_ref, g_ref, w_ref, o_ref):
    o_ref[...] = _dot(_rms(mem_ref[...], g_ref[...]).astype(BF16), w_ref[...]).astype(BF16)


def _mem_kv(mem, g, w):
    batch, n_mem, d = mem.shape
    return pl.pallas_call(
        _mem_kv_kernel,
        out_shape=jax.ShapeDtypeStruct((batch, n_mem, 2 * MEM_W), BF16),
        grid=(batch,),
        in_specs=[pl.BlockSpec((None, n_mem, d), lambda b: (b, 0, 0)),
                  pl.BlockSpec((1, d), lambda b: (0, 0)),
                  pl.BlockSpec((d, 2 * MEM_W), lambda b: (0, 0))],
        out_specs=pl.BlockSpec((None, n_mem, 2 * MEM_W), lambda b: (b, 0, 0)),
        compiler_params=_params("parallel"),
        name="mem_kv",
    )(mem, g, w)


def _mem_attn_kernel(q_ref, kv_ref, o_ref):
    scale = MEM_DIM ** -0.5
    for h in range(MEM_HEADS):
        q = q_ref[:, h * MEM_DIM:(h + 1) * MEM_DIM]
        k = kv_ref[:, h * MEM_DIM:(h + 1) * MEM_DIM]
        v = kv_ref[:, MEM_W + h * MEM_DIM:MEM_W + (h + 1) * MEM_DIM]
        s = _dot_nt(q, k) * scale
        p = jnp.exp(s - jnp.max(s, axis=1, keepdims=True))
        l = jnp.sum(p, axis=1, keepdims=True)
        o_ref[:, h * MEM_DIM:(h + 1) * MEM_DIM] = (_dot(p.astype(BF16), v) / l).astype(BF16)


def _mem_attn(proj, mkv, *, batch, seq, tq=512):
    proj3 = proj.reshape(batch, seq, PROJ_W)
    n_mem = mkv.shape[1]
    out = pl.pallas_call(
        _mem_attn_kernel,
        out_shape=jax.ShapeDtypeStruct((batch, seq, MEM_W), BF16),
        grid=(batch, seq // tq),
        in_specs=[pl.BlockSpec((None, tq, MEM_W), lambda b, i: (b, i, MQ_COL // MEM_W)),
                  pl.BlockSpec((None, n_mem, 2 * MEM_W), lambda b, i: (b, 0, 0))],
        out_specs=pl.BlockSpec((None, tq, MEM_W), lambda b, i: (b, i, 0)),
        compiler_params=_params("parallel", "arbitrary"),
        name="mem_attn",
    )(proj3, mkv)
    return out.reshape(batch * seq, MEM_W)


def _merge_kernel(x_ref, gpre_ref, yf_ref, yd_ref, ym_ref, wgf_ref, wgd_ref, wgm_ref,
                  bf_ref, bd_ref, bm_ref, wf_ref, wd_ref, wm_ref, wo_ref, gpost_ref,
                  o_ref, h_ref, acc_ref):
    n = pl.program_id(1)

    @pl.when(n == 0)
    def _():
        h_ref[...] = _rms(x_ref[...], gpre_ref[...]).astype(BF16)
        acc_ref[...] = jnp.zeros_like(acc_ref)

    h = h_ref[...]

    def gated(wg_ref, b_ref, y_ref, w_ref):
        z = _dot(h, wg_ref[...]) + b_ref[...]
        return (1.0 / (1.0 + jnp.exp(-z))) * _dot(y_ref[...], w_ref[...])

    merged = (gated(wgf_ref, bf_ref, yf_ref, wf_ref) + gated(wgd_ref, bd_ref, yd_ref, wd_ref)
              + gated(wgm_ref, bm_ref, ym_ref, wm_ref))
    acc_ref[...] += _dot(merged.astype(BF16), wo_ref[...])

    @pl.when(n == pl.num_programs(1) - 1)
    def _():
        o_ref[...] = x_ref[...] + _rms(acc_ref[...], gpost_ref[...])


def _merge(x, g_pre, y_fox, y_diff, y_mem, w_gate, b_gate, w_fox, w_diff, w_mem, w_out, g_post,
           *, tm=512, tn=512):
    t, d = x.shape
    nt = d // tn
    row = lambda m, n: (m, 0)
    return pl.pallas_call(
        _merge_kernel,
        out_shape=jax.ShapeDtypeStruct((t, d), F32),
        grid=(t // tm, nt),
        in_specs=[
            pl.BlockSpec((tm, d), row),
            pl.BlockSpec((1, d), lambda m, n: (0, 0)),
            pl.BlockSpec((tm, FOX_W), row),
            pl.BlockSpec((tm, DIFF_V_W), row),
            pl.BlockSpec((tm, MEM_W), row),
            pl.BlockSpec((d, tn), lambda m, n: (0, n)),
            pl.BlockSpec((d, tn), lambda m, n: (0, nt + n)),
            pl.BlockSpec((d, tn), lambda m, n: (0, 2 * nt + n)),
            pl.BlockSpec((1, tn), lambda m, n: (0, n)),
            pl.BlockSpec((1, tn), lambda m, n: (0, nt + n)),
            pl.BlockSpec((1, tn), lambda m, n: (0, 2 * nt + n)),
            pl.BlockSpec((FOX_W, tn), lambda m, n: (0, n)),
            pl.BlockSpec((DIFF_V_W, tn), lambda m, n: (0, n)),
            pl.BlockSpec((MEM_W, tn), lambda m, n: (0, n)),
            pl.BlockSpec((tn, d), lambda m, n: (n, 0)),
            pl.BlockSpec((1, d), lambda m, n: (0, 0)),
        ],
        out_specs=pl.BlockSpec((tm, d), row),
        scratch_shapes=[pltpu.VMEM((tm, d), BF16), pltpu.VMEM((tm, d), F32)],
        compiler_params=_params("parallel", "arbitrary"),
        name="merge",
    )(x, g_pre, y_fox, y_diff, y_mem, w_gate, w_gate, w_gate, b_gate, b_gate, b_gate,
      w_fox, w_diff, w_mem, w_out, g_post)


def kernel(x, mem, ffn1_pre_g, ffn1_w_gate, ffn1_w_up, ffn1_w_down, ffn1_post_g, mix_pre_g, w_in, fox_f_bias, diff_lambda_q1, diff_lambda_k1, diff_lambda_q2, diff_lambda_k2, diff_head_g, mem_norm_g, w_mem_kv, w_branch_fox, w_branch_diff, w_branch_mem, w_merge_gate, b_merge_gate, w_out, mix_post_g, ffn2_pre_g, ffn2_w_gate, ffn2_w_up, ffn2_w_down, ffn2_post_g):
    batch, seq, d = x.shape
    depth = w_in.shape[0]
    xt = x.reshape(batch * seq, d)
    cos_t, sa_t, sb_t = _rope_tables(seq)
    ff_lo = 3 * FOX_W
    ff_hi = ff_lo + FOX_HEADS

    def row(v):
        return v.reshape(1, -1).astype(F32)

    for l in range(depth):
        bf = lambda w: w[l].astype(BF16)
        xt = _ffn(xt, row(ffn1_pre_g[l]), bf(ffn1_w_gate), bf(ffn1_w_up), bf(ffn1_w_down),
                  row(ffn1_post_g[l]))

        w_main = jnp.concatenate([w_in[l][:, :ff_lo], w_in[l][:, ff_hi:]], axis=1).astype(BF16)
        w_ff = jnp.pad(w_in[l][:, ff_lo:ff_hi], ((0, 0), (0, LANES - FOX_HEADS))).astype(BF16)
        proj, ff = _mix_proj(xt, row(mix_pre_g[l]), w_main, w_ff, cos_t, sa_t, sb_t, seq=seq)

        bias = jnp.pad(fox_f_bias[l].astype(F32), (0, LANES - FOX_HEADS)).reshape(1, LANES)
        fb, frow = _fox_gate(ff, bias, batch=batch, seq=seq)
        y_fox = _fox_attn(proj, fb, frow, batch=batch, seq=seq)

        lam_init = 0.8 - 0.6 * math.exp(-0.3 * l)
        lamv = jnp.stack([diff_lambda_q1[l], diff_lambda_k1[l], diff_lambda_q2[l],
                          diff_lambda_k2[l]]).astype(F32)
        y_diff = _diff_attn(proj, lamv, row(diff_head_g[l]), batch=batch, seq=seq,
                            lam_init=lam_init)

        mkv = _mem_kv(mem, row(mem_norm_g[l]), bf(w_mem_kv))
        y_mem = _mem_attn(proj, mkv, batch=batch, seq=seq)

        xt = _merge(xt, row(mix_pre_g[l]), y_fox, y_diff, y_mem, bf(w_merge_gate),
                    row(b_merge_gate[l]), bf(w_branch_fox), bf(w_branch_diff), bf(w_branch_mem),
                    bf(w_out), row(mix_post_g[l]))

        xt = _ffn(xt, row(ffn2_pre_g[l]), bf(ffn2_w_gate), bf(ffn2_w_up), bf(ffn2_w_down),
                  row(ffn2_post_g[l]))

    return xt.reshape(batch, seq, d)
```

```python
import functools
import math

import jax
import jax.numpy as jnp
from jax import lax
from jax.experimental import pallas as pl
from jax.experimental.pallas import tpu as pltpu

D_MODEL = 2048
CHUNK = 64
EPS = 1e-6
ROPE_THETA = 500000.0

FOX_HEADS = 8
FOX_DIM = 128
FOX_W = FOX_HEADS * FOX_DIM

DIFF_HEADS = 4
DIFF_QK_DIM = 64
DIFF_V_DIM = 2 * DIFF_QK_DIM
DIFF_QK_W = DIFF_HEADS * 2 * DIFF_QK_DIM
DIFF_V_W = DIFF_HEADS * DIFF_V_DIM
ROPE_DIM = DIFF_QK_DIM // 4

MEM_HEADS = 4
MEM_DIM = 128
MEM_W = MEM_HEADS * MEM_DIM

LANES = 128
PROJ_W = 3 * FOX_W + 2 * DIFF_QK_W + DIFF_V_W + MEM_W
FQ_BLK, FK_BLK, FV_BLK = 0, FOX_HEADS, 2 * FOX_HEADS
DQ_BLK = 3 * FOX_HEADS
DK_BLK = DQ_BLK + DIFF_HEADS
DV_BLK = DK_BLK + DIFF_HEADS
MQ_COL = 3 * FOX_W + 2 * DIFF_QK_W + DIFF_V_W

VMEM_LIMIT = 56 * 1024 * 1024
FFN_TF = 512

BF16 = jnp.bfloat16
F32 = jnp.float32
NEG_INF = float("-inf")
LOG2E = math.log2(math.e)


def _dot(a, b):
    return jnp.dot(a, b, preferred_element_type=F32)


def _dot_nt(a, b):
    return lax.dot_general(a, b, (((1,), (1,)), ((), ())), preferred_element_type=F32)


def _rms(x, g):
    return x * lax.rsqrt(jnp.mean(x * x, axis=-1, keepdims=True) + EPS) * g


def _params(*sem):
    return pltpu.CompilerParams(dimension_semantics=sem, vmem_limit_bytes=VMEM_LIMIT)


def _ffn_kernel(x_ref, gpre_ref, wgu_ref, wd_ref, gpost_ref, o_ref, h_ref):
    f = pl.program_id(1)
    tf = wd_ref.shape[0]

    @pl.when(f == 0)
    def _():
        h_ref[...] = _rms(x_ref[...], gpre_ref[...]).astype(BF16)
        o_ref[...] = jnp.zeros_like(o_ref)

    gu = _dot(h_ref[...], wgu_ref[...])
    g, u = gu[:, :tf], gu[:, tf:]
    a = (g * (1.0 / (1.0 + jnp.exp(-g)))) * u
    o_ref[...] += _dot(a.astype(BF16), wd_ref[...])

    @pl.when(f == pl.num_programs(1) - 1)
    def _():
        o_ref[...] = x_ref[...] + 0.5 * _rms(o_ref[...], gpost_ref[...])


def _ffn_weights(w_gate, w_up, w_down, tf):
    d, d_ff = w_gate.shape
    nf = d_ff // tf
    gu = jnp.concatenate([w_gate.reshape(d, nf, tf), w_up.reshape(d, nf, tf)], axis=2)
    return gu.transpose(1, 0, 2).astype(BF16), w_down.astype(BF16)


def _ffn(x, g_pre, w_gu, w_down, g_post, *, tm=1024):
    t, d = x.shape
    nf, _, tf2 = w_gu.shape
    tf = tf2 // 2
    return pl.pallas_call(
        _ffn_kernel,
        out_shape=jax.ShapeDtypeStruct((t, d), F32),
        grid=(t // tm, nf),
        in_specs=[
            pl.BlockSpec((tm, d), lambda m, f: (m, 0), pipeline_mode=pl.Buffered(1)),
            pl.BlockSpec((1, d), lambda m, f: (0, 0)),
            pl.BlockSpec((None, d, tf2), lambda m, f: (f, 0, 0)),
            pl.BlockSpec((tf, d), lambda m, f: (f, 0)),
            pl.BlockSpec((1, d), lambda m, f: (0, 0)),
        ],
        out_specs=pl.BlockSpec((tm, d), lambda m, f: (m, 0)),
        scratch_shapes=[pltpu.VMEM((tm, d), BF16)],
        compiler_params=_params("parallel", "arbitrary"),
        name="ffn",
    )(x, g_pre, w_gu, w_down, g_post)


def _proj_kernel(x_ref, g_ref, w_ref, wff_ref, cos_ref, sa_ref, sb_ref, o_ref, ff_ref, h_ref,
                 *, tn):
    h_ref[...] = _rms(x_ref[...], g_ref[...]).astype(BF16)
    ff_ref[...] = _dot(h_ref[...], wff_ref[...])
    half = ROPE_DIM // 2
    for n in range(w_ref.shape[1] // tn):
        y = _dot(h_ref[...], w_ref[:, n * tn:(n + 1) * tn])
        first_blk = n * tn // LANES
        if DQ_BLK <= first_blk < DV_BLK:
            qscale = DIFF_QK_DIM ** -0.5 if first_blk < DK_BLK else 1.0
            for j in range(tn // LANES):
                blk = y[:, j * LANES:(j + 1) * LANES]
                rot = (blk * cos_ref[...] + pltpu.roll(blk, LANES - half, axis=1) * sa_ref[...]
                       + pltpu.roll(blk, half, axis=1) * sb_ref[...])
                o_ref[:, n * tn + j * LANES:n * tn + (j + 1) * LANES] = (rot * qscale).astype(BF16)
        else:
            o_ref[:, n * tn:(n + 1) * tn] = y.astype(BF16)


def _mix_proj(x, g, w, wff, cos_t, sa_t, sb_t, *, seq, tm=512, tn=512):
    t, d = x.shape
    n_out = w.shape[1]
    assert (DQ_BLK * LANES) % tn == 0 and (DK_BLK * LANES) % tn == 0 and (DV_BLK * LANES) % tn == 0
    s_tiles = seq // tm
    resident = dict(pipeline_mode=pl.Buffered(1))
    return pl.pallas_call(
        functools.partial(_proj_kernel, tn=tn),
        out_shape=(jax.ShapeDtypeStruct((t, n_out), BF16),
                   jax.ShapeDtypeStruct((t, LANES), F32)),
        grid=(t // tm,),
        in_specs=[
            pl.BlockSpec((tm, d), lambda m: (m, 0)),
            pl.BlockSpec((1, d), lambda m: (0, 0)),
            pl.BlockSpec((d, n_out), lambda m: (0, 0), **resident),
            pl.BlockSpec((d, LANES), lambda m: (0, 0), **resident),
            pl.BlockSpec((tm, LANES), lambda m: (m % s_tiles, 0)),
            pl.BlockSpec((tm, LANES), lambda m: (m % s_tiles, 0)),
            pl.BlockSpec((tm, LANES), lambda m: (m % s_tiles, 0)),
        ],
        out_specs=(pl.BlockSpec((tm, n_out), lambda m: (m, 0)),
                   pl.BlockSpec((tm, LANES), lambda m: (m, 0))),
        scratch_shapes=[pltpu.VMEM((tm, d), BF16)],
        compiler_params=_params("parallel"),
        name="mix_proj",
    )(x, g, w, wff, cos_t, sa_t, sb_t)


def _rope_tables(seq):
    half = ROPE_DIM // 2
    pos = jnp.arange(seq, dtype=F32)
    inv_freq = ROPE_THETA ** (-jnp.arange(0, ROPE_DIM, 2, dtype=F32) / ROPE_DIM)
    ang = pos[:, None] * inv_freq[None, :]
    cos, sin = jnp.cos(ang), jnp.sin(ang)
    ones = jnp.ones((seq, DIFF_QK_DIM - ROPE_DIM), F32)
    zeros_h = jnp.zeros((seq, half), F32)
    zeros_r = jnp.zeros((seq, DIFF_QK_DIM - ROPE_DIM), F32)
    cos_m = jnp.concatenate([cos, cos, ones], axis=1)
    sa_m = jnp.concatenate([-sin, zeros_h, zeros_r], axis=1)
    sb_m = jnp.concatenate([zeros_h, sin, zeros_r], axis=1)
    rep = LANES // DIFF_QK_DIM
    return (jnp.tile(cos_m, (1, rep)), jnp.tile(sa_m, (1, rep)), jnp.tile(sb_m, (1, rep)))


def _fgate_kernel(ff_ref, bias_ref, fb_ref, frow_ref, fcol_ref, *, cb):
    seq = ff_ref.shape[0]
    z = ff_ref[...] + bias_ref[...]
    lf = jnp.minimum(z, 0.0) - jnp.log1p(jnp.exp(-jnp.abs(z)))
    r = lax.broadcasted_iota(jnp.int32, (cb, cb), 0)
    c = lax.broadcasted_iota(jnp.int32, (cb, cb), 1)
    tri = (r >= c).astype(F32)
    carry = jnp.zeros((1, LANES), F32)
    for i in range(seq // cb):
        cs = jnp.dot(tri, lf[i * cb:(i + 1) * cb], precision=lax.Precision.HIGHEST,
                     preferred_element_type=F32) + carry
        carry = cs[cb - 1:cb, :]
        cs2 = cs * LOG2E
        fcol_ref[i * cb:(i + 1) * cb, :] = cs2
        for h in range(FOX_HEADS):
            fb_ref[h, i * cb:(i + 1) * cb, :] = jnp.broadcast_to(cs2[:, h:h + 1], (cb, LANES))
    frow_ref[...] = fcol_ref[...].T[:FOX_HEADS]


def _fox_gate(ff, bias, *, batch, seq, cb=256):
    ff = ff.reshape(batch, seq, LANES)
    return pl.pallas_call(
        functools.partial(_fgate_kernel, cb=cb),
        out_shape=(jax.ShapeDtypeStruct((batch, FOX_HEADS, seq, LANES), F32),
                   jax.ShapeDtypeStruct((batch, FOX_HEADS, seq), F32)),
        grid=(batch,),
        in_specs=[pl.BlockSpec((None, seq, LANES), lambda b: (b, 0, 0)),
                  pl.BlockSpec((1, LANES), lambda b: (0, 0))],
        out_specs=(pl.BlockSpec((None, FOX_HEADS, seq, LANES), lambda b: (b, 0, 0, 0)),
                   pl.BlockSpec((None, FOX_HEADS, seq), lambda b: (b, 0, 0))),
        scratch_shapes=[pltpu.VMEM((seq, LANES), F32)],
        compiler_params=_params("parallel"),
        name="fox_gate",
    )(ff, bias)


def _dot_tn(a, b):
    return lax.dot_general(a, b, (((0,), (0,)), ((), ())), preferred_element_type=F32)


def _flash_step(carry, a2, v, ft2):
    m, l, acc = carry
    m_new = jnp.maximum(m, jnp.max(a2, axis=0, keepdims=True) + ft2)
    alpha = jnp.exp2(m - m_new)
    p = jnp.exp2(a2 - (m_new - ft2))
    l = alpha * l + jnp.sum(p, axis=0, keepdims=True)
    acc = alpha * acc + _dot_tn(v, p.astype(BF16))
    return m_new, l, acc


def _flash_init(width, cols):
    return (jnp.full((1, cols), NEG_INF, F32), jnp.zeros((1, cols), F32),
            jnp.zeros((width, cols), F32))


def _fox_kernel(q_ref, k_ref, v_ref, fb_ref, fr_ref, o_ref, *, tq, hb):
    i = pl.program_id(2)
    c2 = FOX_DIM ** -0.5 * LOG2E
    heads = range(hb)
    ft2 = [fr_ref[h, pl.ds(i, 1), :] for h in heads]

    def logits(h, j):
        start = pl.multiple_of(j * tq, tq)
        hs = slice(h * FOX_DIM, (h + 1) * FOX_DIM)
        st = _dot_nt(k_ref[pl.ds(start, tq), hs], q_ref[:, hs])
        fb = fb_ref[h, pl.ds(start, tq), :]
        return st * c2 - jnp.concatenate([fb] * (tq // LANES), axis=1)

    def values(h, j):
        return v_ref[pl.ds(pl.multiple_of(j * tq, tq), tq), h * FOX_DIM:(h + 1) * FOX_DIM]

    def body(j, carry):
        cur, state = carry
        nxt, out = [], []
        for h in heads:
            nxt.append(logits(h, j + 1))
            out.append(_flash_step(state[h], cur[h], values(h, j), ft2[h]))
        return tuple(nxt), tuple(out)

    cur, state = lax.fori_loop(
        0, i, body, (tuple(logits(h, 0) for h in heads),
                     tuple(_flash_init(FOX_DIM, tq) for _ in heads)))
    krow = lax.broadcasted_iota(jnp.int32, (tq, tq), 0)
    qcol = lax.broadcasted_iota(jnp.int32, (tq, tq), 1)
    for h in heads:
        a2 = jnp.where(krow <= qcol, cur[h], NEG_INF)
        _, l, acc = _flash_step(state[h], a2, values(h, i), ft2[h])
        o_ref[:, h * FOX_DIM:(h + 1) * FOX_DIM] = (acc / l).T.astype(BF16)


def _fox_attn(proj, fb, frow, *, batch, seq, tq=256, hb=4):
    proj3 = proj.reshape(batch, seq, PROJ_W)
    frow4 = frow.reshape(batch, FOX_HEADS, seq // tq, tq)
    w = hb * FOX_DIM
    groups = FOX_HEADS // hb
    out = pl.pallas_call(
        functools.partial(_fox_kernel, tq=tq, hb=hb),
        out_shape=jax.ShapeDtypeStruct((batch, seq, FOX_W), BF16),
        grid=(batch, groups, seq // tq),
        in_specs=[
            pl.BlockSpec((None, tq, w), lambda b, g, i: (b, i, g)),
            pl.BlockSpec((None, seq, w), lambda b, g, i: (b, 0, groups + g)),
            pl.BlockSpec((None, seq, w), lambda b, g, i: (b, 0, 2 * groups + g)),
            pl.BlockSpec((None, hb, seq, LANES), lambda b, g, i: (b, g, 0, 0)),
            pl.BlockSpec((None, hb, seq // tq, tq), lambda b, g, i: (b, g, 0, 0)),
        ],
        out_specs=pl.BlockSpec((None, tq, w), lambda b, g, i: (b, i, g)),
        compiler_params=_params("parallel", "parallel", "arbitrary"),
        name="fox_attn",
    )(proj3, proj3, proj3, fb, frow4)
    return out.reshape(batch * seq, FOX_W)


def _diff_kernel(lamv_ref, g_ref, q_ref, k_ref, v_ref, o_ref, *, tq, hb, lam_init):
    i = pl.program_id(2)
    heads = range(hb)
    lv = lamv_ref[...]
    lam = (jnp.exp(jnp.sum(lv[0:1] * lv[1:2], axis=1, keepdims=True))
           - jnp.exp(jnp.sum(lv[2:3] * lv[3:4], axis=1, keepdims=True)) + lam_init)

    lane = lax.broadcasted_iota(jnp.int32, (tq, LANES), 1)
    qs = []
    for h in heads:
        q = q_ref[:, h * LANES:(h + 1) * LANES].astype(F32)
        qs.append(jnp.concatenate([jnp.where(lane < DIFF_QK_DIM, q, 0.0),
                                   jnp.where(lane >= DIFF_QK_DIM, q, 0.0)], axis=0).astype(BF16))
    zero = jnp.zeros((1, 2 * tq), F32)

    def logits(h, j):
        start = pl.multiple_of(j * tq, tq)
        st = _dot_nt(k_ref[pl.ds(start, tq), h * LANES:(h + 1) * LANES], qs[h])
        return st * LOG2E

    def values(h, j):
        return v_ref[pl.ds(pl.multiple_of(j * tq, tq), tq), h * LANES:(h + 1) * LANES]

    def body(j, carry):
        cur, state = carry
        nxt, out = [], []
        for h in heads:
            nxt.append(logits(h, j + 1))
            out.append(_flash_step(state[h], cur[h], values(h, j), zero))
        return tuple(nxt), tuple(out)

    cur, state = lax.fori_loop(
        0, i, body, (tuple(logits(h, 0) for h in heads),
                     tuple(_flash_init(DIFF_V_DIM, 2 * tq) for _ in heads)))
    krow = lax.broadcasted_iota(jnp.int32, (tq, 2 * tq), 0)
    qcol = lax.broadcasted_iota(jnp.int32, (tq, 2 * tq), 1)
    qcol = jnp.where(qcol >= tq, qcol - tq, qcol)
    visible = krow // CHUNK <= qcol // CHUNK
    for h in heads:
        a2 = jnp.where(visible, cur[h], NEG_INF)
        _, l, acc = _flash_step(state[h], a2, values(h, i), zero)
        o = acc / l
        yd = (o[:, :tq] - lam * o[:, tq:]).T
        o_ref[:, h * LANES:(h + 1) * LANES] = (
            _rms(yd, g_ref[...]) * (1.0 - lam_init)).astype(BF16)


def _diff_attn(proj, lamv, g, *, batch, seq, lam_init, tq=256, hb=4):
    proj3 = proj.reshape(batch, seq, PROJ_W)
    w = hb * LANES
    groups = DIFF_HEADS // hb
    dq, dk, dv = (DQ_BLK * LANES) // w, (DK_BLK * LANES) // w, (DV_BLK * LANES) // w
    out = pl.pallas_call(
        functools.partial(_diff_kernel, tq=tq, hb=hb, lam_init=lam_init),
        out_shape=jax.ShapeDtypeStruct((batch, seq, DIFF_V_W), BF16),
        grid=(batch, groups, seq // tq),
        in_specs=[
            pl.BlockSpec((4, DIFF_QK_DIM), lambda b, g, i: (0, 0)),
            pl.BlockSpec((1, DIFF_V_DIM), lambda b, g, i: (0, 0)),
            pl.BlockSpec((None, tq, w), lambda b, g, i: (b, i, dq + g)),
            pl.BlockSpec((None, seq, w), lambda b, g, i: (b, 0, dk + g)),
            pl.BlockSpec((None, seq, w), lambda b, g, i: (b, 0, dv + g)),
        ],
        out_specs=pl.BlockSpec((None, tq, w), lambda b, g, i: (b, i, g)),
        compiler_params=_params("parallel", "parallel", "arbitrary"),
        name="diff_attn",
    )(lamv, g, proj3, proj3, proj3)
    return out.reshape(batch * seq, DIFF_V_W)


def _mem_kv_kernel(mem_ref, g_ref, w_ref, o_ref):
    o_ref[...] = _dot(_rms(mem_ref[...], g_ref[...]).astype(BF16), w_ref[...]).astype(BF16)


def _mem_kv(mem, g, w):
    batch, n_mem, d = mem.shape
    return pl.pallas_call(
        _mem_kv_kernel,
        out_shape=jax.ShapeDtypeStruct((batch, n_mem, 2 * MEM_W), BF16),
        grid=(batch,),
        in_specs=[pl.BlockSpec((None, n_mem, d), lambda b: (b, 0, 0)),
                  pl.BlockSpec((1, d), lambda b: (0, 0)),
                  pl.BlockSpec((d, 2 * MEM_W), lambda b: (0, 0))],
        out_specs=pl.BlockSpec((None, n_mem, 2 * MEM_W), lambda b: (b, 0, 0)),
        compiler_params=_params("parallel"),
        name="mem_kv",
    )(mem, g, w)


def _mem_attn_kernel(q_ref, kv_ref, o_ref):
    scale = MEM_DIM ** -0.5
    for h in range(MEM_HEADS):
        q = q_ref[:, h * MEM_DIM:(h + 1) * MEM_DIM]
        k = kv_ref[:, h * MEM_DIM:(h + 1) * MEM_DIM]
        v = kv_ref[:, MEM_W + h * MEM_DIM:MEM_W + (h + 1) * MEM_DIM]
        s = _dot_nt(q, k) * scale
        p = jnp.exp(s - jnp.max(s, axis=1, keepdims=True))
        l = jnp.sum(p, axis=1, keepdims=True)
        o_ref[:, h * MEM_DIM:(h + 1) * MEM_DIM] = (_dot(p.astype(BF16), v) / l).astype(BF16)


def _mem_attn(proj, mkv, *, batch, seq, tq=512):
    proj3 = proj.reshape(batch, seq, PROJ_W)
    n_mem = mkv.shape[1]
    out = pl.pallas_call(
        _mem_attn_kernel,
        out_shape=jax.ShapeDtypeStruct((batch, seq, MEM_W), BF16),
        grid=(batch, seq // tq),
        in_specs=[pl.BlockSpec((None, tq, MEM_W), lambda b, i: (b, i, MQ_COL // MEM_W)),
                  pl.BlockSpec((None, n_mem, 2 * MEM_W), lambda b, i: (b, 0, 0))],
        out_specs=pl.BlockSpec((None, tq, MEM_W), lambda b, i: (b, i, 0)),
        compiler_params=_params("parallel", "arbitrary"),
        name="mem_attn",
    )(proj3, mkv)
    return out.reshape(batch * seq, MEM_W)


def _merge_kernel(x_ref, gpre_ref, yf_ref, yd_ref, ym_ref, wgf_ref, wgd_ref, wgm_ref,
                  bf_ref, bd_ref, bm_ref, wf_ref, wd_ref, wm_ref, wo_ref, gpost_ref,
                  o_ref, h_ref, acc_ref):
    n = pl.program_id(1)

    @pl.when(n == 0)
    def _():
        h_ref[...] = _rms(x_ref[...], gpre_ref[...]).astype(BF16)
        acc_ref[...] = jnp.zeros_like(acc_ref)

    h = h_ref[...]

    def gated(wg_ref, b_ref, y_ref, w_ref):
        z = _dot(h, wg_ref[...]) + b_ref[...]
        return (1.0 / (1.0 + jnp.exp(-z))) * _dot(y_ref[...], w_ref[...])

    merged = (gated(wgf_ref, bf_ref, yf_ref, wf_ref) + gated(wgd_ref, bd_ref, yd_ref, wd_ref)
              + gated(wgm_ref, bm_ref, ym_ref, wm_ref))
    acc_ref[...] += _dot(merged.astype(BF16), wo_ref[...])

    @pl.when(n == pl.num_programs(1) - 1)
    def _():
        o_ref[...] = x_ref[...] + _rms(acc_ref[...], gpost_ref[...])


def _merge(x, g_pre, y_fox, y_diff, y_mem, w_gate, b_gate, w_fox, w_diff, w_mem, w_out, g_post,
           *, tm=512, tn=512):
    t, d = x.shape
    nt = d // tn
    row = lambda m, n: (m, 0)
    return pl.pallas_call(
        _merge_kernel,
        out_shape=jax.ShapeDtypeStruct((t, d), F32),
        grid=(t // tm, nt),
        in_specs=[
            pl.BlockSpec((tm, d), row),
            pl.BlockSpec((1, d), lambda m, n: (0, 0)),
            pl.BlockSpec((tm, FOX_W), row),
            pl.BlockSpec((tm, DIFF_V_W), row),
            pl.BlockSpec((tm, MEM_W), row),
            pl.BlockSpec((d, tn), lambda m, n: (0, n)),
            pl.BlockSpec((d, tn), lambda m, n: (0, nt + n)),
            pl.BlockSpec((d, tn), lambda m, n: (0, 2 * nt + n)),
            pl.BlockSpec((1, tn), lambda m, n: (0, n)),
            pl.BlockSpec((1, tn), lambda m, n: (0, nt + n)),
            pl.BlockSpec((1, tn), lambda m, n: (0, 2 * nt + n)),
            pl.BlockSpec((FOX_W, tn), lambda m, n: (0, n)),
            pl.BlockSpec((DIFF_V_W, tn), lambda m, n: (0, n)),
            pl.BlockSpec((MEM_W, tn), lambda m, n: (0, n)),
            pl.BlockSpec((tn, d), lambda m, n: (n, 0)),
            pl.BlockSpec((1, d), lambda m, n: (0, 0)),
        ],
        out_specs=pl.BlockSpec((tm, d), row),
        scratch_shapes=[pltpu.VMEM((tm, d), BF16), pltpu.VMEM((tm, d), F32)],
        compiler_params=_params("parallel", "arbitrary"),
        name="merge",
    )(x, g_pre, y_fox, y_diff, y_mem, w_gate, w_gate, w_gate, b_gate, b_gate, b_gate,
      w_fox, w_diff, w_mem, w_out, g_post)


def kernel(x, mem, ffn1_pre_g, ffn1_w_gate, ffn1_w_up, ffn1_w_down, ffn1_post_g, mix_pre_g, w_in, fox_f_bias, diff_lambda_q1, diff_lambda_k1, diff_lambda_q2, diff_lambda_k2, diff_head_g, mem_norm_g, w_mem_kv, w_branch_fox, w_branch_diff, w_branch_mem, w_merge_gate, b_merge_gate, w_out, mix_post_g, ffn2_pre_g, ffn2_w_gate, ffn2_w_up, ffn2_w_down, ffn2_post_g):
    batch, seq, d = x.shape
    depth = w_in.shape[0]
    xt = x.reshape(batch * seq, d)
    cos_t, sa_t, sb_t = _rope_tables(seq)
    ff_lo = 3 * FOX_W
    ff_hi = ff_lo + FOX_HEADS

    def row(v):
        return v.reshape(1, -1).astype(F32)

    for l in range(depth):
        bf = lambda w: w[l].astype(BF16)
        xt = _ffn(xt, row(ffn1_pre_g[l]),
                  *_ffn_weights(ffn1_w_gate[l], ffn1_w_up[l], ffn1_w_down[l], FFN_TF),
                  row(ffn1_post_g[l]))

        w_main = jnp.concatenate([w_in[l][:, :ff_lo], w_in[l][:, ff_hi:]], axis=1).astype(BF16)
        w_ff = jnp.pad(w_in[l][:, ff_lo:ff_hi], ((0, 0), (0, LANES - FOX_HEADS))).astype(BF16)
        proj, ff = _mix_proj(xt, row(mix_pre_g[l]), w_main, w_ff, cos_t, sa_t, sb_t, seq=seq)

        bias = jnp.pad(fox_f_bias[l].astype(F32), (0, LANES - FOX_HEADS)).reshape(1, LANES)
        fb, frow = _fox_gate(ff, bias, batch=batch, seq=seq)
        y_fox = _fox_attn(proj, fb, frow, batch=batch, seq=seq)

        lam_init = 0.8 - 0.6 * math.exp(-0.3 * l)
        lamv = jnp.stack([diff_lambda_q1[l], diff_lambda_k1[l], diff_lambda_q2[l],
                          diff_lambda_k2[l]]).astype(F32)
        y_diff = _diff_attn(proj, lamv, row(diff_head_g[l]), batch=batch, seq=seq,
                            lam_init=lam_init)

        mkv = _mem_kv(mem, row(mem_norm_g[l]), bf(w_mem_kv))
        y_mem = _mem_attn(proj, mkv, batch=batch, seq=seq)

        xt = _merge(xt, row(mix_pre_g[l]), y_fox, y_diff, y_mem, bf(w_merge_gate),
                    row(b_merge_gate[l]), bf(w_branch_fox), bf(w_branch_diff), bf(w_branch_mem),
                    bf(w_out), row(mix_post_g[l]))

        xt = _ffn(xt, row(ffn2_pre_g[l]),
                  *_ffn_weights(ffn2_w_gate[l], ffn2_w_up[l], ffn2_w_down[l], FFN_TF),
                  row(ffn2_post_g[l]))

    return xt.reshape(batch, seq, d)
```

```python
import functools
import math

import jax
import jax.numpy as jnp
from jax import lax
from jax.experimental import pallas as pl
from jax.experimental.pallas import tpu as pltpu

D_MODEL = 2048
CHUNK = 64
EPS = 1e-6
ROPE_THETA = 500000.0

FOX_HEADS = 8
FOX_DIM = 128
FOX_W = FOX_HEADS * FOX_DIM

DIFF_HEADS = 4
DIFF_QK_DIM = 64
DIFF_V_DIM = 2 * DIFF_QK_DIM
DIFF_QK_W = DIFF_HEADS * 2 * DIFF_QK_DIM
DIFF_V_W = DIFF_HEADS * DIFF_V_DIM
ROPE_DIM = DIFF_QK_DIM // 4

MEM_HEADS = 4
MEM_DIM = 128
MEM_W = MEM_HEADS * MEM_DIM

LANES = 128
PROJ_W = 3 * FOX_W + 2 * DIFF_QK_W + DIFF_V_W + MEM_W
FQ_BLK, FK_BLK, FV_BLK = 0, FOX_HEADS, 2 * FOX_HEADS
DQ_BLK = 3 * FOX_HEADS
DK_BLK = DQ_BLK + DIFF_HEADS
DV_BLK = DK_BLK + DIFF_HEADS
MQ_COL = 3 * FOX_W + 2 * DIFF_QK_W + DIFF_V_W

VMEM_LIMIT = 56 * 1024 * 1024
BF16 = jnp.bfloat16
F32 = jnp.float32
NEG_INF = float("-inf")
LOG2E = math.log2(math.e)
FOX_QSCALE = FOX_DIM ** -0.5 * LOG2E
DIFF_QSCALE = DIFF_QK_DIM ** -0.5 * LOG2E


def _dot(a, b):
    return jnp.dot(a, b, preferred_element_type=F32)


def _dot_nt(a, b):
    return lax.dot_general(a, b, (((1,), (1,)), ((), ())), preferred_element_type=F32)


def _rms(x, g):
    return x * lax.rsqrt(jnp.mean(x * x, axis=-1, keepdims=True) + EPS) * g


def _params(*sem):
    return pltpu.CompilerParams(dimension_semantics=sem, vmem_limit_bytes=VMEM_LIMIT)


def _ffn_kernel(x_ref, gpre_ref, wg_ref, wu_ref, wd_ref, gpost_ref, o_ref, h_ref, acc_ref):
    f = pl.program_id(1)

    @pl.when(f == 0)
    def _():
        h_ref[...] = _rms(x_ref[...], gpre_ref[...]).astype(BF16)
        acc_ref[...] = jnp.zeros_like(acc_ref)

    h = h_ref[...]
    g = _dot(h, wg_ref[...])
    u = _dot(h, wu_ref[...])
    a = (g * (1.0 / (1.0 + jnp.exp(-g)))) * u
    acc_ref[...] += _dot(a.astype(BF16), wd_ref[...])

    @pl.when(f == pl.num_programs(1) - 1)
    def _():
        o_ref[...] = x_ref[...] + 0.5 * _rms(acc_ref[...], gpost_ref[...])


def _ffn(x, g_pre, w_gate, w_up, w_down, g_post, *, tm=512, tf=512):
    t, d = x.shape
    d_ff = w_gate.shape[1]
    return pl.pallas_call(
        _ffn_kernel,
        out_shape=jax.ShapeDtypeStruct((t, d), F32),
        grid=(t // tm, d_ff // tf),
        in_specs=[
            pl.BlockSpec((tm, d), lambda m, f: (m, 0)),
            pl.BlockSpec((1, d), lambda m, f: (0, 0)),
            pl.BlockSpec((d, tf), lambda m, f: (0, f)),
            pl.BlockSpec((d, tf), lambda m, f: (0, f)),
            pl.BlockSpec((tf, d), lambda m, f: (f, 0)),
            pl.BlockSpec((1, d), lambda m, f: (0, 0)),
        ],
        out_specs=pl.BlockSpec((tm, d), lambda m, f: (m, 0)),
        scratch_shapes=[pltpu.VMEM((tm, d), BF16), pltpu.VMEM((tm, d), F32)],
        compiler_params=_params("parallel", "arbitrary"),
        name="ffn",
    )(x, g_pre, w_gate, w_up, w_down, g_post)


def _proj_kernel(x_ref, g_ref, w_ref, wff_ref, cos_ref, sa_ref, sb_ref, o_ref, ff_ref, h_ref,
                 *, tn):
    h_ref[...] = _rms(x_ref[...], g_ref[...]).astype(BF16)
    ff_ref[...] = _dot(h_ref[...], wff_ref[...])
    half = ROPE_DIM // 2
    for n in range(w_ref.shape[1] // tn):
        y = _dot(h_ref[...], w_ref[:, n * tn:(n + 1) * tn])
        first_blk = n * tn // LANES
        if first_blk < FK_BLK:
            y = y * FOX_QSCALE
        if DQ_BLK <= first_blk < DV_BLK:
            qscale = DIFF_QSCALE if first_blk < DK_BLK else 1.0
            for j in range(tn // LANES):
                blk = y[:, j * LANES:(j + 1) * LANES]
                rot = (blk * cos_ref[...] + pltpu.roll(blk, LANES - half, axis=1) * sa_ref[...]
                       + pltpu.roll(blk, half, axis=1) * sb_ref[...])
                o_ref[:, n * tn + j * LANES:n * tn + (j + 1) * LANES] = (rot * qscale).astype(BF16)
        else:
            o_ref[:, n * tn:(n + 1) * tn] = y.astype(BF16)


def _mix_proj(x, g, w, wff, cos_t, sa_t, sb_t, *, seq, tm=512, tn=512):
    t, d = x.shape
    n_out = w.shape[1]
    assert all((blk * LANES) % tn == 0 for blk in (FK_BLK, DQ_BLK, DK_BLK, DV_BLK))
    s_tiles = seq // tm
    resident = dict(pipeline_mode=pl.Buffered(1))
    return pl.pallas_call(
        functools.partial(_proj_kernel, tn=tn),
        out_shape=(jax.ShapeDtypeStruct((t, n_out), BF16),
                   jax.ShapeDtypeStruct((t, LANES), F32)),
        grid=(t // tm,),
        in_specs=[
            pl.BlockSpec((tm, d), lambda m: (m, 0)),
            pl.BlockSpec((1, d), lambda m: (0, 0)),
            pl.BlockSpec((d, n_out), lambda m: (0, 0), **resident),
            pl.BlockSpec((d, LANES), lambda m: (0, 0), **resident),
            pl.BlockSpec((tm, LANES), lambda m: (m % s_tiles, 0)),
            pl.BlockSpec((tm, LANES), lambda m: (m % s_tiles, 0)),
            pl.BlockSpec((tm, LANES), lambda m: (m % s_tiles, 0)),
        ],
        out_specs=(pl.BlockSpec((tm, n_out), lambda m: (m, 0)),
                   pl.BlockSpec((tm, LANES), lambda m: (m, 0))),
        scratch_shapes=[pltpu.VMEM((tm, d), BF16)],
        compiler_params=_params("parallel"),
        name="mix_proj",
    )(x, g, w, wff, cos_t, sa_t, sb_t)


def _rope_tables(seq):
    half = ROPE_DIM // 2
    pos = jnp.arange(seq, dtype=F32)
    inv_freq = ROPE_THETA ** (-jnp.arange(0, ROPE_DIM, 2, dtype=F32) / ROPE_DIM)
    ang = pos[:, None] * inv_freq[None, :]
    cos, sin = jnp.cos(ang), jnp.sin(ang)
    ones = jnp.ones((seq, DIFF_QK_DIM - ROPE_DIM), F32)
    zeros_h = jnp.zeros((seq, half), F32)
    zeros_r = jnp.zeros((seq, DIFF_QK_DIM - ROPE_DIM), F32)
    cos_m = jnp.concatenate([cos, cos, ones], axis=1)
    sa_m = jnp.concatenate([-sin, zeros_h, zeros_r], axis=1)
    sb_m = jnp.concatenate([zeros_h, sin, zeros_r], axis=1)
    rep = LANES // DIFF_QK_DIM
    return (jnp.tile(cos_m, (1, rep)), jnp.tile(sa_m, (1, rep)), jnp.tile(sb_m, (1, rep)))


def _fgate_kernel(ff_ref, bias_ref, fb_ref, frow_ref, fcol_ref, *, cb):
    seq = ff_ref.shape[0]
    z = ff_ref[...] + bias_ref[...]
    lf = jnp.minimum(z, 0.0) - jnp.log1p(jnp.exp(-jnp.abs(z)))
    r = lax.broadcasted_iota(jnp.int32, (cb, cb), 0)
    c = lax.broadcasted_iota(jnp.int32, (cb, cb), 1)
    tri = (r >= c).astype(F32)
    carry = jnp.zeros((1, LANES), F32)
    for i in range(seq // cb):
        cs = jnp.dot(tri, lf[i * cb:(i + 1) * cb], precision=lax.Precision.HIGHEST,
                     preferred_element_type=F32) + carry
        carry = cs[cb - 1:cb, :]
        cs2 = cs * LOG2E
        fcol_ref[i * cb:(i + 1) * cb, :] = cs2
        for h in range(FOX_HEADS):
            fb_ref[h, i * cb:(i + 1) * cb, :] = jnp.broadcast_to(cs2[:, h:h + 1], (cb, LANES))
    frow_ref[...] = fcol_ref[...].T[:FOX_HEADS]


def _fox_gate(ff, bias, *, batch, seq, cb=256):
    ff = ff.reshape(batch, seq, LANES)
    return pl.pallas_call(
        functools.partial(_fgate_kernel, cb=cb),
        out_shape=(jax.ShapeDtypeStruct((batch, FOX_HEADS, seq, LANES), F32),
                   jax.ShapeDtypeStruct((batch, FOX_HEADS, seq), F32)),
        grid=(batch,),
        in_specs=[pl.BlockSpec((None, seq, LANES), lambda b: (b, 0, 0)),
                  pl.BlockSpec((1, LANES), lambda b: (0, 0))],
        out_specs=(pl.BlockSpec((None, FOX_HEADS, seq, LANES), lambda b: (b, 0, 0, 0)),
                   pl.BlockSpec((None, FOX_HEADS, seq), lambda b: (b, 0, 0))),
        scratch_shapes=[pltpu.VMEM((seq, LANES), F32)],
        compiler_params=_params("parallel"),
        name="fox_gate",
    )(ff, bias)


def _dot_tn(a, b):
    return lax.dot_general(a, b, (((0,), (0,)), ((), ())), preferred_element_type=F32)


def _flash_attend(i, heads, logits, values, ft2, visible, finish,
                  sa_ref, sb_ref, m_ref, l_ref, acc_ref):
    def step(h, s, j):
        m = m_ref[h]
        m_new = jnp.maximum(m, jnp.max(s, axis=0, keepdims=True) + ft2[h])
        alpha = jnp.exp2(m - m_new)
        p = jnp.exp2(s - (m_new - ft2[h]))
        m_ref[h] = m_new
        l_ref[h] = alpha * l_ref[h] + jnp.sum(p, axis=0, keepdims=True)
        acc_ref[h] = alpha * acc_ref[h] + _dot_tn(values(h, j), p.astype(BF16))

    for h in heads:
        m_ref[h] = jnp.full(m_ref.shape[1:], NEG_INF, F32)
        l_ref[h] = jnp.zeros(l_ref.shape[1:], F32)
        acc_ref[h] = jnp.zeros(acc_ref.shape[1:], F32)
        sa_ref[h] = logits(h, 0)

    def pair(jj, carry):
        j = 2 * jj
        for h in heads:
            sb_ref[h] = logits(h, j + 1)
            step(h, sa_ref[h], j)
        for h in heads:
            sa_ref[h] = logits(h, j + 2)
            step(h, sb_ref[h], j + 1)
        return carry

    lax.fori_loop(0, i // 2, pair, 0)

    def last(s_ref):
        for h in heads:
            step(h, jnp.where(visible, s_ref[h], NEG_INF), i)
            finish(h, acc_ref[h] / l_ref[h])

    @pl.when(i % 2 == 0)
    def _():
        last(sa_ref)

    @pl.when(i % 2 == 1)
    def _():
        for h in heads:
            sb_ref[h] = logits(h, i)
            step(h, sa_ref[h], i - 1)
        last(sb_ref)


def _flash_scratch(hb, width, tq, cols):
    return [pltpu.VMEM((hb, tq, cols), F32), pltpu.VMEM((hb, tq, cols), F32),
            pltpu.VMEM((hb, 1, cols), F32), pltpu.VMEM((hb, 1, cols), F32),
            pltpu.VMEM((hb, width, cols), F32)]


def _fox_kernel(q_ref, k_ref, v_ref, fb_ref, fr_ref, o_ref, *scratch, tq, hb):
    i = pl.program_id(2)
    heads = range(hb)
    ft2 = [fr_ref[h, pl.ds(i, 1), :] for h in heads]

    def logits(h, j):
        start = pl.multiple_of(j * tq, tq)
        hs = slice(h * FOX_DIM, (h + 1) * FOX_DIM)
        st = _dot_nt(k_ref[pl.ds(start, tq), hs], q_ref[:, hs])
        fb = fb_ref[h, pl.ds(start, tq), :]
        return st - jnp.concatenate([fb] * (tq // LANES), axis=1)

    def values(h, j):
        return v_ref[pl.ds(pl.multiple_of(j * tq, tq), tq), h * FOX_DIM:(h + 1) * FOX_DIM]

    def finish(h, o_t):
        o_ref[:, h * FOX_DIM:(h + 1) * FOX_DIM] = o_t.T.astype(BF16)

    krow = lax.broadcasted_iota(jnp.int32, (tq, tq), 0)
    qcol = lax.broadcasted_iota(jnp.int32, (tq, tq), 1)
    _flash_attend(i, heads, logits, values, ft2, krow <= qcol, finish, *scratch)


def _fox_attn(proj, fb, frow, *, batch, seq, tq=256, hb=4):
    proj3 = proj.reshape(batch, seq, PROJ_W)
    frow4 = frow.reshape(batch, FOX_HEADS, seq // tq, tq)
    w = hb * FOX_DIM
    groups = FOX_HEADS // hb
    out = pl.pallas_call(
        functools.partial(_fox_kernel, tq=tq, hb=hb),
        out_shape=jax.ShapeDtypeStruct((batch, seq, FOX_W), BF16),
        grid=(batch, groups, seq // tq),
        in_specs=[
            pl.BlockSpec((None, tq, w), lambda b, g, i: (b, i, g)),
            pl.BlockSpec((None, seq, w), lambda b, g, i: (b, 0, groups + g)),
            pl.BlockSpec((None, seq, w), lambda b, g, i: (b, 0, 2 * groups + g)),
            pl.BlockSpec((None, hb, seq, LANES), lambda b, g, i: (b, g, 0, 0)),
            pl.BlockSpec((None, hb, seq // tq, tq), lambda b, g, i: (b, g, 0, 0)),
        ],
        out_specs=pl.BlockSpec((None, tq, w), lambda b, g, i: (b, i, g)),
        scratch_shapes=_flash_scratch(hb, FOX_DIM, tq, tq),
        compiler_params=_params("parallel", "parallel", "arbitrary"),
        name="fox_attn",
    )(proj3, proj3, proj3, fb, frow4)
    return out.reshape(batch * seq, FOX_W)


def _diff_kernel(lamv_ref, g_ref, q_ref, k_ref, v_ref, o_ref, *scratch, tq, hb, lam_init):
    i = pl.program_id(2)
    heads = range(hb)
    lv = lamv_ref[...]
    lam = (jnp.exp(jnp.sum(lv[0:1] * lv[1:2], axis=1, keepdims=True))
           - jnp.exp(jnp.sum(lv[2:3] * lv[3:4], axis=1, keepdims=True)) + lam_init)

    lane = lax.broadcasted_iota(jnp.int32, (tq, LANES), 1)
    qs = []
    for h in heads:
        q = q_ref[:, h * LANES:(h + 1) * LANES].astype(F32)
        qs.append(jnp.concatenate([jnp.where(lane < DIFF_QK_DIM, q, 0.0),
                                   jnp.where(lane >= DIFF_QK_DIM, q, 0.0)], axis=0).astype(BF16))
    zero = jnp.zeros((1, 2 * tq), F32)

    def logits(h, j):
        start = pl.multiple_of(j * tq, tq)
        return _dot_nt(k_ref[pl.ds(start, tq), h * LANES:(h + 1) * LANES], qs[h])

    def values(h, j):
        return v_ref[pl.ds(pl.multiple_of(j * tq, tq), tq), h * LANES:(h + 1) * LANES]

    def finish(h, o_t):
        yd = (o_t[:, :tq] - lam * o_t[:, tq:]).T
        o_ref[:, h * LANES:(h + 1) * LANES] = (
            _rms(yd, g_ref[...]) * (1.0 - lam_init)).astype(BF16)

    krow = lax.broadcasted_iota(jnp.int32, (tq, 2 * tq), 0)
    qcol = lax.broadcasted_iota(jnp.int32, (tq, 2 * tq), 1)
    qcol = jnp.where(qcol >= tq, qcol - tq, qcol)
    visible = krow // CHUNK <= qcol // CHUNK
    _flash_attend(i, heads, logits, values, [zero] * hb, visible, finish, *scratch)


def _diff_attn(proj, lamv, g, *, batch, seq, lam_init, tq=256, hb=4):
    proj3 = proj.reshape(batch, seq, PROJ_W)
    w = hb * LANES
    groups = DIFF_HEADS // hb
    dq, dk, dv = (DQ_BLK * LANES) // w, (DK_BLK * LANES) // w, (DV_BLK * LANES) // w
    out = pl.pallas_call(
        functools.partial(_diff_kernel, tq=tq, hb=hb, lam_init=lam_init),
        out_shape=jax.ShapeDtypeStruct((batch, seq, DIFF_V_W), BF16),
        grid=(batch, groups, seq // tq),
        in_specs=[
            pl.BlockSpec((4, DIFF_QK_DIM), lambda b, g, i: (0, 0)),
            pl.BlockSpec((1, DIFF_V_DIM), lambda b, g, i: (0, 0)),
            pl.BlockSpec((None, tq, w), lambda b, g, i: (b, i, dq + g)),
            pl.BlockSpec((None, seq, w), lambda b, g, i: (b, 0, dk + g)),
            pl.BlockSpec((None, seq, w), lambda b, g, i: (b, 0, dv + g)),
        ],
        out_specs=pl.BlockSpec((None, tq, w), lambda b, g, i: (b, i, g)),
        scratch_shapes=_flash_scratch(hb, DIFF_V_DIM, tq, 2 * tq),
        compiler_params=_params("parallel", "parallel", "arbitrary"),
        name="diff_attn",
    )(lamv, g, proj3, proj3, proj3)
    return out.reshape(batch * seq, DIFF_V_W)


def _mem_kv_kernel(mem_ref, g_ref, w_ref, o_ref):
    o_ref[...] = _dot(_rms(mem_ref[...], g_ref[...]).astype(BF16), w_ref[...]).astype(BF16)


def _mem_kv(mem, g, w):
    batch, n_mem, d = mem.shape
    return pl.pallas_call(
        _mem_kv_kernel,
        out_shape=jax.ShapeDtypeStruct((batch, n_mem, 2 * MEM_W), BF16),
        grid=(batch,),
        in_specs=[pl.BlockSpec((None, n_mem, d), lambda b: (b, 0, 0)),
                  pl.BlockSpec((1, d), lambda b: (0, 0)),
                  pl.BlockSpec((d, 2 * MEM_W), lambda b: (0, 0))],
        out_specs=pl.BlockSpec((None, n_mem, 2 * MEM_W), lambda b: (b, 0, 0)),
        compiler_params=_params("parallel"),
        name="mem_kv",
    )(mem, g, w)


def _mem_attn_kernel(q_ref, kv_ref, o_ref):
    scale = MEM_DIM ** -0.5
    for h in range(MEM_HEADS):
        q = q_ref[:, h * MEM_DIM:(h + 1) * MEM_DIM]
        k = kv_ref[:, h * MEM_DIM:(h + 1) * MEM_DIM]
        v = kv_ref[:, MEM_W + h * MEM_DIM:MEM_W + (h + 1) * MEM_DIM]
        s = _dot_nt(q, k) * scale
        p = jnp.exp(s - jnp.max(s, axis=1, keepdims=True))
        l = jnp.sum(p, axis=1, keepdims=True)
        o_ref[:, h * MEM_DIM:(h + 1) * MEM_DIM] = (_dot(p.astype(BF16), v) / l).astype(BF16)


def _mem_attn(proj, mkv, *, batch, seq, tq=512):
    proj3 = proj.reshape(batch, seq, PROJ_W)
    n_mem = mkv.shape[1]
    out = pl.pallas_call(
        _mem_attn_kernel,
        out_shape=jax.ShapeDtypeStruct((batch, seq, MEM_W), BF16),
        grid=(batch, seq // tq),
        in_specs=[pl.BlockSpec((None, tq, MEM_W), lambda b, i: (b, i, MQ_COL // MEM_W)),
                  pl.BlockSpec((None, n_mem, 2 * MEM_W), lambda b, i: (b, 0, 0))],
        out_specs=pl.BlockSpec((None, tq, MEM_W), lambda b, i: (b, i, 0)),
        compiler_params=_params("parallel", "arbitrary"),
        name="mem_attn",
    )(proj3, mkv)
    return out.reshape(batch * seq, MEM_W)


def _merge_kernel(x_ref, gpre_ref, yf_ref, yd_ref, ym_ref, wgf_ref, wgd_ref, wgm_ref,
                  bf_ref, bd_ref, bm_ref, wf_ref, wd_ref, wm_ref, wo_ref, gpost_ref,
                  o_ref, h_ref, acc_ref):
    n = pl.program_id(1)

    @pl.when(n == 0)
    def _():
        h_ref[...] = _rms(x_ref[...], gpre_ref[...]).astype(BF16)
        acc_ref[...] = jnp.zeros_like(acc_ref)

    h = h_ref[...]

    def gated(wg_ref, b_ref, y_ref, w_ref):
        z = _dot(h, wg_ref[...]) + b_ref[...]
        return (1.0 / (1.0 + jnp.exp(-z))) * _dot(y_ref[...], w_ref[...])

    merged = (gated(wgf_ref, bf_ref, yf_ref, wf_ref) + gated(wgd_ref, bd_ref, yd_ref, wd_ref)
              + gated(wgm_ref, bm_ref, ym_ref, wm_ref))
    acc_ref[...] += _dot(merged.astype(BF16), wo_ref[...])

    @pl.when(n == pl.num_programs(1) - 1)
    def _():
        o_ref[...] = x_ref[...] + _rms(acc_ref[...], gpost_ref[...])


def _merge(x, g_pre, y_fox, y_diff, y_mem, w_gate, b_gate, w_fox, w_diff, w_mem, w_out, g_post,
           *, tm=512, tn=512):
    t, d = x.shape
    nt = d // tn
    row = lambda m, n: (m, 0)
    return pl.pallas_call(
        _merge_kernel,
        out_shape=jax.ShapeDtypeStruct((t, d), F32),
        grid=(t // tm, nt),
        in_specs=[
            pl.BlockSpec((tm, d), row),
            pl.BlockSpec((1, d), lambda m, n: (0, 0)),
            pl.BlockSpec((tm, FOX_W), row),
            pl.BlockSpec((tm, DIFF_V_W), row),
            pl.BlockSpec((tm, MEM_W), row),
            pl.BlockSpec((d, tn), lambda m, n: (0, n)),
            pl.BlockSpec((d, tn), lambda m, n: (0, nt + n)),
            pl.BlockSpec((d, tn), lambda m, n: (0, 2 * nt + n)),
            pl.BlockSpec((1, tn), lambda m, n: (0, n)),
            pl.BlockSpec((1, tn), lambda m, n: (0, nt + n)),
            pl.BlockSpec((1, tn), lambda m, n: (0, 2 * nt + n)),
            pl.BlockSpec((FOX_W, tn), lambda m, n: (0, n)),
            pl.BlockSpec((DIFF_V_W, tn), lambda m, n: (0, n)),
            pl.BlockSpec((MEM_W, tn), lambda m, n: (0, n)),
            pl.BlockSpec((tn, d), lambda m, n: (n, 0)),
            pl.BlockSpec((1, d), lambda m, n: (0, 0)),
        ],
        out_specs=pl.BlockSpec((tm, d), row),
        scratch_shapes=[pltpu.VMEM((tm, d), BF16), pltpu.VMEM((tm, d), F32)],
        compiler_params=_params("parallel", "arbitrary"),
        name="merge",
    )(x, g_pre, y_fox, y_diff, y_mem, w_gate, w_gate, w_gate, b_gate, b_gate, b_gate,
      w_fox, w_diff, w_mem, w_out, g_post)


def kernel(x, mem, ffn1_pre_g, ffn1_w_gate, ffn1_w_up, ffn1_w_down, ffn1_post_g, mix_pre_g, w_in, fox_f_bias, diff_lambda_q1, diff_lambda_k1, diff_lambda_q2, diff_lambda_k2, diff_head_g, mem_norm_g, w_mem_kv, w_branch_fox, w_branch_diff, w_branch_mem, w_merge_gate, b_merge_gate, w_out, mix_post_g, ffn2_pre_g, ffn2_w_gate, ffn2_w_up, ffn2_w_down, ffn2_post_g):
    batch, seq, d = x.shape
    depth = w_in.shape[0]
    xt = x.reshape(batch * seq, d)
    cos_t, sa_t, sb_t = _rope_tables(seq)
    ff_lo = 3 * FOX_W
    ff_hi = ff_lo + FOX_HEADS

    def row(v):
        return v.reshape(1, -1).astype(F32)

    for l in range(depth):
        bf = lambda w: w[l].astype(BF16)
        xt = _ffn(xt, row(ffn1_pre_g[l]), bf(ffn1_w_gate), bf(ffn1_w_up), bf(ffn1_w_down),
                  row(ffn1_post_g[l]))

        w_main = jnp.concatenate([w_in[l][:, :ff_lo], w_in[l][:, ff_hi:]], axis=1).astype(BF16)
        w_ff = jnp.pad(w_in[l][:, ff_lo:ff_hi], ((0, 0), (0, LANES - FOX_HEADS))).astype(BF16)
        proj, ff = _mix_proj(xt, row(mix_pre_g[l]), w_main, w_ff, cos_t, sa_t, sb_t, seq=seq)

        bias = jnp.pad(fox_f_bias[l].astype(F32), (0, LANES - FOX_HEADS)).reshape(1, LANES)
        fb, frow = _fox_gate(ff, bias, batch=batch, seq=seq)
        y_fox = _fox_attn(proj, fb, frow, batch=batch, seq=seq)

        lam_init = 0.8 - 0.6 * math.exp(-0.3 * l)
        lamv = jnp.stack([diff_lambda_q1[l], diff_lambda_k1[l], diff_lambda_q2[l],
                          diff_lambda_k2[l]]).astype(F32)
        y_diff = _diff_attn(proj, lamv, row(diff_head_g[l]), batch=batch, seq=seq,
                            lam_init=lam_init)

        mkv = _mem_kv(mem, row(mem_norm_g[l]), bf(w_mem_kv))
        y_mem = _mem_attn(proj, mkv, batch=batch, seq=seq)

        xt = _merge(xt, row(mix_pre_g[l]), y_fox, y_diff, y_mem, bf(w_merge_gate),
                    row(b_merge_gate[l]), bf(w_branch_fox), bf(w_branch_diff), bf(w_branch_mem),
                    bf(w_out), row(mix_post_g[l]))

        xt = _ffn(xt, row(ffn2_pre_g[l]), bf(ffn2_w_gate), bf(ffn2_w_up), bf(ffn2_w_down),
                  row(ffn2_post_g[l]))

    return xt.reshape(batch, seq, d)
```

```python
import functools
import math

import jax
import jax.numpy as jnp
from jax import lax
from jax.experimental import pallas as pl
from jax.experimental.pallas import tpu as pltpu

D_MODEL = 2048
CHUNK = 64
EPS = 1e-6
ROPE_THETA = 500000.0

FOX_HEADS = 8
FOX_DIM = 128
FOX_W = FOX_HEADS * FOX_DIM

DIFF_HEADS = 4
DIFF_QK_DIM = 64
DIFF_V_DIM = 2 * DIFF_QK_DIM
DIFF_QK_W = DIFF_HEADS * 2 * DIFF_QK_DIM
DIFF_V_W = DIFF_HEADS * DIFF_V_DIM
ROPE_DIM = DIFF_QK_DIM // 4

MEM_HEADS = 4
MEM_DIM = 128
MEM_W = MEM_HEADS * MEM_DIM

LANES = 128
F32_ROWS = 8
BF16_ROWS = 16
RMS_ROWS_IN_FLIGHT = 128
PROJ_W = 3 * FOX_W + 2 * DIFF_QK_W + DIFF_V_W + MEM_W
FQ_BLK, FK_BLK, FV_BLK = 0, FOX_HEADS, 2 * FOX_HEADS
DQ_BLK = 3 * FOX_HEADS
DK_BLK = DQ_BLK + DIFF_HEADS
DV_BLK = DK_BLK + DIFF_HEADS
MQ_COL = 3 * FOX_W + 2 * DIFF_QK_W + DIFF_V_W
FF_COL = 3 * FOX_W

VMEM_LIMIT = 56 * 1024 * 1024
BF16 = jnp.bfloat16
F32 = jnp.float32
NEG_INF = float("-inf")
LOG2E = math.log2(math.e)
FOX_QSCALE = FOX_DIM ** -0.5 * LOG2E
DIFF_QSCALE = DIFF_QK_DIM ** -0.5 * LOG2E


def _dot(a, b):
    return jnp.dot(a, b, preferred_element_type=F32)


def _dot_nt(a, b):
    return lax.dot_general(a, b, (((1,), (1,)), ((), ())), preferred_element_type=F32)


def _rms(x, g):
    return x * lax.rsqrt(jnp.mean(x * x, axis=-1, keepdims=True) + EPS) * g


def _rms_rows(src_ref, g, emit, rows):
    def body(r, carry):
        sl = pl.ds(pl.multiple_of(r * rows, rows), rows)
        emit(sl, _rms(src_ref[sl, :], g))
        return carry

    lax.fori_loop(0, src_ref.shape[0] // rows, body, 0, unroll=RMS_ROWS_IN_FLIGHT // rows)


def _params(*sem):
    return pltpu.CompilerParams(dimension_semantics=sem, vmem_limit_bytes=VMEM_LIMIT)


def _ffn_kernel(x_ref, gpre_ref, wg_ref, wu_ref, wd_ref, gpost_ref, o_ref, h_ref, acc_ref):
    f = pl.program_id(1)

    @pl.when(f == 0)
    def _():
        def emit(sl, y):
            h_ref[sl, :] = y.astype(BF16)

        _rms_rows(x_ref, gpre_ref[...], emit, BF16_ROWS)
        acc_ref[...] = jnp.zeros_like(acc_ref)

    h = h_ref[...]
    g = _dot(h, wg_ref[...])
    u = _dot(h, wu_ref[...])
    a = (g * (1.0 / (1.0 + jnp.exp(-g)))) * u
    acc_ref[...] += _dot(a.astype(BF16), wd_ref[...])

    @pl.when(f == pl.num_programs(1) - 1)
    def _():
        def emit(sl, y):
            o_ref[sl, :] = x_ref[sl, :] + y

        _rms_rows(acc_ref, 0.5 * gpost_ref[...], emit, F32_ROWS)


def _ffn(x, g_pre, w_gate, w_up, w_down, g_post, *, tm=512, tf=512):
    t, d = x.shape
    d_ff = w_gate.shape[1]
    return pl.pallas_call(
        _ffn_kernel,
        out_shape=jax.ShapeDtypeStruct((t, d), F32),
        grid=(t // tm, d_ff // tf),
        in_specs=[
            pl.BlockSpec((tm, d), lambda m, f: (m, 0)),
            pl.BlockSpec((1, d), lambda m, f: (0, 0)),
            pl.BlockSpec((d, tf), lambda m, f: (0, f)),
            pl.BlockSpec((d, tf), lambda m, f: (0, f)),
            pl.BlockSpec((tf, d), lambda m, f: (f, 0)),
            pl.BlockSpec((1, d), lambda m, f: (0, 0)),
        ],
        out_specs=pl.BlockSpec((tm, d), lambda m, f: (m, 0)),
        scratch_shapes=[pltpu.VMEM((tm, d), BF16), pltpu.VMEM((tm, d), F32)],
        compiler_params=_params("parallel", "arbitrary"),
        name="ffn",
    )(x, g_pre, w_gate, w_up, w_down, g_post)


def _proj_kernel(x_ref, g_ref, wa_ref, wff_ref, wb_ref, cos_ref, sa_ref, sb_ref, o_ref, ff_ref,
                 h_ref, *, tn):
    h_ref[...] = _rms(x_ref[...], g_ref[...]).astype(BF16)
    ff_ref[...] = _dot(h_ref[...], wff_ref[...])
    half = ROPE_DIM // 2
    na = wa_ref.shape[1] // tn
    for n in range(na + wb_ref.shape[1] // tn):
        w_tile = (wa_ref[:, n * tn:(n + 1) * tn] if n < na
                  else wb_ref[:, (n - na) * tn:(n - na + 1) * tn])
        y = _dot(h_ref[...], w_tile)
        first_blk = n * tn // LANES
        if first_blk < FK_BLK:
            y = y * FOX_QSCALE
        if DQ_BLK <= first_blk < DV_BLK:
            qscale = DIFF_QSCALE if first_blk < DK_BLK else 1.0
            for j in range(tn // LANES):
                blk = y[:, j * LANES:(j + 1) * LANES]
                rot = (blk * cos_ref[...] + pltpu.roll(blk, LANES - half, axis=1) * sa_ref[...]
                       + pltpu.roll(blk, half, axis=1) * sb_ref[...])
                o_ref[:, n * tn + j * LANES:n * tn + (j + 1) * LANES] = (rot * qscale).astype(BF16)
        else:
            o_ref[:, n * tn:(n + 1) * tn] = y.astype(BF16)


def _mix_proj(x, g, w_head, w_tail, cos_t, sa_t, sb_t, *, seq, tm=512, tn=512):
    t, d = x.shape
    n_out = FF_COL + w_tail.shape[1]
    assert all((blk * LANES) % tn == 0 for blk in (FK_BLK, DQ_BLK, DK_BLK, DV_BLK))
    assert w_head.shape[1] == FF_COL + LANES and n_out == PROJ_W
    s_tiles = seq // tm
    resident = dict(pipeline_mode=pl.Buffered(1))
    return pl.pallas_call(
        functools.partial(_proj_kernel, tn=tn),
        out_shape=(jax.ShapeDtypeStruct((t, n_out), BF16),
                   jax.ShapeDtypeStruct((t, LANES), F32)),
        grid=(t // tm,),
        in_specs=[
            pl.BlockSpec((tm, d), lambda m: (m, 0)),
            pl.BlockSpec((1, d), lambda m: (0, 0)),
            pl.BlockSpec((d, FF_COL), lambda m: (0, 0), **resident),
            pl.BlockSpec((d, LANES), lambda m: (0, FF_COL // LANES), **resident),
            pl.BlockSpec((d, w_tail.shape[1]), lambda m: (0, 0), **resident),
            pl.BlockSpec((tm, LANES), lambda m: (m % s_tiles, 0)),
            pl.BlockSpec((tm, LANES), lambda m: (m % s_tiles, 0)),
            pl.BlockSpec((tm, LANES), lambda m: (m % s_tiles, 0)),
        ],
        out_specs=(pl.BlockSpec((tm, n_out), lambda m: (m, 0)),
                   pl.BlockSpec((tm, LANES), lambda m: (m, 0))),
        scratch_shapes=[pltpu.VMEM((tm, d), BF16)],
        compiler_params=_params("parallel"),
        name="mix_proj",
    )(x, g, w_head, w_head, w_tail, cos_t, sa_t, sb_t)


def _rope_tables(seq):
    half = ROPE_DIM // 2
    pos = jnp.arange(seq, dtype=F32)
    inv_freq = ROPE_THETA ** (-jnp.arange(0, ROPE_DIM, 2, dtype=F32) / ROPE_DIM)
    ang = pos[:, None] * inv_freq[None, :]
    cos, sin = jnp.cos(ang), jnp.sin(ang)
    ones = jnp.ones((seq, DIFF_QK_DIM - ROPE_DIM), F32)
    zeros_h = jnp.zeros((seq, half), F32)
    zeros_r = jnp.zeros((seq, DIFF_QK_DIM - ROPE_DIM), F32)
    cos_m = jnp.concatenate([cos, cos, ones], axis=1)
    sa_m = jnp.concatenate([-sin, zeros_h, zeros_r], axis=1)
    sb_m = jnp.concatenate([zeros_h, sin, zeros_r], axis=1)
    rep = LANES // DIFF_QK_DIM
    return (jnp.tile(cos_m, (1, rep)), jnp.tile(sa_m, (1, rep)), jnp.tile(sb_m, (1, rep)))


def _fgate_kernel(ff_ref, bias_ref, fb_ref, frow_ref, fcol_ref, *, cb):
    seq = ff_ref.shape[0]
    z = ff_ref[...] + bias_ref[...]
    lf = jnp.minimum(z, 0.0) - jnp.log1p(jnp.exp(-jnp.abs(z)))
    r = lax.broadcasted_iota(jnp.int32, (cb, cb), 0)
    c = lax.broadcasted_iota(jnp.int32, (cb, cb), 1)
    tri = (r >= c).astype(F32)
    carry = jnp.zeros((1, LANES), F32)
    for i in range(seq // cb):
        cs = jnp.dot(tri, lf[i * cb:(i + 1) * cb], precision=lax.Precision.HIGHEST,
                     preferred_element_type=F32) + carry
        carry = cs[cb - 1:cb, :]
        cs2 = cs * LOG2E
        fcol_ref[i * cb:(i + 1) * cb, :] = cs2
        for h in range(FOX_HEADS):
            fb_ref[h, i * cb:(i + 1) * cb, :] = jnp.broadcast_to(cs2[:, h:h + 1], (cb, LANES))
    frow_ref[...] = fcol_ref[...].T[:FOX_HEADS]


def _fox_gate(ff, bias, *, batch, seq, cb=256):
    ff = ff.reshape(batch, seq, LANES)
    return pl.pallas_call(
        functools.partial(_fgate_kernel, cb=cb),
        out_shape=(jax.ShapeDtypeStruct((batch, FOX_HEADS, seq, LANES), F32),
                   jax.ShapeDtypeStruct((batch, FOX_HEADS, seq), F32)),
        grid=(batch,),
        in_specs=[pl.BlockSpec((None, seq, LANES), lambda b: (b, 0, 0)),
                  pl.BlockSpec((1, LANES), lambda b: (0, 0))],
        out_specs=(pl.BlockSpec((None, FOX_HEADS, seq, LANES), lambda b: (b, 0, 0, 0)),
                   pl.BlockSpec((None, FOX_HEADS, seq), lambda b: (b, 0, 0))),
        scratch_shapes=[pltpu.VMEM((seq, LANES), F32)],
        compiler_params=_params("parallel"),
        name="fox_gate",
    )(ff, bias)


def _dot_tn(a, b):
    return lax.dot_general(a, b, (((0,), (0,)), ((), ())), preferred_element_type=F32)


def _flash_attend(i, heads, logits, values, ft2, visible, finish,
                  sa_ref, sb_ref, m_ref, l_ref, acc_ref):
    def step(h, s, j):
        m = m_ref[h]
        m_new = jnp.maximum(m, jnp.max(s, axis=0, keepdims=True) + ft2[h])
        alpha = jnp.exp2(m - m_new)
        p = jnp.exp2(s - (m_new - ft2[h]))
        m_ref[h] = m_new
        l_ref[h] = alpha * l_ref[h] + jnp.sum(p, axis=0, keepdims=True)
        acc_ref[h] = alpha * acc_ref[h] + _dot_tn(values(h, j), p.astype(BF16))

    for h in heads:
        m_ref[h] = jnp.full(m_ref.shape[1:], NEG_INF, F32)
        l_ref[h] = jnp.zeros(l_ref.shape[1:], F32)
        acc_ref[h] = jnp.zeros(acc_ref.shape[1:], F32)
        sa_ref[h] = logits(h, 0)

    def pair(jj, carry):
        j = 2 * jj
        for h in heads:
            sb_ref[h] = logits(h, j + 1)
            step(h, sa_ref[h], j)
        for h in heads:
            sa_ref[h] = logits(h, j + 2)
            step(h, sb_ref[h], j + 1)
        return carry

    lax.fori_loop(0, i // 2, pair, 0)

    def last(s_ref):
        for h in heads:
            step(h, jnp.where(visible, s_ref[h], NEG_INF), i)
            finish(h, acc_ref[h] / l_ref[h])

    @pl.when(i % 2 == 0)
    def _():
        last(sa_ref)

    @pl.when(i % 2 == 1)
    def _():
        for h in heads:
            sb_ref[h] = logits(h, i)
            step(h, sa_ref[h], i - 1)
        last(sb_ref)


def _flash_scratch(hb, width, tq, cols):
    return [pltpu.VMEM((hb, tq, cols), F32), pltpu.VMEM((hb, tq, cols), F32),
            pltpu.VMEM((hb, 1, cols), F32), pltpu.VMEM((hb, 1, cols), F32),
            pltpu.VMEM((hb, width, cols), F32)]


def _fox_kernel(q_ref, k_ref, v_ref, fb_ref, fr_ref, o_ref, *scratch, tq, hb):
    i = pl.program_id(2)
    heads = range(hb)
    ft2 = [fr_ref[h, pl.ds(i, 1), :] for h in heads]

    def logits(h, j):
        start = pl.multiple_of(j * tq, tq)
        hs = slice(h * FOX_DIM, (h + 1) * FOX_DIM)
        st = _dot_nt(k_ref[pl.ds(start, tq), hs], q_ref[:, hs])
        fb = fb_ref[h, pl.ds(start, tq), :]
        return st - jnp.concatenate([fb] * (tq // LANES), axis=1)

    def values(h, j):
        return v_ref[pl.ds(pl.multiple_of(j * tq, tq), tq), h * FOX_DIM:(h + 1) * FOX_DIM]

    def finish(h, o_t):
        o_ref[:, h * FOX_DIM:(h + 1) * FOX_DIM] = o_t.T.astype(BF16)

    krow = lax.broadcasted_iota(jnp.int32, (tq, tq), 0)
    qcol = lax.broadcasted_iota(jnp.int32, (tq, tq), 1)
    _flash_attend(i, heads, logits, values, ft2, krow <= qcol, finish, *scratch)


def _fox_attn(proj, fb, frow, *, batch, seq, tq=256, hb=8):
    proj3 = proj.reshape(batch, seq, PROJ_W)
    frow4 = frow.reshape(batch, FOX_HEADS, seq // tq, tq)
    w = hb * FOX_DIM
    groups = FOX_HEADS // hb
    out = pl.pallas_call(
        functools.partial(_fox_kernel, tq=tq, hb=hb),
        out_shape=jax.ShapeDtypeStruct((batch, seq, FOX_W), BF16),
        grid=(batch, groups, seq // tq),
        in_specs=[
            pl.BlockSpec((None, tq, w), lambda b, g, i: (b, i, g)),
            pl.BlockSpec((None, seq, w), lambda b, g, i: (b, 0, groups + g)),
            pl.BlockSpec((None, seq, w), lambda b, g, i: (b, 0, 2 * groups + g)),
            pl.BlockSpec((None, hb, seq, LANES), lambda b, g, i: (b, g, 0, 0)),
            pl.BlockSpec((None, hb, seq // tq, tq), lambda b, g, i: (b, g, 0, 0)),
        ],
        out_specs=pl.BlockSpec((None, tq, w), lambda b, g, i: (b, i, g)),
        scratch_shapes=_flash_scratch(hb, FOX_DIM, tq, tq),
        compiler_params=_params("parallel", "parallel", "arbitrary"),
        name="fox_attn",
    )(proj3, proj3, proj3, fb, frow4)
    return out.reshape(batch * seq, FOX_W)


def _diff_kernel(lamv_ref, g_ref, q_ref, k_ref, v_ref, o_ref, *scratch, tq, hb, lam_init):
    i = pl.program_id(2)
    heads = range(hb)
    lv = lamv_ref[...]
    lam = (jnp.exp(jnp.sum(lv[0:1] * lv[1:2], axis=1, keepdims=True))
           - jnp.exp(jnp.sum(lv[2:3] * lv[3:4], axis=1, keepdims=True)) + lam_init)

    lane = lax.broadcasted_iota(jnp.int32, (tq, LANES), 1)
    qs = []
    for h in heads:
        q = q_ref[:, h * LANES:(h + 1) * LANES].astype(F32)
        qs.append(jnp.concatenate([jnp.where(lane < DIFF_QK_DIM, q, 0.0),
                                   jnp.where(lane >= DIFF_QK_DIM, q, 0.0)], axis=0).astype(BF16))
    zero = jnp.zeros((1, 2 * tq), F32)

    def logits(h, j):
        start = pl.multiple_of(j * tq, tq)
        return _dot_nt(k_ref[pl.ds(start, tq), h * LANES:(h + 1) * LANES], qs[h])

    def values(h, j):
        return v_ref[pl.ds(pl.multiple_of(j * tq, tq), tq), h * LANES:(h + 1) * LANES]

    def finish(h, o_t):
        yd = (o_t[:, :tq] - lam * o_t[:, tq:]).T
        o_ref[:, h * LANES:(h + 1) * LANES] = (
            _rms(yd, g_ref[...]) * (1.0 - lam_init)).astype(BF16)

    krow = lax.broadcasted_iota(jnp.int32, (tq, 2 * tq), 0)
    qcol = lax.broadcasted_iota(jnp.int32, (tq, 2 * tq), 1)
    qcol = jnp.where(qcol >= tq, qcol - tq, qcol)
    visible = krow // CHUNK <= qcol // CHUNK
    _flash_attend(i, heads, logits, values, [zero] * hb, visible, finish, *scratch)


def _diff_attn(proj, lamv, g, *, batch, seq, lam_init, tq=256, hb=4):
    proj3 = proj.reshape(batch, seq, PROJ_W)
    w = hb * LANES
    groups = DIFF_HEADS // hb
    dq, dk, dv = (DQ_BLK * LANES) // w, (DK_BLK * LANES) // w, (DV_BLK * LANES) // w
    out = pl.pallas_call(
        functools.partial(_diff_kernel, tq=tq, hb=hb, lam_init=lam_init),
        out_shape=jax.ShapeDtypeStruct((batch, seq, DIFF_V_W), BF16),
        grid=(batch, groups, seq // tq),
        in_specs=[
            pl.BlockSpec((4, DIFF_QK_DIM), lambda b, g, i: (0, 0)),
            pl.BlockSpec((1, DIFF_V_DIM), lambda b, g, i: (0, 0)),
            pl.BlockSpec((None, tq, w), lambda b, g, i: (b, i, dq + g)),
            pl.BlockSpec((None, seq, w), lambda b, g, i: (b, 0, dk + g)),
            pl.BlockSpec((None, seq, w), lambda b, g, i: (b, 0, dv + g)),
        ],
        out_specs=pl.BlockSpec((None, tq, w), lambda b, g, i: (b, i, g)),
        scratch_shapes=_flash_scratch(hb, DIFF_V_DIM, tq, 2 * tq),
        compiler_params=_params("parallel", "parallel", "arbitrary"),
        name="diff_attn",
    )(lamv, g, proj3, proj3, proj3)
    return out.reshape(batch * seq, DIFF_V_W)


def _mem_kv_kernel(mem_ref, g_ref, w_ref, o_ref):
    o_ref[...] = _dot(_rms(mem_ref[...], g_ref[...]).astype(BF16), w_ref[...]).astype(BF16)


def _mem_kv(mem, g, w):
    batch, n_mem, d = mem.shape
    return pl.pallas_call(
        _mem_kv_kernel,
        out_shape=jax.ShapeDtypeStruct((batch, n_mem, 2 * MEM_W), BF16),
        grid=(batch,),
        in_specs=[pl.BlockSpec((None, n_mem, d), lambda b: (b, 0, 0)),
                  pl.BlockSpec((1, d), lambda b: (0, 0)),
                  pl.BlockSpec((d, 2 * MEM_W), lambda b: (0, 0))],
        out_specs=pl.BlockSpec((None, n_mem, 2 * MEM_W), lambda b: (b, 0, 0)),
        compiler_params=_params("parallel"),
        name="mem_kv",
    )(mem, g, w)


def _mem_attn_kernel(q_ref, kv_ref, o_ref):
    scale = MEM_DIM ** -0.5
    for h in range(MEM_HEADS):
        q = q_ref[:, h * MEM_DIM:(h + 1) * MEM_DIM]
        k = kv_ref[:, h * MEM_DIM:(h + 1) * MEM_DIM]
        v = kv_ref[:, MEM_W + h * MEM_DIM:MEM_W + (h + 1) * MEM_DIM]
        s = _dot_nt(q, k) * scale
        p = jnp.exp(s - jnp.max(s, axis=1, keepdims=True))
        l = jnp.sum(p, axis=1, keepdims=True)
        o_ref[:, h * MEM_DIM:(h + 1) * MEM_DIM] = (_dot(p.astype(BF16), v) / l).astype(BF16)


def _mem_attn(proj, mkv, *, batch, seq, tq=512):
    proj3 = proj.reshape(batch, seq, PROJ_W)
    n_mem = mkv.shape[1]
    out = pl.pallas_call(
        _mem_attn_kernel,
        out_shape=jax.ShapeDtypeStruct((batch, seq, MEM_W), BF16),
        grid=(batch, seq // tq),
        in_specs=[pl.BlockSpec((None, tq, MEM_W), lambda b, i: (b, i, MQ_COL // MEM_W)),
                  pl.BlockSpec((None, n_mem, 2 * MEM_W), lambda b, i: (b, 0, 0))],
        out_specs=pl.BlockSpec((None, tq, MEM_W), lambda b, i: (b, i, 0)),
        compiler_params=_params("parallel", "arbitrary"),
        name="mem_attn",
    )(proj3, mkv)
    return out.reshape(batch * seq, MEM_W)


def _merge_kernel(x_ref, gpre_ref, yf_ref, yd_ref, ym_ref, wgf_ref, wgd_ref, wgm_ref,
                  bf_ref, bd_ref, bm_ref, wf_ref, wd_ref, wm_ref, wo_ref, gpost_ref,
                  o_ref, h_ref, acc_ref):
    n = pl.program_id(1)

    @pl.when(n == 0)
    def _():
        def emit(sl, y):
            h_ref[sl, :] = y.astype(BF16)

        _rms_rows(x_ref, gpre_ref[...], emit, BF16_ROWS)
        acc_ref[...] = jnp.zeros_like(acc_ref)

    h = h_ref[...]

    def gated(wg_ref, b_ref, y_ref, w_ref):
        z = _dot(h, wg_ref[...]) + b_ref[...]
        return (1.0 / (1.0 + jnp.exp(-z))) * _dot(y_ref[...], w_ref[...])

    merged = (gated(wgf_ref, bf_ref, yf_ref, wf_ref) + gated(wgd_ref, bd_ref, yd_ref, wd_ref)
              + gated(wgm_ref, bm_ref, ym_ref, wm_ref))
    acc_ref[...] += _dot(merged.astype(BF16), wo_ref[...])

    @pl.when(n == pl.num_programs(1) - 1)
    def _():
        def emit(sl, y):
            o_ref[sl, :] = x_ref[sl, :] + y

        _rms_rows(acc_ref, gpost_ref[...], emit, F32_ROWS)


def _merge(x, g_pre, y_fox, y_diff, y_mem, w_gate, b_gate, w_fox, w_diff, w_mem, w_out, g_post,
           *, tm=512, tn=512):
    t, d = x.shape
    nt = d // tn
    row = lambda m, n: (m, 0)
    return pl.pallas_call(
        _merge_kernel,
        out_shape=jax.ShapeDtypeStruct((t, d), F32),
        grid=(t // tm, nt),
        in_specs=[
            pl.BlockSpec((tm, d), row),
            pl.BlockSpec((1, d), lambda m, n: (0, 0)),
            pl.BlockSpec((tm, FOX_W), row),
            pl.BlockSpec((tm, DIFF_V_W), row),
            pl.BlockSpec((tm, MEM_W), row),
            pl.BlockSpec((d, tn), lambda m, n: (0, n)),
            pl.BlockSpec((d, tn), lambda m, n: (0, nt + n)),
            pl.BlockSpec((d, tn), lambda m, n: (0, 2 * nt + n)),
            pl.BlockSpec((1, tn), lambda m, n: (0, n)),
            pl.BlockSpec((1, tn), lambda m, n: (0, nt + n)),
            pl.BlockSpec((1, tn), lambda m, n: (0, 2 * nt + n)),
            pl.BlockSpec((FOX_W, tn), lambda m, n: (0, n)),
            pl.BlockSpec((DIFF_V_W, tn), lambda m, n: (0, n)),
            pl.BlockSpec((MEM_W, tn), lambda m, n: (0, n)),
            pl.BlockSpec((tn, d), lambda m, n: (n, 0)),
            pl.BlockSpec((1, d), lambda m, n: (0, 0)),
        ],
        out_specs=pl.BlockSpec((tm, d), row),
        scratch_shapes=[pltpu.VMEM((tm, d), BF16), pltpu.VMEM((tm, d), F32)],
        compiler_params=_params("parallel", "arbitrary"),
        name="merge",
    )(x, g_pre, y_fox, y_diff, y_mem, w_gate, w_gate, w_gate, b_gate, b_gate, b_gate,
      w_fox, w_diff, w_mem, w_out, g_post)


def kernel(x, mem, ffn1_pre_g, ffn1_w_gate, ffn1_w_up, ffn1_w_down, ffn1_post_g, mix_pre_g, w_in, fox_f_bias, diff_lambda_q1, diff_lambda_k1, diff_lambda_q2, diff_lambda_k2, diff_head_g, mem_norm_g, w_mem_kv, w_branch_fox, w_branch_diff, w_branch_mem, w_merge_gate, b_merge_gate, w_out, mix_post_g, ffn2_pre_g, ffn2_w_gate, ffn2_w_up, ffn2_w_down, ffn2_post_g):
    batch, seq, d = x.shape
    depth = w_in.shape[0]
    xt = x.reshape(batch * seq, d)
    cos_t, sa_t, sb_t = _rope_tables(seq)
    ff_lo = 3 * FOX_W
    ff_hi = ff_lo + FOX_HEADS

    def row(v):
        return v.reshape(1, -1).astype(F32)

    for l in range(depth):
        bf = lambda w: w[l].astype(BF16)
        xt = _ffn(xt, row(ffn1_pre_g[l]), bf(ffn1_w_gate), bf(ffn1_w_up), bf(ffn1_w_down),
                  row(ffn1_post_g[l]))

        w_head = w_in[l][:, :FF_COL + LANES].astype(BF16)
        w_tail = w_in[l][:, FF_COL + FOX_HEADS:].astype(BF16)
        proj, ff = _mix_proj(xt, row(mix_pre_g[l]), w_head, w_tail, cos_t, sa_t, sb_t, seq=seq)

        bias = jnp.pad(fox_f_bias[l].astype(F32), (0, LANES - FOX_HEADS)).reshape(1, LANES)
        fb, frow = _fox_gate(ff, bias, batch=batch, seq=seq)
        y_fox = _fox_attn(proj, fb, frow, batch=batch, seq=seq)

        lam_init = 0.8 - 0.6 * math.exp(-0.3 * l)
        lamv = jnp.stack([diff_lambda_q1[l], diff_lambda_k1[l], diff_lambda_q2[l],
                          diff_lambda_k2[l]]).astype(F32)
        y_diff = _diff_attn(proj, lamv, row(diff_head_g[l]), batch=batch, seq=seq,
                            lam_init=lam_init)

        mkv = _mem_kv(mem, row(mem_norm_g[l]), bf(w_mem_kv))
        y_mem = _mem_attn(proj, mkv, batch=batch, seq=seq)

        xt = _merge(xt, row(mix_pre_g[l]), y_fox, y_diff, y_mem, bf(w_merge_gate),
                    row(b_merge_gate[l]), bf(w_branch_fox), bf(w_branch_diff), bf(w_branch_mem),
                    bf(w_out), row(mix_post_g[l]))

        xt = _ffn(xt, row(ffn2_pre_g[l]), bf(ffn2_w_gate), bf(ffn2_w_up), bf(ffn2_w_down),
                  row(ffn2_post_g[l]))

    return xt.reshape(batch, seq, d)
```

```python
import functools
import math

import jax
import jax.numpy as jnp
from jax import lax
from jax.experimental import pallas as pl
from jax.experimental.pallas import tpu as pltpu

D_MODEL = 2048
CHUNK = 64
EPS = 1e-6
ROPE_THETA = 500000.0

FOX_HEADS = 8
FOX_DIM = 128
FOX_W = FOX_HEADS * FOX_DIM

DIFF_HEADS = 4
DIFF_QK_DIM = 64
DIFF_V_DIM = 2 * DIFF_QK_DIM
DIFF_QK_W = DIFF_HEADS * 2 * DIFF_QK_DIM
DIFF_V_W = DIFF_HEADS * DIFF_V_DIM
ROPE_DIM = DIFF_QK_DIM // 4

MEM_HEADS = 4
MEM_DIM = 128
MEM_W = MEM_HEADS * MEM_DIM

LANES = 128
F32_ROWS = 8
BF16_ROWS = 16
RMS_ROWS_IN_FLIGHT = 128
PROJ_W = 3 * FOX_W + 2 * DIFF_QK_W + DIFF_V_W + MEM_W
FQ_BLK, FK_BLK, FV_BLK = 0, FOX_HEADS, 2 * FOX_HEADS
DQ_BLK = 3 * FOX_HEADS
DK_BLK = DQ_BLK + DIFF_HEADS
DV_BLK = DK_BLK + DIFF_HEADS
MQ_COL = 3 * FOX_W + 2 * DIFF_QK_W + DIFF_V_W
FF_COL = 3 * FOX_W

VMEM_LIMIT = 56 * 1024 * 1024
BF16 = jnp.bfloat16
F32 = jnp.float32
NEG_INF = float("-inf")
LOG2E = math.log2(math.e)
FOX_QSCALE = FOX_DIM ** -0.5 * LOG2E
DIFF_QSCALE = DIFF_QK_DIM ** -0.5 * LOG2E


def _dot(a, b):
    return jnp.dot(a, b, preferred_element_type=F32)


def _dot_nt(a, b):
    return lax.dot_general(a, b, (((1,), (1,)), ((), ())), preferred_element_type=F32)


def _rms(x, g):
    return x * lax.rsqrt(jnp.mean(x * x, axis=-1, keepdims=True) + EPS) * g


def _rms_rows(src_ref, g, emit, rows):
    def body(r, carry):
        sl = pl.ds(pl.multiple_of(r * rows, rows), rows)
        emit(sl, _rms(src_ref[sl, :], g))
        return carry

    lax.fori_loop(0, src_ref.shape[0] // rows, body, 0, unroll=RMS_ROWS_IN_FLIGHT // rows)


def _params(*sem):
    return pltpu.CompilerParams(dimension_semantics=sem, vmem_limit_bytes=VMEM_LIMIT)


def _ffn_kernel(*refs, casts):
    n_in = 6 + len(casts)
    x_ref, gpre_ref, wg_ref, wu_ref, wd_ref, gpost_ref = refs[:6]
    o_ref = refs[n_in]
    h_ref, acc_ref = refs[-2:]
    f = pl.program_id(1)

    @pl.when(f == 0)
    def _():
        def emit(sl, y):
            h_ref[sl, :] = y.astype(BF16)

        _rms_rows(x_ref, gpre_ref[...], emit, BF16_ROWS)
        acc_ref[...] = jnp.zeros_like(acc_ref)

    jobs = []
    out_pos = n_in + 1
    for src_ref, windows in zip(refs[6:n_in], casts):
        for lo, hi in windows:
            jobs.append((refs[out_pos], src_ref, lo, hi))
            out_pos += 1

    def run_casts(part, parts=3):
        for dst_ref, src_ref, lo, hi in jobs[part::parts]:
            dst_ref[...] = src_ref[:, lo:hi].astype(BF16)

    h = h_ref[...]
    run_casts(0)
    g = _dot(h, wg_ref[...])
    run_casts(1)
    u = _dot(h, wu_ref[...])
    run_casts(2)
    a = (g * (1.0 / (1.0 + jnp.exp(-g)))) * u
    acc_ref[...] += _dot(a.astype(BF16), wd_ref[...])

    @pl.when(f == pl.num_programs(1) - 1)
    def _():
        def emit(sl, y):
            o_ref[sl, :] = x_ref[sl, :] + y

        _rms_rows(acc_ref, 0.5 * gpost_ref[...], emit, F32_ROWS)


def _ffn(x, g_pre, w_gate, w_up, w_down, g_post, *, cast=(), tm=512, tf=512):
    t, d = x.shape
    d_ff = w_gate.shape[1]
    m_tiles, f_steps = t // tm, d_ff // tf
    in_specs = [
        pl.BlockSpec((tm, d), lambda m, f: (m, 0)),
        pl.BlockSpec((1, d), lambda m, f: (0, 0)),
        pl.BlockSpec((d, tf), lambda m, f: (0, f)),
        pl.BlockSpec((d, tf), lambda m, f: (0, f)),
        pl.BlockSpec((tf, d), lambda m, f: (f, 0)),
        pl.BlockSpec((1, d), lambda m, f: (0, 0)),
    ]
    out_specs = [pl.BlockSpec((tm, d), lambda m, f: (m, 0))]
    out_shape = [jax.ShapeDtypeStruct((t, d), F32)]
    plans = []
    for arr, windows in cast:
        rows, cols = arr.shape
        br = BF16_ROWS
        while rows // m_tiles // br > f_steps:
            br *= 2
        steps = rows // m_tiles // br
        assert steps * br * m_tiles == rows
        index = lambda m, f, steps=steps: (m * steps + jnp.minimum(f, steps - 1), 0)
        in_specs.append(pl.BlockSpec((br, cols), index))
        for lo, hi in windows:
            out_specs.append(pl.BlockSpec((br, hi - lo), index))
            out_shape.append(jax.ShapeDtypeStruct((rows, hi - lo), BF16))
        plans.append(tuple(windows))
    outs = pl.pallas_call(
        functools.partial(_ffn_kernel, casts=tuple(plans)),
        out_shape=out_shape,
        grid=(m_tiles, f_steps),
        in_specs=in_specs,
        out_specs=out_specs,
        scratch_shapes=[pltpu.VMEM((tm, d), BF16), pltpu.VMEM((tm, d), F32)],
        compiler_params=_params("parallel", "arbitrary"),
        name="ffn",
    )(x, g_pre, w_gate, w_up, w_down, g_post, *[arr for arr, _ in cast])
    return outs[0], outs[1:]


def _proj_kernel(x_ref, g_ref, wa_ref, wff_ref, wb_ref, cos_ref, sa_ref, sb_ref, o_ref, ff_ref,
                 h_ref, *, tn):
    h_ref[...] = _rms(x_ref[...], g_ref[...]).astype(BF16)
    ff_ref[...] = _dot(h_ref[...], wff_ref[...])
    half = ROPE_DIM // 2
    na = wa_ref.shape[1] // tn
    for n in range(na + wb_ref.shape[1] // tn):
        w_tile = (wa_ref[:, n * tn:(n + 1) * tn] if n < na
                  else wb_ref[:, (n - na) * tn:(n - na + 1) * tn])
        y = _dot(h_ref[...], w_tile)
        first_blk = n * tn // LANES
        if first_blk < FK_BLK:
            y = y * FOX_QSCALE
        if DQ_BLK <= first_blk < DV_BLK:
            qscale = DIFF_QSCALE if first_blk < DK_BLK else 1.0
            for j in range(tn // LANES):
                blk = y[:, j * LANES:(j + 1) * LANES]
                rot = (blk * cos_ref[...] + pltpu.roll(blk, LANES - half, axis=1) * sa_ref[...]
                       + pltpu.roll(blk, half, axis=1) * sb_ref[...])
                o_ref[:, n * tn + j * LANES:n * tn + (j + 1) * LANES] = (rot * qscale).astype(BF16)
        else:
            o_ref[:, n * tn:(n + 1) * tn] = y.astype(BF16)


def _mix_proj(x, g, w_head, w_tail, cos_t, sa_t, sb_t, *, seq, tm=512, tn=512):
    t, d = x.shape
    n_out = FF_COL + w_tail.shape[1]
    assert all((blk * LANES) % tn == 0 for blk in (FK_BLK, DQ_BLK, DK_BLK, DV_BLK))
    assert w_head.shape[1] == FF_COL + LANES and n_out == PROJ_W
    s_tiles = seq // tm
    resident = dict(pipeline_mode=pl.Buffered(1))
    return pl.pallas_call(
        functools.partial(_proj_kernel, tn=tn),
        out_shape=(jax.ShapeDtypeStruct((t, n_out), BF16),
                   jax.ShapeDtypeStruct((t, LANES), F32)),
        grid=(t // tm,),
        in_specs=[
            pl.BlockSpec((tm, d), lambda m: (m, 0)),
            pl.BlockSpec((1, d), lambda m: (0, 0)),
            pl.BlockSpec((d, FF_COL), lambda m: (0, 0), **resident),
            pl.BlockSpec((d, LANES), lambda m: (0, FF_COL // LANES), **resident),
            pl.BlockSpec((d, w_tail.shape[1]), lambda m: (0, 0), **resident),
            pl.BlockSpec((tm, LANES), lambda m: (m % s_tiles, 0)),
            pl.BlockSpec((tm, LANES), lambda m: (m % s_tiles, 0)),
            pl.BlockSpec((tm, LANES), lambda m: (m % s_tiles, 0)),
        ],
        out_specs=(pl.BlockSpec((tm, n_out), lambda m: (m, 0)),
                   pl.BlockSpec((tm, LANES), lambda m: (m, 0))),
        scratch_shapes=[pltpu.VMEM((tm, d), BF16)],
        compiler_params=_params("parallel"),
        name="mix_proj",
    )(x, g, w_head, w_head, w_tail, cos_t, sa_t, sb_t)


def _rope_tables(seq):
    half = ROPE_DIM // 2
    pos = jnp.arange(seq, dtype=F32)
    inv_freq = ROPE_THETA ** (-jnp.arange(0, ROPE_DIM, 2, dtype=F32) / ROPE_DIM)
    ang = pos[:, None] * inv_freq[None, :]
    cos, sin = jnp.cos(ang), jnp.sin(ang)
    ones = jnp.ones((seq, DIFF_QK_DIM - ROPE_DIM), F32)
    zeros_h = jnp.zeros((seq, half), F32)
    zeros_r = jnp.zeros((seq, DIFF_QK_DIM - ROPE_DIM), F32)
    cos_m = jnp.concatenate([cos, cos, ones], axis=1)
    sa_m = jnp.concatenate([-sin, zeros_h, zeros_r], axis=1)
    sb_m = jnp.concatenate([zeros_h, sin, zeros_r], axis=1)
    rep = LANES // DIFF_QK_DIM
    return (jnp.tile(cos_m, (1, rep)), jnp.tile(sa_m, (1, rep)), jnp.tile(sb_m, (1, rep)))


def _fgate_kernel(ff_ref, bias_ref, fb_ref, frow_ref, fcol_ref, *, cb):
    seq = ff_ref.shape[0]
    z = ff_ref[...] + bias_ref[...]
    lf = jnp.minimum(z, 0.0) - jnp.log1p(jnp.exp(-jnp.abs(z)))
    r = lax.broadcasted_iota(jnp.int32, (cb, cb), 0)
    c = lax.broadcasted_iota(jnp.int32, (cb, cb), 1)
    tri = (r >= c).astype(F32)
    carry = jnp.zeros((1, LANES), F32)
    for i in range(seq // cb):
        cs = jnp.dot(tri, lf[i * cb:(i + 1) * cb], precision=lax.Precision.HIGHEST,
                     preferred_element_type=F32) + carry
        carry = cs[cb - 1:cb, :]
        cs2 = cs * LOG2E
        fcol_ref[i * cb:(i + 1) * cb, :] = cs2
        for h in range(FOX_HEADS):
            fb_ref[h, i * cb:(i + 1) * cb, :] = jnp.broadcast_to(cs2[:, h:h + 1], (cb, LANES))
    frow_ref[...] = fcol_ref[...].T[:FOX_HEADS]


def _fox_gate(ff, bias, *, batch, seq, cb=256):
    ff = ff.reshape(batch, seq, LANES)
    return pl.pallas_call(
        functools.partial(_fgate_kernel, cb=cb),
        out_shape=(jax.ShapeDtypeStruct((batch, FOX_HEADS, seq, LANES), F32),
                   jax.ShapeDtypeStruct((batch, FOX_HEADS, seq), F32)),
        grid=(batch,),
        in_specs=[pl.BlockSpec((None, seq, LANES), lambda b: (b, 0, 0)),
                  pl.BlockSpec((1, LANES), lambda b: (0, 0))],
        out_specs=(pl.BlockSpec((None, FOX_HEADS, seq, LANES), lambda b: (b, 0, 0, 0)),
                   pl.BlockSpec((None, FOX_HEADS, seq), lambda b: (b, 0, 0))),
        scratch_shapes=[pltpu.VMEM((seq, LANES), F32)],
        compiler_params=_params("parallel"),
        name="fox_gate",
    )(ff, bias)


def _dot_tn(a, b):
    return lax.dot_general(a, b, (((0,), (0,)), ((), ())), preferred_element_type=F32)


def _flash_attend(i, heads, logits, values, ft2, visible, finish,
                  sa_ref, sb_ref, m_ref, l_ref, acc_ref):
    def step(h, s, j):
        m = m_ref[h]
        m_new = jnp.maximum(m, jnp.max(s, axis=0, keepdims=True) + ft2[h])
        alpha = jnp.exp2(m - m_new)
        p = jnp.exp2(s - (m_new - ft2[h]))
        m_ref[h] = m_new
        l_ref[h] = alpha * l_ref[h] + jnp.sum(p, axis=0, keepdims=True)
        acc_ref[h] = alpha * acc_ref[h] + _dot_tn(values(h, j), p.astype(BF16))

    for h in heads:
        m_ref[h] = jnp.full(m_ref.shape[1:], NEG_INF, F32)
        l_ref[h] = jnp.zeros(l_ref.shape[1:], F32)
        acc_ref[h] = jnp.zeros(acc_ref.shape[1:], F32)
        sa_ref[h] = logits(h, 0)

    def pair(jj, carry):
        j = 2 * jj
        for h in heads:
            sb_ref[h] = logits(h, j + 1)
            step(h, sa_ref[h], j)
        for h in heads:
            sa_ref[h] = logits(h, j + 2)
            step(h, sb_ref[h], j + 1)
        return carry

    lax.fori_loop(0, i // 2, pair, 0)

    def last(s_ref):
        for h in heads:
            step(h, jnp.where(visible, s_ref[h], NEG_INF), i)
            finish(h, acc_ref[h] / l_ref[h])

    @pl.when(i % 2 == 0)
    def _():
        last(sa_ref)

    @pl.when(i % 2 == 1)
    def _():
        for h in heads:
            sb_ref[h] = logits(h, i)
            step(h, sa_ref[h], i - 1)
        last(sb_ref)


def _flash_scratch(hb, width, tq, cols):
    return [pltpu.VMEM((hb, tq, cols), F32), pltpu.VMEM((hb, tq, cols), F32),
            pltpu.VMEM((hb, 1, cols), F32), pltpu.VMEM((hb, 1, cols), F32),
            pltpu.VMEM((hb, width, cols), F32)]


def _fox_kernel(q_ref, k_ref, v_ref, fb_ref, fr_ref, o_ref, *scratch, tq, hb):
    i = pl.program_id(2)
    heads = range(hb)
    ft2 = [fr_ref[h, pl.ds(i, 1), :] for h in heads]

    def logits(h, j):
        start = pl.multiple_of(j * tq, tq)
        hs = slice(h * FOX_DIM, (h + 1) * FOX_DIM)
        st = _dot_nt(k_ref[pl.ds(start, tq), hs], q_ref[:, hs])
        fb = fb_ref[h, pl.ds(start, tq), :]
        return st - jnp.concatenate([fb] * (tq // LANES), axis=1)

    def values(h, j):
        return v_ref[pl.ds(pl.multiple_of(j * tq, tq), tq), h * FOX_DIM:(h + 1) * FOX_DIM]

    def finish(h, o_t):
        o_ref[:, h * FOX_DIM:(h + 1) * FOX_DIM] = o_t.T.astype(BF16)

    krow = lax.broadcasted_iota(jnp.int32, (tq, tq), 0)
    qcol = lax.broadcasted_iota(jnp.int32, (tq, tq), 1)
    _flash_attend(i, heads, logits, values, ft2, krow <= qcol, finish, *scratch)


def _fox_attn(proj, fb, frow, *, batch, seq, tq=256, hb=8):
    proj3 = proj.reshape(batch, seq, PROJ_W)
    frow4 = frow.reshape(batch, FOX_HEADS, seq // tq, tq)
    w = hb * FOX_DIM
    groups = FOX_HEADS // hb
    out = pl.pallas_call(
        functools.partial(_fox_kernel, tq=tq, hb=hb),
        out_shape=jax.ShapeDtypeStruct((batch, seq, FOX_W), BF16),
        grid=(batch, groups, seq // tq),
        in_specs=[
            pl.BlockSpec((None, tq, w), lambda b, g, i: (b, i, g)),
            pl.BlockSpec((None, seq, w), lambda b, g, i: (b, 0, groups + g)),
            pl.BlockSpec((None, seq, w), lambda b, g, i: (b, 0, 2 * groups + g)),
            pl.BlockSpec((None, hb, seq, LANES), lambda b, g, i: (b, g, 0, 0)),
            pl.BlockSpec((None, hb, seq // tq, tq), lambda b, g, i: (b, g, 0, 0)),
        ],
        out_specs=pl.BlockSpec((None, tq, w), lambda b, g, i: (b, i, g)),
        scratch_shapes=_flash_scratch(hb, FOX_DIM, tq, tq),
        compiler_params=_params("parallel", "parallel", "arbitrary"),
        name="fox_attn",
    )(proj3, proj3, proj3, fb, frow4)
    return out.reshape(batch * seq, FOX_W)


def _diff_kernel(lamv_ref, g_ref, q_ref, k_ref, v_ref, o_ref, *scratch, tq, hb, lam_init):
    i = pl.program_id(2)
    heads = range(hb)
    lv = lamv_ref[...]
    lam = (jnp.exp(jnp.sum(lv[0:1] * lv[1:2], axis=1, keepdims=True))
           - jnp.exp(jnp.sum(lv[2:3] * lv[3:4], axis=1, keepdims=True)) + lam_init)

    lane = lax.broadcasted_iota(jnp.int32, (tq, LANES), 1)
    qs = []
    for h in heads:
        q = q_ref[:, h * LANES:(h + 1) * LANES].astype(F32)
        qs.append(jnp.concatenate([jnp.where(lane < DIFF_QK_DIM, q, 0.0),
                                   jnp.where(lane >= DIFF_QK_DIM, q, 0.0)], axis=0).astype(BF16))
    zero = jnp.zeros((1, 2 * tq), F32)

    def logits(h, j):
        start = pl.multiple_of(j * tq, tq)
        return _dot_nt(k_ref[pl.ds(start, tq), h * LANES:(h + 1) * LANES], qs[h])

    def values(h, j):
        return v_ref[pl.ds(pl.multiple_of(j * tq, tq), tq), h * LANES:(h + 1) * LANES]

    def finish(h, o_t):
        yd = (o_t[:, :tq] - lam * o_t[:, tq:]).T
        o_ref[:, h * LANES:(h + 1) * LANES] = (
            _rms(yd, g_ref[...]) * (1.0 - lam_init)).astype(BF16)

    krow = lax.broadcasted_iota(jnp.int32, (tq, 2 * tq), 0)
    qcol = lax.broadcasted_iota(jnp.int32, (tq, 2 * tq), 1)
    qcol = jnp.where(qcol >= tq, qcol - tq, qcol)
    visible = krow // CHUNK <= qcol // CHUNK
    _flash_attend(i, heads, logits, values, [zero] * hb, visible, finish, *scratch)


def _diff_attn(proj, lamv, g, *, batch, seq, lam_init, tq=256, hb=4):
    proj3 = proj.reshape(batch, seq, PROJ_W)
    w = hb * LANES
    groups = DIFF_HEADS // hb
    dq, dk, dv = (DQ_BLK * LANES) // w, (DK_BLK * LANES) // w, (DV_BLK * LANES) // w
    out = pl.pallas_call(
        functools.partial(_diff_kernel, tq=tq, hb=hb, lam_init=lam_init),
        out_shape=jax.ShapeDtypeStruct((batch, seq, DIFF_V_W), BF16),
        grid=(batch, groups, seq // tq),
        in_specs=[
            pl.BlockSpec((4, DIFF_QK_DIM), lambda b, g, i: (0, 0)),
            pl.BlockSpec((1, DIFF_V_DIM), lambda b, g, i: (0, 0)),
            pl.BlockSpec((None, tq, w), lambda b, g, i: (b, i, dq + g)),
            pl.BlockSpec((None, seq, w), lambda b, g, i: (b, 0, dk + g)),
            pl.BlockSpec((None, seq, w), lambda b, g, i: (b, 0, dv + g)),
        ],
        out_specs=pl.BlockSpec((None, tq, w), lambda b, g, i: (b, i, g)),
        scratch_shapes=_flash_scratch(hb, DIFF_V_DIM, tq, 2 * tq),
        compiler_params=_params("parallel", "parallel", "arbitrary"),
        name="diff_attn",
    )(lamv, g, proj3, proj3, proj3)
    return out.reshape(batch * seq, DIFF_V_W)


def _mem_kv_kernel(mem_ref, g_ref, w_ref, o_ref):
    o_ref[...] = _dot(_rms(mem_ref[...], g_ref[...]).astype(BF16), w_ref[...]).astype(BF16)


def _mem_kv(mem, g, w):
    batch, n_mem, d = mem.shape
    return pl.pallas_call(
        _mem_kv_kernel,
        out_shape=jax.ShapeDtypeStruct((batch, n_mem, 2 * MEM_W), BF16),
        grid=(batch,),
        in_specs=[pl.BlockSpec((None, n_mem, d), lambda b: (b, 0, 0)),
                  pl.BlockSpec((1, d), lambda b: (0, 0)),
                  pl.BlockSpec((d, 2 * MEM_W), lambda b: (0, 0))],
        out_specs=pl.BlockSpec((None, n_mem, 2 * MEM_W), lambda b: (b, 0, 0)),
        compiler_params=_params("parallel"),
        name="mem_kv",
    )(mem, g, w)


def _mem_attn_kernel(q_ref, kv_ref, o_ref):
    scale = MEM_DIM ** -0.5
    for h in range(MEM_HEADS):
        q = q_ref[:, h * MEM_DIM:(h + 1) * MEM_DIM]
        k = kv_ref[:, h * MEM_DIM:(h + 1) * MEM_DIM]
        v = kv_ref[:, MEM_W + h * MEM_DIM:MEM_W + (h + 1) * MEM_DIM]
        s = _dot_nt(q, k) * scale
        p = jnp.exp(s - jnp.max(s, axis=1, keepdims=True))
        l = jnp.sum(p, axis=1, keepdims=True)
        o_ref[:, h * MEM_DIM:(h + 1) * MEM_DIM] = (_dot(p.astype(BF16), v) / l).astype(BF16)


def _mem_attn(proj, mkv, *, batch, seq, tq=512):
    proj3 = proj.reshape(batch, seq, PROJ_W)
    n_mem = mkv.shape[1]
    out = pl.pallas_call(
        _mem_attn_kernel,
        out_shape=jax.ShapeDtypeStruct((batch, seq, MEM_W), BF16),
        grid=(batch, seq // tq),
        in_specs=[pl.BlockSpec((None, tq, MEM_W), lambda b, i: (b, i, MQ_COL // MEM_W)),
                  pl.BlockSpec((None, n_mem, 2 * MEM_W), lambda b, i: (b, 0, 0))],
        out_specs=pl.BlockSpec((None, tq, MEM_W), lambda b, i: (b, i, 0)),
        compiler_params=_params("parallel", "arbitrary"),
        name="mem_attn",
    )(proj3, mkv)
    return out.reshape(batch * seq, MEM_W)


def _merge_kernel(x_ref, gpre_ref, yf_ref, yd_ref, ym_ref, wgf_ref, wgd_ref, wgm_ref,
                  bf_ref, bd_ref, bm_ref, wf_ref, wd_ref, wm_ref, wo_ref, gpost_ref,
                  o_ref, h_ref, acc_ref):
    n = pl.program_id(1)

    @pl.when(n == 0)
    def _():
        def emit(sl, y):
            h_ref[sl, :] = y.astype(BF16)

        _rms_rows(x_ref, gpre_ref[...], emit, BF16_ROWS)
        acc_ref[...] = jnp.zeros_like(acc_ref)

    h = h_ref[...]

    def gated(wg_ref, b_ref, y_ref, w_ref):
        z = _dot(h, wg_ref[...]) + b_ref[...]
        return (1.0 / (1.0 + jnp.exp(-z))) * _dot(y_ref[...], w_ref[...])

    merged = (gated(wgf_ref, bf_ref, yf_ref, wf_ref) + gated(wgd_ref, bd_ref, yd_ref, wd_ref)
              + gated(wgm_ref, bm_ref, ym_ref, wm_ref))
    acc_ref[...] += _dot(merged.astype(BF16), wo_ref[...])

    @pl.when(n == pl.num_programs(1) - 1)
    def _():
        def emit(sl, y):
            o_ref[sl, :] = x_ref[sl, :] + y

        _rms_rows(acc_ref, gpost_ref[...], emit, F32_ROWS)


def _merge(x, g_pre, y_fox, y_diff, y_mem, w_gate, b_gate, w_fox, w_diff, w_mem, w_out, g_post,
           *, tm=512, tn=512):
    t, d = x.shape
    nt = d // tn
    row = lambda m, n: (m, 0)
    return pl.pallas_call(
        _merge_kernel,
        out_shape=jax.ShapeDtypeStruct((t, d), F32),
        grid=(t // tm, nt),
        in_specs=[
            pl.BlockSpec((tm, d), row),
            pl.BlockSpec((1, d), lambda m, n: (0, 0)),
            pl.BlockSpec((tm, FOX_W), row),
            pl.BlockSpec((tm, DIFF_V_W), row),
            pl.BlockSpec((tm, MEM_W), row),
            pl.BlockSpec((d, tn), lambda m, n: (0, n)),
            pl.BlockSpec((d, tn), lambda m, n: (0, nt + n)),
            pl.BlockSpec((d, tn), lambda m, n: (0, 2 * nt + n)),
            pl.BlockSpec((1, tn), lambda m, n: (0, n)),
            pl.BlockSpec((1, tn), lambda m, n: (0, nt + n)),
            pl.BlockSpec((1, tn), lambda m, n: (0, 2 * nt + n)),
            pl.BlockSpec((FOX_W, tn), lambda m, n: (0, n)),
            pl.BlockSpec((DIFF_V_W, tn), lambda m, n: (0, n)),
            pl.BlockSpec((MEM_W, tn), lambda m, n: (0, n)),
            pl.BlockSpec((tn, d), lambda m, n: (n, 0)),
            pl.BlockSpec((1, d), lambda m, n: (0, 0)),
        ],
        out_specs=pl.BlockSpec((tm, d), row),
        scratch_shapes=[pltpu.VMEM((tm, d), BF16), pltpu.VMEM((tm, d), F32)],
        compiler_params=_params("parallel", "arbitrary"),
        name="merge",
    )(x, g_pre, y_fox, y_diff, y_mem, w_gate, w_gate, w_gate, b_gate, b_gate, b_gate,
      w_fox, w_diff, w_mem, w_out, g_post)


def kernel(x, mem, ffn1_pre_g, ffn1_w_gate, ffn1_w_up, ffn1_w_down, ffn1_post_g, mix_pre_g, w_in, fox_f_bias, diff_lambda_q1, diff_lambda_k1, diff_lambda_q2, diff_lambda_k2, diff_head_g, mem_norm_g, w_mem_kv, w_branch_fox, w_branch_diff, w_branch_mem, w_merge_gate, b_merge_gate, w_out, mix_post_g, ffn2_pre_g, ffn2_w_gate, ffn2_w_up, ffn2_w_down, ffn2_post_g):
    batch, seq, d = x.shape
    depth = w_in.shape[0]
    xt = x.reshape(batch * seq, d)
    cos_t, sa_t, sb_t = _rope_tables(seq)

    def row(v):
        return v.reshape(1, -1).astype(F32)

    for l in range(depth):
        bf = lambda w: w[l].astype(BF16)
        later = [ffn2_w_gate, ffn2_w_up, ffn2_w_down, w_merge_gate]
        cast = [(w[l], ((0, w.shape[2]),)) for w in later]
        cast.append((w_in[l], ((0, FF_COL + LANES), (FF_COL + FOX_HEADS, w_in.shape[2]))))
        xt, cast_out = _ffn(xt, row(ffn1_pre_g[l]), bf(ffn1_w_gate), bf(ffn1_w_up),
                            bf(ffn1_w_down), row(ffn1_post_g[l]), cast=cast)
        w2_gate, w2_up, w2_down, wb_merge, w_head, w_tail = cast_out

        proj, ff = _mix_proj(xt, row(mix_pre_g[l]), w_head, w_tail, cos_t, sa_t, sb_t, seq=seq)

        bias = jnp.pad(fox_f_bias[l].astype(F32), (0, LANES - FOX_HEADS)).reshape(1, LANES)
        fb, frow = _fox_gate(ff, bias, batch=batch, seq=seq)
        y_fox = _fox_attn(proj, fb, frow, batch=batch, seq=seq)

        lam_init = 0.8 - 0.6 * math.exp(-0.3 * l)
        lamv = jnp.stack([diff_lambda_q1[l], diff_lambda_k1[l], diff_lambda_q2[l],
                          diff_lambda_k2[l]]).astype(F32)
        y_diff = _diff_attn(proj, lamv, row(diff_head_g[l]), batch=batch, seq=seq,
                            lam_init=lam_init)

        mkv = _mem_kv(mem, row(mem_norm_g[l]), bf(w_mem_kv))
        y_mem = _mem_attn(proj, mkv, batch=batch, seq=seq)

        xt = _merge(xt, row(mix_pre_g[l]), y_fox, y_diff, y_mem, wb_merge,
                    row(b_merge_gate[l]), bf(w_branch_fox), bf(w_branch_diff), bf(w_branch_mem),
                    bf(w_out), row(mix_post_g[l]))

        xt, _ = _ffn(xt, row(ffn2_pre_g[l]), w2_gate, w2_up, w2_down, row(ffn2_post_g[l]))

    return xt.reshape(batch, seq, d)
```

```python
import functools
import math

import jax
import jax.numpy as jnp
from jax import lax
from jax.experimental import pallas as pl
from jax.experimental.pallas import tpu as pltpu

D_MODEL = 2048
CHUNK = 64
EPS = 1e-6
ROPE_THETA = 500000.0

FOX_HEADS = 8
FOX_DIM = 128
FOX_W = FOX_HEADS * FOX_DIM

DIFF_HEADS = 4
DIFF_QK_DIM = 64
DIFF_V_DIM = 2 * DIFF_QK_DIM
DIFF_QK_W = DIFF_HEADS * 2 * DIFF_QK_DIM
DIFF_V_W = DIFF_HEADS * DIFF_V_DIM
ROPE_DIM = DIFF_QK_DIM // 4

MEM_HEADS = 4
MEM_DIM = 128
MEM_W = MEM_HEADS * MEM_DIM

LANES = 128
F32_ROWS = 8
BF16_ROWS = 16
RMS_ROWS_IN_FLIGHT = 128
PROJ_W = 3 * FOX_W + 2 * DIFF_QK_W + DIFF_V_W + MEM_W
FQ_BLK, FK_BLK, FV_BLK = 0, FOX_HEADS, 2 * FOX_HEADS
DQ_BLK = 3 * FOX_HEADS
DK_BLK = DQ_BLK + DIFF_HEADS
DV_BLK = DK_BLK + DIFF_HEADS
MQ_COL = 3 * FOX_W + 2 * DIFF_QK_W + DIFF_V_W
FF_COL = 3 * FOX_W

VMEM_LIMIT = 56 * 1024 * 1024
BF16 = jnp.bfloat16
F32 = jnp.float32
NEG_INF = float("-inf")
LOG2E = math.log2(math.e)
FOX_QSCALE = FOX_DIM ** -0.5 * LOG2E
DIFF_QSCALE = DIFF_QK_DIM ** -0.5 * LOG2E


def _dot(a, b):
    return jnp.dot(a, b, preferred_element_type=F32)


def _dot_nt(a, b):
    return lax.dot_general(a, b, (((1,), (1,)), ((), ())), preferred_element_type=F32)


def _rms(x, g):
    return x * lax.rsqrt(jnp.mean(x * x, axis=-1, keepdims=True) + EPS) * g


def _rms_rows(src_ref, g, emit, rows):
    def body(r, carry):
        sl = pl.ds(pl.multiple_of(r * rows, rows), rows)
        emit(sl, _rms(src_ref[sl, :], g))
        return carry

    lax.fori_loop(0, src_ref.shape[0] // rows, body, 0, unroll=RMS_ROWS_IN_FLIGHT // rows)


def _params(*sem):
    return pltpu.CompilerParams(dimension_semantics=sem, vmem_limit_bytes=VMEM_LIMIT)


def _ffn_kernel(*refs, casts):
    n_in = 6 + len(casts)
    x_ref, gpre_ref, wg_ref, wu_ref, wd_ref, gpost_ref = refs[:6]
    o_ref = refs[n_in]
    h_ref, acc_ref = refs[-2:]
    f = pl.program_id(1)

    @pl.when(f == 0)
    def _():
        def emit(sl, y):
            h_ref[sl, :] = y.astype(BF16)
            acc_ref[sl, :] = jnp.zeros_like(y)

        _rms_rows(x_ref, gpre_ref[...], emit, BF16_ROWS)

    jobs = []
    out_pos = n_in + 1
    for src_ref, windows in zip(refs[6:n_in], casts):
        for lo, hi in windows:
            jobs.append((refs[out_pos], src_ref, lo, hi))
            out_pos += 1

    def run_casts(part, parts=3):
        for dst_ref, src_ref, lo, hi in jobs[part::parts]:
            dst_ref[...] = src_ref[:, lo:hi].astype(BF16)

    h = h_ref[...]
    run_casts(0)
    g = _dot(h, wg_ref[...])
    run_casts(1)
    u = _dot(h, wu_ref[...])
    run_casts(2)
    a = (g * (1.0 / (1.0 + jnp.exp(-g)))) * u
    acc_ref[...] += _dot(a.astype(BF16), wd_ref[...])

    @pl.when(f == pl.num_programs(1) - 1)
    def _():
        def emit(sl, y):
            o_ref[sl, :] = x_ref[sl, :] + y

        _rms_rows(acc_ref, 0.5 * gpost_ref[...], emit, F32_ROWS)


def _ffn(x, g_pre, w_gate, w_up, w_down, g_post, *, cast=(), tm=512, tf=512):
    t, d = x.shape
    d_ff = w_gate.shape[1]
    m_tiles, f_steps = t // tm, d_ff // tf
    in_specs = [
        pl.BlockSpec((tm, d), lambda m, f: (m, 0)),
        pl.BlockSpec((1, d), lambda m, f: (0, 0)),
        pl.BlockSpec((d, tf), lambda m, f: (0, f)),
        pl.BlockSpec((d, tf), lambda m, f: (0, f)),
        pl.BlockSpec((tf, d), lambda m, f: (f, 0)),
        pl.BlockSpec((1, d), lambda m, f: (0, 0)),
    ]
    out_specs = [pl.BlockSpec((tm, d), lambda m, f: (m, 0))]
    out_shape = [jax.ShapeDtypeStruct((t, d), F32)]
    plans = []
    for arr, layer, windows in cast:
        _, rows, cols = arr.shape
        br = BF16_ROWS
        while rows // m_tiles // br > f_steps:
            br *= 2
        steps = rows // m_tiles // br
        assert steps * br * m_tiles == rows

        def index(m, f, steps=steps):
            return m * steps + jnp.minimum(f, steps - 1)

        in_specs.append(pl.BlockSpec((None, br, cols),
                                     lambda m, f, index=index, layer=layer: (layer, index(m, f), 0)))
        for lo, hi in windows:
            out_specs.append(pl.BlockSpec((br, hi - lo), lambda m, f, index=index: (index(m, f), 0)))
            out_shape.append(jax.ShapeDtypeStruct((rows, hi - lo), BF16))
        plans.append(tuple(windows))
    outs = pl.pallas_call(
        functools.partial(_ffn_kernel, casts=tuple(plans)),
        out_shape=out_shape,
        grid=(m_tiles, f_steps),
        in_specs=in_specs,
        out_specs=out_specs,
        scratch_shapes=[pltpu.VMEM((tm, d), BF16), pltpu.VMEM((tm, d), F32)],
        compiler_params=_params("parallel", "arbitrary"),
        name="ffn",
    )(x, g_pre, w_gate, w_up, w_down, g_post, *[arr for arr, _, _ in cast])
    return outs[0], outs[1:]


def _proj_kernel(*refs, tn, n_cast):
    x_ref, g_ref, wa_ref, wff_ref, wb_ref, cos_ref, sa_ref, sb_ref = refs[:8]
    o_ref, ff_ref = refs[8 + n_cast:10 + n_cast]
    h_ref = refs[-1]
    for src_ref, dst_ref in zip(refs[8:8 + n_cast], refs[10 + n_cast:10 + 2 * n_cast]):
        dst_ref[...] = src_ref[...].astype(BF16)
    h_ref[...] = _rms(x_ref[...], g_ref[...]).astype(BF16)
    ff_ref[...] = _dot(h_ref[...], wff_ref[...])
    half = ROPE_DIM // 2
    na = wa_ref.shape[1] // tn
    for n in range(na + wb_ref.shape[1] // tn):
        w_tile = (wa_ref[:, n * tn:(n + 1) * tn] if n < na
                  else wb_ref[:, (n - na) * tn:(n - na + 1) * tn])
        y = _dot(h_ref[...], w_tile)
        first_blk = n * tn // LANES
        if first_blk < FK_BLK:
            y = y * FOX_QSCALE
        if DQ_BLK <= first_blk < DV_BLK:
            qscale = DIFF_QSCALE if first_blk < DK_BLK else 1.0
            for j in range(tn // LANES):
                blk = y[:, j * LANES:(j + 1) * LANES]
                rot = (blk * cos_ref[...] + pltpu.roll(blk, LANES - half, axis=1) * sa_ref[...]
                       + pltpu.roll(blk, half, axis=1) * sb_ref[...])
                o_ref[:, n * tn + j * LANES:n * tn + (j + 1) * LANES] = (rot * qscale).astype(BF16)
        else:
            o_ref[:, n * tn:(n + 1) * tn] = y.astype(BF16)


def _mix_proj(x, g, w_head, w_tail, cos_t, sa_t, sb_t, *, seq, cast=(), tm=512, tn=512):
    t, d = x.shape
    n_out = FF_COL + w_tail.shape[1]
    assert all((blk * LANES) % tn == 0 for blk in (FK_BLK, DQ_BLK, DK_BLK, DV_BLK))
    assert w_head.shape[1] == FF_COL + LANES and n_out == PROJ_W
    s_tiles = seq // tm
    m_tiles = t // tm
    resident = dict(pipeline_mode=pl.Buffered(1))
    in_specs = [
        pl.BlockSpec((tm, d), lambda m: (m, 0)),
        pl.BlockSpec((1, d), lambda m: (0, 0)),
        pl.BlockSpec((d, FF_COL), lambda m: (0, 0), **resident),
        pl.BlockSpec((d, LANES), lambda m: (0, FF_COL // LANES), **resident),
        pl.BlockSpec((d, w_tail.shape[1]), lambda m: (0, 0), **resident),
        pl.BlockSpec((tm, LANES), lambda m: (m % s_tiles, 0)),
        pl.BlockSpec((tm, LANES), lambda m: (m % s_tiles, 0)),
        pl.BlockSpec((tm, LANES), lambda m: (m % s_tiles, 0)),
    ]
    out_specs = [pl.BlockSpec((tm, n_out), lambda m: (m, 0)),
                 pl.BlockSpec((tm, LANES), lambda m: (m, 0))]
    out_shape = [jax.ShapeDtypeStruct((t, n_out), BF16), jax.ShapeDtypeStruct((t, LANES), F32)]
    for arr, layer in cast:
        _, rows, cols = arr.shape
        br = rows // m_tiles
        assert br * m_tiles == rows and br % BF16_ROWS == 0
        in_specs.append(pl.BlockSpec((None, br, cols), lambda m, layer=layer: (layer, m, 0)))
        out_specs.append(pl.BlockSpec((br, cols), lambda m: (m, 0)))
        out_shape.append(jax.ShapeDtypeStruct((rows, cols), BF16))
    outs = pl.pallas_call(
        functools.partial(_proj_kernel, tn=tn, n_cast=len(cast)),
        out_shape=out_shape,
        grid=(m_tiles,),
        in_specs=in_specs,
        out_specs=out_specs,
        scratch_shapes=[pltpu.VMEM((tm, d), BF16)],
        compiler_params=_params("parallel"),
        name="mix_proj",
    )(x, g, w_head, w_head, w_tail, cos_t, sa_t, sb_t, *[arr for arr, _ in cast])
    return outs[0], outs[1], outs[2:]


def _rope_tables(seq):
    half = ROPE_DIM // 2
    pos = jnp.arange(seq, dtype=F32)
    inv_freq = ROPE_THETA ** (-jnp.arange(0, ROPE_DIM, 2, dtype=F32) / ROPE_DIM)
    ang = pos[:, None] * inv_freq[None, :]
    cos, sin = jnp.cos(ang), jnp.sin(ang)
    ones = jnp.ones((seq, DIFF_QK_DIM - ROPE_DIM), F32)
    zeros_h = jnp.zeros((seq, half), F32)
    zeros_r = jnp.zeros((seq, DIFF_QK_DIM - ROPE_DIM), F32)
    cos_m = jnp.concatenate([cos, cos, ones], axis=1)
    sa_m = jnp.concatenate([-sin, zeros_h, zeros_r], axis=1)
    sb_m = jnp.concatenate([zeros_h, sin, zeros_r], axis=1)
    rep = LANES // DIFF_QK_DIM
    return (jnp.tile(cos_m, (1, rep)), jnp.tile(sa_m, (1, rep)), jnp.tile(sb_m, (1, rep)))


def _fgate_kernel(ff_ref, bias_ref, fb_ref, frow_ref, fcol_ref, *, cb):
    seq = ff_ref.shape[0]
    z = ff_ref[...] + bias_ref[...]
    lf = jnp.minimum(z, 0.0) - jnp.log1p(jnp.exp(-jnp.abs(z)))
    r = lax.broadcasted_iota(jnp.int32, (cb, cb), 0)
    c = lax.broadcasted_iota(jnp.int32, (cb, cb), 1)
    tri = (r >= c).astype(F32)
    carry = jnp.zeros((1, LANES), F32)
    for i in range(seq // cb):
        cs = jnp.dot(tri, lf[i * cb:(i + 1) * cb], precision=lax.Precision.HIGHEST,
                     preferred_element_type=F32) + carry
        carry = cs[cb - 1:cb, :]
        cs2 = cs * LOG2E
        fcol_ref[i * cb:(i + 1) * cb, :] = cs2
        for h in range(FOX_HEADS):
            fb_ref[h, i * cb:(i + 1) * cb, :] = jnp.broadcast_to(cs2[:, h:h + 1], (cb, LANES))
    frow_ref[...] = fcol_ref[...].T[:FOX_HEADS]


def _fox_gate(ff, bias, *, batch, seq, cb=256):
    ff = ff.reshape(batch, seq, LANES)
    return pl.pallas_call(
        functools.partial(_fgate_kernel, cb=cb),
        out_shape=(jax.ShapeDtypeStruct((batch, FOX_HEADS, seq, LANES), F32),
                   jax.ShapeDtypeStruct((batch, FOX_HEADS, seq), F32)),
        grid=(batch,),
        in_specs=[pl.BlockSpec((None, seq, LANES), lambda b: (b, 0, 0)),
                  pl.BlockSpec((1, LANES), lambda b: (0, 0))],
        out_specs=(pl.BlockSpec((None, FOX_HEADS, seq, LANES), lambda b: (b, 0, 0, 0)),
                   pl.BlockSpec((None, FOX_HEADS, seq), lambda b: (b, 0, 0))),
        scratch_shapes=[pltpu.VMEM((seq, LANES), F32)],
        compiler_params=_params("parallel"),
        name="fox_gate",
    )(ff, bias)


def _dot_tn(a, b):
    return lax.dot_general(a, b, (((0,), (0,)), ((), ())), preferred_element_type=F32)


def _flash_attend(i, heads, logits, values, ft2, visible, finish,
                  sa_ref, sb_ref, m_ref, l_ref, acc_ref):
    def step(h, s, j):
        m = m_ref[h]
        m_new = jnp.maximum(m, jnp.max(s, axis=0, keepdims=True) + ft2[h])
        alpha = jnp.exp2(m - m_new)
        p = jnp.exp2(s - (m_new - ft2[h]))
        m_ref[h] = m_new
        l_ref[h] = alpha * l_ref[h] + jnp.sum(p, axis=0, keepdims=True)
        acc_ref[h] = alpha * acc_ref[h] + _dot_tn(values(h, j), p.astype(BF16))

    for h in heads:
        m_ref[h] = jnp.full(m_ref.shape[1:], NEG_INF, F32)
        l_ref[h] = jnp.zeros(l_ref.shape[1:], F32)
        acc_ref[h] = jnp.zeros(acc_ref.shape[1:], F32)
        sa_ref[h] = logits(h, 0)

    def pair(jj, carry):
        j = 2 * jj
        for h in heads:
            sb_ref[h] = logits(h, j + 1)
            step(h, sa_ref[h], j)
        for h in heads:
            sa_ref[h] = logits(h, j + 2)
            step(h, sb_ref[h], j + 1)
        return carry

    lax.fori_loop(0, i // 2, pair, 0)

    def last(s_ref):
        for h in heads:
            step(h, jnp.where(visible, s_ref[h], NEG_INF), i)
            finish(h, acc_ref[h] / l_ref[h])

    @pl.when(i % 2 == 0)
    def _():
        last(sa_ref)

    @pl.when(i % 2 == 1)
    def _():
        for h in heads:
            sb_ref[h] = logits(h, i)
            step(h, sa_ref[h], i - 1)
        last(sb_ref)


def _flash_scratch(hb, width, tq, cols):
    return [pltpu.VMEM((hb, tq, cols), F32), pltpu.VMEM((hb, tq, cols), F32),
            pltpu.VMEM((hb, 1, cols), F32), pltpu.VMEM((hb, 1, cols), F32),
            pltpu.VMEM((hb, width, cols), F32)]


def _fox_kernel(q_ref, k_ref, v_ref, fb_ref, fr_ref, o_ref, *scratch, tq, hb):
    i = pl.program_id(2)
    heads = range(hb)
    ft2 = [fr_ref[h, pl.ds(i, 1), :] for h in heads]

    def logits(h, j):
        start = pl.multiple_of(j * tq, tq)
        hs = slice(h * FOX_DIM, (h + 1) * FOX_DIM)
        st = _dot_nt(k_ref[pl.ds(start, tq), hs], q_ref[:, hs])
        fb = fb_ref[h, pl.ds(start, tq), :]
        return st - jnp.concatenate([fb] * (tq // LANES), axis=1)

    def values(h, j):
        return v_ref[pl.ds(pl.multiple_of(j * tq, tq), tq), h * FOX_DIM:(h + 1) * FOX_DIM]

    def finish(h, o_t):
        o_ref[:, h * FOX_DIM:(h + 1) * FOX_DIM] = o_t.T.astype(BF16)

    krow = lax.broadcasted_iota(jnp.int32, (tq, tq), 0)
    qcol = lax.broadcasted_iota(jnp.int32, (tq, tq), 1)
    _flash_attend(i, heads, logits, values, ft2, krow <= qcol, finish, *scratch)


def _fox_attn(proj, fb, frow, *, batch, seq, tq=256, hb=8):
    proj3 = proj.reshape(batch, seq, PROJ_W)
    frow4 = frow.reshape(batch, FOX_HEADS, seq // tq, tq)
    w = hb * FOX_DIM
    groups = FOX_HEADS // hb
    out = pl.pallas_call(
        functools.partial(_fox_kernel, tq=tq, hb=hb),
        out_shape=jax.ShapeDtypeStruct((batch, seq, FOX_W), BF16),
        grid=(batch, groups, seq // tq),
        in_specs=[
            pl.BlockSpec((None, tq, w), lambda b, g, i: (b, i, g)),
            pl.BlockSpec((None, seq, w), lambda b, g, i: (b, 0, groups + g)),
            pl.BlockSpec((None, seq, w), lambda b, g, i: (b, 0, 2 * groups + g)),
            pl.BlockSpec((None, hb, seq, LANES), lambda b, g, i: (b, g, 0, 0)),
            pl.BlockSpec((None, hb, seq // tq, tq), lambda b, g, i: (b, g, 0, 0)),
        ],
        out_specs=pl.BlockSpec((None, tq, w), lambda b, g, i: (b, i, g)),
        scratch_shapes=_flash_scratch(hb, FOX_DIM, tq, tq),
        compiler_params=_params("parallel", "parallel", "arbitrary"),
        name="fox_attn",
    )(proj3, proj3, proj3, fb, frow4)
    return out.reshape(batch * seq, FOX_W)


def _diff_kernel(lamv_ref, g_ref, q_ref, k_ref, v_ref, o_ref, *scratch, tq, hb, lam_init):
    i = pl.program_id(2)
    heads = range(hb)
    lv = lamv_ref[...]
    lam = (jnp.exp(jnp.sum(lv[0:1] * lv[1:2], axis=1, keepdims=True))
           - jnp.exp(jnp.sum(lv[2:3] * lv[3:4], axis=1, keepdims=True)) + lam_init)

    lane = lax.broadcasted_iota(jnp.int32, (tq, LANES), 1)
    qs = []
    for h in heads:
        q = q_ref[:, h * LANES:(h + 1) * LANES].astype(F32)
        qs.append(jnp.concatenate([jnp.where(lane < DIFF_QK_DIM, q, 0.0),
                                   jnp.where(lane >= DIFF_QK_DIM, q, 0.0)], axis=0).astype(BF16))
    zero = jnp.zeros((1, 2 * tq), F32)

    def logits(h, j):
        start = pl.multiple_of(j * tq, tq)
        return _dot_nt(k_ref[pl.ds(start, tq), h * LANES:(h + 1) * LANES], qs[h])

    def values(h, j):
        return v_ref[pl.ds(pl.multiple_of(j * tq, tq), tq), h * LANES:(h + 1) * LANES]

    def finish(h, o_t):
        yd = (o_t[:, :tq] - lam * o_t[:, tq:]).T
        o_ref[:, h * LANES:(h + 1) * LANES] = (
            _rms(yd, g_ref[...]) * (1.0 - lam_init)).astype(BF16)

    krow = lax.broadcasted_iota(jnp.int32, (tq, 2 * tq), 0)
    qcol = lax.broadcasted_iota(jnp.int32, (tq, 2 * tq), 1)
    qcol = jnp.where(qcol >= tq, qcol - tq, qcol)
    visible = krow // CHUNK <= qcol // CHUNK
    _flash_attend(i, heads, logits, values, [zero] * hb, visible, finish, *scratch)


def _diff_attn(proj, lamv, g, *, batch, seq, lam_init, tq=256, hb=4):
    proj3 = proj.reshape(batch, seq, PROJ_W)
    w = hb * LANES
    groups = DIFF_HEADS // hb
    dq, dk, dv = (DQ_BLK * LANES) // w, (DK_BLK * LANES) // w, (DV_BLK * LANES) // w
    out = pl.pallas_call(
        functools.partial(_diff_kernel, tq=tq, hb=hb, lam_init=lam_init),
        out_shape=jax.ShapeDtypeStruct((batch, seq, DIFF_V_W), BF16),
        grid=(batch, groups, seq // tq),
        in_specs=[
            pl.BlockSpec((4, DIFF_QK_DIM), lambda b, g, i: (0, 0)),
            pl.BlockSpec((1, DIFF_V_DIM), lambda b, g, i: (0, 0)),
            pl.BlockSpec((None, tq, w), lambda b, g, i: (b, i, dq + g)),
            pl.BlockSpec((None, seq, w), lambda b, g, i: (b, 0, dk + g)),
            pl.BlockSpec((None, seq, w), lambda b, g, i: (b, 0, dv + g)),
        ],
        out_specs=pl.BlockSpec((None, tq, w), lambda b, g, i: (b, i, g)),
        scratch_shapes=_flash_scratch(hb, DIFF_V_DIM, tq, 2 * tq),
        compiler_params=_params("parallel", "parallel", "arbitrary"),
        name="diff_attn",
    )(lamv, g, proj3, proj3, proj3)
    return out.reshape(batch * seq, DIFF_V_W)


def _mem_kv_kernel(mem_ref, g_ref, w_ref, o_ref):
    o_ref[...] = _dot(_rms(mem_ref[...], g_ref[...]).astype(BF16), w_ref[...]).astype(BF16)


def _mem_kv(mem, g, w):
    batch, n_mem, d = mem.shape
    return pl.pallas_call(
        _mem_kv_kernel,
        out_shape=jax.ShapeDtypeStruct((batch, n_mem, 2 * MEM_W), BF16),
        grid=(batch,),
        in_specs=[pl.BlockSpec((None, n_mem, d), lambda b: (b, 0, 0)),
                  pl.BlockSpec((1, d), lambda b: (0, 0)),
                  pl.BlockSpec((d, 2 * MEM_W), lambda b: (0, 0))],
        out_specs=pl.BlockSpec((None, n_mem, 2 * MEM_W), lambda b: (b, 0, 0)),
        compiler_params=_params("parallel"),
        name="mem_kv",
    )(mem, g, w)


def _mem_attn_kernel(q_ref, kv_ref, o_ref):
    scale = MEM_DIM ** -0.5
    for h in range(MEM_HEADS):
        q = q_ref[:, h * MEM_DIM:(h + 1) * MEM_DIM]
        k = kv_ref[:, h * MEM_DIM:(h + 1) * MEM_DIM]
        v = kv_ref[:, MEM_W + h * MEM_DIM:MEM_W + (h + 1) * MEM_DIM]
        s = _dot_nt(q, k) * scale
        p = jnp.exp(s - jnp.max(s, axis=1, keepdims=True))
        l = jnp.sum(p, axis=1, keepdims=True)
        o_ref[:, h * MEM_DIM:(h + 1) * MEM_DIM] = (_dot(p.astype(BF16), v) / l).astype(BF16)


def _mem_attn(proj, mkv, *, batch, seq, tq=512):
    proj3 = proj.reshape(batch, seq, PROJ_W)
    n_mem = mkv.shape[1]
    out = pl.pallas_call(
        _mem_attn_kernel,
        out_shape=jax.ShapeDtypeStruct((batch, seq, MEM_W), BF16),
        grid=(batch, seq // tq),
        in_specs=[pl.BlockSpec((None, tq, MEM_W), lambda b, i: (b, i, MQ_COL // MEM_W)),
                  pl.BlockSpec((None, n_mem, 2 * MEM_W), lambda b, i: (b, 0, 0))],
        out_specs=pl.BlockSpec((None, tq, MEM_W), lambda b, i: (b, i, 0)),
        compiler_params=_params("parallel", "arbitrary"),
        name="mem_attn",
    )(proj3, mkv)
    return out.reshape(batch * seq, MEM_W)


def _merge_kernel(x_ref, gpre_ref, yf_ref, yd_ref, ym_ref, wgf_ref, wgd_ref, wgm_ref,
                  bf_ref, bd_ref, bm_ref, wf_ref, wd_ref, wm_ref, wo_ref, gpost_ref,
                  o_ref, h_ref, acc_ref):
    n = pl.program_id(1)

    @pl.when(n == 0)
    def _():
        def emit(sl, y):
            h_ref[sl, :] = y.astype(BF16)
            acc_ref[sl, :] = jnp.zeros_like(y)

        _rms_rows(x_ref, gpre_ref[...], emit, BF16_ROWS)

    h = h_ref[...]

    def gated(wg_ref, b_ref, y_ref, w_ref):
        z = _dot(h, wg_ref[...]) + b_ref[...]
        return (1.0 / (1.0 + jnp.exp(-z))) * _dot(y_ref[...], w_ref[...])

    merged = (gated(wgf_ref, bf_ref, yf_ref, wf_ref) + gated(wgd_ref, bd_ref, yd_ref, wd_ref)
              + gated(wgm_ref, bm_ref, ym_ref, wm_ref))
    acc_ref[...] += _dot(merged.astype(BF16), wo_ref[...])

    @pl.when(n == pl.num_programs(1) - 1)
    def _():
        def emit(sl, y):
            o_ref[sl, :] = x_ref[sl, :] + y

        _rms_rows(acc_ref, gpost_ref[...], emit, F32_ROWS)


def _merge(x, g_pre, y_fox, y_diff, y_mem, w_gate, b_gate, w_fox, w_diff, w_mem, w_out, g_post,
           *, tm=512, tn=512):
    t, d = x.shape
    nt = d // tn
    row = lambda m, n: (m, 0)
    return pl.pallas_call(
        _merge_kernel,
        out_shape=jax.ShapeDtypeStruct((t, d), F32),
        grid=(t // tm, nt),
        in_specs=[
            pl.BlockSpec((tm, d), row),
            pl.BlockSpec((1, d), lambda m, n: (0, 0)),
            pl.BlockSpec((tm, FOX_W), row),
            pl.BlockSpec((tm, DIFF_V_W), row),
            pl.BlockSpec((tm, MEM_W), row),
            pl.BlockSpec((d, tn), lambda m, n: (0, n)),
            pl.BlockSpec((d, tn), lambda m, n: (0, nt + n)),
            pl.BlockSpec((d, tn), lambda m, n: (0, 2 * nt + n)),
            pl.BlockSpec((1, tn), lambda m, n: (0, n)),
            pl.BlockSpec((1, tn), lambda m, n: (0, nt + n)),
            pl.BlockSpec((1, tn), lambda m, n: (0, 2 * nt + n)),
            pl.BlockSpec((FOX_W, tn), lambda m, n: (0, n)),
            pl.BlockSpec((DIFF_V_W, tn), lambda m, n: (0, n)),
            pl.BlockSpec((MEM_W, tn), lambda m, n: (0, n)),
            pl.BlockSpec((tn, d), lambda m, n: (n, 0)),
            pl.BlockSpec((1, d), lambda m, n: (0, 0)),
        ],
        out_specs=pl.BlockSpec((tm, d), row),
        scratch_shapes=[pltpu.VMEM((tm, d), BF16), pltpu.VMEM((tm, d), F32)],
        compiler_params=_params("parallel", "arbitrary"),
        name="merge",
    )(x, g_pre, y_fox, y_diff, y_mem, w_gate, w_gate, w_gate, b_gate, b_gate, b_gate,
      w_fox, w_diff, w_mem, w_out, g_post)


def kernel(x, mem, ffn1_pre_g, ffn1_w_gate, ffn1_w_up, ffn1_w_down, ffn1_post_g, mix_pre_g, w_in, fox_f_bias, diff_lambda_q1, diff_lambda_k1, diff_lambda_q2, diff_lambda_k2, diff_head_g, mem_norm_g, w_mem_kv, w_branch_fox, w_branch_diff, w_branch_mem, w_merge_gate, b_merge_gate, w_out, mix_post_g, ffn2_pre_g, ffn2_w_gate, ffn2_w_up, ffn2_w_down, ffn2_post_g):
    batch, seq, d = x.shape
    depth = w_in.shape[0]
    xt = x.reshape(batch * seq, d)
    cos_t, sa_t, sb_t = _rope_tables(seq)

    def row(v):
        return v.reshape(1, -1).astype(F32)

    for l in range(depth):
        bf = lambda w: w[l].astype(BF16)
        later = [ffn2_w_gate, ffn2_w_up, ffn2_w_down, w_merge_gate]
        cast = [(w, l, ((0, w.shape[2]),)) for w in later]
        cast.append((w_in, l, ((0, FF_COL + LANES), (FF_COL + FOX_HEADS, w_in.shape[2]))))
        xt, cast_out = _ffn(xt, row(ffn1_pre_g[l]), bf(ffn1_w_gate), bf(ffn1_w_up),
                            bf(ffn1_w_down), row(ffn1_post_g[l]), cast=cast)
        w2_gate, w2_up, w2_down, wb_merge, w_head, w_tail = cast_out

        small = [w_out, w_branch_fox, w_branch_diff, w_branch_mem]
        proj, ff, (wb_out, wb_fox, wb_diff, wb_mem) = _mix_proj(
            xt, row(mix_pre_g[l]), w_head, w_tail, cos_t, sa_t, sb_t, seq=seq,
            cast=[(w, l) for w in small])

        bias = jnp.pad(fox_f_bias[l].astype(F32), (0, LANES - FOX_HEADS)).reshape(1, LANES)
        fb, frow = _fox_gate(ff, bias, batch=batch, seq=seq)
        y_fox = _fox_attn(proj, fb, frow, batch=batch, seq=seq)

        lam_init = 0.8 - 0.6 * math.exp(-0.3 * l)
        lamv = jnp.stack([diff_lambda_q1[l], diff_lambda_k1[l], diff_lambda_q2[l],
                          diff_lambda_k2[l]]).astype(F32)
        y_diff = _diff_attn(proj, lamv, row(diff_head_g[l]), batch=batch, seq=seq,
                            lam_init=lam_init)

        mkv = _mem_kv(mem, row(mem_norm_g[l]), bf(w_mem_kv))
        y_mem = _mem_attn(proj, mkv, batch=batch, seq=seq)

        xt = _merge(xt, row(mix_pre_g[l]), y_fox, y_diff, y_mem, wb_merge,
                    row(b_merge_gate[l]), wb_fox, wb_diff, wb_mem, wb_out, row(mix_post_g[l]))

        xt, _ = _ffn(xt, row(ffn2_pre_g[l]), w2_gate, w2_up, w2_down, row(ffn2_post_g[l]))

    return xt.reshape(batch, seq, d)
```

```python
import functools
import math

import jax
import jax.numpy as jnp
from jax import lax
from jax.experimental import pallas as pl
from jax.experimental.pallas import tpu as pltpu

D_MODEL = 2048
CHUNK = 64
EPS = 1e-6
ROPE_THETA = 500000.0

FOX_HEADS = 8
FOX_DIM = 128
FOX_W = FOX_HEADS * FOX_DIM

DIFF_HEADS = 4
DIFF_QK_DIM = 64
DIFF_V_DIM = 2 * DIFF_QK_DIM
DIFF_QK_W = DIFF_HEADS * 2 * DIFF_QK_DIM
DIFF_V_W = DIFF_HEADS * DIFF_V_DIM
ROPE_DIM = DIFF_QK_DIM // 4

MEM_HEADS = 4
MEM_DIM = 128
MEM_W = MEM_HEADS * MEM_DIM

LANES = 128
F32_ROWS = 8
BF16_ROWS = 16
RMS_ROWS_IN_FLIGHT = 128
PROJ_W = 3 * FOX_W + 2 * DIFF_QK_W + DIFF_V_W + MEM_W
FQ_BLK, FK_BLK, FV_BLK = 0, FOX_HEADS, 2 * FOX_HEADS
DQ_BLK = 3 * FOX_HEADS
DK_BLK = DQ_BLK + DIFF_HEADS
DV_BLK = DK_BLK + DIFF_HEADS
MQ_COL = 3 * FOX_W + 2 * DIFF_QK_W + DIFF_V_W
FF_COL = 3 * FOX_W

VMEM_LIMIT = 56 * 1024 * 1024
BF16 = jnp.bfloat16
F32 = jnp.float32
NEG_INF = float("-inf")
LOG2E = math.log2(math.e)
FOX_QSCALE = FOX_DIM ** -0.5 * LOG2E
DIFF_QSCALE = DIFF_QK_DIM ** -0.5 * LOG2E


def _dot(a, b):
    return jnp.dot(a, b, preferred_element_type=F32)


def _dot_nt(a, b):
    return lax.dot_general(a, b, (((1,), (1,)), ((), ())), preferred_element_type=F32)


def _rms(x, g):
    return x * lax.rsqrt(jnp.mean(x * x, axis=-1, keepdims=True) + EPS) * g


def _rms_rows(src_ref, g, emit, rows):
    def body(r, carry):
        sl = pl.ds(pl.multiple_of(r * rows, rows), rows)
        emit(sl, _rms(src_ref[sl, :], g))
        return carry

    lax.fori_loop(0, src_ref.shape[0] // rows, body, 0, unroll=RMS_ROWS_IN_FLIGHT // rows)


def _params(*sem):
    return pltpu.CompilerParams(dimension_semantics=sem, vmem_limit_bytes=VMEM_LIMIT)


def _ffn_kernel(*refs, transposed):
    n_cast = len(transposed)
    x_ref, gpre_ref, wg_ref, wu_ref, wd_ref, gpost_ref = refs[:6]
    o_ref = refs[6 + n_cast]
    h_ref, acc_ref = refs[-2:]
    f = pl.program_id(1)

    @pl.when(f == 0)
    def _():
        def emit(sl, y):
            h_ref[sl, :] = y.astype(BF16)
            acc_ref[sl, :] = jnp.zeros_like(y)

        _rms_rows(x_ref, gpre_ref[...], emit, BF16_ROWS)

    jobs = list(zip(refs[6:6 + n_cast], refs[7 + n_cast:7 + 2 * n_cast], transposed))

    def run_casts(part, parts=3):
        for src_ref, dst_ref, flip in jobs[part::parts]:
            blk = src_ref[...]
            dst_ref[...] = (blk.T if flip else blk).astype(BF16)

    h = h_ref[...]
    run_casts(0)
    g = _dot(h, wg_ref[...])
    run_casts(1)
    u = _dot(h, wu_ref[...])
    run_casts(2)
    a = (g * (1.0 / (1.0 + jnp.exp(-g)))) * u
    acc_ref[...] += _dot(a.astype(BF16), wd_ref[...])

    @pl.when(f == pl.num_programs(1) - 1)
    def _():
        def emit(sl, y):
            o_ref[sl, :] = x_ref[sl, :] + y

        _rms_rows(acc_ref, 0.5 * gpost_ref[...], emit, F32_ROWS)


def _ffn(x, g_pre, w_gate, w_up, w_down, g_post, *, cast=(), cast_t=(), tm=512, tf=512):
    t, d = x.shape
    d_ff = w_gate.shape[1]
    m_tiles, f_steps = t // tm, d_ff // tf
    in_specs = [
        pl.BlockSpec((tm, d), lambda m, f: (m, 0)),
        pl.BlockSpec((1, d), lambda m, f: (0, 0)),
        pl.BlockSpec((d, tf), lambda m, f: (0, f)),
        pl.BlockSpec((d, tf), lambda m, f: (0, f)),
        pl.BlockSpec((tf, d), lambda m, f: (f, 0)),
        pl.BlockSpec((1, d), lambda m, f: (0, 0)),
    ]
    out_specs = [pl.BlockSpec((tm, d), lambda m, f: (m, 0))]
    out_shape = [jax.ShapeDtypeStruct((t, d), F32)]
    for arr, layer in cast:
        _, rows, cols = arr.shape
        br = BF16_ROWS
        while rows // m_tiles // br > f_steps:
            br *= 2
        steps = rows // m_tiles // br
        assert steps * br * m_tiles == rows

        def index(m, f, steps=steps):
            return m * steps + jnp.minimum(f, steps - 1)

        in_specs.append(pl.BlockSpec((None, br, cols),
                                     lambda m, f, index=index, layer=layer: (layer, index(m, f), 0)))
        out_specs.append(pl.BlockSpec((br, cols), lambda m, f, index=index: (index(m, f), 0)))
        out_shape.append(jax.ShapeDtypeStruct((rows, cols), BF16))
    for arr, lo, hi in cast_t:
        _, cols = arr.shape
        n_blk = (hi - lo) // LANES
        assert n_blk * LANES == hi - lo and n_blk <= m_tiles * f_steps

        def block(m, f, n_blk=n_blk):
            return jnp.minimum(m * f_steps + f, n_blk - 1)

        in_specs.append(pl.BlockSpec(
            (pl.Element(LANES), pl.Element(cols)),
            lambda m, f, block=block, lo=lo: (
                pl.multiple_of(lo + LANES * block(m, f), math.gcd(lo, LANES)), 0)))
        out_specs.append(pl.BlockSpec((cols, LANES), lambda m, f, block=block: (0, block(m, f))))
        out_shape.append(jax.ShapeDtypeStruct((cols, hi - lo), BF16))
    outs = pl.pallas_call(
        functools.partial(_ffn_kernel, transposed=(False,) * len(cast) + (True,) * len(cast_t)),
        out_shape=out_shape,
        grid=(m_tiles, f_steps),
        in_specs=in_specs,
        out_specs=out_specs,
        scratch_shapes=[pltpu.VMEM((tm, d), BF16), pltpu.VMEM((tm, d), F32)],
        compiler_params=_params("parallel", "arbitrary"),
        name="ffn",
    )(x, g_pre, w_gate, w_up, w_down, g_post, *[c[0] for c in cast], *[c[0] for c in cast_t])
    return outs[0], outs[1:]


def _proj_kernel(*refs, tn, n_cast):
    x_ref, g_ref, wa_ref, wff_ref, wb_ref, cos_ref, sa_ref, sb_ref = refs[:8]
    o_ref, ff_ref = refs[8 + n_cast:10 + n_cast]
    h_ref = refs[-1]
    for src_ref, dst_ref in zip(refs[8:8 + n_cast], refs[10 + n_cast:10 + 2 * n_cast]):
        dst_ref[...] = src_ref[...].astype(BF16)
    h_ref[...] = _rms(x_ref[...], g_ref[...]).astype(BF16)
    ff_ref[...] = _dot(h_ref[...], wff_ref[...])
    half = ROPE_DIM // 2
    na = wa_ref.shape[1] // tn
    for n in range(na + wb_ref.shape[1] // tn):
        w_tile = (wa_ref[:, n * tn:(n + 1) * tn] if n < na
                  else wb_ref[:, (n - na) * tn:(n - na + 1) * tn])
        y = _dot(h_ref[...], w_tile)
        first_blk = n * tn // LANES
        if first_blk < FK_BLK:
            y = y * FOX_QSCALE
        if DQ_BLK <= first_blk < DV_BLK:
            qscale = DIFF_QSCALE if first_blk < DK_BLK else 1.0
            for j in range(tn // LANES):
                blk = y[:, j * LANES:(j + 1) * LANES]
                rot = (blk * cos_ref[...] + pltpu.roll(blk, LANES - half, axis=1) * sa_ref[...]
                       + pltpu.roll(blk, half, axis=1) * sb_ref[...])
                o_ref[:, n * tn + j * LANES:n * tn + (j + 1) * LANES] = (rot * qscale).astype(BF16)
        else:
            o_ref[:, n * tn:(n + 1) * tn] = y.astype(BF16)


def _mix_proj(x, g, w_head, w_tail, cos_t, sa_t, sb_t, *, seq, cast=(), tm=512, tn=512):
    t, d = x.shape
    n_out = FF_COL + w_tail.shape[1]
    assert all((blk * LANES) % tn == 0 for blk in (FK_BLK, DQ_BLK, DK_BLK, DV_BLK))
    assert w_head.shape[1] == FF_COL + LANES and n_out == PROJ_W
    s_tiles = seq // tm
    m_tiles = t // tm
    resident = dict(pipeline_mode=pl.Buffered(1))
    in_specs = [
        pl.BlockSpec((tm, d), lambda m: (m, 0)),
        pl.BlockSpec((1, d), lambda m: (0, 0)),
        pl.BlockSpec((d, FF_COL), lambda m: (0, 0), **resident),
        pl.BlockSpec((d, LANES), lambda m: (0, FF_COL // LANES), **resident),
        pl.BlockSpec((d, w_tail.shape[1]), lambda m: (0, 0), **resident),
        pl.BlockSpec((tm, LANES), lambda m: (m % s_tiles, 0)),
        pl.BlockSpec((tm, LANES), lambda m: (m % s_tiles, 0)),
        pl.BlockSpec((tm, LANES), lambda m: (m % s_tiles, 0)),
    ]
    out_specs = [pl.BlockSpec((tm, n_out), lambda m: (m, 0)),
                 pl.BlockSpec((tm, LANES), lambda m: (m, 0))]
    out_shape = [jax.ShapeDtypeStruct((t, n_out), BF16), jax.ShapeDtypeStruct((t, LANES), F32)]
    for arr, layer in cast:
        _, rows, cols = arr.shape
        br = rows // m_tiles
        assert br * m_tiles == rows and br % BF16_ROWS == 0
        in_specs.append(pl.BlockSpec((None, br, cols), lambda m, layer=layer: (layer, m, 0)))
        out_specs.append(pl.BlockSpec((br, cols), lambda m: (m, 0)))
        out_shape.append(jax.ShapeDtypeStruct((rows, cols), BF16))
    outs = pl.pallas_call(
        functools.partial(_proj_kernel, tn=tn, n_cast=len(cast)),
        out_shape=out_shape,
        grid=(m_tiles,),
        in_specs=in_specs,
        out_specs=out_specs,
        scratch_shapes=[pltpu.VMEM((tm, d), BF16)],
        compiler_params=_params("parallel"),
        name="mix_proj",
    )(x, g, w_head, w_head, w_tail, cos_t, sa_t, sb_t, *[arr for arr, _ in cast])
    return outs[0], outs[1], outs[2:]


def _rope_tables(seq):
    half = ROPE_DIM // 2
    pos = jnp.arange(seq, dtype=F32)
    inv_freq = ROPE_THETA ** (-jnp.arange(0, ROPE_DIM, 2, dtype=F32) / ROPE_DIM)
    ang = pos[:, None] * inv_freq[None, :]
    cos, sin = jnp.cos(ang), jnp.sin(ang)
    ones = jnp.ones((seq, DIFF_QK_DIM - ROPE_DIM), F32)
    zeros_h = jnp.zeros((seq, half), F32)
    zeros_r = jnp.zeros((seq, DIFF_QK_DIM - ROPE_DIM), F32)
    cos_m = jnp.concatenate([cos, cos, ones], axis=1)
    sa_m = jnp.concatenate([-sin, zeros_h, zeros_r], axis=1)
    sb_m = jnp.concatenate([zeros_h, sin, zeros_r], axis=1)
    rep = LANES // DIFF_QK_DIM
    return (jnp.tile(cos_m, (1, rep)), jnp.tile(sa_m, (1, rep)), jnp.tile(sb_m, (1, rep)))


def _fgate_kernel(ff_ref, bias_ref, fb_ref, frow_ref, fcol_ref, *, cb):
    seq = ff_ref.shape[0]
    z = ff_ref[...] + bias_ref[...]
    lf = jnp.minimum(z, 0.0) - jnp.log1p(jnp.exp(-jnp.abs(z)))
    r = lax.broadcasted_iota(jnp.int32, (cb, cb), 0)
    c = lax.broadcasted_iota(jnp.int32, (cb, cb), 1)
    tri = (r >= c).astype(F32)
    carry = jnp.zeros((1, LANES), F32)
    for i in range(seq // cb):
        cs = jnp.dot(tri, lf[i * cb:(i + 1) * cb], precision=lax.Precision.HIGHEST,
                     preferred_element_type=F32) + carry
        carry = cs[cb - 1:cb, :]
        cs2 = cs * LOG2E
        fcol_ref[i * cb:(i + 1) * cb, :] = cs2
        for h in range(FOX_HEADS):
            fb_ref[h, i * cb:(i + 1) * cb, :] = jnp.broadcast_to(cs2[:, h:h + 1], (cb, LANES))
    frow_ref[...] = fcol_ref[...].T[:FOX_HEADS]


def _fox_gate(ff, bias, *, batch, seq, cb=256):
    ff = ff.reshape(batch, seq, LANES)
    return pl.pallas_call(
        functools.partial(_fgate_kernel, cb=cb),
        out_shape=(jax.ShapeDtypeStruct((batch, FOX_HEADS, seq, LANES), F32),
                   jax.ShapeDtypeStruct((batch, FOX_HEADS, seq), F32)),
        grid=(batch,),
        in_specs=[pl.BlockSpec((None, seq, LANES), lambda b: (b, 0, 0)),
                  pl.BlockSpec((1, LANES), lambda b: (0, 0))],
        out_specs=(pl.BlockSpec((None, FOX_HEADS, seq, LANES), lambda b: (b, 0, 0, 0)),
                   pl.BlockSpec((None, FOX_HEADS, seq), lambda b: (b, 0, 0))),
        scratch_shapes=[pltpu.VMEM((seq, LANES), F32)],
        compiler_params=_params("parallel"),
        name="fox_gate",
    )(ff, bias)


def _dot_tn(a, b):
    return lax.dot_general(a, b, (((0,), (0,)), ((), ())), preferred_element_type=F32)


def _flash_attend(i, heads, logits, values, ft2, visible, finish,
                  sa_ref, sb_ref, m_ref, l_ref, acc_ref):
    def step(h, s, j):
        m = m_ref[h]
        m_new = jnp.maximum(m, jnp.max(s, axis=0, keepdims=True) + ft2[h])
        alpha = jnp.exp2(m - m_new)
        p = jnp.exp2(s - (m_new - ft2[h]))
        m_ref[h] = m_new
        l_ref[h] = alpha * l_ref[h] + jnp.sum(p, axis=0, keepdims=True)
        acc_ref[h] = alpha * acc_ref[h] + _dot_tn(values(h, j), p.astype(BF16))

    for h in heads:
        m_ref[h] = jnp.full(m_ref.shape[1:], NEG_INF, F32)
        l_ref[h] = jnp.zeros(l_ref.shape[1:], F32)
        acc_ref[h] = jnp.zeros(acc_ref.shape[1:], F32)
        sa_ref[h] = logits(h, 0)

    def pair(jj, carry):
        j = 2 * jj
        for h in heads:
            sb_ref[h] = logits(h, j + 1)
            step(h, sa_ref[h], j)
        for h in heads:
            sa_ref[h] = logits(h, j + 2)
            step(h, sb_ref[h], j + 1)
        return carry

    lax.fori_loop(0, i // 2, pair, 0)

    def last(s_ref):
        for h in heads:
            step(h, jnp.where(visible, s_ref[h], NEG_INF), i)
            finish(h, acc_ref[h] / l_ref[h])

    @pl.when(i % 2 == 0)
    def _():
        last(sa_ref)

    @pl.when(i % 2 == 1)
    def _():
        for h in heads:
            sb_ref[h] = logits(h, i)
            step(h, sa_ref[h], i - 1)
        last(sb_ref)


def _flash_scratch(hb, width, tq, cols):
    return [pltpu.VMEM((hb, tq, cols), F32), pltpu.VMEM((hb, tq, cols), F32),
            pltpu.VMEM((hb, 1, cols), F32), pltpu.VMEM((hb, 1, cols), F32),
            pltpu.VMEM((hb, width, cols), F32)]


def _fox_kernel(q_ref, k_ref, v_ref, fb_ref, fr_ref, o_ref, *scratch, tq, hb):
    i = pl.program_id(2)
    heads = range(hb)
    ft2 = [fr_ref[h, pl.ds(i, 1), :] for h in heads]

    def logits(h, j):
        start = pl.multiple_of(j * tq, tq)
        hs = slice(h * FOX_DIM, (h + 1) * FOX_DIM)
        st = _dot_nt(k_ref[pl.ds(start, tq), hs], q_ref[:, hs])
        fb = fb_ref[h, pl.ds(start, tq), :]
        return st - jnp.concatenate([fb] * (tq // LANES), axis=1)

    def values(h, j):
        return v_ref[pl.ds(pl.multiple_of(j * tq, tq), tq), h * FOX_DIM:(h + 1) * FOX_DIM]

    def finish(h, o_t):
        o_ref[:, h * FOX_DIM:(h + 1) * FOX_DIM] = o_t.T.astype(BF16)

    krow = lax.broadcasted_iota(jnp.int32, (tq, tq), 0)
    qcol = lax.broadcasted_iota(jnp.int32, (tq, tq), 1)
    _flash_attend(i, heads, logits, values, ft2, krow <= qcol, finish, *scratch)


def _fox_attn(proj, fb, frow, *, batch, seq, tq=256, hb=8):
    proj3 = proj.reshape(batch, seq, PROJ_W)
    frow4 = frow.reshape(batch, FOX_HEADS, seq // tq, tq)
    w = hb * FOX_DIM
    groups = FOX_HEADS // hb
    out = pl.pallas_call(
        functools.partial(_fox_kernel, tq=tq, hb=hb),
        out_shape=jax.ShapeDtypeStruct((batch, seq, FOX_W), BF16),
        grid=(batch, groups, seq // tq),
        in_specs=[
            pl.BlockSpec((None, tq, w), lambda b, g, i: (b, i, g)),
            pl.BlockSpec((None, seq, w), lambda b, g, i: (b, 0, groups + g)),
            pl.BlockSpec((None, seq, w), lambda b, g, i: (b, 0, 2 * groups + g)),
            pl.BlockSpec((None, hb, seq, LANES), lambda b, g, i: (b, g, 0, 0)),
            pl.BlockSpec((None, hb, seq // tq, tq), lambda b, g, i: (b, g, 0, 0)),
        ],
        out_specs=pl.BlockSpec((None, tq, w), lambda b, g, i: (b, i, g)),
        scratch_shapes=_flash_scratch(hb, FOX_DIM, tq, tq),
        compiler_params=_params("parallel", "parallel", "arbitrary"),
        name="fox_attn",
    )(proj3, proj3, proj3, fb, frow4)
    return out.reshape(batch * seq, FOX_W)


def _diff_kernel(lamv_ref, g_ref, q_ref, k_ref, v_ref, o_ref, *scratch, tq, hb, lam_init):
    i = pl.program_id(2)
    heads = range(hb)
    lv = lamv_ref[...]
    lam = (jnp.exp(jnp.sum(lv[0:1] * lv[1:2], axis=1, keepdims=True))
           - jnp.exp(jnp.sum(lv[2:3] * lv[3:4], axis=1, keepdims=True)) + lam_init)

    lane = lax.broadcasted_iota(jnp.int32, (tq, LANES), 1)
    qs = []
    for h in heads:
        q = q_ref[:, h * LANES:(h + 1) * LANES].astype(F32)
        qs.append(jnp.concatenate([jnp.where(lane < DIFF_QK_DIM, q, 0.0),
                                   jnp.where(lane >= DIFF_QK_DIM, q, 0.0)], axis=0).astype(BF16))
    zero = jnp.zeros((1, 2 * tq), F32)

    def logits(h, j):
        start = pl.multiple_of(j * tq, tq)
        return _dot_nt(k_ref[pl.ds(start, tq), h * LANES:(h + 1) * LANES], qs[h])

    def values(h, j):
        return v_ref[pl.ds(pl.multiple_of(j * tq, tq), tq), h * LANES:(h + 1) * LANES]

    def finish(h, o_t):
        yd = (o_t[:, :tq] - lam * o_t[:, tq:]).T
        o_ref[:, h * LANES:(h + 1) * LANES] = (
            _rms(yd, g_ref[...]) * (1.0 - lam_init)).astype(BF16)

    krow = lax.broadcasted_iota(jnp.int32, (tq, 2 * tq), 0)
    qcol = lax.broadcasted_iota(jnp.int32, (tq, 2 * tq), 1)
    qcol = jnp.where(qcol >= tq, qcol - tq, qcol)
    visible = krow // CHUNK <= qcol // CHUNK
    _flash_attend(i, heads, logits, values, [zero] * hb, visible, finish, *scratch)


def _diff_attn(proj, lamv, g, *, batch, seq, lam_init, tq=256, hb=4):
    proj3 = proj.reshape(batch, seq, PROJ_W)
    w = hb * LANES
    groups = DIFF_HEADS // hb
    dq, dk, dv = (DQ_BLK * LANES) // w, (DK_BLK * LANES) // w, (DV_BLK * LANES) // w
    out = pl.pallas_call(
        functools.partial(_diff_kernel, tq=tq, hb=hb, lam_init=lam_init),
        out_shape=jax.ShapeDtypeStruct((batch, seq, DIFF_V_W), BF16),
        grid=(batch, groups, seq // tq),
        in_specs=[
            pl.BlockSpec((4, DIFF_QK_DIM), lambda b, g, i: (0, 0)),
            pl.BlockSpec((1, DIFF_V_DIM), lambda b, g, i: (0, 0)),
            pl.BlockSpec((None, tq, w), lambda b, g, i: (b, i, dq + g)),
            pl.BlockSpec((None, seq, w), lambda b, g, i: (b, 0, dk + g)),
            pl.BlockSpec((None, seq, w), lambda b, g, i: (b, 0, dv + g)),
        ],
        out_specs=pl.BlockSpec((None, tq, w), lambda b, g, i: (b, i, g)),
        scratch_shapes=_flash_scratch(hb, DIFF_V_DIM, tq, 2 * tq),
        compiler_params=_params("parallel", "parallel", "arbitrary"),
        name="diff_attn",
    )(lamv, g, proj3, proj3, proj3)
    return out.reshape(batch * seq, DIFF_V_W)


def _mem_kv_kernel(mem_ref, g_ref, w_ref, o_ref):
    o_ref[...] = _dot(_rms(mem_ref[...], g_ref[...]).astype(BF16), w_ref[...]).astype(BF16)


def _mem_kv(mem, g, w):
    batch, n_mem, d = mem.shape
    return pl.pallas_call(
        _mem_kv_kernel,
        out_shape=jax.ShapeDtypeStruct((batch, n_mem, 2 * MEM_W), BF16),
        grid=(batch,),
        in_specs=[pl.BlockSpec((None, n_mem, d), lambda b: (b, 0, 0)),
                  pl.BlockSpec((1, d), lambda b: (0, 0)),
                  pl.BlockSpec((d, 2 * MEM_W), lambda b: (0, 0))],
        out_specs=pl.BlockSpec((None, n_mem, 2 * MEM_W), lambda b: (b, 0, 0)),
        compiler_params=_params("parallel"),
        name="mem_kv",
    )(mem, g, w)


def _mem_attn_kernel(q_ref, kv_ref, o_ref):
    scale = MEM_DIM ** -0.5
    for h in range(MEM_HEADS):
        q = q_ref[:, h * MEM_DIM:(h + 1) * MEM_DIM]
        k = kv_ref[:, h * MEM_DIM:(h + 1) * MEM_DIM]
        v = kv_ref[:, MEM_W + h * MEM_DIM:MEM_W + (h + 1) * MEM_DIM]
        s = _dot_nt(q, k) * scale
        p = jnp.exp(s - jnp.max(s, axis=1, keepdims=True))
        l = jnp.sum(p, axis=1, keepdims=True)
        o_ref[:, h * MEM_DIM:(h + 1) * MEM_DIM] = (_dot(p.astype(BF16), v) / l).astype(BF16)


def _mem_attn(proj, mkv, *, batch, seq, tq=512):
    proj3 = proj.reshape(batch, seq, PROJ_W)
    n_mem = mkv.shape[1]
    out = pl.pallas_call(
        _mem_attn_kernel,
        out_shape=jax.ShapeDtypeStruct((batch, seq, MEM_W), BF16),
        grid=(batch, seq // tq),
        in_specs=[pl.BlockSpec((None, tq, MEM_W), lambda b, i: (b, i, MQ_COL // MEM_W)),
                  pl.BlockSpec((None, n_mem, 2 * MEM_W), lambda b, i: (b, 0, 0))],
        out_specs=pl.BlockSpec((None, tq, MEM_W), lambda b, i: (b, i, 0)),
        compiler_params=_params("parallel", "arbitrary"),
        name="mem_attn",
    )(proj3, mkv)
    return out.reshape(batch * seq, MEM_W)


def _merge_kernel(x_ref, gpre_ref, yf_ref, yd_ref, ym_ref, wgf_ref, wgd_ref, wgm_ref,
                  bf_ref, bd_ref, bm_ref, wf_ref, wd_ref, wm_ref, wo_ref, gpost_ref,
                  o_ref, h_ref, acc_ref):
    n = pl.program_id(1)

    @pl.when(n == 0)
    def _():
        def emit(sl, y):
            h_ref[sl, :] = y.astype(BF16)
            acc_ref[sl, :] = jnp.zeros_like(y)

        _rms_rows(x_ref, gpre_ref[...], emit, BF16_ROWS)

    h = h_ref[...]

    def gated(wg_ref, b_ref, y_ref, w_ref):
        z = _dot(h, wg_ref[...]) + b_ref[...]
        return (1.0 / (1.0 + jnp.exp(-z))) * _dot(y_ref[...], w_ref[...])

    merged = (gated(wgf_ref, bf_ref, yf_ref, wf_ref) + gated(wgd_ref, bd_ref, yd_ref, wd_ref)
              + gated(wgm_ref, bm_ref, ym_ref, wm_ref))
    acc_ref[...] += _dot(merged.astype(BF16), wo_ref[...])

    @pl.when(n == pl.num_programs(1) - 1)
    def _():
        def emit(sl, y):
            o_ref[sl, :] = x_ref[sl, :] + y

        _rms_rows(acc_ref, gpost_ref[...], emit, F32_ROWS)


def _merge(x, g_pre, y_fox, y_diff, y_mem, w_gate, b_gate, w_fox, w_diff, w_mem, w_out, g_post,
           *, tm=512, tn=512):
    t, d = x.shape
    nt = d // tn
    row = lambda m, n: (m, 0)
    return pl.pallas_call(
        _merge_kernel,
        out_shape=jax.ShapeDtypeStruct((t, d), F32),
        grid=(t // tm, nt),
        in_specs=[
            pl.BlockSpec((tm, d), row),
            pl.BlockSpec((1, d), lambda m, n: (0, 0)),
            pl.BlockSpec((tm, FOX_W), row),
            pl.BlockSpec((tm, DIFF_V_W), row),
            pl.BlockSpec((tm, MEM_W), row),
            pl.BlockSpec((d, tn), lambda m, n: (0, n)),
            pl.BlockSpec((d, tn), lambda m, n: (0, nt + n)),
            pl.BlockSpec((d, tn), lambda m, n: (0, 2 * nt + n)),
            pl.BlockSpec((1, tn), lambda m, n: (0, n)),
            pl.BlockSpec((1, tn), lambda m, n: (0, nt + n)),
            pl.BlockSpec((1, tn), lambda m, n: (0, 2 * nt + n)),
            pl.BlockSpec((FOX_W, tn), lambda m, n: (0, n)),
            pl.BlockSpec((DIFF_V_W, tn), lambda m, n: (0, n)),
            pl.BlockSpec((MEM_W, tn), lambda m, n: (0, n)),
            pl.BlockSpec((tn, d), lambda m, n: (n, 0)),
            pl.BlockSpec((1, d), lambda m, n: (0, 0)),
        ],
        out_specs=pl.BlockSpec((tm, d), row),
        scratch_shapes=[pltpu.VMEM((tm, d), BF16), pltpu.VMEM((tm, d), F32)],
        compiler_params=_params("parallel", "arbitrary"),
        name="merge",
    )(x, g_pre, y_fox, y_diff, y_mem, w_gate, w_gate, w_gate, b_gate, b_gate, b_gate,
      w_fox, w_diff, w_mem, w_out, g_post)


def kernel(x, mem, ffn1_pre_g, ffn1_w_gate, ffn1_w_up, ffn1_w_down, ffn1_post_g, mix_pre_g, w_in, fox_f_bias, diff_lambda_q1, diff_lambda_k1, diff_lambda_q2, diff_lambda_k2, diff_head_g, mem_norm_g, w_mem_kv, w_branch_fox, w_branch_diff, w_branch_mem, w_merge_gate, b_merge_gate, w_out, mix_post_g, ffn2_pre_g, ffn2_w_gate, ffn2_w_up, ffn2_w_down, ffn2_post_g):
    batch, seq, d = x.shape
    depth = w_in.shape[0]
    xt = x.reshape(batch * seq, d)
    cos_t, sa_t, sb_t = _rope_tables(seq)

    def row(v):
        return v.reshape(1, -1).astype(F32)

    for l in range(depth):
        bf = lambda w: w[l].astype(BF16)
        later = [ffn2_w_gate, ffn2_w_up, ffn2_w_down, w_merge_gate]
        w_in_t = w_in[l].T
        xt, cast_out = _ffn(
            xt, row(ffn1_pre_g[l]), bf(ffn1_w_gate), bf(ffn1_w_up), bf(ffn1_w_down),
            row(ffn1_post_g[l]), cast=[(w, l) for w in later],
            cast_t=[(w_in_t, 0, FF_COL + LANES), (w_in_t, FF_COL + FOX_HEADS, w_in.shape[2])])
        w2_gate, w2_up, w2_down, wb_merge, w_head, w_tail = cast_out

        small = [w_out, w_branch_fox, w_branch_diff, w_branch_mem]
        proj, ff, (wb_out, wb_fox, wb_diff, wb_mem) = _mix_proj(
            xt, row(mix_pre_g[l]), w_head, w_tail, cos_t, sa_t, sb_t, seq=seq,
            cast=[(w, l) for w in small])

        bias = jnp.pad(fox_f_bias[l].astype(F32), (0, LANES - FOX_HEADS)).reshape(1, LANES)
        fb, frow = _fox_gate(ff, bias, batch=batch, seq=seq)
        y_fox = _fox_attn(proj, fb, frow, batch=batch, seq=seq)

        lam_init = 0.8 - 0.6 * math.exp(-0.3 * l)
        lamv = jnp.stack([diff_lambda_q1[l], diff_lambda_k1[l], diff_lambda_q2[l],
                          diff_lambda_k2[l]]).astype(F32)
        y_diff = _diff_attn(proj, lamv, row(diff_head_g[l]), batch=batch, seq=seq,
                            lam_init=lam_init)

        mkv = _mem_kv(mem, row(mem_norm_g[l]), bf(w_mem_kv))
        y_mem = _mem_attn(proj, mkv, batch=batch, seq=seq)

        xt = _merge(xt, row(mix_pre_g[l]), y_fox, y_diff, y_mem, wb_merge,
                    row(b_merge_gate[l]), wb_fox, wb_diff, wb_mem, wb_out, row(mix_post_g[l]))

        xt, _ = _ffn(xt, row(ffn2_pre_g[l]), w2_gate, w2_up, w2_down, row(ffn2_post_g[l]))

    return xt.reshape(batch, seq, d)
```

```python
import functools
import math

import jax
import jax.numpy as jnp
from jax import lax
from jax.experimental import pallas as pl
from jax.experimental.pallas import tpu as pltpu

D_MODEL = 2048
CHUNK = 64
EPS = 1e-6
ROPE_THETA = 500000.0

FOX_HEADS = 8
FOX_DIM = 128
FOX_W = FOX_HEADS * FOX_DIM

DIFF_HEADS = 4
DIFF_QK_DIM = 64
DIFF_V_DIM = 2 * DIFF_QK_DIM
DIFF_QK_W = DIFF_HEADS * 2 * DIFF_QK_DIM
DIFF_V_W = DIFF_HEADS * DIFF_V_DIM
ROPE_DIM = DIFF_QK_DIM // 4

MEM_HEADS = 4
MEM_DIM = 128
MEM_W = MEM_HEADS * MEM_DIM

LANES = 128
F32_ROWS = 8
BF16_ROWS = 16
RMS_ROWS_IN_FLIGHT = 128
PROJ_W = 3 * FOX_W + 2 * DIFF_QK_W + DIFF_V_W + MEM_W
FQ_BLK, FK_BLK, FV_BLK = 0, FOX_HEADS, 2 * FOX_HEADS
DQ_BLK = 3 * FOX_HEADS
DK_BLK = DQ_BLK + DIFF_HEADS
DV_BLK = DK_BLK + DIFF_HEADS
MQ_COL = 3 * FOX_W + 2 * DIFF_QK_W + DIFF_V_W
FF_COL = 3 * FOX_W

VMEM_LIMIT = 56 * 1024 * 1024
BF16 = jnp.bfloat16
F32 = jnp.float32
NEG_INF = float("-inf")
LOG2E = math.log2(math.e)
FOX_QSCALE = FOX_DIM ** -0.5 * LOG2E
DIFF_QSCALE = DIFF_QK_DIM ** -0.5 * LOG2E


def _dot(a, b):
    return jnp.dot(a, b, preferred_element_type=F32)


def _dot_nt(a, b):
    return lax.dot_general(a, b, (((1,), (1,)), ((), ())), preferred_element_type=F32)


def _rms(x, g):
    return x * lax.rsqrt(jnp.mean(x * x, axis=-1, keepdims=True) + EPS) * g


def _rms_rows(src_ref, g, emit, rows):
    def body(r, carry):
        sl = pl.ds(pl.multiple_of(r * rows, rows), rows)
        emit(sl, _rms(src_ref[sl, :], g))
        return carry

    lax.fori_loop(0, src_ref.shape[0] // rows, body, 0, unroll=RMS_ROWS_IN_FLIGHT // rows)


def _params(*sem):
    return pltpu.CompilerParams(dimension_semantics=sem, vmem_limit_bytes=VMEM_LIMIT)


def _ffn_kernel(*refs, transposed):
    n_cast = len(transposed)
    x_ref, gpre_ref, wg_ref, wu_ref, wd_ref, gpost_ref = refs[:6]
    o_ref = refs[6 + n_cast]
    h_ref, acc_ref = refs[-2:]
    f = pl.program_id(1)

    @pl.when(f == 0)
    def _():
        def emit(sl, y):
            h_ref[sl, :] = y.astype(BF16)
            acc_ref[sl, :] = jnp.zeros_like(y)

        _rms_rows(x_ref, gpre_ref[...], emit, BF16_ROWS)

    jobs = list(zip(refs[6:6 + n_cast], refs[7 + n_cast:7 + 2 * n_cast], transposed))

    def run_casts(part, parts=3):
        for src_ref, dst_ref, flip in jobs[part::parts]:
            blk = src_ref[...]
            dst_ref[...] = (blk.T if flip else blk).astype(BF16)

    h = h_ref[...]
    run_casts(0)
    g = _dot(h, wg_ref[...])
    run_casts(1)
    u = _dot(h, wu_ref[...])
    run_casts(2)
    a = (g * (1.0 / (1.0 + jnp.exp(-g)))) * u
    acc_ref[...] += _dot(a.astype(BF16), wd_ref[...])

    @pl.when(f == pl.num_programs(1) - 1)
    def _():
        def emit(sl, y):
            o_ref[sl, :] = x_ref[sl, :] + y

        _rms_rows(acc_ref, 0.5 * gpost_ref[...], emit, F32_ROWS)


def _ffn(x, g_pre, w_gate, w_up, w_down, g_post, *, cast=(), cast_t=(), tm=512, tf=512):
    t, d = x.shape
    d_ff = w_gate.shape[1]
    m_tiles, f_steps = t // tm, d_ff // tf
    in_specs = [
        pl.BlockSpec((tm, d), lambda m, f: (m, 0)),
        pl.BlockSpec((1, d), lambda m, f: (0, 0)),
        pl.BlockSpec((d, tf), lambda m, f: (0, f)),
        pl.BlockSpec((d, tf), lambda m, f: (0, f)),
        pl.BlockSpec((tf, d), lambda m, f: (f, 0)),
        pl.BlockSpec((1, d), lambda m, f: (0, 0)),
    ]
    out_specs = [pl.BlockSpec((tm, d), lambda m, f: (m, 0))]
    out_shape = [jax.ShapeDtypeStruct((t, d), F32)]
    for arr, layer in cast:
        _, rows, cols = arr.shape
        br = BF16_ROWS
        while rows // m_tiles // br > f_steps:
            br *= 2
        steps = rows // m_tiles // br
        assert steps * br * m_tiles == rows

        def index(m, f, steps=steps):
            return m * steps + jnp.minimum(f, steps - 1)

        in_specs.append(pl.BlockSpec((None, br, cols),
                                     lambda m, f, index=index, layer=layer: (layer, index(m, f), 0)))
        out_specs.append(pl.BlockSpec((br, cols), lambda m, f, index=index: (index(m, f), 0)))
        out_shape.append(jax.ShapeDtypeStruct((rows, cols), BF16))
    for arr, lo, hi in cast_t:
        _, cols = arr.shape
        n_blk = (hi - lo) // LANES
        assert n_blk * LANES == hi - lo and n_blk <= m_tiles * f_steps

        def block(m, f, n_blk=n_blk):
            return jnp.minimum(m * f_steps + f, n_blk - 1)

        in_specs.append(pl.BlockSpec(
            (pl.Element(LANES), pl.Element(cols)),
            lambda m, f, block=block, lo=lo: (
                pl.multiple_of(lo + LANES * block(m, f), math.gcd(lo, LANES)), 0)))
        out_specs.append(pl.BlockSpec((cols, LANES), lambda m, f, block=block: (0, block(m, f))))
        out_shape.append(jax.ShapeDtypeStruct((cols, hi - lo), BF16))
    outs = pl.pallas_call(
        functools.partial(_ffn_kernel, transposed=(False,) * len(cast) + (True,) * len(cast_t)),
        out_shape=out_shape,
        grid=(m_tiles, f_steps),
        in_specs=in_specs,
        out_specs=out_specs,
        scratch_shapes=[pltpu.VMEM((tm, d), BF16), pltpu.VMEM((tm, d), F32)],
        compiler_params=_params("parallel", "arbitrary"),
        name="ffn",
    )(x, g_pre, w_gate, w_up, w_down, g_post, *[c[0] for c in cast], *[c[0] for c in cast_t])
    return outs[0], outs[1:]


def _proj_kernel(*refs, tn, n_cast):
    x_ref, g_ref, wa_ref, wff_ref, wb_ref, cos_ref, sa_ref, sb_ref = refs[:8]
    o_ref, ff_ref, h_ref = refs[8 + n_cast:11 + n_cast]
    for src_ref, dst_ref in zip(refs[8:8 + n_cast], refs[11 + n_cast:]):
        dst_ref[...] = src_ref[...].astype(BF16)
    h_ref[...] = _rms(x_ref[...], g_ref[...]).astype(BF16)
    ff_ref[...] = _dot(h_ref[...], wff_ref[...])
    half = ROPE_DIM // 2
    na = wa_ref.shape[1] // tn
    for n in range(na + wb_ref.shape[1] // tn):
        w_tile = (wa_ref[:, n * tn:(n + 1) * tn] if n < na
                  else wb_ref[:, (n - na) * tn:(n - na + 1) * tn])
        y = _dot(h_ref[...], w_tile)
        first_blk = n * tn // LANES
        if first_blk < FK_BLK:
            y = y * FOX_QSCALE
        if DQ_BLK <= first_blk < DV_BLK:
            qscale = DIFF_QSCALE if first_blk < DK_BLK else 1.0
            for j in range(tn // LANES):
                blk = y[:, j * LANES:(j + 1) * LANES]
                rot = (blk * cos_ref[...] + pltpu.roll(blk, LANES - half, axis=1) * sa_ref[...]
                       + pltpu.roll(blk, half, axis=1) * sb_ref[...])
                o_ref[:, n * tn + j * LANES:n * tn + (j + 1) * LANES] = (rot * qscale).astype(BF16)
        else:
            o_ref[:, n * tn:(n + 1) * tn] = y.astype(BF16)


def _mix_proj(x, g, w_head, w_tail, cos_t, sa_t, sb_t, *, seq, cast=(), tm=512, tn=512):
    t, d = x.shape
    n_out = FF_COL + w_tail.shape[1]
    assert all((blk * LANES) % tn == 0 for blk in (FK_BLK, DQ_BLK, DK_BLK, DV_BLK))
    assert w_head.shape[1] == FF_COL + LANES and n_out == PROJ_W
    s_tiles = seq // tm
    m_tiles = t // tm
    resident = dict(pipeline_mode=pl.Buffered(1))
    in_specs = [
        pl.BlockSpec((tm, d), lambda m: (m, 0)),
        pl.BlockSpec((1, d), lambda m: (0, 0)),
        pl.BlockSpec((d, FF_COL), lambda m: (0, 0), **resident),
        pl.BlockSpec((d, LANES), lambda m: (0, FF_COL // LANES), **resident),
        pl.BlockSpec((d, w_tail.shape[1]), lambda m: (0, 0), **resident),
        pl.BlockSpec((tm, LANES), lambda m: (m % s_tiles, 0)),
        pl.BlockSpec((tm, LANES), lambda m: (m % s_tiles, 0)),
        pl.BlockSpec((tm, LANES), lambda m: (m % s_tiles, 0)),
    ]
    out_specs = [pl.BlockSpec((tm, n_out), lambda m: (m, 0)),
                 pl.BlockSpec((tm, LANES), lambda m: (m, 0)),
                 pl.BlockSpec((tm, d), lambda m: (m, 0))]
    out_shape = [jax.ShapeDtypeStruct((t, n_out), BF16), jax.ShapeDtypeStruct((t, LANES), F32),
                 jax.ShapeDtypeStruct((t, d), BF16)]
    for arr, layer in cast:
        _, rows, cols = arr.shape
        br = rows // m_tiles
        assert br * m_tiles == rows and br % BF16_ROWS == 0
        in_specs.append(pl.BlockSpec((None, br, cols), lambda m, layer=layer: (layer, m, 0)))
        out_specs.append(pl.BlockSpec((br, cols), lambda m: (m, 0)))
        out_shape.append(jax.ShapeDtypeStruct((rows, cols), BF16))
    outs = pl.pallas_call(
        functools.partial(_proj_kernel, tn=tn, n_cast=len(cast)),
        out_shape=out_shape,
        grid=(m_tiles,),
        in_specs=in_specs,
        out_specs=out_specs,
        compiler_params=_params("parallel"),
        name="mix_proj",
    )(x, g, w_head, w_head, w_tail, cos_t, sa_t, sb_t, *[arr for arr, _ in cast])
    return outs[0], outs[1], outs[2], outs[3:]


def _rope_tables(seq):
    half = ROPE_DIM // 2
    pos = jnp.arange(seq, dtype=F32)
    inv_freq = ROPE_THETA ** (-jnp.arange(0, ROPE_DIM, 2, dtype=F32) / ROPE_DIM)
    ang = pos[:, None] * inv_freq[None, :]
    cos, sin = jnp.cos(ang), jnp.sin(ang)
    ones = jnp.ones((seq, DIFF_QK_DIM - ROPE_DIM), F32)
    zeros_h = jnp.zeros((seq, half), F32)
    zeros_r = jnp.zeros((seq, DIFF_QK_DIM - ROPE_DIM), F32)
    cos_m = jnp.concatenate([cos, cos, ones], axis=1)
    sa_m = jnp.concatenate([-sin, zeros_h, zeros_r], axis=1)
    sb_m = jnp.concatenate([zeros_h, sin, zeros_r], axis=1)
    rep = LANES // DIFF_QK_DIM
    return (jnp.tile(cos_m, (1, rep)), jnp.tile(sa_m, (1, rep)), jnp.tile(sb_m, (1, rep)))


def _fgate_kernel(ff_ref, bias_ref, fp_ref, frow_ref, fcol_ref, *, cb):
    seq = ff_ref.shape[0]
    head_lane = lax.broadcasted_iota(jnp.int32, (cb, LANES), 1) < FOX_HEADS
    z = ff_ref[...] + bias_ref[...]
    lf = jnp.minimum(z, 0.0) - jnp.log1p(jnp.exp(-jnp.abs(z)))
    r = lax.broadcasted_iota(jnp.int32, (cb, cb), 0)
    c = lax.broadcasted_iota(jnp.int32, (cb, cb), 1)
    tri = (r >= c).astype(F32)
    carry = jnp.zeros((1, LANES), F32)
    for i in range(seq // cb):
        cs = jnp.dot(tri, lf[i * cb:(i + 1) * cb], precision=lax.Precision.HIGHEST,
                     preferred_element_type=F32) + carry
        carry = cs[cb - 1:cb, :]
        cs2 = jnp.where(head_lane, cs * LOG2E, 0.0)
        fcol_ref[i * cb:(i + 1) * cb, :] = cs2
        hi = cs2.astype(BF16).astype(F32)
        mid = (cs2 - hi).astype(BF16).astype(F32)
        lo = (cs2 - hi - mid).astype(BF16).astype(F32)
        pieces = hi + pltpu.roll(mid, FOX_HEADS, axis=1) + pltpu.roll(lo, 2 * FOX_HEADS, axis=1)
        fp_ref[i * cb:(i + 1) * cb, :] = pieces.astype(BF16)
    frow_ref[...] = fcol_ref[...].T[:FOX_HEADS]


def _fox_gate(ff, bias, *, batch, seq, cb=256):
    ff = ff.reshape(batch, seq, LANES)
    return pl.pallas_call(
        functools.partial(_fgate_kernel, cb=cb),
        out_shape=(jax.ShapeDtypeStruct((batch, seq, LANES), BF16),
                   jax.ShapeDtypeStruct((batch, FOX_HEADS, seq), F32)),
        grid=(batch,),
        in_specs=[pl.BlockSpec((None, seq, LANES), lambda b: (b, 0, 0)),
                  pl.BlockSpec((1, LANES), lambda b: (0, 0))],
        out_specs=(pl.BlockSpec((None, seq, LANES), lambda b: (b, 0, 0)),
                   pl.BlockSpec((None, FOX_HEADS, seq), lambda b: (b, 0, 0))),
        scratch_shapes=[pltpu.VMEM((seq, LANES), F32)],
        compiler_params=_params("parallel"),
        name="fox_gate",
    )(ff, bias)


def _dot_tn(a, b):
    return lax.dot_general(a, b, (((0,), (0,)), ((), ())), preferred_element_type=F32)


def _flash_attend(i, heads, logits, values, ft2, visible, finish,
                  sa_ref, sb_ref, m_ref, l_ref, acc_ref):
    def step(h, s, j):
        m = m_ref[h]
        m_new = jnp.maximum(m, jnp.max(s, axis=0, keepdims=True) + ft2[h])
        alpha = jnp.exp2(m - m_new)
        p = jnp.exp2(s - (m_new - ft2[h]))
        m_ref[h] = m_new
        l_ref[h] = alpha * l_ref[h] + jnp.sum(p, axis=0, keepdims=True)
        acc_ref[h] = alpha * acc_ref[h] + _dot_tn(values(h, j), p.astype(BF16))

    for h in heads:
        m_ref[h] = jnp.full(m_ref.shape[1:], NEG_INF, F32)
        l_ref[h] = jnp.zeros(l_ref.shape[1:], F32)
        acc_ref[h] = jnp.zeros(acc_ref.shape[1:], F32)
        sa_ref[h] = logits(h, 0)

    def pair(jj, carry):
        j = 2 * jj
        for h in heads:
            sb_ref[h] = logits(h, j + 1)
            step(h, sa_ref[h], j)
        for h in heads:
            sa_ref[h] = logits(h, j + 2)
            step(h, sb_ref[h], j + 1)
        return carry

    lax.fori_loop(0, i // 2, pair, 0)

    def last(s_ref):
        for h in heads:
            step(h, jnp.where(visible, s_ref[h], NEG_INF), i)
            finish(h, acc_ref[h] / l_ref[h])

    @pl.when(i % 2 == 0)
    def _():
        last(sa_ref)

    @pl.when(i % 2 == 1)
    def _():
        for h in heads:
            sb_ref[h] = logits(h, i)
            step(h, sa_ref[h], i - 1)
        last(sb_ref)


def _flash_scratch(hb, width, tq, cols):
    return [pltpu.VMEM((hb, tq, cols), F32), pltpu.VMEM((hb, tq, cols), F32),
            pltpu.VMEM((hb, 1, cols), F32), pltpu.VMEM((hb, 1, cols), F32),
            pltpu.VMEM((hb, width, cols), F32)]


def _fox_kernel(q_ref, k_ref, v_ref, fp_ref, fr_ref, o_ref, *scratch, tq, hb):
    i = pl.program_id(2)
    h0 = pl.program_id(1) * hb
    heads = range(hb)
    ft2 = [fr_ref[h, pl.ds(i, 1), :] for h in heads]

    lane = lax.broadcasted_iota(jnp.int32, (tq, LANES), 1)
    q_ext = []
    for h in heads:
        mine = (lane % FOX_HEADS == h0 + h) & (lane < 3 * FOX_HEADS)
        sel = jnp.where(mine, -1.0, 0.0).astype(BF16)
        q_ext.append(jnp.concatenate([q_ref[:, h * FOX_DIM:(h + 1) * FOX_DIM], sel], axis=1))

    def logits(h, j):
        start = pl.multiple_of(j * tq, tq)
        k_ext = jnp.concatenate([k_ref[pl.ds(start, tq), h * FOX_DIM:(h + 1) * FOX_DIM],
                                 fp_ref[pl.ds(start, tq), :]], axis=1)
        return _dot_nt(k_ext, q_ext[h])

    def values(h, j):
        return v_ref[pl.ds(pl.multiple_of(j * tq, tq), tq), h * FOX_DIM:(h + 1) * FOX_DIM]

    def finish(h, o_t):
        o_ref[:, h * FOX_DIM:(h + 1) * FOX_DIM] = o_t.T.astype(BF16)

    krow = lax.broadcasted_iota(jnp.int32, (tq, tq), 0)
    qcol = lax.broadcasted_iota(jnp.int32, (tq, tq), 1)
    _flash_attend(i, heads, logits, values, ft2, krow <= qcol, finish, *scratch)


def _fox_attn(proj, fp, frow, *, batch, seq, tq=256, hb=8):
    proj3 = proj.reshape(batch, seq, PROJ_W)
    frow4 = frow.reshape(batch, FOX_HEADS, seq // tq, tq)
    w = hb * FOX_DIM
    groups = FOX_HEADS // hb
    out = pl.pallas_call(
        functools.partial(_fox_kernel, tq=tq, hb=hb),
        out_shape=jax.ShapeDtypeStruct((batch, seq, FOX_W), BF16),
        grid=(batch, groups, seq // tq),
        in_specs=[
            pl.BlockSpec((None, tq, w), lambda b, g, i: (b, i, g)),
            pl.BlockSpec((None, seq, w), lambda b, g, i: (b, 0, groups + g)),
            pl.BlockSpec((None, seq, w), lambda b, g, i: (b, 0, 2 * groups + g)),
            pl.BlockSpec((None, seq, LANES), lambda b, g, i: (b, 0, 0)),
            pl.BlockSpec((None, hb, seq // tq, tq), lambda b, g, i: (b, g, 0, 0)),
        ],
        out_specs=pl.BlockSpec((None, tq, w), lambda b, g, i: (b, i, g)),
        scratch_shapes=_flash_scratch(hb, FOX_DIM, tq, tq),
        compiler_params=_params("parallel", "parallel", "arbitrary"),
        name="fox_attn",
    )(proj3, proj3, proj3, fp, frow4)
    return out.reshape(batch * seq, FOX_W)


def _diff_kernel(lamv_ref, g_ref, q_ref, k_ref, v_ref, o_ref, *scratch, tq, hb, lam_init):
    i = pl.program_id(2)
    heads = range(hb)
    lv = lamv_ref[...]
    lam = (jnp.exp(jnp.sum(lv[0:1] * lv[1:2], axis=1, keepdims=True))
           - jnp.exp(jnp.sum(lv[2:3] * lv[3:4], axis=1, keepdims=True)) + lam_init)

    lane = lax.broadcasted_iota(jnp.int32, (tq, LANES), 1)
    qs = []
    for h in heads:
        q = q_ref[:, h * LANES:(h + 1) * LANES].astype(F32)
        qs.append(jnp.concatenate([jnp.where(lane < DIFF_QK_DIM, q, 0.0),
                                   jnp.where(lane >= DIFF_QK_DIM, q, 0.0)], axis=0).astype(BF16))
    zero = jnp.zeros((1, 2 * tq), F32)

    def logits(h, j):
        start = pl.multiple_of(j * tq, tq)
        return _dot_nt(k_ref[pl.ds(start, tq), h * LANES:(h + 1) * LANES], qs[h])

    def values(h, j):
        return v_ref[pl.ds(pl.multiple_of(j * tq, tq), tq), h * LANES:(h + 1) * LANES]

    def finish(h, o_t):
        yd = (o_t[:, :tq] - lam * o_t[:, tq:]).T
        o_ref[:, h * LANES:(h + 1) * LANES] = (
            _rms(yd, g_ref[...]) * (1.0 - lam_init)).astype(BF16)

    krow = lax.broadcasted_iota(jnp.int32, (tq, 2 * tq), 0)
    qcol = lax.broadcasted_iota(jnp.int32, (tq, 2 * tq), 1)
    qcol = jnp.where(qcol >= tq, qcol - tq, qcol)
    visible = krow // CHUNK <= qcol // CHUNK
    _flash_attend(i, heads, logits, values, [zero] * hb, visible, finish, *scratch)


def _diff_attn(proj, lamv, g, *, batch, seq, lam_init, tq=256, hb=4):
    proj3 = proj.reshape(batch, seq, PROJ_W)
    w = hb * LANES
    groups = DIFF_HEADS // hb
    dq, dk, dv = (DQ_BLK * LANES) // w, (DK_BLK * LANES) // w, (DV_BLK * LANES) // w
    out = pl.pallas_call(
        functools.partial(_diff_kernel, tq=tq, hb=hb, lam_init=lam_init),
        out_shape=jax.ShapeDtypeStruct((batch, seq, DIFF_V_W), BF16),
        grid=(batch, groups, seq // tq),
        in_specs=[
            pl.BlockSpec((4, DIFF_QK_DIM), lambda b, g, i: (0, 0)),
            pl.BlockSpec((1, DIFF_V_DIM), lambda b, g, i: (0, 0)),
            pl.BlockSpec((None, tq, w), lambda b, g, i: (b, i, dq + g)),
            pl.BlockSpec((None, seq, w), lambda b, g, i: (b, 0, dk + g)),
            pl.BlockSpec((None, seq, w), lambda b, g, i: (b, 0, dv + g)),
        ],
        out_specs=pl.BlockSpec((None, tq, w), lambda b, g, i: (b, i, g)),
        scratch_shapes=_flash_scratch(hb, DIFF_V_DIM, tq, 2 * tq),
        compiler_params=_params("parallel", "parallel", "arbitrary"),
        name="diff_attn",
    )(lamv, g, proj3, proj3, proj3)
    return out.reshape(batch * seq, DIFF_V_W)


def _mem_kv_kernel(mem_ref, g_ref, w_ref, o_ref):
    o_ref[...] = _dot(_rms(mem_ref[...], g_ref[...]).astype(BF16), w_ref[...]).astype(BF16)


def _mem_kv(mem, g, w):
    batch, n_mem, d = mem.shape
    return pl.pallas_call(
        _mem_kv_kernel,
        out_shape=jax.ShapeDtypeStruct((batch, n_mem, 2 * MEM_W), BF16),
        grid=(batch,),
        in_specs=[pl.BlockSpec((None, n_mem, d), lambda b: (b, 0, 0)),
                  pl.BlockSpec((1, d), lambda b: (0, 0)),
                  pl.BlockSpec((d, 2 * MEM_W), lambda b: (0, 0))],
        out_specs=pl.BlockSpec((None, n_mem, 2 * MEM_W), lambda b: (b, 0, 0)),
        compiler_params=_params("parallel"),
        name="mem_kv",
    )(mem, g, w)


def _mem_attn_kernel(q_ref, kv_ref, o_ref):
    scale = MEM_DIM ** -0.5
    for h in range(MEM_HEADS):
        q = q_ref[:, h * MEM_DIM:(h + 1) * MEM_DIM]
        k = kv_ref[:, h * MEM_DIM:(h + 1) * MEM_DIM]
        v = kv_ref[:, MEM_W + h * MEM_DIM:MEM_W + (h + 1) * MEM_DIM]
        s = _dot_nt(q, k) * scale
        p = jnp.exp(s - jnp.max(s, axis=1, keepdims=True))
        l = jnp.sum(p, axis=1, keepdims=True)
        o_ref[:, h * MEM_DIM:(h + 1) * MEM_DIM] = (_dot(p.astype(BF16), v) / l).astype(BF16)


def _mem_attn(proj, mkv, *, batch, seq, tq=512):
    proj3 = proj.reshape(batch, seq, PROJ_W)
    n_mem = mkv.shape[1]
    out = pl.pallas_call(
        _mem_attn_kernel,
        out_shape=jax.ShapeDtypeStruct((batch, seq, MEM_W), BF16),
        grid=(batch, seq // tq),
        in_specs=[pl.BlockSpec((None, tq, MEM_W), lambda b, i: (b, i, MQ_COL // MEM_W)),
                  pl.BlockSpec((None, n_mem, 2 * MEM_W), lambda b, i: (b, 0, 0))],
        out_specs=pl.BlockSpec((None, tq, MEM_W), lambda b, i: (b, i, 0)),
        compiler_params=_params("parallel", "arbitrary"),
        name="mem_attn",
    )(proj3, mkv)
    return out.reshape(batch * seq, MEM_W)


def _merge_kernel(x_ref, h_ref, yf_ref, yd_ref, ym_ref, wgf_ref, wgd_ref, wgm_ref,
                  bf_ref, bd_ref, bm_ref, wf_ref, wd_ref, wm_ref, wo_ref, gpost_ref,
                  o_ref, acc_ref):
    n = pl.program_id(1)

    @pl.when(n == 0)
    def _():
        acc_ref[...] = jnp.zeros_like(acc_ref)

    h = h_ref[...]

    def gated(wg_ref, b_ref, y_ref, w_ref):
        z = _dot(h, wg_ref[...]) + b_ref[...]
        return (1.0 / (1.0 + jnp.exp(-z))) * _dot(y_ref[...], w_ref[...])

    merged = (gated(wgf_ref, bf_ref, yf_ref, wf_ref) + gated(wgd_ref, bd_ref, yd_ref, wd_ref)
              + gated(wgm_ref, bm_ref, ym_ref, wm_ref))
    acc_ref[...] += _dot(merged.astype(BF16), wo_ref[...])

    @pl.when(n == pl.num_programs(1) - 1)
    def _():
        def emit(sl, y):
            o_ref[sl, :] = x_ref[sl, :] + y

        _rms_rows(acc_ref, gpost_ref[...], emit, F32_ROWS)


def _merge(x, h, y_fox, y_diff, y_mem, w_gate, b_gate, w_fox, w_diff, w_mem, w_out, g_post,
           *, tm=512, tn=512):
    t, d = x.shape
    nt = d // tn
    row = lambda m, n: (m, 0)
    return pl.pallas_call(
        _merge_kernel,
        out_shape=jax.ShapeDtypeStruct((t, d), F32),
        grid=(t // tm, nt),
        in_specs=[
            pl.BlockSpec((tm, d), row),
            pl.BlockSpec((tm, d), row),
            pl.BlockSpec((tm, FOX_W), row),
            pl.BlockSpec((tm, DIFF_V_W), row),
            pl.BlockSpec((tm, MEM_W), row),
            pl.BlockSpec((d, tn), lambda m, n: (0, n)),
            pl.BlockSpec((d, tn), lambda m, n: (0, nt + n)),
            pl.BlockSpec((d, tn), lambda m, n: (0, 2 * nt + n)),
            pl.BlockSpec((1, tn), lambda m, n: (0, n)),
            pl.BlockSpec((1, tn), lambda m, n: (0, nt + n)),
            pl.BlockSpec((1, tn), lambda m, n: (0, 2 * nt + n)),
            pl.BlockSpec((FOX_W, tn), lambda m, n: (0, n)),
            pl.BlockSpec((DIFF_V_W, tn), lambda m, n: (0, n)),
            pl.BlockSpec((MEM_W, tn), lambda m, n: (0, n)),
            pl.BlockSpec((tn, d), lambda m, n: (n, 0)),
            pl.BlockSpec((1, d), lambda m, n: (0, 0)),
        ],
        out_specs=pl.BlockSpec((tm, d), row),
        scratch_shapes=[pltpu.VMEM((tm, d), F32)],
        compiler_params=_params("parallel", "arbitrary"),
        name="merge",
    )(x, h, y_fox, y_diff, y_mem, w_gate, w_gate, w_gate, b_gate, b_gate, b_gate,
      w_fox, w_diff, w_mem, w_out, g_post)


def kernel(x, mem, ffn1_pre_g, ffn1_w_gate, ffn1_w_up, ffn1_w_down, ffn1_post_g, mix_pre_g, w_in, fox_f_bias, diff_lambda_q1, diff_lambda_k1, diff_lambda_q2, diff_lambda_k2, diff_head_g, mem_norm_g, w_mem_kv, w_branch_fox, w_branch_diff, w_branch_mem, w_merge_gate, b_merge_gate, w_out, mix_post_g, ffn2_pre_g, ffn2_w_gate, ffn2_w_up, ffn2_w_down, ffn2_post_g):
    batch, seq, d = x.shape
    depth = w_in.shape[0]
    xt = x.reshape(batch * seq, d)
    cos_t, sa_t, sb_t = _rope_tables(seq)

    def row(v):
        return v.reshape(1, -1).astype(F32)

    for l in range(depth):
        bf = lambda w: w[l].astype(BF16)
        later = [ffn2_w_gate, ffn2_w_up, ffn2_w_down, w_merge_gate]
        w_in_t = w_in[l].T
        xt, cast_out = _ffn(
            xt, row(ffn1_pre_g[l]), bf(ffn1_w_gate), bf(ffn1_w_up), bf(ffn1_w_down),
            row(ffn1_post_g[l]), cast=[(w, l) for w in later],
            cast_t=[(w_in_t, 0, FF_COL + LANES), (w_in_t, FF_COL + FOX_HEADS, w_in.shape[2])])
        w2_gate, w2_up, w2_down, wb_merge, w_head, w_tail = cast_out

        small = [w_out, w_branch_fox, w_branch_diff, w_branch_mem]
        proj, ff, h_mix, (wb_out, wb_fox, wb_diff, wb_mem) = _mix_proj(
            xt, row(mix_pre_g[l]), w_head, w_tail, cos_t, sa_t, sb_t, seq=seq,
            cast=[(w, l) for w in small])

        bias = jnp.pad(fox_f_bias[l].astype(F32), (0, LANES - FOX_HEADS)).reshape(1, LANES)
        fp, frow = _fox_gate(ff, bias, batch=batch, seq=seq)
        y_fox = _fox_attn(proj, fp, frow, batch=batch, seq=seq)

        lam_init = 0.8 - 0.6 * math.exp(-0.3 * l)
        lamv = jnp.stack([diff_lambda_q1[l], diff_lambda_k1[l], diff_lambda_q2[l],
                          diff_lambda_k2[l]]).astype(F32)
        y_diff = _diff_attn(proj, lamv, row(diff_head_g[l]), batch=batch, seq=seq,
                            lam_init=lam_init)

        mkv = _mem_kv(mem, row(mem_norm_g[l]), bf(w_mem_kv))
        y_mem = _mem_attn(proj, mkv, batch=batch, seq=seq)

        xt = _merge(xt, h_mix, y_fox, y_diff, y_mem, wb_merge,
                    row(b_merge_gate[l]), wb_fox, wb_diff, wb_mem, wb_out, row(mix_post_g[l]))

        xt, _ = _ffn(xt, row(ffn2_pre_g[l]), w2_gate, w2_up, w2_down, row(ffn2_post_g[l]))

    return xt.reshape(batch, seq, d)
```

```python
import functools
import math

import jax
import jax.numpy as jnp
from jax import lax
from jax.experimental import pallas as pl
from jax.experimental.pallas import tpu as pltpu

D_MODEL = 2048
CHUNK = 64
EPS = 1e-6
ROPE_THETA = 500000.0

FOX_HEADS = 8
FOX_DIM = 128
FOX_W = FOX_HEADS * FOX_DIM

DIFF_HEADS = 4
DIFF_QK_DIM = 64
DIFF_V_DIM = 2 * DIFF_QK_DIM
DIFF_QK_W = DIFF_HEADS * 2 * DIFF_QK_DIM
DIFF_V_W = DIFF_HEADS * DIFF_V_DIM
ROPE_DIM = DIFF_QK_DIM // 4

MEM_HEADS = 4
MEM_DIM = 128
MEM_W = MEM_HEADS * MEM_DIM

LANES = 128
F32_ROWS = 8
BF16_ROWS = 16
RMS_ROWS_IN_FLIGHT = 128
PROJ_W = 3 * FOX_W + 2 * DIFF_QK_W + DIFF_V_W + MEM_W
FQ_BLK, FK_BLK, FV_BLK = 0, FOX_HEADS, 2 * FOX_HEADS
DQ_BLK = 3 * FOX_HEADS
DK_BLK = DQ_BLK + DIFF_HEADS
DV_BLK = DK_BLK + DIFF_HEADS
MQ_COL = 3 * FOX_W + 2 * DIFF_QK_W + DIFF_V_W
FF_COL = 3 * FOX_W

VMEM_LIMIT = 56 * 1024 * 1024
BF16 = jnp.bfloat16
F32 = jnp.float32
NEG_INF = float("-inf")
LOG2E = math.log2(math.e)
FOX_QSCALE = FOX_DIM ** -0.5 * LOG2E
DIFF_QSCALE = DIFF_QK_DIM ** -0.5 * LOG2E
MEM_QSCALE = MEM_DIM ** -0.5 * LOG2E


def _dot(a, b):
    return jnp.dot(a, b, preferred_element_type=F32)


def _dot_nt(a, b):
    return lax.dot_general(a, b, (((1,), (1,)), ((), ())), preferred_element_type=F32)


def _rms(x, g):
    return x * lax.rsqrt(jnp.mean(x * x, axis=-1, keepdims=True) + EPS) * g


def _rms_rows(src_ref, g, emit, rows):
    def body(r, carry):
        sl = pl.ds(pl.multiple_of(r * rows, rows), rows)
        emit(sl, _rms(src_ref[sl, :], g))
        return carry

    lax.fori_loop(0, src_ref.shape[0] // rows, body, 0, unroll=RMS_ROWS_IN_FLIGHT // rows)


def _params(*sem):
    return pltpu.CompilerParams(dimension_semantics=sem, vmem_limit_bytes=VMEM_LIMIT)


def _slab_cast_specs(cast, n_steps, step_index):
    in_specs, out_specs, out_shape = [], [], []
    for arr, layer in cast:
        _, rows, cols = arr.shape
        br = rows // n_steps
        assert br * n_steps == rows and br % BF16_ROWS == 0
        in_specs.append(pl.BlockSpec((None, br, cols),
                                     lambda *g, layer=layer: (layer, step_index(*g), 0)))
        out_specs.append(pl.BlockSpec((br, cols), lambda *g: (step_index(*g), 0)))
        out_shape.append(jax.ShapeDtypeStruct((rows, cols), BF16))
    return in_specs, out_specs, out_shape


def _run_slab_casts(src_refs, dst_refs):
    for src_ref, dst_ref in zip(src_refs, dst_refs):
        dst_ref[...] = src_ref[...].astype(BF16)


def _ffn_kernel(*refs, transposed):
    n_cast = len(transposed)
    x_ref, gpre_ref, wg_ref, wu_ref, wd_ref, gpost_ref = refs[:6]
    o_ref = refs[6 + n_cast]
    h_ref, acc_ref = refs[-2:]
    f = pl.program_id(1)

    @pl.when(f == 0)
    def _():
        def emit(sl, y):
            h_ref[sl, :] = y.astype(BF16)
            acc_ref[sl, :] = jnp.zeros_like(y)

        _rms_rows(x_ref, gpre_ref[...], emit, BF16_ROWS)

    jobs = list(zip(refs[6:6 + n_cast], refs[7 + n_cast:7 + 2 * n_cast], transposed))

    def run_casts(part, parts=3):
        for src_ref, dst_ref, flip in jobs[part::parts]:
            blk = src_ref[...]
            dst_ref[...] = (blk.T if flip else blk).astype(BF16)

    h = h_ref[...]
    run_casts(0)
    g = _dot(h, wg_ref[...])
    run_casts(1)
    u = _dot(h, wu_ref[...])
    run_casts(2)
    a = (g * (1.0 / (1.0 + jnp.exp(-g)))) * u
    acc_ref[...] += _dot(a.astype(BF16), wd_ref[...])

    @pl.when(f == pl.num_programs(1) - 1)
    def _():
        def emit(sl, y):
            o_ref[sl, :] = x_ref[sl, :] + y

        _rms_rows(acc_ref, 0.5 * gpost_ref[...], emit, F32_ROWS)


def _ffn(x, g_pre, w_gate, w_up, w_down, g_post, *, cast=(), cast_t=(), tm=512, tf=512):
    t, d = x.shape
    d_ff = w_gate.shape[1]
    m_tiles, f_steps = t // tm, d_ff // tf
    in_specs = [
        pl.BlockSpec((tm, d), lambda m, f: (m, 0)),
        pl.BlockSpec((1, d), lambda m, f: (0, 0)),
        pl.BlockSpec((d, tf), lambda m, f: (0, f)),
        pl.BlockSpec((d, tf), lambda m, f: (0, f)),
        pl.BlockSpec((tf, d), lambda m, f: (f, 0)),
        pl.BlockSpec((1, d), lambda m, f: (0, 0)),
    ]
    out_specs = [pl.BlockSpec((tm, d), lambda m, f: (m, 0))]
    out_shape = [jax.ShapeDtypeStruct((t, d), F32)]
    for arr, layer in cast:
        _, rows, cols = arr.shape
        br = BF16_ROWS
        while rows // m_tiles // br > f_steps:
            br *= 2
        steps = rows // m_tiles // br
        assert steps * br * m_tiles == rows

        def index(m, f, steps=steps):
            return m * steps + jnp.minimum(f, steps - 1)

        in_specs.append(pl.BlockSpec((None, br, cols),
                                     lambda m, f, index=index, layer=layer: (layer, index(m, f), 0)))
        out_specs.append(pl.BlockSpec((br, cols), lambda m, f, index=index: (index(m, f), 0)))
        out_shape.append(jax.ShapeDtypeStruct((rows, cols), BF16))
    for arr, lo, hi in cast_t:
        _, cols = arr.shape
        n_blk = (hi - lo) // LANES
        assert n_blk * LANES == hi - lo and n_blk <= m_tiles * f_steps

        def block(m, f, n_blk=n_blk):
            return jnp.minimum(m * f_steps + f, n_blk - 1)

        in_specs.append(pl.BlockSpec(
            (pl.Element(LANES), pl.Element(cols)),
            lambda m, f, block=block, lo=lo: (
                pl.multiple_of(lo + LANES * block(m, f), math.gcd(lo, LANES)), 0)))
        out_specs.append(pl.BlockSpec((cols, LANES), lambda m, f, block=block: (0, block(m, f))))
        out_shape.append(jax.ShapeDtypeStruct((cols, hi - lo), BF16))
    outs = pl.pallas_call(
        functools.partial(_ffn_kernel, transposed=(False,) * len(cast) + (True,) * len(cast_t)),
        out_shape=out_shape,
        grid=(m_tiles, f_steps),
        in_specs=in_specs,
        out_specs=out_specs,
        scratch_shapes=[pltpu.VMEM((tm, d), BF16), pltpu.VMEM((tm, d), F32)],
        compiler_params=_params("parallel", "arbitrary"),
        name="ffn",
    )(x, g_pre, w_gate, w_up, w_down, g_post, *[c[0] for c in cast], *[c[0] for c in cast_t])
    return outs[0], outs[1:]


def _proj_kernel(*refs, tn, n_cast):
    x_ref, g_ref, wa_ref, wff_ref, wb_ref, cos_ref, sa_ref, sb_ref = refs[:8]
    o_ref, ff_ref, h_ref = refs[8 + n_cast:11 + n_cast]
    _run_slab_casts(refs[8:8 + n_cast], refs[11 + n_cast:])
    h_ref[...] = _rms(x_ref[...], g_ref[...]).astype(BF16)
    ff_ref[...] = _dot(h_ref[...], wff_ref[...])
    half = ROPE_DIM // 2
    na = wa_ref.shape[1] // tn
    for n in range(na + wb_ref.shape[1] // tn):
        w_tile = (wa_ref[:, n * tn:(n + 1) * tn] if n < na
                  else wb_ref[:, (n - na) * tn:(n - na + 1) * tn])
        y = _dot(h_ref[...], w_tile)
        first_blk = n * tn // LANES
        if first_blk < FK_BLK:
            y = y * FOX_QSCALE
        if first_blk >= MQ_COL // LANES:
            y = y * MEM_QSCALE
        if DQ_BLK <= first_blk < DV_BLK:
            qscale = DIFF_QSCALE if first_blk < DK_BLK else 1.0
            for j in range(tn // LANES):
                blk = y[:, j * LANES:(j + 1) * LANES]
                rot = (blk * cos_ref[...] + pltpu.roll(blk, LANES - half, axis=1) * sa_ref[...]
                       + pltpu.roll(blk, half, axis=1) * sb_ref[...])
                o_ref[:, n * tn + j * LANES:n * tn + (j + 1) * LANES] = (rot * qscale).astype(BF16)
        else:
            o_ref[:, n * tn:(n + 1) * tn] = y.astype(BF16)


def _mix_proj(x, g, w_head, w_tail, cos_t, sa_t, sb_t, *, seq, cast=(), tm=512, tn=512):
    t, d = x.shape
    n_out = FF_COL + w_tail.shape[1]
    assert all((blk * LANES) % tn == 0 for blk in (FK_BLK, DQ_BLK, DK_BLK, DV_BLK))
    assert w_head.shape[1] == FF_COL + LANES and n_out == PROJ_W
    s_tiles = seq // tm
    m_tiles = t // tm
    resident = dict(pipeline_mode=pl.Buffered(1))
    in_specs = [
        pl.BlockSpec((tm, d), lambda m: (m, 0)),
        pl.BlockSpec((1, d), lambda m: (0, 0)),
        pl.BlockSpec((d, FF_COL), lambda m: (0, 0), **resident),
        pl.BlockSpec((d, LANES), lambda m: (0, FF_COL // LANES), **resident),
        pl.BlockSpec((d, w_tail.shape[1]), lambda m: (0, 0), **resident),
        pl.BlockSpec((tm, LANES), lambda m: (m % s_tiles, 0)),
        pl.BlockSpec((tm, LANES), lambda m: (m % s_tiles, 0)),
        pl.BlockSpec((tm, LANES), lambda m: (m % s_tiles, 0)),
    ]
    out_specs = [pl.BlockSpec((tm, n_out), lambda m: (m, 0)),
                 pl.BlockSpec((tm, LANES), lambda m: (m, 0)),
                 pl.BlockSpec((tm, d), lambda m: (m, 0))]
    out_shape = [jax.ShapeDtypeStruct((t, n_out), BF16), jax.ShapeDtypeStruct((t, LANES), F32),
                 jax.ShapeDtypeStruct((t, d), BF16)]
    cast_in, cast_out, cast_shape = _slab_cast_specs(cast, m_tiles, lambda m: m)
    in_specs, out_specs, out_shape = in_specs + cast_in, out_specs + cast_out, out_shape + cast_shape
    outs = pl.pallas_call(
        functools.partial(_proj_kernel, tn=tn, n_cast=len(cast)),
        out_shape=out_shape,
        grid=(m_tiles,),
        in_specs=in_specs,
        out_specs=out_specs,
        compiler_params=_params("parallel"),
        name="mix_proj",
    )(x, g, w_head, w_head, w_tail, cos_t, sa_t, sb_t, *[arr for arr, _ in cast])
    return outs[0], outs[1], outs[2], outs[3:]


def _rope_tables(seq):
    half = ROPE_DIM // 2
    pos = jnp.arange(seq, dtype=F32)
    inv_freq = ROPE_THETA ** (-jnp.arange(0, ROPE_DIM, 2, dtype=F32) / ROPE_DIM)
    ang = pos[:, None] * inv_freq[None, :]
    cos, sin = jnp.cos(ang), jnp.sin(ang)
    ones = jnp.ones((seq, DIFF_QK_DIM - ROPE_DIM), F32)
    zeros_h = jnp.zeros((seq, half), F32)
    zeros_r = jnp.zeros((seq, DIFF_QK_DIM - ROPE_DIM), F32)
    cos_m = jnp.concatenate([cos, cos, ones], axis=1)
    sa_m = jnp.concatenate([-sin, zeros_h, zeros_r], axis=1)
    sb_m = jnp.concatenate([zeros_h, sin, zeros_r], axis=1)
    rep = LANES // DIFF_QK_DIM
    return (jnp.tile(cos_m, (1, rep)), jnp.tile(sa_m, (1, rep)), jnp.tile(sb_m, (1, rep)))


def _fgate_kernel(ff_ref, bias_ref, fp_ref, frow_ref, fcol_ref, *, cb):
    seq = ff_ref.shape[0]
    head_lane = lax.broadcasted_iota(jnp.int32, (cb, LANES), 1) < FOX_HEADS
    z = ff_ref[...] + bias_ref[...]
    lf = jnp.minimum(z, 0.0) - jnp.log1p(jnp.exp(-jnp.abs(z)))
    r = lax.broadcasted_iota(jnp.int32, (cb, cb), 0)
    c = lax.broadcasted_iota(jnp.int32, (cb, cb), 1)
    tri = (r >= c).astype(F32)
    carry = jnp.zeros((1, LANES), F32)
    for i in range(seq // cb):
        cs = jnp.dot(tri, lf[i * cb:(i + 1) * cb], precision=lax.Precision.HIGHEST,
                     preferred_element_type=F32) + carry
        carry = cs[cb - 1:cb, :]
        cs2 = jnp.where(head_lane, cs * LOG2E, 0.0)
        fcol_ref[i * cb:(i + 1) * cb, :] = cs2
        hi = cs2.astype(BF16).astype(F32)
        mid = (cs2 - hi).astype(BF16).astype(F32)
        lo = (cs2 - hi - mid).astype(BF16).astype(F32)
        pieces = hi + pltpu.roll(mid, FOX_HEADS, axis=1) + pltpu.roll(lo, 2 * FOX_HEADS, axis=1)
        fp_ref[i * cb:(i + 1) * cb, :] = pieces.astype(BF16)
    frow_ref[...] = fcol_ref[...].T[:FOX_HEADS]


def _fox_gate(ff, bias, *, batch, seq, cb=256):
    ff = ff.reshape(batch, seq, LANES)
    return pl.pallas_call(
        functools.partial(_fgate_kernel, cb=cb),
        out_shape=(jax.ShapeDtypeStruct((batch, seq, LANES), BF16),
                   jax.ShapeDtypeStruct((batch, FOX_HEADS, seq), F32)),
        grid=(batch,),
        in_specs=[pl.BlockSpec((None, seq, LANES), lambda b: (b, 0, 0)),
                  pl.BlockSpec((1, LANES), lambda b: (0, 0))],
        out_specs=(pl.BlockSpec((None, seq, LANES), lambda b: (b, 0, 0)),
                   pl.BlockSpec((None, FOX_HEADS, seq), lambda b: (b, 0, 0))),
        scratch_shapes=[pltpu.VMEM((seq, LANES), F32)],
        compiler_params=_params("parallel"),
        name="fox_gate",
    )(ff, bias)


def _dot_tn(a, b):
    return lax.dot_general(a, b, (((0,), (0,)), ((), ())), preferred_element_type=F32)


def _flash_attend(i, heads, logits, values, ft2, visible, finish, side_job,
                  sa_ref, sb_ref, m_ref, l_ref, acc_ref):
    def step(h, s, j):
        m = m_ref[h]
        m_new = jnp.maximum(m, jnp.max(s, axis=0, keepdims=True) + ft2[h])
        alpha = jnp.exp2(m - m_new)
        p = jnp.exp2(s - (m_new - ft2[h]))
        m_ref[h] = m_new
        l_ref[h] = alpha * l_ref[h] + jnp.sum(p, axis=0, keepdims=True)
        acc_ref[h] = alpha * acc_ref[h] + _dot_tn(values(h, j), p.astype(BF16))

    for h in heads:
        m_ref[h] = jnp.full(m_ref.shape[1:], NEG_INF, F32)
        l_ref[h] = jnp.zeros(l_ref.shape[1:], F32)
        acc_ref[h] = jnp.zeros(acc_ref.shape[1:], F32)
        sa_ref[h] = logits(h, 0)
    side_job()

    def pair(jj, carry):
        j = 2 * jj
        for h in heads:
            sb_ref[h] = logits(h, j + 1)
            step(h, sa_ref[h], j)
        for h in heads:
            sa_ref[h] = logits(h, j + 2)
            step(h, sb_ref[h], j + 1)
        return carry

    lax.fori_loop(0, i // 2, pair, 0)

    def last(s_ref):
        for h in heads:
            step(h, jnp.where(visible, s_ref[h], NEG_INF), i)
            finish(h, acc_ref[h] / l_ref[h])

    @pl.when(i % 2 == 0)
    def _():
        last(sa_ref)

    @pl.when(i % 2 == 1)
    def _():
        for h in heads:
            sb_ref[h] = logits(h, i)
            step(h, sa_ref[h], i - 1)
        last(sb_ref)


def _flash_scratch(hb, width, tq, cols):
    return [pltpu.VMEM((hb, tq, cols), F32), pltpu.VMEM((hb, tq, cols), F32),
            pltpu.VMEM((hb, 1, cols), F32), pltpu.VMEM((hb, 1, cols), F32),
            pltpu.VMEM((hb, width, cols), F32)]


def _fox_kernel(*refs, tq, hb, n_cast):
    q_ref, k_ref, v_ref, fp_ref, fr_ref = refs[:5]
    o_ref = refs[5 + n_cast]
    scratch = refs[-5:]
    side_job = functools.partial(_run_slab_casts, refs[5:5 + n_cast], refs[6 + n_cast:-5])
    i = pl.program_id(2)
    h0 = pl.program_id(1) * hb
    heads = range(hb)
    ft2 = [fr_ref[h, pl.ds(i, 1), :] for h in heads]

    lane = lax.broadcasted_iota(jnp.int32, (tq, LANES), 1)
    q_ext = []
    for h in heads:
        mine = (lane % FOX_HEADS == h0 + h) & (lane < 3 * FOX_HEADS)
        sel = jnp.where(mine, -1.0, 0.0).astype(BF16)
        q_ext.append(jnp.concatenate([q_ref[:, h * FOX_DIM:(h + 1) * FOX_DIM], sel], axis=1))

    def logits(h, j):
        start = pl.multiple_of(j * tq, tq)
        k_ext = jnp.concatenate([k_ref[pl.ds(start, tq), h * FOX_DIM:(h + 1) * FOX_DIM],
                                 fp_ref[pl.ds(start, tq), :]], axis=1)
        return _dot_nt(k_ext, q_ext[h])

    def values(h, j):
        return v_ref[pl.ds(pl.multiple_of(j * tq, tq), tq), h * FOX_DIM:(h + 1) * FOX_DIM]

    def finish(h, o_t):
        o_ref[:, h * FOX_DIM:(h + 1) * FOX_DIM] = o_t.T.astype(BF16)

    krow = lax.broadcasted_iota(jnp.int32, (tq, tq), 0)
    qcol = lax.broadcasted_iota(jnp.int32, (tq, tq), 1)
    _flash_attend(i, heads, logits, values, ft2, krow <= qcol, finish, side_job, *scratch)


def _fox_attn(proj, fp, frow, *, batch, seq, cast=(), tq=256, hb=8):
    proj3 = proj.reshape(batch, seq, PROJ_W)
    n_q = seq // tq
    frow4 = frow.reshape(batch, FOX_HEADS, n_q, tq)
    w = hb * FOX_DIM
    groups = FOX_HEADS // hb
    assert groups == 1 or not cast
    cast_in, cast_out, cast_shape = _slab_cast_specs(cast, batch * n_q, lambda b, g, i: b * n_q + i)
    outs = pl.pallas_call(
        functools.partial(_fox_kernel, tq=tq, hb=hb, n_cast=len(cast)),
        out_shape=[jax.ShapeDtypeStruct((batch, seq, FOX_W), BF16)] + cast_shape,
        grid=(batch, groups, n_q),
        in_specs=[
            pl.BlockSpec((None, tq, w), lambda b, g, i: (b, i, g)),
            pl.BlockSpec((None, seq, w), lambda b, g, i: (b, 0, groups + g)),
            pl.BlockSpec((None, seq, w), lambda b, g, i: (b, 0, 2 * groups + g)),
            pl.BlockSpec((None, seq, LANES), lambda b, g, i: (b, 0, 0)),
            pl.BlockSpec((None, hb, n_q, tq), lambda b, g, i: (b, g, 0, 0)),
        ] + cast_in,
        out_specs=[pl.BlockSpec((None, tq, w), lambda b, g, i: (b, i, g))] + cast_out,
        scratch_shapes=_flash_scratch(hb, FOX_DIM, tq, tq),
        compiler_params=_params("parallel", "parallel", "arbitrary"),
        name="fox_attn",
    )(proj3, proj3, proj3, fp, frow4, *[arr for arr, _ in cast])
    return outs[0].reshape(batch * seq, FOX_W), outs[1:]


def _diff_kernel(*refs, tq, hb, lam_init, n_cast):
    lamv_ref, g_ref, q_ref, k_ref, v_ref = refs[:5]
    o_ref = refs[5 + n_cast]
    scratch = refs[-5:]
    side_job = functools.partial(_run_slab_casts, refs[5:5 + n_cast], refs[6 + n_cast:-5])
    i = pl.program_id(2)
    heads = range(hb)
    lv = lamv_ref[...]
    lam = (jnp.exp(jnp.sum(lv[0:1] * lv[1:2], axis=1, keepdims=True))
           - jnp.exp(jnp.sum(lv[2:3] * lv[3:4], axis=1, keepdims=True)) + lam_init)

    lane = lax.broadcasted_iota(jnp.int32, (tq, LANES), 1)
    qs = []
    for h in heads:
        q = q_ref[:, h * LANES:(h + 1) * LANES].astype(F32)
        qs.append(jnp.concatenate([jnp.where(lane < DIFF_QK_DIM, q, 0.0),
                                   jnp.where(lane >= DIFF_QK_DIM, q, 0.0)], axis=0).astype(BF16))
    zero = jnp.zeros((1, 2 * tq), F32)

    def logits(h, j):
        start = pl.multiple_of(j * tq, tq)
        return _dot_nt(k_ref[pl.ds(start, tq), h * LANES:(h + 1) * LANES], qs[h])

    def values(h, j):
        return v_ref[pl.ds(pl.multiple_of(j * tq, tq), tq), h * LANES:(h + 1) * LANES]

    def finish(h, o_t):
        yd = (o_t[:, :tq] - lam * o_t[:, tq:]).T
        o_ref[:, h * LANES:(h + 1) * LANES] = (
            _rms(yd, g_ref[...]) * (1.0 - lam_init)).astype(BF16)

    krow = lax.broadcasted_iota(jnp.int32, (tq, 2 * tq), 0)
    qcol = lax.broadcasted_iota(jnp.int32, (tq, 2 * tq), 1)
    qcol = jnp.where(qcol >= tq, qcol - tq, qcol)
    visible = krow // CHUNK <= qcol // CHUNK
    _flash_attend(i, heads, logits, values, [zero] * hb, visible, finish, side_job, *scratch)


def _diff_attn(proj, lamv, g, *, batch, seq, lam_init, cast=(), tq=256, hb=4):
    proj3 = proj.reshape(batch, seq, PROJ_W)
    n_q = seq // tq
    w = hb * LANES
    groups = DIFF_HEADS // hb
    assert groups == 1 or not cast
    dq, dk, dv = (DQ_BLK * LANES) // w, (DK_BLK * LANES) // w, (DV_BLK * LANES) // w
    cast_in, cast_out, cast_shape = _slab_cast_specs(cast, batch * n_q, lambda b, g, i: b * n_q + i)
    outs = pl.pallas_call(
        functools.partial(_diff_kernel, tq=tq, hb=hb, lam_init=lam_init, n_cast=len(cast)),
        out_shape=[jax.ShapeDtypeStruct((batch, seq, DIFF_V_W), BF16)] + cast_shape,
        grid=(batch, groups, n_q),
        in_specs=[
            pl.BlockSpec((4, DIFF_QK_DIM), lambda b, g, i: (0, 0)),
            pl.BlockSpec((1, DIFF_V_DIM), lambda b, g, i: (0, 0)),
            pl.BlockSpec((None, tq, w), lambda b, g, i: (b, i, dq + g)),
            pl.BlockSpec((None, seq, w), lambda b, g, i: (b, 0, dk + g)),
            pl.BlockSpec((None, seq, w), lambda b, g, i: (b, 0, dv + g)),
        ] + cast_in,
        out_specs=[pl.BlockSpec((None, tq, w), lambda b, g, i: (b, i, g))] + cast_out,
        scratch_shapes=_flash_scratch(hb, DIFF_V_DIM, tq, 2 * tq),
        compiler_params=_params("parallel", "parallel", "arbitrary"),
        name="diff_attn",
    )(lamv, g, proj3, proj3, proj3, *[arr for arr, _ in cast])
    return outs[0].reshape(batch * seq, DIFF_V_W), outs[1:]


def _mem_kv_kernel(mem_ref, g_ref, w_ref, o_ref):
    o_ref[...] = _dot(_rms(mem_ref[...], g_ref[...]).astype(BF16), w_ref[...]).astype(BF16)


def _mem_kv(mem, g, w):
    batch, n_mem, d = mem.shape
    return pl.pallas_call(
        _mem_kv_kernel,
        out_shape=jax.ShapeDtypeStruct((batch, n_mem, 2 * MEM_W), BF16),
        grid=(batch,),
        in_specs=[pl.BlockSpec((None, n_mem, d), lambda b: (b, 0, 0)),
                  pl.BlockSpec((1, d), lambda b: (0, 0)),
                  pl.BlockSpec((d, 2 * MEM_W), lambda b: (0, 0))],
        out_specs=pl.BlockSpec((None, n_mem, 2 * MEM_W), lambda b: (b, 0, 0)),
        compiler_params=_params("parallel"),
        name="mem_kv",
    )(mem, g, w)


def _mem_attn_kernel(q_ref, kv_ref, o_ref):
    logits = [_dot_nt(kv_ref[:, h * MEM_DIM:(h + 1) * MEM_DIM],
                      q_ref[:, h * MEM_DIM:(h + 1) * MEM_DIM]) for h in range(MEM_HEADS)]
    for h in range(MEM_HEADS):
        v = kv_ref[:, MEM_W + h * MEM_DIM:MEM_W + (h + 1) * MEM_DIM]
        s = logits[h]
        p = jnp.exp2(s - jnp.max(s, axis=0, keepdims=True))
        l = jnp.sum(p, axis=0, keepdims=True)
        o_t = _dot_tn(v, p.astype(BF16)) / l
        o_ref[:, h * MEM_DIM:(h + 1) * MEM_DIM] = o_t.T.astype(BF16)


def _mem_attn(proj, mkv, *, batch, seq, tq=512):
    proj3 = proj.reshape(batch, seq, PROJ_W)
    n_mem = mkv.shape[1]
    out = pl.pallas_call(
        _mem_attn_kernel,
        out_shape=jax.ShapeDtypeStruct((batch, seq, MEM_W), BF16),
        grid=(batch, seq // tq),
        in_specs=[pl.BlockSpec((None, tq, MEM_W), lambda b, i: (b, i, MQ_COL // MEM_W)),
                  pl.BlockSpec((None, n_mem, 2 * MEM_W), lambda b, i: (b, 0, 0))],
        out_specs=pl.BlockSpec((None, tq, MEM_W), lambda b, i: (b, i, 0)),
        compiler_params=_params("parallel", "arbitrary"),
        name="mem_attn",
    )(proj3, mkv)
    return out.reshape(batch * seq, MEM_W)


def _merge_kernel(x_ref, h_ref, yf_ref, yd_ref, ym_ref, wgf_ref, wgd_ref, wgm_ref,
                  bf_ref, bd_ref, bm_ref, wf_ref, wd_ref, wm_ref, wo_ref, gpost_ref,
                  o_ref, acc_ref):
    n = pl.program_id(1)

    @pl.when(n == 0)
    def _():
        acc_ref[...] = jnp.zeros_like(acc_ref)

    h = h_ref[...]

    def gated(wg_ref, b_ref, y_ref, w_ref):
        z = _dot(h, wg_ref[...]) + b_ref[...]
        return (1.0 / (1.0 + jnp.exp(-z))) * _dot(y_ref[...], w_ref[...])

    merged = (gated(wgf_ref, bf_ref, yf_ref, wf_ref) + gated(wgd_ref, bd_ref, yd_ref, wd_ref)
              + gated(wgm_ref, bm_ref, ym_ref, wm_ref))
    acc_ref[...] += _dot(merged.astype(BF16), wo_ref[...])

    @pl.when(n == pl.num_programs(1) - 1)
    def _():
        def emit(sl, y):
            o_ref[sl, :] = x_ref[sl, :] + y

        _rms_rows(acc_ref, gpost_ref[...], emit, F32_ROWS)


def _merge(x, h, y_fox, y_diff, y_mem, w_gate, b_gate, w_fox, w_diff, w_mem, w_out, g_post,
           *, tm=512, tn=512):
    t, d = x.shape
    nt = d // tn
    row = lambda m, n: (m, 0)
    return pl.pallas_call(
        _merge_kernel,
        out_shape=jax.ShapeDtypeStruct((t, d), F32),
        grid=(t // tm, nt),
        in_specs=[
            pl.BlockSpec((tm, d), row),
            pl.BlockSpec((tm, d), row),
            pl.BlockSpec((tm, FOX_W), row),
            pl.BlockSpec((tm, DIFF_V_W), row),
            pl.BlockSpec((tm, MEM_W), row),
            pl.BlockSpec((d, tn), lambda m, n: (0, n)),
            pl.BlockSpec((d, tn), lambda m, n: (0, nt + n)),
            pl.BlockSpec((d, tn), lambda m, n: (0, 2 * nt + n)),
            pl.BlockSpec((1, tn), lambda m, n: (0, n)),
            pl.BlockSpec((1, tn), lambda m, n: (0, nt + n)),
            pl.BlockSpec((1, tn), lambda m, n: (0, 2 * nt + n)),
            pl.BlockSpec((FOX_W, tn), lambda m, n: (0, n)),
            pl.BlockSpec((DIFF_V_W, tn), lambda m, n: (0, n)),
            pl.BlockSpec((MEM_W, tn), lambda m, n: (0, n)),
            pl.BlockSpec((tn, d), lambda m, n: (n, 0)),
            pl.BlockSpec((1, d), lambda m, n: (0, 0)),
        ],
        out_specs=pl.BlockSpec((tm, d), row),
        scratch_shapes=[pltpu.VMEM((tm, d), F32)],
        compiler_params=_params("parallel", "arbitrary"),
        name="merge",
    )(x, h, y_fox, y_diff, y_mem, w_gate, w_gate, w_gate, b_gate, b_gate, b_gate,
      w_fox, w_diff, w_mem, w_out, g_post)


def kernel(x, mem, ffn1_pre_g, ffn1_w_gate, ffn1_w_up, ffn1_w_down, ffn1_post_g, mix_pre_g, w_in, fox_f_bias, diff_lambda_q1, diff_lambda_k1, diff_lambda_q2, diff_lambda_k2, diff_head_g, mem_norm_g, w_mem_kv, w_branch_fox, w_branch_diff, w_branch_mem, w_merge_gate, b_merge_gate, w_out, mix_post_g, ffn2_pre_g, ffn2_w_gate, ffn2_w_up, ffn2_w_down, ffn2_post_g):
    batch, seq, d = x.shape
    depth = w_in.shape[0]
    xt = x.reshape(batch * seq, d)
    cos_t, sa_t, sb_t = _rope_tables(seq)

    def row(v):
        return v.reshape(1, -1).astype(F32)

    for l in range(depth):
        bf = lambda w: w[l].astype(BF16)
        w_in_t = w_in[l].T
        xt, (w_head, w_tail) = _ffn(
            xt, row(ffn1_pre_g[l]), bf(ffn1_w_gate), bf(ffn1_w_up), bf(ffn1_w_down),
            row(ffn1_post_g[l]),
            cast_t=[(w_in_t, 0, FF_COL + LANES), (w_in_t, FF_COL + FOX_HEADS, w_in.shape[2])])

        small = [w_out, w_branch_fox, w_branch_diff, w_branch_mem]
        proj, ff, h_mix, (wb_out, wb_fox, wb_diff, wb_mem) = _mix_proj(
            xt, row(mix_pre_g[l]), w_head, w_tail, cos_t, sa_t, sb_t, seq=seq,
            cast=[(w, l) for w in small])

        bias = jnp.pad(fox_f_bias[l].astype(F32), (0, LANES - FOX_HEADS)).reshape(1, LANES)
        fp, frow = _fox_gate(ff, bias, batch=batch, seq=seq)
        y_fox, (w2_gate, w2_up, w2_down) = _fox_attn(
            proj, fp, frow, batch=batch, seq=seq,
            cast=[(w, l) for w in (ffn2_w_gate, ffn2_w_up, ffn2_w_down)])

        lam_init = 0.8 - 0.6 * math.exp(-0.3 * l)
        lamv = jnp.stack([diff_lambda_q1[l], diff_lambda_k1[l], diff_lambda_q2[l],
                          diff_lambda_k2[l]]).astype(F32)
        y_diff, (wb_merge,) = _diff_attn(proj, lamv, row(diff_head_g[l]), batch=batch, seq=seq,
                                         lam_init=lam_init, cast=[(w_merge_gate, l)])

        mkv = _mem_kv(mem, row(mem_norm_g[l]), bf(w_mem_kv))
        y_mem = _mem_attn(proj, mkv, batch=batch, seq=seq)

        xt = _merge(xt, h_mix, y_fox, y_diff, y_mem, wb_merge,
                    row(b_merge_gate[l]), wb_fox, wb_diff, wb_mem, wb_out, row(mix_post_g[l]))

        xt, _ = _ffn(xt, row(ffn2_pre_g[l]), w2_gate, w2_up, w2_down, row(ffn2_post_g[l]))

    return xt.reshape(batch, seq, d)
```

```python
import functools
import math

import jax
import jax.numpy as jnp
from jax import lax
from jax.experimental import pallas as pl
from jax.experimental.pallas import tpu as pltpu

D_MODEL = 2048
CHUNK = 64
EPS = 1e-6
ROPE_THETA = 500000.0

FOX_HEADS = 8
FOX_DIM = 128
FOX_W = FOX_HEADS * FOX_DIM

DIFF_HEADS = 4
DIFF_QK_DIM = 64
DIFF_V_DIM = 2 * DIFF_QK_DIM
DIFF_QK_W = DIFF_HEADS * 2 * DIFF_QK_DIM
DIFF_V_W = DIFF_HEADS * DIFF_V_DIM
ROPE_DIM = DIFF_QK_DIM // 4

MEM_HEADS = 4
MEM_DIM = 128
MEM_W = MEM_HEADS * MEM_DIM

LANES = 128
F32_ROWS = 8
BF16_ROWS = 16
RMS_ROWS_IN_FLIGHT = 128
PROJ_W = 3 * FOX_W + 2 * DIFF_QK_W + DIFF_V_W + MEM_W
FQ_BLK, FK_BLK, FV_BLK = 0, FOX_HEADS, 2 * FOX_HEADS
DQ_BLK = 3 * FOX_HEADS
DK_BLK = DQ_BLK + DIFF_HEADS
DV_BLK = DK_BLK + DIFF_HEADS
MQ_COL = 3 * FOX_W + 2 * DIFF_QK_W + DIFF_V_W
FF_COL = 3 * FOX_W

VMEM_LIMIT = 56 * 1024 * 1024
BF16 = jnp.bfloat16
F32 = jnp.float32
NEG_INF = float("-inf")
LOG2E = math.log2(math.e)
FOX_QSCALE = FOX_DIM ** -0.5 * LOG2E
DIFF_QSCALE = DIFF_QK_DIM ** -0.5 * LOG2E
MEM_QSCALE = MEM_DIM ** -0.5 * LOG2E


def _dot(a, b):
    return jnp.dot(a, b, preferred_element_type=F32)


def _dot_nt(a, b):
    return lax.dot_general(a, b, (((1,), (1,)), ((), ())), preferred_element_type=F32)


def _rms(x, g):
    return x * lax.rsqrt(jnp.mean(x * x, axis=-1, keepdims=True) + EPS) * g


def _rms_rows(src_ref, g, emit, rows):
    def body(r, carry):
        sl = pl.ds(pl.multiple_of(r * rows, rows), rows)
        emit(sl, _rms(src_ref[sl, :], g))
        return carry

    lax.fori_loop(0, src_ref.shape[0] // rows, body, 0, unroll=RMS_ROWS_IN_FLIGHT // rows)


def _params(*sem):
    return pltpu.CompilerParams(dimension_semantics=sem, vmem_limit_bytes=VMEM_LIMIT)


def _slab_cast_specs(cast, n_steps, step_index):
    in_specs, out_specs, out_shape = [], [], []
    for arr, layer in cast:
        _, rows, cols = arr.shape
        br = rows // n_steps
        assert br * n_steps == rows and br % BF16_ROWS == 0
        in_specs.append(pl.BlockSpec((None, br, cols),
                                     lambda *g, layer=layer: (layer, step_index(*g), 0)))
        out_specs.append(pl.BlockSpec((br, cols), lambda *g: (step_index(*g), 0)))
        out_shape.append(jax.ShapeDtypeStruct((rows, cols), BF16))
    return in_specs, out_specs, out_shape


def _run_slab_casts(src_refs, dst_refs):
    for src_ref, dst_ref in zip(src_refs, dst_refs):
        dst_ref[...] = src_ref[...].astype(BF16)


def _ffn_kernel(*refs, transposed):
    n_cast = len(transposed)
    x_ref, gpre_ref, wg_ref, wu_ref, wd_ref, gpost_ref = refs[:6]
    o_ref = refs[6 + n_cast]
    h_ref, acc_ref = refs[-2:]
    f = pl.program_id(1)

    @pl.when(f == 0)
    def _():
        def emit(sl, y):
            h_ref[sl, :] = y.astype(BF16)
            acc_ref[sl, :] = jnp.zeros_like(y)

        _rms_rows(x_ref, gpre_ref[...], emit, BF16_ROWS)

    jobs = list(zip(refs[6:6 + n_cast], refs[7 + n_cast:7 + 2 * n_cast], transposed))

    def run_casts(part, parts=3):
        for src_ref, dst_ref, flip in jobs[part::parts]:
            blk = src_ref[...]
            dst_ref[...] = (blk.T if flip else blk).astype(BF16)

    h = h_ref[...]
    run_casts(0)
    g = _dot(h, wg_ref[...])
    run_casts(1)
    u = _dot(h, wu_ref[...])
    run_casts(2)
    a = (g * (1.0 / (1.0 + jnp.exp(-g)))) * u
    acc_ref[...] += _dot(a.astype(BF16), wd_ref[...])

    @pl.when(f == pl.num_programs(1) - 1)
    def _():
        def emit(sl, y):
            o_ref[sl, :] = x_ref[sl, :] + y

        _rms_rows(acc_ref, 0.5 * gpost_ref[...], emit, F32_ROWS)


def _ffn(x, g_pre, w_gate, w_up, w_down, g_post, *, cast=(), cast_t=(), tm=512, tf=512):
    t, d = x.shape
    d_ff = w_down.shape[0]
    m_tiles, f_steps = t // tm, d_ff // tf

    def up_spec(w):
        if w.ndim == 3:
            assert w.shape == (f_steps, d, tf)
            return pl.BlockSpec((None, d, tf), lambda m, f: (f, 0, 0))
        return pl.BlockSpec((d, tf), lambda m, f: (0, f))

    in_specs = [
        pl.BlockSpec((tm, d), lambda m, f: (m, 0)),
        pl.BlockSpec((1, d), lambda m, f: (0, 0)),
        up_spec(w_gate),
        up_spec(w_up),
        pl.BlockSpec((tf, d), lambda m, f: (f, 0)),
        pl.BlockSpec((1, d), lambda m, f: (0, 0)),
    ]
    out_specs = [pl.BlockSpec((tm, d), lambda m, f: (m, 0))]
    out_shape = [jax.ShapeDtypeStruct((t, d), F32)]
    for arr, layer in cast:
        _, rows, cols = arr.shape
        br = rows // m_tiles
        assert br * m_tiles == rows and cols == d_ff
        in_specs.append(pl.BlockSpec((None, br, tf), lambda m, f, layer=layer: (layer, m, f)))
        out_specs.append(pl.BlockSpec((None, br, tf), lambda m, f: (f, m, 0)))
        out_shape.append(jax.ShapeDtypeStruct((f_steps, rows, tf), BF16))
    for arr, lo, hi in cast_t:
        _, cols = arr.shape
        n_blk = (hi - lo) // LANES
        assert n_blk * LANES == hi - lo and n_blk <= m_tiles * f_steps

        def block(m, f, n_blk=n_blk):
            return jnp.minimum(m * f_steps + f, n_blk - 1)

        in_specs.append(pl.BlockSpec(
            (pl.Element(LANES), pl.Element(cols)),
            lambda m, f, block=block, lo=lo: (
                pl.multiple_of(lo + LANES * block(m, f), math.gcd(lo, LANES)), 0)))
        out_specs.append(pl.BlockSpec((cols, LANES), lambda m, f, block=block: (0, block(m, f))))
        out_shape.append(jax.ShapeDtypeStruct((cols, hi - lo), BF16))
    outs = pl.pallas_call(
        functools.partial(_ffn_kernel, transposed=(False,) * len(cast) + (True,) * len(cast_t)),
        out_shape=out_shape,
        grid=(m_tiles, f_steps),
        in_specs=in_specs,
        out_specs=out_specs,
        scratch_shapes=[pltpu.VMEM((tm, d), BF16), pltpu.VMEM((tm, d), F32)],
        compiler_params=_params("parallel", "arbitrary"),
        name="ffn",
    )(x, g_pre, w_gate, w_up, w_down, g_post, *[c[0] for c in cast], *[c[0] for c in cast_t])
    return outs[0], outs[1:]


def _proj_kernel(*refs, tn, n_cast):
    x_ref, g_ref, wa_ref, wff_ref, wb_ref, cos_ref, sa_ref, sb_ref = refs[:8]
    o_ref, ff_ref, h_ref = refs[8 + n_cast:11 + n_cast]
    _run_slab_casts(refs[8:8 + n_cast], refs[11 + n_cast:])
    h_ref[...] = _rms(x_ref[...], g_ref[...]).astype(BF16)
    ff_ref[...] = _dot(h_ref[...], wff_ref[...])
    half = ROPE_DIM // 2
    na = wa_ref.shape[1] // tn
    for n in range(na + wb_ref.shape[1] // tn):
        w_tile = (wa_ref[:, n * tn:(n + 1) * tn] if n < na
                  else wb_ref[:, (n - na) * tn:(n - na + 1) * tn])
        y = _dot(h_ref[...], w_tile)
        first_blk = n * tn // LANES
        if first_blk < FK_BLK:
            y = y * FOX_QSCALE
        if first_blk >= MQ_COL // LANES:
            y = y * MEM_QSCALE
        if DQ_BLK <= first_blk < DV_BLK:
            qscale = DIFF_QSCALE if first_blk < DK_BLK else 1.0
            for j in range(tn // LANES):
                blk = y[:, j * LANES:(j + 1) * LANES]
                rot = (blk * cos_ref[...] + pltpu.roll(blk, LANES - half, axis=1) * sa_ref[...]
                       + pltpu.roll(blk, half, axis=1) * sb_ref[...])
                o_ref[:, n * tn + j * LANES:n * tn + (j + 1) * LANES] = (rot * qscale).astype(BF16)
        else:
            o_ref[:, n * tn:(n + 1) * tn] = y.astype(BF16)


def _mix_proj(x, g, w_head, w_tail, cos_t, sa_t, sb_t, *, seq, cast=(), tm=512, tn=512):
    t, d = x.shape
    n_out = FF_COL + w_tail.shape[1]
    assert all((blk * LANES) % tn == 0 for blk in (FK_BLK, DQ_BLK, DK_BLK, DV_BLK))
    assert w_head.shape[1] == FF_COL + LANES and n_out == PROJ_W
    s_tiles = seq // tm
    m_tiles = t // tm
    resident = dict(pipeline_mode=pl.Buffered(1))
    in_specs = [
        pl.BlockSpec((tm, d), lambda m: (m, 0)),
        pl.BlockSpec((1, d), lambda m: (0, 0)),
        pl.BlockSpec((d, FF_COL), lambda m: (0, 0), **resident),
        pl.BlockSpec((d, LANES), lambda m: (0, FF_COL // LANES), **resident),
        pl.BlockSpec((d, w_tail.shape[1]), lambda m: (0, 0), **resident),
        pl.BlockSpec((tm, LANES), lambda m: (m % s_tiles, 0)),
        pl.BlockSpec((tm, LANES), lambda m: (m % s_tiles, 0)),
        pl.BlockSpec((tm, LANES), lambda m: (m % s_tiles, 0)),
    ]
    out_specs = [pl.BlockSpec((tm, n_out), lambda m: (m, 0)),
                 pl.BlockSpec((tm, LANES), lambda m: (m, 0)),
                 pl.BlockSpec((tm, d), lambda m: (m, 0))]
    out_shape = [jax.ShapeDtypeStruct((t, n_out), BF16), jax.ShapeDtypeStruct((t, LANES), F32),
                 jax.ShapeDtypeStruct((t, d), BF16)]
    cast_in, cast_out, cast_shape = _slab_cast_specs(cast, m_tiles, lambda m: m)
    in_specs, out_specs, out_shape = in_specs + cast_in, out_specs + cast_out, out_shape + cast_shape
    outs = pl.pallas_call(
        functools.partial(_proj_kernel, tn=tn, n_cast=len(cast)),
        out_shape=out_shape,
        grid=(m_tiles,),
        in_specs=in_specs,
        out_specs=out_specs,
        compiler_params=_params("parallel"),
        name="mix_proj",
    )(x, g, w_head, w_head, w_tail, cos_t, sa_t, sb_t, *[arr for arr, _ in cast])
    return outs[0], outs[1], outs[2], outs[3:]


def _rope_tables(seq):
    half = ROPE_DIM // 2
    pos = jnp.arange(seq, dtype=F32)
    inv_freq = ROPE_THETA ** (-jnp.arange(0, ROPE_DIM, 2, dtype=F32) / ROPE_DIM)
    ang = pos[:, None] * inv_freq[None, :]
    cos, sin = jnp.cos(ang), jnp.sin(ang)
    ones = jnp.ones((seq, DIFF_QK_DIM - ROPE_DIM), F32)
    zeros_h = jnp.zeros((seq, half), F32)
    zeros_r = jnp.zeros((seq, DIFF_QK_DIM - ROPE_DIM), F32)
    cos_m = jnp.concatenate([cos, cos, ones], axis=1)
    sa_m = jnp.concatenate([-sin, zeros_h, zeros_r], axis=1)
    sb_m = jnp.concatenate([zeros_h, sin, zeros_r], axis=1)
    rep = LANES // DIFF_QK_DIM
    return (jnp.tile(cos_m, (1, rep)), jnp.tile(sa_m, (1, rep)), jnp.tile(sb_m, (1, rep)))


def _fgate_kernel(ff_ref, bias_ref, fp_ref, frow_ref, fcol_ref, *, cb):
    seq = ff_ref.shape[0]
    head_lane = lax.broadcasted_iota(jnp.int32, (cb, LANES), 1) < FOX_HEADS
    z = ff_ref[...] + bias_ref[...]
    lf = jnp.minimum(z, 0.0) - jnp.log1p(jnp.exp(-jnp.abs(z)))
    r = lax.broadcasted_iota(jnp.int32, (cb, cb), 0)
    c = lax.broadcasted_iota(jnp.int32, (cb, cb), 1)
    tri = (r >= c).astype(F32)
    carry = jnp.zeros((1, LANES), F32)
    for i in range(seq // cb):
        cs = jnp.dot(tri, lf[i * cb:(i + 1) * cb], precision=lax.Precision.HIGHEST,
                     preferred_element_type=F32) + carry
        carry = cs[cb - 1:cb, :]
        cs2 = jnp.where(head_lane, cs * LOG2E, 0.0)
        fcol_ref[i * cb:(i + 1) * cb, :] = cs2
        hi = cs2.astype(BF16).astype(F32)
        mid = (cs2 - hi).astype(BF16).astype(F32)
        lo = (cs2 - hi - mid).astype(BF16).astype(F32)
        pieces = hi + pltpu.roll(mid, FOX_HEADS, axis=1) + pltpu.roll(lo, 2 * FOX_HEADS, axis=1)
        fp_ref[i * cb:(i + 1) * cb, :] = pieces.astype(BF16)
    frow_ref[...] = fcol_ref[...].T[:FOX_HEADS]


def _fox_gate(ff, bias, *, batch, seq, cb=256):
    ff = ff.reshape(batch, seq, LANES)
    return pl.pallas_call(
        functools.partial(_fgate_kernel, cb=cb),
        out_shape=(jax.ShapeDtypeStruct((batch, seq, LANES), BF16),
                   jax.ShapeDtypeStruct((batch, FOX_HEADS, seq), F32)),
        grid=(batch,),
        in_specs=[pl.BlockSpec((None, seq, LANES), lambda b: (b, 0, 0)),
                  pl.BlockSpec((1, LANES), lambda b: (0, 0))],
        out_specs=(pl.BlockSpec((None, seq, LANES), lambda b: (b, 0, 0)),
                   pl.BlockSpec((None, FOX_HEADS, seq), lambda b: (b, 0, 0))),
        scratch_shapes=[pltpu.VMEM((seq, LANES), F32)],
        compiler_params=_params("parallel"),
        name="fox_gate",
    )(ff, bias)


def _dot_tn(a, b):
    return lax.dot_general(a, b, (((0,), (0,)), ((), ())), preferred_element_type=F32)


def _flash_attend(i, heads, logits, values, ft2, visible, finish, side_job,
                  sa_ref, sb_ref, m_ref, l_ref, acc_ref):
    def step(h, s, j):
        m = m_ref[h]
        m_new = jnp.maximum(m, jnp.max(s, axis=0, keepdims=True) + ft2[h])
        alpha = jnp.exp2(m - m_new)
        p = jnp.exp2(s - (m_new - ft2[h]))
        m_ref[h] = m_new
        l_ref[h] = alpha * l_ref[h] + jnp.sum(p, axis=0, keepdims=True)
        acc_ref[h] = alpha * acc_ref[h] + _dot_tn(values(h, j), p.astype(BF16))

    for h in heads:
        m_ref[h] = jnp.full(m_ref.shape[1:], NEG_INF, F32)
        l_ref[h] = jnp.zeros(l_ref.shape[1:], F32)
        acc_ref[h] = jnp.zeros(acc_ref.shape[1:], F32)
        sa_ref[h] = logits(h, 0)
    side_job()

    def pair(jj, carry):
        j = 2 * jj
        for h in heads:
            sb_ref[h] = logits(h, j + 1)
            step(h, sa_ref[h], j)
        for h in heads:
            sa_ref[h] = logits(h, j + 2)
            step(h, sb_ref[h], j + 1)
        return carry

    lax.fori_loop(0, i // 2, pair, 0)

    def last(s_ref):
        for h in heads:
            step(h, jnp.where(visible, s_ref[h], NEG_INF), i)
            finish(h, acc_ref[h] / l_ref[h])

    @pl.when(i % 2 == 0)
    def _():
        last(sa_ref)

    @pl.when(i % 2 == 1)
    def _():
        for h in heads:
            sb_ref[h] = logits(h, i)
            step(h, sa_ref[h], i - 1)
        last(sb_ref)


def _flash_scratch(hb, width, tq, cols):
    return [pltpu.VMEM((hb, tq, cols), F32), pltpu.VMEM((hb, tq, cols), F32),
            pltpu.VMEM((hb, 1, cols), F32), pltpu.VMEM((hb, 1, cols), F32),
            pltpu.VMEM((hb, width, cols), F32)]


def _fox_kernel(*refs, tq, hb, n_cast):
    q_ref, k_ref, v_ref, fp_ref, fr_ref = refs[:5]
    o_ref = refs[5 + n_cast]
    scratch = refs[-5:]
    side_job = functools.partial(_run_slab_casts, refs[5:5 + n_cast], refs[6 + n_cast:-5])
    i = pl.program_id(2)
    h0 = pl.program_id(1) * hb
    heads = range(hb)
    ft2 = [fr_ref[h, pl.ds(i, 1), :] for h in heads]

    lane = lax.broadcasted_iota(jnp.int32, (tq, LANES), 1)
    q_ext = []
    for h in heads:
        mine = (lane % FOX_HEADS == h0 + h) & (lane < 3 * FOX_HEADS)
        sel = jnp.where(mine, -1.0, 0.0).astype(BF16)
        q_ext.append(jnp.concatenate([q_ref[:, h * FOX_DIM:(h + 1) * FOX_DIM], sel], axis=1))

    def logits(h, j):
        start = pl.multiple_of(j * tq, tq)
        k_ext = jnp.concatenate([k_ref[pl.ds(start, tq), h * FOX_DIM:(h + 1) * FOX_DIM],
                                 fp_ref[pl.ds(start, tq), :]], axis=1)
        return _dot_nt(k_ext, q_ext[h])

    def values(h, j):
        return v_ref[pl.ds(pl.multiple_of(j * tq, tq), tq), h * FOX_DIM:(h + 1) * FOX_DIM]

    def finish(h, o_t):
        o_ref[:, h * FOX_DIM:(h + 1) * FOX_DIM] = o_t.T.astype(BF16)

    krow = lax.broadcasted_iota(jnp.int32, (tq, tq), 0)
    qcol = lax.broadcasted_iota(jnp.int32, (tq, tq), 1)
    _flash_attend(i, heads, logits, values, ft2, krow <= qcol, finish, side_job, *scratch)


def _fox_attn(proj, fp, frow, *, batch, seq, cast=(), tq=256, hb=8):
    proj3 = proj.reshape(batch, seq, PROJ_W)
    n_q = seq // tq
    frow4 = frow.reshape(batch, FOX_HEADS, n_q, tq)
    w = hb * FOX_DIM
    groups = FOX_HEADS // hb
    assert groups == 1 or not cast
    cast_in, cast_out, cast_shape = _slab_cast_specs(cast, batch * n_q, lambda b, g, i: b * n_q + i)
    outs = pl.pallas_call(
        functools.partial(_fox_kernel, tq=tq, hb=hb, n_cast=len(cast)),
        out_shape=[jax.ShapeDtypeStruct((batch, seq, FOX_W), BF16)] + cast_shape,
        grid=(batch, groups, n_q),
        in_specs=[
            pl.BlockSpec((None, tq, w), lambda b, g, i: (b, i, g)),
            pl.BlockSpec((None, seq, w), lambda b, g, i: (b, 0, groups + g)),
            pl.BlockSpec((None, seq, w), lambda b, g, i: (b, 0, 2 * groups + g)),
            pl.BlockSpec((None, seq, LANES), lambda b, g, i: (b, 0, 0)),
            pl.BlockSpec((None, hb, n_q, tq), lambda b, g, i: (b, g, 0, 0)),
        ] + cast_in,
        out_specs=[pl.BlockSpec((None, tq, w), lambda b, g, i: (b, i, g))] + cast_out,
        scratch_shapes=_flash_scratch(hb, FOX_DIM, tq, tq),
        compiler_params=_params("parallel", "parallel", "arbitrary"),
        name="fox_attn",
    )(proj3, proj3, proj3, fp, frow4, *[arr for arr, _ in cast])
    return outs[0].reshape(batch * seq, FOX_W), outs[1:]


def _diff_kernel(*refs, tq, hb, lam_init, n_cast):
    lamv_ref, g_ref, q_ref, k_ref, v_ref = refs[:5]
    o_ref = refs[5 + n_cast]
    scratch = refs[-5:]
    side_job = functools.partial(_run_slab_casts, refs[5:5 + n_cast], refs[6 + n_cast:-5])
    i = pl.program_id(2)
    heads = range(hb)
    lv = lamv_ref[...]
    lam = (jnp.exp(jnp.sum(lv[0:1] * lv[1:2], axis=1, keepdims=True))
           - jnp.exp(jnp.sum(lv[2:3] * lv[3:4], axis=1, keepdims=True)) + lam_init)

    lane = lax.broadcasted_iota(jnp.int32, (tq, LANES), 1)
    qs = []
    for h in heads:
        q = q_ref[:, h * LANES:(h + 1) * LANES].astype(F32)
        qs.append(jnp.concatenate([jnp.where(lane < DIFF_QK_DIM, q, 0.0),
                                   jnp.where(lane >= DIFF_QK_DIM, q, 0.0)], axis=0).astype(BF16))
    zero = jnp.zeros((1, 2 * tq), F32)

    def logits(h, j):
        start = pl.multiple_of(j * tq, tq)
        return _dot_nt(k_ref[pl.ds(start, tq), h * LANES:(h + 1) * LANES], qs[h])

    def values(h, j):
        return v_ref[pl.ds(pl.multiple_of(j * tq, tq), tq), h * LANES:(h + 1) * LANES]

    def finish(h, o_t):
        yd = (o_t[:, :tq] - lam * o_t[:, tq:]).T
        o_ref[:, h * LANES:(h + 1) * LANES] = (
            _rms(yd, g_ref[...]) * (1.0 - lam_init)).astype(BF16)

    krow = lax.broadcasted_iota(jnp.int32, (tq, 2 * tq), 0)
    qcol = lax.broadcasted_iota(jnp.int32, (tq, 2 * tq), 1)
    qcol = jnp.where(qcol >= tq, qcol - tq, qcol)
    visible = krow // CHUNK <= qcol // CHUNK
    _flash_attend(i, heads, logits, values, [zero] * hb, visible, finish, side_job, *scratch)


def _diff_attn(proj, lamv, g, *, batch, seq, lam_init, cast=(), tq=256, hb=4):
    proj3 = proj.reshape(batch, seq, PROJ_W)
    n_q = seq // tq
    w = hb * LANES
    groups = DIFF_HEADS // hb
    assert groups == 1 or not cast
    dq, dk, dv = (DQ_BLK * LANES) // w, (DK_BLK * LANES) // w, (DV_BLK * LANES) // w
    cast_in, cast_out, cast_shape = _slab_cast_specs(cast, batch * n_q, lambda b, g, i: b * n_q + i)
    outs = pl.pallas_call(
        functools.partial(_diff_kernel, tq=tq, hb=hb, lam_init=lam_init, n_cast=len(cast)),
        out_shape=[jax.ShapeDtypeStruct((batch, seq, DIFF_V_W), BF16)] + cast_shape,
        grid=(batch, groups, n_q),
        in_specs=[
            pl.BlockSpec((4, DIFF_QK_DIM), lambda b, g, i: (0, 0)),
            pl.BlockSpec((1, DIFF_V_DIM), lambda b, g, i: (0, 0)),
            pl.BlockSpec((None, tq, w), lambda b, g, i: (b, i, dq + g)),
            pl.BlockSpec((None, seq, w), lambda b, g, i: (b, 0, dk + g)),
            pl.BlockSpec((None, seq, w), lambda b, g, i: (b, 0, dv + g)),
        ] + cast_in,
        out_specs=[pl.BlockSpec((None, tq, w), lambda b, g, i: (b, i, g))] + cast_out,
        scratch_shapes=_flash_scratch(hb, DIFF_V_DIM, tq, 2 * tq),
        compiler_params=_params("parallel", "parallel", "arbitrary"),
        name="diff_attn",
    )(lamv, g, proj3, proj3, proj3, *[arr for arr, _ in cast])
    return outs[0].reshape(batch * seq, DIFF_V_W), outs[1:]


def _mem_kv_kernel(mem_ref, g_ref, w_ref, o_ref):
    o_ref[...] = _dot(_rms(mem_ref[...], g_ref[...]).astype(BF16), w_ref[...]).astype(BF16)


def _mem_kv(mem, g, w):
    batch, n_mem, d = mem.shape
    return pl.pallas_call(
        _mem_kv_kernel,
        out_shape=jax.ShapeDtypeStruct((batch, n_mem, 2 * MEM_W), BF16),
        grid=(batch,),
        in_specs=[pl.BlockSpec((None, n_mem, d), lambda b: (b, 0, 0)),
                  pl.BlockSpec((1, d), lambda b: (0, 0)),
                  pl.BlockSpec((d, 2 * MEM_W), lambda b: (0, 0))],
        out_specs=pl.BlockSpec((None, n_mem, 2 * MEM_W), lambda b: (b, 0, 0)),
        compiler_params=_params("parallel"),
        name="mem_kv",
    )(mem, g, w)


def _mem_attn_kernel(q_ref, kv_ref, o_ref):
    logits = [_dot_nt(kv_ref[:, h * MEM_DIM:(h + 1) * MEM_DIM],
                      q_ref[:, h * MEM_DIM:(h + 1) * MEM_DIM]) for h in range(MEM_HEADS)]
    for h in range(MEM_HEADS):
        v = kv_ref[:, MEM_W + h * MEM_DIM:MEM_W + (h + 1) * MEM_DIM]
        s = logits[h]
        p = jnp.exp2(s - jnp.max(s, axis=0, keepdims=True))
        l = jnp.sum(p, axis=0, keepdims=True)
        o_t = _dot_tn(v, p.astype(BF16)) / l
        o_ref[:, h * MEM_DIM:(h + 1) * MEM_DIM] = o_t.T.astype(BF16)


def _mem_attn(proj, mkv, *, batch, seq, tq=512):
    proj3 = proj.reshape(batch, seq, PROJ_W)
    n_mem = mkv.shape[1]
    out = pl.pallas_call(
        _mem_attn_kernel,
        out_shape=jax.ShapeDtypeStruct((batch, seq, MEM_W), BF16),
        grid=(batch, seq // tq),
        in_specs=[pl.BlockSpec((None, tq, MEM_W), lambda b, i: (b, i, MQ_COL // MEM_W)),
                  pl.BlockSpec((None, n_mem, 2 * MEM_W), lambda b, i: (b, 0, 0))],
        out_specs=pl.BlockSpec((None, tq, MEM_W), lambda b, i: (b, i, 0)),
        compiler_params=_params("parallel", "arbitrary"),
        name="mem_attn",
    )(proj3, mkv)
    return out.reshape(batch * seq, MEM_W)


def _merge_kernel(x_ref, h_ref, yf_ref, yd_ref, ym_ref, wgf_ref, wgd_ref, wgm_ref,
                  bf_ref, bd_ref, bm_ref, wf_ref, wd_ref, wm_ref, wo_ref, gpost_ref,
                  o_ref, acc_ref):
    n = pl.program_id(1)

    @pl.when(n == 0)
    def _():
        acc_ref[...] = jnp.zeros_like(acc_ref)

    h = h_ref[...]

    def gated(wg_ref, b_ref, y_ref, w_ref):
        z = _dot(h, wg_ref[...]) + b_ref[...]
        return (1.0 / (1.0 + jnp.exp(-z))) * _dot(y_ref[...], w_ref[...])

    merged = (gated(wgf_ref, bf_ref, yf_ref, wf_ref) + gated(wgd_ref, bd_ref, yd_ref, wd_ref)
              + gated(wgm_ref, bm_ref, ym_ref, wm_ref))
    acc_ref[...] += _dot(merged.astype(BF16), wo_ref[...])

    @pl.when(n == pl.num_programs(1) - 1)
    def _():
        def emit(sl, y):
            o_ref[sl, :] = x_ref[sl, :] + y

        _rms_rows(acc_ref, gpost_ref[...], emit, F32_ROWS)


def _merge(x, h, y_fox, y_diff, y_mem, w_gate, b_gate, w_fox, w_diff, w_mem, w_out, g_post,
           *, tm=512, tn=512):
    t, d = x.shape
    nt = d // tn
    row = lambda m, n: (m, 0)
    return pl.pallas_call(
        _merge_kernel,
        out_shape=jax.ShapeDtypeStruct((t, d), F32),
        grid=(t // tm, nt),
        in_specs=[
            pl.BlockSpec((tm, d), row),
            pl.BlockSpec((tm, d), row),
            pl.BlockSpec((tm, FOX_W), row),
            pl.BlockSpec((tm, DIFF_V_W), row),
            pl.BlockSpec((tm, MEM_W), row),
            pl.BlockSpec((d, tn), lambda m, n: (0, n)),
            pl.BlockSpec((d, tn), lambda m, n: (0, nt + n)),
            pl.BlockSpec((d, tn), lambda m, n: (0, 2 * nt + n)),
            pl.BlockSpec((1, tn), lambda m, n: (0, n)),
            pl.BlockSpec((1, tn), lambda m, n: (0, nt + n)),
            pl.BlockSpec((1, tn), lambda m, n: (0, 2 * nt + n)),
            pl.BlockSpec((FOX_W, tn), lambda m, n: (0, n)),
            pl.BlockSpec((DIFF_V_W, tn), lambda m, n: (0, n)),
            pl.BlockSpec((MEM_W, tn), lambda m, n: (0, n)),
            pl.BlockSpec((tn, d), lambda m, n: (n, 0)),
            pl.BlockSpec((1, d), lambda m, n: (0, 0)),
        ],
        out_specs=pl.BlockSpec((tm, d), row),
        scratch_shapes=[pltpu.VMEM((tm, d), F32)],
        compiler_params=_params("parallel", "arbitrary"),
        name="merge",
    )(x, h, y_fox, y_diff, y_mem, w_gate, w_gate, w_gate, b_gate, b_gate, b_gate,
      w_fox, w_diff, w_mem, w_out, g_post)


def kernel(x, mem, ffn1_pre_g, ffn1_w_gate, ffn1_w_up, ffn1_w_down, ffn1_post_g, mix_pre_g, w_in, fox_f_bias, diff_lambda_q1, diff_lambda_k1, diff_lambda_q2, diff_lambda_k2, diff_head_g, mem_norm_g, w_mem_kv, w_branch_fox, w_branch_diff, w_branch_mem, w_merge_gate, b_merge_gate, w_out, mix_post_g, ffn2_pre_g, ffn2_w_gate, ffn2_w_up, ffn2_w_down, ffn2_post_g):
    batch, seq, d = x.shape
    depth = w_in.shape[0]
    xt = x.reshape(batch * seq, d)
    cos_t, sa_t, sb_t = _rope_tables(seq)

    def row(v):
        return v.reshape(1, -1).astype(F32)

    for l in range(depth):
        bf = lambda w: w[l].astype(BF16)
        w_in_t = w_in[l].T
        xt, (w2_gate, w2_up, w_head, w_tail) = _ffn(
            xt, row(ffn1_pre_g[l]), bf(ffn1_w_gate), bf(ffn1_w_up), bf(ffn1_w_down),
            row(ffn1_post_g[l]), cast=[(ffn2_w_gate, l), (ffn2_w_up, l)],
            cast_t=[(w_in_t, 0, FF_COL + LANES), (w_in_t, FF_COL + FOX_HEADS, w_in.shape[2])])

        small = [w_out, w_branch_fox, w_branch_diff, w_branch_mem]
        proj, ff, h_mix, (wb_out, wb_fox, wb_diff, wb_mem) = _mix_proj(
            xt, row(mix_pre_g[l]), w_head, w_tail, cos_t, sa_t, sb_t, seq=seq,
            cast=[(w, l) for w in small])

        bias = jnp.pad(fox_f_bias[l].astype(F32), (0, LANES - FOX_HEADS)).reshape(1, LANES)
        fp, frow = _fox_gate(ff, bias, batch=batch, seq=seq)
        y_fox, (w2_down,) = _fox_attn(proj, fp, frow, batch=batch, seq=seq,
                                      cast=[(ffn2_w_down, l)])

        lam_init = 0.8 - 0.6 * math.exp(-0.3 * l)
        lamv = jnp.stack([diff_lambda_q1[l], diff_lambda_k1[l], diff_lambda_q2[l],
                          diff_lambda_k2[l]]).astype(F32)
        y_diff, (wb_merge,) = _diff_attn(proj, lamv, row(diff_head_g[l]), batch=batch, seq=seq,
                                         lam_init=lam_init, cast=[(w_merge_gate, l)])

        mkv = _mem_kv(mem, row(mem_norm_g[l]), bf(w_mem_kv))
        y_mem = _mem_attn(proj, mkv, batch=batch, seq=seq)

        xt = _merge(xt, h_mix, y_fox, y_diff, y_mem, wb_merge,
                    row(b_merge_gate[l]), wb_fox, wb_diff, wb_mem, wb_out, row(mix_post_g[l]))

        xt, _ = _ffn(xt, row(ffn2_pre_g[l]), w2_gate, w2_up, w2_down, row(ffn2_post_g[l]))

    return xt.reshape(batch, seq, d)
```

```python
import functools
import math

import jax
import jax.numpy as jnp
from jax import lax
from jax.experimental import pallas as pl
from jax.experimental.pallas import tpu as pltpu

D_MODEL = 2048
CHUNK = 64
EPS = 1e-6
ROPE_THETA = 500000.0

FOX_HEADS = 8
FOX_DIM = 128
FOX_W = FOX_HEADS * FOX_DIM

DIFF_HEADS = 4
DIFF_QK_DIM = 64
DIFF_V_DIM = 2 * DIFF_QK_DIM
DIFF_QK_W = DIFF_HEADS * 2 * DIFF_QK_DIM
DIFF_V_W = DIFF_HEADS * DIFF_V_DIM
ROPE_DIM = DIFF_QK_DIM // 4

MEM_HEADS = 4
MEM_DIM = 128
MEM_W = MEM_HEADS * MEM_DIM

LANES = 128
F32_ROWS = 8
BF16_ROWS = 16
RMS_ROWS_IN_FLIGHT = 128
MERGE_TN = 512
PROJ_W = 3 * FOX_W + 2 * DIFF_QK_W + DIFF_V_W + MEM_W
FQ_BLK, FK_BLK, FV_BLK = 0, FOX_HEADS, 2 * FOX_HEADS
DQ_BLK = 3 * FOX_HEADS
DK_BLK = DQ_BLK + DIFF_HEADS
DV_BLK = DK_BLK + DIFF_HEADS
MQ_COL = 3 * FOX_W + 2 * DIFF_QK_W + DIFF_V_W
FF_COL = 3 * FOX_W

VMEM_LIMIT = 56 * 1024 * 1024
BF16 = jnp.bfloat16
F32 = jnp.float32
NEG_INF = float("-inf")
LOG2E = math.log2(math.e)
FOX_QSCALE = FOX_DIM ** -0.5 * LOG2E
DIFF_QSCALE = DIFF_QK_DIM ** -0.5 * LOG2E
MEM_QSCALE = MEM_DIM ** -0.5 * LOG2E


def _dot(a, b):
    return jnp.dot(a, b, preferred_element_type=F32)


def _dot_nt(a, b):
    return lax.dot_general(a, b, (((1,), (1,)), ((), ())), preferred_element_type=F32)


def _rms(x, g):
    return x * lax.rsqrt(jnp.mean(x * x, axis=-1, keepdims=True) + EPS) * g


def _rms_rows(src_ref, g, emit, rows):
    def body(r, carry):
        sl = pl.ds(pl.multiple_of(r * rows, rows), rows)
        emit(sl, _rms(src_ref[sl, :], g))
        return carry

    lax.fori_loop(0, src_ref.shape[0] // rows, body, 0, unroll=RMS_ROWS_IN_FLIGHT // rows)


def _params(*sem):
    return pltpu.CompilerParams(dimension_semantics=sem, vmem_limit_bytes=VMEM_LIMIT)


def _slab_cast_specs(cast, n_steps, step_index):
    in_specs, out_specs, out_shape = [], [], []
    for arr, layer in cast:
        _, rows, cols = arr.shape
        br = rows // n_steps
        assert br * n_steps == rows and br % BF16_ROWS == 0
        in_specs.append(pl.BlockSpec((None, br, cols),
                                     lambda *g, layer=layer: (layer, step_index(*g), 0)))
        out_specs.append(pl.BlockSpec((br, cols), lambda *g: (step_index(*g), 0)))
        out_shape.append(jax.ShapeDtypeStruct((rows, cols), BF16))
    return in_specs, out_specs, out_shape


def _run_slab_casts(src_refs, dst_refs, moves=None):
    for src_ref, dst_ref in zip(src_refs, dst_refs):
        if moves is None:
            dst_ref[...] = src_ref[...].astype(BF16)
        else:
            for dst_lo, src_lo, width in moves:
                dst_ref[:, dst_lo:dst_lo + width] = src_ref[:, src_lo:src_lo + width].astype(BF16)


def _ffn_kernel(*refs, fused, n_pair, n_flip):
    n_w = 1 if fused else 2
    x_ref, gpre_ref = refs[:2]
    w_refs = refs[2:2 + n_w]
    wd_ref, gpost_ref = refs[2 + n_w:4 + n_w]
    side_in = refs[4 + n_w:4 + n_w + 2 * n_pair + n_flip]
    o_ref = refs[4 + n_w + len(side_in)]
    side_out = refs[5 + n_w + len(side_in):-2]
    h_ref, acc_ref = refs[-2:]
    tf = wd_ref.shape[0]
    f = pl.program_id(1)

    @pl.when(f == 0)
    def _():
        def emit(sl, y):
            h_ref[sl, :] = y.astype(BF16)
            acc_ref[sl, :] = jnp.zeros_like(y)

        _rms_rows(x_ref, gpre_ref[...], emit, BF16_ROWS)

    def pair_job(k):
        side_out[k][:, :tf] = side_in[2 * k][...].astype(BF16)
        side_out[k][:, tf:] = side_in[2 * k + 1][...].astype(BF16)

    def flip_job(k):
        side_out[n_pair + k][...] = side_in[2 * n_pair + k][...].T.astype(BF16)

    jobs = ([functools.partial(pair_job, k) for k in range(n_pair)]
            + [functools.partial(flip_job, k) for k in range(n_flip)])

    def run_jobs(part, parts=2):
        for job in jobs[part::parts]:
            job()

    h = h_ref[...]
    run_jobs(0)
    if fused:
        gu = _dot(h, w_refs[0][...])
        g, u = gu[:, :tf], gu[:, tf:]
    else:
        g = _dot(h, w_refs[0][...])
        u = _dot(h, w_refs[1][...])
    run_jobs(1)
    a = (g * (1.0 / (1.0 + jnp.exp(-g)))) * u
    acc_ref[...] += _dot(a.astype(BF16), wd_ref[...])

    @pl.when(f == pl.num_programs(1) - 1)
    def _():
        def emit(sl, y):
            o_ref[sl, :] = x_ref[sl, :] + y

        _rms_rows(acc_ref, 0.5 * gpost_ref[...], emit, F32_ROWS)


def _ffn(x, g_pre, w_gu, w_down, g_post, *, cast_pair=(), cast_t=(), tm=512, tf=512):
    t, d = x.shape
    d_ff = w_down.shape[0]
    m_tiles, f_steps = t // tm, d_ff // tf
    fused = not isinstance(w_gu, tuple)
    if fused:
        assert w_gu.shape == (f_steps, d, 2 * tf)
        w_gu, w_specs = (w_gu,), [pl.BlockSpec((None, d, 2 * tf), lambda m, f: (f, 0, 0))]
    else:
        w_specs = [pl.BlockSpec((d, tf), lambda m, f: (0, f))] * 2
    in_specs = [
        pl.BlockSpec((tm, d), lambda m, f: (m, 0)),
        pl.BlockSpec((1, d), lambda m, f: (0, 0)),
        *w_specs,
        pl.BlockSpec((tf, d), lambda m, f: (f, 0)),
        pl.BlockSpec((1, d), lambda m, f: (0, 0)),
    ]
    out_specs = [pl.BlockSpec((tm, d), lambda m, f: (m, 0))]
    out_shape = [jax.ShapeDtypeStruct((t, d), F32)]
    side_in = []
    for gate, up, layer in cast_pair:
        _, rows, cols = gate.shape
        br = rows // m_tiles
        assert br * m_tiles == rows and cols == d_ff and up.shape == gate.shape
        in_specs += [pl.BlockSpec((None, br, tf), lambda m, f, layer=layer: (layer, m, f))] * 2
        out_specs.append(pl.BlockSpec((None, br, 2 * tf), lambda m, f: (f, m, 0)))
        out_shape.append(jax.ShapeDtypeStruct((f_steps, rows, 2 * tf), BF16))
        side_in += [gate, up]
    for arr, lo, hi in cast_t:
        _, cols = arr.shape
        n_blk = (hi - lo) // LANES
        assert n_blk * LANES == hi - lo and n_blk <= m_tiles * f_steps

        def block(m, f, n_blk=n_blk):
            return jnp.minimum(m * f_steps + f, n_blk - 1)

        in_specs.append(pl.BlockSpec(
            (pl.Element(LANES), pl.Element(cols)),
            lambda m, f, block=block, lo=lo: (
                pl.multiple_of(lo + LANES * block(m, f), math.gcd(lo, LANES)), 0)))
        out_specs.append(pl.BlockSpec((cols, LANES), lambda m, f, block=block: (0, block(m, f))))
        out_shape.append(jax.ShapeDtypeStruct((cols, hi - lo), BF16))
        side_in.append(arr)
    outs = pl.pallas_call(
        functools.partial(_ffn_kernel, fused=fused, n_pair=len(cast_pair), n_flip=len(cast_t)),
        out_shape=out_shape,
        grid=(m_tiles, f_steps),
        in_specs=in_specs,
        out_specs=out_specs,
        scratch_shapes=[pltpu.VMEM((tm, d), BF16), pltpu.VMEM((tm, d), F32)],
        compiler_params=_params("parallel", "arbitrary"),
        name="ffn",
    )(x, g_pre, *w_gu, w_down, g_post, *side_in)
    return outs[0], outs[1:]


def _proj_kernel(*refs, tn, n_cast):
    x_ref, g_ref, wa_ref, wff_ref, wb_ref, cos_ref, sa_ref, sb_ref = refs[:8]
    o_ref, ff_ref, h_ref = refs[8 + n_cast:11 + n_cast]
    _run_slab_casts(refs[8:8 + n_cast], refs[11 + n_cast:])
    h_ref[...] = _rms(x_ref[...], g_ref[...]).astype(BF16)
    ff_ref[...] = _dot(h_ref[...], wff_ref[...])
    half = ROPE_DIM // 2
    na = wa_ref.shape[1] // tn
    for n in range(na + wb_ref.shape[1] // tn):
        w_tile = (wa_ref[:, n * tn:(n + 1) * tn] if n < na
                  else wb_ref[:, (n - na) * tn:(n - na + 1) * tn])
        y = _dot(h_ref[...], w_tile)
        first_blk = n * tn // LANES
        if first_blk < FK_BLK:
            y = y * FOX_QSCALE
        if first_blk >= MQ_COL // LANES:
            y = y * MEM_QSCALE
        if DQ_BLK <= first_blk < DV_BLK:
            qscale = DIFF_QSCALE if first_blk < DK_BLK else 1.0
            for j in range(tn // LANES):
                blk = y[:, j * LANES:(j + 1) * LANES]
                rot = (blk * cos_ref[...] + pltpu.roll(blk, LANES - half, axis=1) * sa_ref[...]
                       + pltpu.roll(blk, half, axis=1) * sb_ref[...])
                o_ref[:, n * tn + j * LANES:n * tn + (j + 1) * LANES] = (rot * qscale).astype(BF16)
        else:
            o_ref[:, n * tn:(n + 1) * tn] = y.astype(BF16)


def _mix_proj(x, g, w_head, w_tail, cos_t, sa_t, sb_t, *, seq, cast=(), tm=512, tn=512):
    t, d = x.shape
    n_out = FF_COL + w_tail.shape[1]
    assert all((blk * LANES) % tn == 0 for blk in (FK_BLK, DQ_BLK, DK_BLK, DV_BLK))
    assert w_head.shape[1] == FF_COL + LANES and n_out == PROJ_W
    s_tiles = seq // tm
    m_tiles = t // tm
    resident = dict(pipeline_mode=pl.Buffered(1))
    in_specs = [
        pl.BlockSpec((tm, d), lambda m: (m, 0)),
        pl.BlockSpec((1, d), lambda m: (0, 0)),
        pl.BlockSpec((d, FF_COL), lambda m: (0, 0), **resident),
        pl.BlockSpec((d, LANES), lambda m: (0, FF_COL // LANES), **resident),
        pl.BlockSpec((d, w_tail.shape[1]), lambda m: (0, 0), **resident),
        pl.BlockSpec((tm, LANES), lambda m: (m % s_tiles, 0)),
        pl.BlockSpec((tm, LANES), lambda m: (m % s_tiles, 0)),
        pl.BlockSpec((tm, LANES), lambda m: (m % s_tiles, 0)),
    ]
    out_specs = [pl.BlockSpec((tm, n_out), lambda m: (m, 0)),
                 pl.BlockSpec((tm, LANES), lambda m: (m, 0)),
                 pl.BlockSpec((tm, d), lambda m: (m, 0))]
    out_shape = [jax.ShapeDtypeStruct((t, n_out), BF16), jax.ShapeDtypeStruct((t, LANES), F32),
                 jax.ShapeDtypeStruct((t, d), BF16)]
    cast_in, cast_out, cast_shape = _slab_cast_specs(cast, m_tiles, lambda m: m)
    in_specs, out_specs, out_shape = in_specs + cast_in, out_specs + cast_out, out_shape + cast_shape
    outs = pl.pallas_call(
        functools.partial(_proj_kernel, tn=tn, n_cast=len(cast)),
        out_shape=out_shape,
        grid=(m_tiles,),
        in_specs=in_specs,
        out_specs=out_specs,
        compiler_params=_params("parallel"),
        name="mix_proj",
    )(x, g, w_head, w_head, w_tail, cos_t, sa_t, sb_t, *[arr for arr, _ in cast])
    return outs[0], outs[1], outs[2], outs[3:]


def _rope_tables(seq):
    half = ROPE_DIM // 2
    pos = jnp.arange(seq, dtype=F32)
    inv_freq = ROPE_THETA ** (-jnp.arange(0, ROPE_DIM, 2, dtype=F32) / ROPE_DIM)
    ang = pos[:, None] * inv_freq[None, :]
    cos, sin = jnp.cos(ang), jnp.sin(ang)
    ones = jnp.ones((seq, DIFF_QK_DIM - ROPE_DIM), F32)
    zeros_h = jnp.zeros((seq, half), F32)
    zeros_r = jnp.zeros((seq, DIFF_QK_DIM - ROPE_DIM), F32)
    cos_m = jnp.concatenate([cos, cos, ones], axis=1)
    sa_m = jnp.concatenate([-sin, zeros_h, zeros_r], axis=1)
    sb_m = jnp.concatenate([zeros_h, sin, zeros_r], axis=1)
    rep = LANES // DIFF_QK_DIM
    return (jnp.tile(cos_m, (1, rep)), jnp.tile(sa_m, (1, rep)), jnp.tile(sb_m, (1, rep)))


def _fgate_kernel(ff_ref, bias_ref, fp_ref, frow_ref, fcol_ref, *, cb):
    seq = ff_ref.shape[0]
    head_lane = lax.broadcasted_iota(jnp.int32, (cb, LANES), 1) < FOX_HEADS
    z = ff_ref[...] + bias_ref[...]
    lf = jnp.minimum(z, 0.0) - jnp.log1p(jnp.exp(-jnp.abs(z)))
    r = lax.broadcasted_iota(jnp.int32, (cb, cb), 0)
    c = lax.broadcasted_iota(jnp.int32, (cb, cb), 1)
    tri = (r >= c).astype(F32)
    carry = jnp.zeros((1, LANES), F32)
    for i in range(seq // cb):
        cs = jnp.dot(tri, lf[i * cb:(i + 1) * cb], precision=lax.Precision.HIGHEST,
                     preferred_element_type=F32) + carry
        carry = cs[cb - 1:cb, :]
        cs2 = jnp.where(head_lane, cs * LOG2E, 0.0)
        fcol_ref[i * cb:(i + 1) * cb, :] = cs2
        hi = cs2.astype(BF16).astype(F32)
        mid = (cs2 - hi).astype(BF16).astype(F32)
        lo = (cs2 - hi - mid).astype(BF16).astype(F32)
        pieces = hi + pltpu.roll(mid, FOX_HEADS, axis=1) + pltpu.roll(lo, 2 * FOX_HEADS, axis=1)
        fp_ref[i * cb:(i + 1) * cb, :] = pieces.astype(BF16)
    frow_ref[...] = fcol_ref[...].T[:FOX_HEADS]


def _fox_gate(ff, bias, *, batch, seq, cb=256):
    ff = ff.reshape(batch, seq, LANES)
    return pl.pallas_call(
        functools.partial(_fgate_kernel, cb=cb),
        out_shape=(jax.ShapeDtypeStruct((batch, seq, LANES), BF16),
                   jax.ShapeDtypeStruct((batch, FOX_HEADS, seq), F32)),
        grid=(batch,),
        in_specs=[pl.BlockSpec((None, seq, LANES), lambda b: (b, 0, 0)),
                  pl.BlockSpec((1, LANES), lambda b: (0, 0))],
        out_specs=(pl.BlockSpec((None, seq, LANES), lambda b: (b, 0, 0)),
                   pl.BlockSpec((None, FOX_HEADS, seq), lambda b: (b, 0, 0))),
        scratch_shapes=[pltpu.VMEM((seq, LANES), F32)],
        compiler_params=_params("parallel"),
        name="fox_gate",
    )(ff, bias)


def _dot_tn(a, b):
    return lax.dot_general(a, b, (((0,), (0,)), ((), ())), preferred_element_type=F32)


def _flash_attend(i, heads, logits, values, ft2, visible, finish, side_job,
                  sa_ref, sb_ref, m_ref, l_ref, acc_ref):
    def step(h, s, j):
        m = m_ref[h]
        m_new = jnp.maximum(m, jnp.max(s, axis=0, keepdims=True) + ft2[h])
        alpha = jnp.exp2(m - m_new)
        p = jnp.exp2(s - (m_new - ft2[h]))
        m_ref[h] = m_new
        l_ref[h] = alpha * l_ref[h] + jnp.sum(p, axis=0, keepdims=True)
        acc_ref[h] = alpha * acc_ref[h] + _dot_tn(values(h, j), p.astype(BF16))

    for h in heads:
        m_ref[h] = jnp.full(m_ref.shape[1:], NEG_INF, F32)
        l_ref[h] = jnp.zeros(l_ref.shape[1:], F32)
        acc_ref[h] = jnp.zeros(acc_ref.shape[1:], F32)
        sa_ref[h] = logits(h, 0)
    side_job()

    def pair(jj, carry):
        j = 2 * jj
        for h in heads:
            sb_ref[h] = logits(h, j + 1)
            step(h, sa_ref[h], j)
        for h in heads:
            sa_ref[h] = logits(h, j + 2)
            step(h, sb_ref[h], j + 1)
        return carry

    lax.fori_loop(0, i // 2, pair, 0)

    def last(s_ref):
        for h in heads:
            step(h, jnp.where(visible, s_ref[h], NEG_INF), i)
            finish(h, acc_ref[h] / l_ref[h])

    @pl.when(i % 2 == 0)
    def _():
        last(sa_ref)

    @pl.when(i % 2 == 1)
    def _():
        for h in heads:
            sb_ref[h] = logits(h, i)
            step(h, sa_ref[h], i - 1)
        last(sb_ref)


def _flash_scratch(hb, width, tq, cols):
    return [pltpu.VMEM((hb, tq, cols), F32), pltpu.VMEM((hb, tq, cols), F32),
            pltpu.VMEM((hb, 1, cols), F32), pltpu.VMEM((hb, 1, cols), F32),
            pltpu.VMEM((hb, width, cols), F32)]


def _fox_kernel(*refs, tq, hb, n_cast):
    q_ref, k_ref, v_ref, fp_ref, fr_ref = refs[:5]
    o_ref = refs[5 + n_cast]
    scratch = refs[-5:]
    side_job = functools.partial(_run_slab_casts, refs[5:5 + n_cast], refs[6 + n_cast:-5])
    i = pl.program_id(2)
    h0 = pl.program_id(1) * hb
    heads = range(hb)
    ft2 = [fr_ref[h, pl.ds(i, 1), :] for h in heads]

    lane = lax.broadcasted_iota(jnp.int32, (tq, LANES), 1)
    q_ext = []
    for h in heads:
        mine = (lane % FOX_HEADS == h0 + h) & (lane < 3 * FOX_HEADS)
        sel = jnp.where(mine, -1.0, 0.0).astype(BF16)
        q_ext.append(jnp.concatenate([q_ref[:, h * FOX_DIM:(h + 1) * FOX_DIM], sel], axis=1))

    def logits(h, j):
        start = pl.multiple_of(j * tq, tq)
        k_ext = jnp.concatenate([k_ref[pl.ds(start, tq), h * FOX_DIM:(h + 1) * FOX_DIM],
                                 fp_ref[pl.ds(start, tq), :]], axis=1)
        return _dot_nt(k_ext, q_ext[h])

    def values(h, j):
        return v_ref[pl.ds(pl.multiple_of(j * tq, tq), tq), h * FOX_DIM:(h + 1) * FOX_DIM]

    def finish(h, o_t):
        o_ref[:, h * FOX_DIM:(h + 1) * FOX_DIM] = o_t.T.astype(BF16)

    krow = lax.broadcasted_iota(jnp.int32, (tq, tq), 0)
    qcol = lax.broadcasted_iota(jnp.int32, (tq, tq), 1)
    _flash_attend(i, heads, logits, values, ft2, krow <= qcol, finish, side_job, *scratch)


def _fox_attn(proj, fp, frow, *, batch, seq, cast=(), tq=256, hb=8):
    proj3 = proj.reshape(batch, seq, PROJ_W)
    n_q = seq // tq
    frow4 = frow.reshape(batch, FOX_HEADS, n_q, tq)
    w = hb * FOX_DIM
    groups = FOX_HEADS // hb
    assert groups == 1 or not cast
    cast_in, cast_out, cast_shape = _slab_cast_specs(cast, batch * n_q, lambda b, g, i: b * n_q + i)
    outs = pl.pallas_call(
        functools.partial(_fox_kernel, tq=tq, hb=hb, n_cast=len(cast)),
        out_shape=[jax.ShapeDtypeStruct((batch, seq, FOX_W), BF16)] + cast_shape,
        grid=(batch, groups, n_q),
        in_specs=[
            pl.BlockSpec((None, tq, w), lambda b, g, i: (b, i, g)),
            pl.BlockSpec((None, seq, w), lambda b, g, i: (b, 0, groups + g)),
            pl.BlockSpec((None, seq, w), lambda b, g, i: (b, 0, 2 * groups + g)),
            pl.BlockSpec((None, seq, LANES), lambda b, g, i: (b, 0, 0)),
            pl.BlockSpec((None, hb, n_q, tq), lambda b, g, i: (b, g, 0, 0)),
        ] + cast_in,
        out_specs=[pl.BlockSpec((None, tq, w), lambda b, g, i: (b, i, g))] + cast_out,
        scratch_shapes=_flash_scratch(hb, FOX_DIM, tq, tq),
        compiler_params=_params("parallel", "parallel", "arbitrary"),
        name="fox_attn",
    )(proj3, proj3, proj3, fp, frow4, *[arr for arr, _ in cast])
    return outs[0].reshape(batch * seq, FOX_W), outs[1:]


def _diff_kernel(*refs, tq, hb, lam_init, n_cast, cast_moves):
    lamv_ref, g_ref, q_ref, k_ref, v_ref = refs[:5]
    o_ref = refs[5 + n_cast]
    scratch = refs[-5:]
    side_job = functools.partial(_run_slab_casts, refs[5:5 + n_cast], refs[6 + n_cast:-5],
                                 cast_moves)
    i = pl.program_id(2)
    heads = range(hb)
    lv = lamv_ref[...]
    lam = (jnp.exp(jnp.sum(lv[0:1] * lv[1:2], axis=1, keepdims=True))
           - jnp.exp(jnp.sum(lv[2:3] * lv[3:4], axis=1, keepdims=True)) + lam_init)

    lane = lax.broadcasted_iota(jnp.int32, (tq, LANES), 1)
    qs = []
    for h in heads:
        q = q_ref[:, h * LANES:(h + 1) * LANES].astype(F32)
        qs.append(jnp.concatenate([jnp.where(lane < DIFF_QK_DIM, q, 0.0),
                                   jnp.where(lane >= DIFF_QK_DIM, q, 0.0)], axis=0).astype(BF16))
    zero = jnp.zeros((1, 2 * tq), F32)

    def logits(h, j):
        start = pl.multiple_of(j * tq, tq)
        return _dot_nt(k_ref[pl.ds(start, tq), h * LANES:(h + 1) * LANES], qs[h])

    def values(h, j):
        return v_ref[pl.ds(pl.multiple_of(j * tq, tq), tq), h * LANES:(h + 1) * LANES]

    def finish(h, o_t):
        yd = (o_t[:, :tq] - lam * o_t[:, tq:]).T
        o_ref[:, h * LANES:(h + 1) * LANES] = (
            _rms(yd, g_ref[...]) * (1.0 - lam_init)).astype(BF16)

    krow = lax.broadcasted_iota(jnp.int32, (tq, 2 * tq), 0)
    qcol = lax.broadcasted_iota(jnp.int32, (tq, 2 * tq), 1)
    qcol = jnp.where(qcol >= tq, qcol - tq, qcol)
    visible = krow // CHUNK <= qcol // CHUNK
    _flash_attend(i, heads, logits, values, [zero] * hb, visible, finish, side_job, *scratch)


def _diff_attn(proj, lamv, g, *, batch, seq, lam_init, cast=(), cast_moves=None, tq=256, hb=4):
    proj3 = proj.reshape(batch, seq, PROJ_W)
    n_q = seq // tq
    w = hb * LANES
    groups = DIFF_HEADS // hb
    assert groups == 1 or not cast
    dq, dk, dv = (DQ_BLK * LANES) // w, (DK_BLK * LANES) // w, (DV_BLK * LANES) // w
    cast_in, cast_out, cast_shape = _slab_cast_specs(cast, batch * n_q, lambda b, g, i: b * n_q + i)
    outs = pl.pallas_call(
        functools.partial(_diff_kernel, tq=tq, hb=hb, lam_init=lam_init, n_cast=len(cast),
                          cast_moves=cast_moves),
        out_shape=[jax.ShapeDtypeStruct((batch, seq, DIFF_V_W), BF16)] + cast_shape,
        grid=(batch, groups, n_q),
        in_specs=[
            pl.BlockSpec((4, DIFF_QK_DIM), lambda b, g, i: (0, 0)),
            pl.BlockSpec((1, DIFF_V_DIM), lambda b, g, i: (0, 0)),
            pl.BlockSpec((None, tq, w), lambda b, g, i: (b, i, dq + g)),
            pl.BlockSpec((None, seq, w), lambda b, g, i: (b, 0, dk + g)),
            pl.BlockSpec((None, seq, w), lambda b, g, i: (b, 0, dv + g)),
        ] + cast_in,
        out_specs=[pl.BlockSpec((None, tq, w), lambda b, g, i: (b, i, g))] + cast_out,
        scratch_shapes=_flash_scratch(hb, DIFF_V_DIM, tq, 2 * tq),
        compiler_params=_params("parallel", "parallel", "arbitrary"),
        name="diff_attn",
    )(lamv, g, proj3, proj3, proj3, *[arr for arr, _ in cast])
    return outs[0].reshape(batch * seq, DIFF_V_W), outs[1:]


def _mem_kv_kernel(mem_ref, g_ref, w_ref, o_ref):
    o_ref[...] = _dot(_rms(mem_ref[...], g_ref[...]).astype(BF16), w_ref[...]).astype(BF16)


def _mem_kv(mem, g, w):
    batch, n_mem, d = mem.shape
    return pl.pallas_call(
        _mem_kv_kernel,
        out_shape=jax.ShapeDtypeStruct((batch, n_mem, 2 * MEM_W), BF16),
        grid=(batch,),
        in_specs=[pl.BlockSpec((None, n_mem, d), lambda b: (b, 0, 0)),
                  pl.BlockSpec((1, d), lambda b: (0, 0)),
                  pl.BlockSpec((d, 2 * MEM_W), lambda b: (0, 0))],
        out_specs=pl.BlockSpec((None, n_mem, 2 * MEM_W), lambda b: (b, 0, 0)),
        compiler_params=_params("parallel"),
        name="mem_kv",
    )(mem, g, w)


def _mem_attn_kernel(q_ref, kv_ref, o_ref):
    logits = [_dot_nt(kv_ref[:, h * MEM_DIM:(h + 1) * MEM_DIM],
                      q_ref[:, h * MEM_DIM:(h + 1) * MEM_DIM]) for h in range(MEM_HEADS)]
    for h in range(MEM_HEADS):
        v = kv_ref[:, MEM_W + h * MEM_DIM:MEM_W + (h + 1) * MEM_DIM]
        s = logits[h]
        p = jnp.exp2(s - jnp.max(s, axis=0, keepdims=True))
        l = jnp.sum(p, axis=0, keepdims=True)
        o_t = _dot_tn(v, p.astype(BF16)) / l
        o_ref[:, h * MEM_DIM:(h + 1) * MEM_DIM] = o_t.T.astype(BF16)


def _mem_attn(proj, mkv, *, batch, seq, tq=512):
    proj3 = proj.reshape(batch, seq, PROJ_W)
    n_mem = mkv.shape[1]
    out = pl.pallas_call(
        _mem_attn_kernel,
        out_shape=jax.ShapeDtypeStruct((batch, seq, MEM_W), BF16),
        grid=(batch, seq // tq),
        in_specs=[pl.BlockSpec((None, tq, MEM_W), lambda b, i: (b, i, MQ_COL // MEM_W)),
                  pl.BlockSpec((None, n_mem, 2 * MEM_W), lambda b, i: (b, 0, 0))],
        out_specs=pl.BlockSpec((None, tq, MEM_W), lambda b, i: (b, i, 0)),
        compiler_params=_params("parallel", "arbitrary"),
        name="mem_attn",
    )(proj3, mkv)
    return out.reshape(batch * seq, MEM_W)


def _merge_kernel(x_ref, h_ref, yf_ref, yd_ref, ym_ref, wg_ref, bg_ref, wf_ref, wd_ref, wm_ref,
                  wo_ref, gpost_ref, o_ref, acc_ref):
    n = pl.program_id(1)
    tn = wo_ref.shape[0]

    @pl.when(n == 0)
    def _():
        acc_ref[...] = jnp.zeros_like(acc_ref)

    z = _dot(h_ref[...], wg_ref[...]) + bg_ref[...]
    gates = 1.0 / (1.0 + jnp.exp(-z))
    merged = (gates[:, :tn] * _dot(yf_ref[...], wf_ref[...])
              + gates[:, tn:2 * tn] * _dot(yd_ref[...], wd_ref[...])
              + gates[:, 2 * tn:] * _dot(ym_ref[...], wm_ref[...]))
    acc_ref[...] += _dot(merged.astype(BF16), wo_ref[...])

    @pl.when(n == pl.num_programs(1) - 1)
    def _():
        def emit(sl, y):
            o_ref[sl, :] = x_ref[sl, :] + y

        _rms_rows(acc_ref, gpost_ref[...], emit, F32_ROWS)


def _gate_tile_order(d):
    tn = MERGE_TN
    nt = d // tn
    return tuple(((n * 3 + b) * tn, (b * nt + n) * tn, tn) for n in range(nt) for b in range(3))


def _merge(x, h, y_fox, y_diff, y_mem, w_gate, b_gate, w_fox, w_diff, w_mem, w_out, g_post,
           *, tm=512):
    t, d = x.shape
    tn = MERGE_TN
    nt = d // tn
    row = lambda m, n: (m, 0)
    return pl.pallas_call(
        _merge_kernel,
        out_shape=jax.ShapeDtypeStruct((t, d), F32),
        grid=(t // tm, nt),
        in_specs=[
            pl.BlockSpec((tm, d), row),
            pl.BlockSpec((tm, d), row),
            pl.BlockSpec((tm, FOX_W), row),
            pl.BlockSpec((tm, DIFF_V_W), row),
            pl.BlockSpec((tm, MEM_W), row),
            pl.BlockSpec((d, 3 * tn), lambda m, n: (0, n)),
            pl.BlockSpec((1, 3 * tn), lambda m, n: (0, n)),
            pl.BlockSpec((FOX_W, tn), lambda m, n: (0, n)),
            pl.BlockSpec((DIFF_V_W, tn), lambda m, n: (0, n)),
            pl.BlockSpec((MEM_W, tn), lambda m, n: (0, n)),
            pl.BlockSpec((tn, d), lambda m, n: (n, 0)),
            pl.BlockSpec((1, d), lambda m, n: (0, 0)),
        ],
        out_specs=pl.BlockSpec((tm, d), row),
        scratch_shapes=[pltpu.VMEM((tm, d), F32)],
        compiler_params=_params("parallel", "arbitrary"),
        name="merge",
    )(x, h, y_fox, y_diff, y_mem, w_gate, b_gate, w_fox, w_diff, w_mem, w_out, g_post)


def kernel(x, mem, ffn1_pre_g, ffn1_w_gate, ffn1_w_up, ffn1_w_down, ffn1_post_g, mix_pre_g, w_in, fox_f_bias, diff_lambda_q1, diff_lambda_k1, diff_lambda_q2, diff_lambda_k2, diff_head_g, mem_norm_g, w_mem_kv, w_branch_fox, w_branch_diff, w_branch_mem, w_merge_gate, b_merge_gate, w_out, mix_post_g, ffn2_pre_g, ffn2_w_gate, ffn2_w_up, ffn2_w_down, ffn2_post_g):
    batch, seq, d = x.shape
    depth = w_in.shape[0]
    xt = x.reshape(batch * seq, d)
    cos_t, sa_t, sb_t = _rope_tables(seq)

    def row(v):
        return v.reshape(1, -1).astype(F32)

    for l in range(depth):
        bf = lambda w: w[l].astype(BF16)
        w_in_t = w_in[l].T
        xt, (w2_gu, w_head, w_tail) = _ffn(
            xt, row(ffn1_pre_g[l]), (bf(ffn1_w_gate), bf(ffn1_w_up)), bf(ffn1_w_down),
            row(ffn1_post_g[l]), cast_pair=[(ffn2_w_gate, ffn2_w_up, l)],
            cast_t=[(w_in_t, 0, FF_COL + LANES), (w_in_t, FF_COL + FOX_HEADS, w_in.shape[2])])

        small = [w_out, w_branch_fox, w_branch_diff, w_branch_mem]
        proj, ff, h_mix, (wb_out, wb_fox, wb_diff, wb_mem) = _mix_proj(
            xt, row(mix_pre_g[l]), w_head, w_tail, cos_t, sa_t, sb_t, seq=seq,
            cast=[(w, l) for w in small])

        bias = jnp.pad(fox_f_bias[l].astype(F32), (0, LANES - FOX_HEADS)).reshape(1, LANES)
        fp, frow = _fox_gate(ff, bias, batch=batch, seq=seq)
        y_fox, (w2_down,) = _fox_attn(proj, fp, frow, batch=batch, seq=seq,
                                      cast=[(ffn2_w_down, l)])

        lam_init = 0.8 - 0.6 * math.exp(-0.3 * l)
        lamv = jnp.stack([diff_lambda_q1[l], diff_lambda_k1[l], diff_lambda_q2[l],
                          diff_lambda_k2[l]]).astype(F32)
        gate_order = _gate_tile_order(d)
        y_diff, (wb_merge,) = _diff_attn(proj, lamv, row(diff_head_g[l]), batch=batch, seq=seq,
                                         lam_init=lam_init, cast=[(w_merge_gate, l)],
                                         cast_moves=gate_order)
        b_gate = row(b_merge_gate[l])
        b_gate = jnp.concatenate([b_gate[:, src:src + w] for _, src, w in gate_order], axis=1)

        mkv = _mem_kv(mem, row(mem_norm_g[l]), bf(w_mem_kv))
        y_mem = _mem_attn(proj, mkv, batch=batch, seq=seq)

        xt = _merge(xt, h_mix, y_fox, y_diff, y_mem, wb_merge, b_gate, wb_fox, wb_diff, wb_mem,
                    wb_out, row(mix_post_g[l]))

        xt, _ = _ffn(xt, row(ffn2_pre_g[l]), w2_gu, w2_down, row(ffn2_post_g[l]))

    return xt.reshape(batch, seq, d)
```

```python
import functools
import math

import jax
import jax.numpy as jnp
from jax import lax
from jax.experimental import pallas as pl
from jax.experimental.pallas import tpu as pltpu

D_MODEL = 2048
CHUNK = 64
EPS = 1e-6
ROPE_THETA = 500000.0

FOX_HEADS = 8
FOX_DIM = 128
FOX_W = FOX_HEADS * FOX_DIM

DIFF_HEADS = 4
DIFF_QK_DIM = 64
DIFF_V_DIM = 2 * DIFF_QK_DIM
DIFF_QK_W = DIFF_HEADS * 2 * DIFF_QK_DIM
DIFF_V_W = DIFF_HEADS * DIFF_V_DIM
ROPE_DIM = DIFF_QK_DIM // 4

MEM_HEADS = 4
MEM_DIM = 128
MEM_W = MEM_HEADS * MEM_DIM

LANES = 128
F32_ROWS = 8
BF16_ROWS = 16
RMS_ROWS_IN_FLIGHT = 128
MERGE_TN = 512
PROJ_W = 3 * FOX_W + 2 * DIFF_QK_W + DIFF_V_W + MEM_W
FQ_BLK, FK_BLK, FV_BLK = 0, FOX_HEADS, 2 * FOX_HEADS
DQ_BLK = 3 * FOX_HEADS
DK_BLK = DQ_BLK + DIFF_HEADS
DV_BLK = DK_BLK + DIFF_HEADS
MQ_COL = 3 * FOX_W + 2 * DIFF_QK_W + DIFF_V_W
FF_COL = 3 * FOX_W

VMEM_LIMIT = 56 * 1024 * 1024
BF16 = jnp.bfloat16
F32 = jnp.float32
NEG_INF = float("-inf")
LOG2E = math.log2(math.e)
FOX_QSCALE = FOX_DIM ** -0.5 * LOG2E
DIFF_QSCALE = DIFF_QK_DIM ** -0.5 * LOG2E
MEM_QSCALE = MEM_DIM ** -0.5 * LOG2E


def _dot(a, b):
    return jnp.dot(a, b, preferred_element_type=F32)


def _dot_nt(a, b):
    return lax.dot_general(a, b, (((1,), (1,)), ((), ())), preferred_element_type=F32)


def _rms(x, g):
    return x * lax.rsqrt(jnp.mean(x * x, axis=-1, keepdims=True) + EPS) * g


def _rms_rows(src_ref, g, emit, rows):
    def body(r, carry):
        sl = pl.ds(pl.multiple_of(r * rows, rows), rows)
        emit(sl, _rms(src_ref[sl, :], g))
        return carry

    lax.fori_loop(0, src_ref.shape[0] // rows, body, 0, unroll=RMS_ROWS_IN_FLIGHT // rows)


def _params(*sem):
    return pltpu.CompilerParams(dimension_semantics=sem, vmem_limit_bytes=VMEM_LIMIT)


def _slab_cast_specs(cast, n_steps, step_index):
    in_specs, out_specs, out_shape = [], [], []
    for arr, layer in cast:
        _, rows, cols = arr.shape
        br = rows // n_steps
        assert br * n_steps == rows and br % BF16_ROWS == 0
        in_specs.append(pl.BlockSpec((None, br, cols),
                                     lambda *g, layer=layer: (layer, step_index(*g), 0)))
        out_specs.append(pl.BlockSpec((br, cols), lambda *g: (step_index(*g), 0)))
        out_shape.append(jax.ShapeDtypeStruct((rows, cols), BF16))
    return in_specs, out_specs, out_shape


def _run_slab_casts(src_refs, dst_refs, moves=None):
    for src_ref, dst_ref in zip(src_refs, dst_refs):
        if moves is None:
            dst_ref[...] = src_ref[...].astype(BF16)
        else:
            for dst_lo, src_lo, width in moves:
                dst_ref[:, dst_lo:dst_lo + width] = src_ref[:, src_lo:src_lo + width].astype(BF16)


def _ffn_kernel(*refs, fused, n_pair, flip_blocks):
    n_flip = len(flip_blocks)
    n_w = 1 if fused else 2
    x_ref, gpre_ref = refs[:2]
    w_refs = refs[2:2 + n_w]
    wd_ref, gpost_ref = refs[2 + n_w:4 + n_w]
    side_in = refs[4 + n_w:4 + n_w + 2 * n_pair + n_flip]
    o_ref = refs[4 + n_w + len(side_in)]
    side_out = refs[5 + n_w + len(side_in):-2]
    h_ref, acc_ref = refs[-2:]
    tf = wd_ref.shape[0]
    f = pl.program_id(1)

    @pl.when(f == 0)
    def _():
        def emit(sl, y):
            h_ref[sl, :] = y.astype(BF16)
            acc_ref[sl, :] = jnp.zeros_like(y)

        _rms_rows(x_ref, gpre_ref[...], emit, BF16_ROWS)

    step = pl.program_id(0) * pl.num_programs(1) + f
    for k, n_blk in enumerate(flip_blocks):
        @pl.when(step < n_blk)
        def _(k=k):
            side_out[n_pair + k][...] = side_in[2 * n_pair + k][...].T.astype(BF16)

    h = h_ref[...]
    if fused:
        gu = _dot(h, w_refs[0][...])
        g, u = gu[:, :tf], gu[:, tf:]
    else:
        g = _dot(h, w_refs[0][...])
        u = _dot(h, w_refs[1][...])
    for k in range(n_pair):
        side_out[k][:, :tf] = side_in[2 * k][...].astype(BF16)
        side_out[k][:, tf:] = side_in[2 * k + 1][...].astype(BF16)
    a = (g * (1.0 / (1.0 + jnp.exp(-g)))) * u
    acc_ref[...] += _dot(a.astype(BF16), wd_ref[...])

    @pl.when(f == pl.num_programs(1) - 1)
    def _():
        def emit(sl, y):
            o_ref[sl, :] = x_ref[sl, :] + y

        _rms_rows(acc_ref, 0.5 * gpost_ref[...], emit, F32_ROWS)


def _ffn(x, g_pre, w_gu, w_down, g_post, *, cast_pair=(), cast_t=(), tm=512, tf=512):
    t, d = x.shape
    d_ff = w_down.shape[0]
    m_tiles, f_steps = t // tm, d_ff // tf
    fused = not isinstance(w_gu, tuple)
    if fused:
        assert w_gu.shape == (f_steps, d, 2 * tf)
        w_gu, w_specs = (w_gu,), [pl.BlockSpec((None, d, 2 * tf), lambda m, f: (f, 0, 0))]
    else:
        w_specs = [pl.BlockSpec((d, tf), lambda m, f: (0, f))] * 2
    in_specs = [
        pl.BlockSpec((tm, d), lambda m, f: (m, 0)),
        pl.BlockSpec((1, d), lambda m, f: (0, 0)),
        *w_specs,
        pl.BlockSpec((tf, d), lambda m, f: (f, 0)),
        pl.BlockSpec((1, d), lambda m, f: (0, 0)),
    ]
    out_specs = [pl.BlockSpec((tm, d), lambda m, f: (m, 0))]
    out_shape = [jax.ShapeDtypeStruct((t, d), F32)]
    side_in = []
    for gate, up, layer in cast_pair:
        _, rows, cols = gate.shape
        br = rows // m_tiles
        assert br * m_tiles == rows and cols == d_ff and up.shape == gate.shape
        in_specs += [pl.BlockSpec((None, br, tf), lambda m, f, layer=layer: (layer, m, f))] * 2
        out_specs.append(pl.BlockSpec((None, br, 2 * tf), lambda m, f: (f, m, 0)))
        out_shape.append(jax.ShapeDtypeStruct((f_steps, rows, 2 * tf), BF16))
        side_in += [gate, up]
    for arr, lo, hi in cast_t:
        _, cols = arr.shape
        n_blk = (hi - lo) // LANES
        assert n_blk * LANES == hi - lo and n_blk <= m_tiles * f_steps

        def block(m, f, n_blk=n_blk):
            return jnp.minimum(m * f_steps + f, n_blk - 1)

        in_specs.append(pl.BlockSpec(
            (pl.Element(LANES), pl.Element(cols)),
            lambda m, f, block=block, lo=lo: (
                pl.multiple_of(lo + LANES * block(m, f), math.gcd(lo, LANES)), 0)))
        out_specs.append(pl.BlockSpec((cols, LANES), lambda m, f, block=block: (0, block(m, f))))
        out_shape.append(jax.ShapeDtypeStruct((cols, hi - lo), BF16))
        side_in.append(arr)
    outs = pl.pallas_call(
        functools.partial(_ffn_kernel, fused=fused, n_pair=len(cast_pair),
                          flip_blocks=tuple((hi - lo) // LANES for _, lo, hi in cast_t)),
        out_shape=out_shape,
        grid=(m_tiles, f_steps),
        in_specs=in_specs,
        out_specs=out_specs,
        scratch_shapes=[pltpu.VMEM((tm, d), BF16), pltpu.VMEM((tm, d), F32)],
        compiler_params=_params("parallel", "arbitrary"),
        name="ffn",
    )(x, g_pre, *w_gu, w_down, g_post, *side_in)
    return outs[0], outs[1:]


def _proj_kernel(*refs, tn, n_cast):
    x_ref, g_ref, wa_ref, wff_ref, wb_ref, cos_ref, sa_ref, sb_ref = refs[:8]
    o_ref, ff_ref, h_ref = refs[8 + n_cast:11 + n_cast]
    _run_slab_casts(refs[8:8 + n_cast], refs[11 + n_cast:])
    h_ref[...] = _rms(x_ref[...], g_ref[...]).astype(BF16)
    ff_ref[...] = _dot(h_ref[...], wff_ref[...])
    half = ROPE_DIM // 2
    na = wa_ref.shape[1] // tn
    for n in range(na + wb_ref.shape[1] // tn):
        w_tile = (wa_ref[:, n * tn:(n + 1) * tn] if n < na
                  else wb_ref[:, (n - na) * tn:(n - na + 1) * tn])
        y = _dot(h_ref[...], w_tile)
        first_blk = n * tn // LANES
        if first_blk < FK_BLK:
            y = y * FOX_QSCALE
        if first_blk >= MQ_COL // LANES:
            y = y * MEM_QSCALE
        if DQ_BLK <= first_blk < DV_BLK:
            qscale = DIFF_QSCALE if first_blk < DK_BLK else 1.0
            for j in range(tn // LANES):
                blk = y[:, j * LANES:(j + 1) * LANES]
                rot = (blk * cos_ref[...] + pltpu.roll(blk, LANES - half, axis=1) * sa_ref[...]
                       + pltpu.roll(blk, half, axis=1) * sb_ref[...])
                o_ref[:, n * tn + j * LANES:n * tn + (j + 1) * LANES] = (rot * qscale).astype(BF16)
        else:
            o_ref[:, n * tn:(n + 1) * tn] = y.astype(BF16)


def _mix_proj(x, g, w_head, w_tail, cos_t, sa_t, sb_t, *, seq, cast=(), tm=512, tn=512):
    t, d = x.shape
    n_out = FF_COL + w_tail.shape[1]
    assert all((blk * LANES) % tn == 0 for blk in (FK_BLK, DQ_BLK, DK_BLK, DV_BLK))
    assert w_head.shape[1] == FF_COL + LANES and n_out == PROJ_W
    s_tiles = seq // tm
    m_tiles = t // tm
    resident = dict(pipeline_mode=pl.Buffered(1))
    in_specs = [
        pl.BlockSpec((tm, d), lambda m: (m, 0)),
        pl.BlockSpec((1, d), lambda m: (0, 0)),
        pl.BlockSpec((d, FF_COL), lambda m: (0, 0), **resident),
        pl.BlockSpec((d, LANES), lambda m: (0, FF_COL // LANES), **resident),
        pl.BlockSpec((d, w_tail.shape[1]), lambda m: (0, 0), **resident),
        pl.BlockSpec((tm, LANES), lambda m: (m % s_tiles, 0)),
        pl.BlockSpec((tm, LANES), lambda m: (m % s_tiles, 0)),
        pl.BlockSpec((tm, LANES), lambda m: (m % s_tiles, 0)),
    ]
    out_specs = [pl.BlockSpec((tm, n_out), lambda m: (m, 0)),
                 pl.BlockSpec((tm, LANES), lambda m: (m, 0)),
                 pl.BlockSpec((tm, d), lambda m: (m, 0))]
    out_shape = [jax.ShapeDtypeStruct((t, n_out), BF16), jax.ShapeDtypeStruct((t, LANES), F32),
                 jax.ShapeDtypeStruct((t, d), BF16)]
    cast_in, cast_out, cast_shape = _slab_cast_specs(cast, m_tiles, lambda m: m)
    in_specs, out_specs, out_shape = in_specs + cast_in, out_specs + cast_out, out_shape + cast_shape
    outs = pl.pallas_call(
        functools.partial(_proj_kernel, tn=tn, n_cast=len(cast)),
        out_shape=out_shape,
        grid=(m_tiles,),
        in_specs=in_specs,
        out_specs=out_specs,
        compiler_params=_params("parallel"),
        name="mix_proj",
    )(x, g, w_head, w_head, w_tail, cos_t, sa_t, sb_t, *[arr for arr, _ in cast])
    return outs[0], outs[1], outs[2], outs[3:]


def _rope_tables(seq):
    half = ROPE_DIM // 2
    pos = jnp.arange(seq, dtype=F32)
    inv_freq = ROPE_THETA ** (-jnp.arange(0, ROPE_DIM, 2, dtype=F32) / ROPE_DIM)
    ang = pos[:, None] * inv_freq[None, :]
    cos, sin = jnp.cos(ang), jnp.sin(ang)
    ones = jnp.ones((seq, DIFF_QK_DIM - ROPE_DIM), F32)
    zeros_h = jnp.zeros((seq, half), F32)
    zeros_r = jnp.zeros((seq, DIFF_QK_DIM - ROPE_DIM), F32)
    cos_m = jnp.concatenate([cos, cos, ones], axis=1)
    sa_m = jnp.concatenate([-sin, zeros_h, zeros_r], axis=1)
    sb_m = jnp.concatenate([zeros_h, sin, zeros_r], axis=1)
    rep = LANES // DIFF_QK_DIM
    return (jnp.tile(cos_m, (1, rep)), jnp.tile(sa_m, (1, rep)), jnp.tile(sb_m, (1, rep)))


def _fgate_kernel(ff_ref, bias_ref, fp_ref, frow_ref, fcol_ref, *, cb):
    seq = ff_ref.shape[0]
    head_lane = lax.broadcasted_iota(jnp.int32, (cb, LANES), 1) < FOX_HEADS
    z = ff_ref[...] + bias_ref[...]
    lf = jnp.minimum(z, 0.0) - jnp.log1p(jnp.exp(-jnp.abs(z)))
    r = lax.broadcasted_iota(jnp.int32, (cb, cb), 0)
    c = lax.broadcasted_iota(jnp.int32, (cb, cb), 1)
    tri = (r >= c).astype(F32)
    carry = jnp.zeros((1, LANES), F32)
    for i in range(seq // cb):
        cs = jnp.dot(tri, lf[i * cb:(i + 1) * cb], precision=lax.Precision.HIGHEST,
                     preferred_element_type=F32) + carry
        carry = cs[cb - 1:cb, :]
        cs2 = jnp.where(head_lane, cs * LOG2E, 0.0)
        fcol_ref[i * cb:(i + 1) * cb, :] = cs2
        hi = cs2.astype(BF16).astype(F32)
        mid = (cs2 - hi).astype(BF16).astype(F32)
        lo = (cs2 - hi - mid).astype(BF16).astype(F32)
        pieces = hi + pltpu.roll(mid, FOX_HEADS, axis=1) + pltpu.roll(lo, 2 * FOX_HEADS, axis=1)
        fp_ref[i * cb:(i + 1) * cb, :] = pieces.astype(BF16)
    frow_ref[...] = fcol_ref[...].T[:FOX_HEADS]


def _fox_gate(ff, bias, *, batch, seq, cb=256):
    ff = ff.reshape(batch, seq, LANES)
    return pl.pallas_call(
        functools.partial(_fgate_kernel, cb=cb),
        out_shape=(jax.ShapeDtypeStruct((batch, seq, LANES), BF16),
                   jax.ShapeDtypeStruct((batch, FOX_HEADS, seq), F32)),
        grid=(batch,),
        in_specs=[pl.BlockSpec((None, seq, LANES), lambda b: (b, 0, 0)),
                  pl.BlockSpec((1, LANES), lambda b: (0, 0))],
        out_specs=(pl.BlockSpec((None, seq, LANES), lambda b: (b, 0, 0)),
                   pl.BlockSpec((None, FOX_HEADS, seq), lambda b: (b, 0, 0))),
        scratch_shapes=[pltpu.VMEM((seq, LANES), F32)],
        compiler_params=_params("parallel"),
        name="fox_gate",
    )(ff, bias)


def _dot_tn(a, b):
    return lax.dot_general(a, b, (((0,), (0,)), ((), ())), preferred_element_type=F32)


def _flash_attend(i, heads, logits, values, ft2, visible, finish, side_job,
                  sa_ref, sb_ref, m_ref, l_ref, acc_ref):
    def step(h, s, j):
        m = m_ref[h]
        m_new = jnp.maximum(m, jnp.max(s, axis=0, keepdims=True) + ft2[h])
        alpha = jnp.exp2(m - m_new)
        p = jnp.exp2(s - (m_new - ft2[h]))
        m_ref[h] = m_new
        l_ref[h] = alpha * l_ref[h] + jnp.sum(p, axis=0, keepdims=True)
        acc_ref[h] = alpha * acc_ref[h] + _dot_tn(values(h, j), p.astype(BF16))

    for h in heads:
        m_ref[h] = jnp.full(m_ref.shape[1:], NEG_INF, F32)
        l_ref[h] = jnp.zeros(l_ref.shape[1:], F32)
        acc_ref[h] = jnp.zeros(acc_ref.shape[1:], F32)
        sa_ref[h] = logits(h, 0)
    side_job()

    def pair(jj, carry):
        j = 2 * jj
        for h in heads:
            sb_ref[h] = logits(h, j + 1)
            step(h, sa_ref[h], j)
        for h in heads:
            sa_ref[h] = logits(h, j + 2)
            step(h, sb_ref[h], j + 1)
        return carry

    lax.fori_loop(0, i // 2, pair, 0)

    def last(s_ref):
        for h in heads:
            step(h, jnp.where(visible, s_ref[h], NEG_INF), i)
            finish(h, acc_ref[h] / l_ref[h])

    @pl.when(i % 2 == 0)
    def _():
        last(sa_ref)

    @pl.when(i % 2 == 1)
    def _():
        for h in heads:
            sb_ref[h] = logits(h, i)
            step(h, sa_ref[h], i - 1)
        last(sb_ref)


def _flash_scratch(hb, width, tq, cols):
    return [pltpu.VMEM((hb, tq, cols), F32), pltpu.VMEM((hb, tq, cols), F32),
            pltpu.VMEM((hb, 1, cols), F32), pltpu.VMEM((hb, 1, cols), F32),
            pltpu.VMEM((hb, width, cols), F32)]


def _fox_kernel(*refs, tq, hb, n_cast):
    q_ref, k_ref, v_ref, fp_ref, fr_ref = refs[:5]
    o_ref = refs[5 + n_cast]
    scratch = refs[-5:]
    side_job = functools.partial(_run_slab_casts, refs[5:5 + n_cast], refs[6 + n_cast:-5])
    i = pl.program_id(2)
    h0 = pl.program_id(1) * hb
    heads = range(hb)
    ft2 = [fr_ref[h, pl.ds(i, 1), :] for h in heads]

    lane = lax.broadcasted_iota(jnp.int32, (tq, LANES), 1)
    q_ext = []
    for h in heads:
        mine = (lane % FOX_HEADS == h0 + h) & (lane < 3 * FOX_HEADS)
        sel = jnp.where(mine, -1.0, 0.0).astype(BF16)
        q_ext.append(jnp.concatenate([q_ref[:, h * FOX_DIM:(h + 1) * FOX_DIM], sel], axis=1))

    def logits(h, j):
        start = pl.multiple_of(j * tq, tq)
        k_ext = jnp.concatenate([k_ref[pl.ds(start, tq), h * FOX_DIM:(h + 1) * FOX_DIM],
                                 fp_ref[pl.ds(start, tq), :]], axis=1)
        return _dot_nt(k_ext, q_ext[h])

    def values(h, j):
        return v_ref[pl.ds(pl.multiple_of(j * tq, tq), tq), h * FOX_DIM:(h + 1) * FOX_DIM]

    def finish(h, o_t):
        o_ref[:, h * FOX_DIM:(h + 1) * FOX_DIM] = o_t.T.astype(BF16)

    krow = lax.broadcasted_iota(jnp.int32, (tq, tq), 0)
    qcol = lax.broadcasted_iota(jnp.int32, (tq, tq), 1)
    _flash_attend(i, heads, logits, values, ft2, krow <= qcol, finish, side_job, *scratch)


def _fox_attn(proj, fp, frow, *, batch, seq, cast=(), tq=256, hb=8):
    proj3 = proj.reshape(batch, seq, PROJ_W)
    n_q = seq // tq
    frow4 = frow.reshape(batch, FOX_HEADS, n_q, tq)
    w = hb * FOX_DIM
    groups = FOX_HEADS // hb
    assert groups == 1 or not cast
    cast_in, cast_out, cast_shape = _slab_cast_specs(cast, batch * n_q, lambda b, g, i: b * n_q + i)
    outs = pl.pallas_call(
        functools.partial(_fox_kernel, tq=tq, hb=hb, n_cast=len(cast)),
        out_shape=[jax.ShapeDtypeStruct((batch, seq, FOX_W), BF16)] + cast_shape,
        grid=(batch, groups, n_q),
        in_specs=[
            pl.BlockSpec((None, tq, w), lambda b, g, i: (b, i, g)),
            pl.BlockSpec((None, seq, w), lambda b, g, i: (b, 0, groups + g)),
            pl.BlockSpec((None, seq, w), lambda b, g, i: (b, 0, 2 * groups + g)),
            pl.BlockSpec((None, seq, LANES), lambda b, g, i: (b, 0, 0)),
            pl.BlockSpec((None, hb, n_q, tq), lambda b, g, i: (b, g, 0, 0)),
        ] + cast_in,
        out_specs=[pl.BlockSpec((None, tq, w), lambda b, g, i: (b, i, g))] + cast_out,
        scratch_shapes=_flash_scratch(hb, FOX_DIM, tq, tq),
        compiler_params=_params("parallel", "parallel", "arbitrary"),
        name="fox_attn",
    )(proj3, proj3, proj3, fp, frow4, *[arr for arr, _ in cast])
    return outs[0].reshape(batch * seq, FOX_W), outs[1:]


def _diff_kernel(*refs, tq, hb, lam_init, n_cast, cast_moves):
    lamv_ref, g_ref, q_ref, k_ref, v_ref = refs[:5]
    o_ref = refs[5 + n_cast]
    scratch = refs[-5:]
    side_job = functools.partial(_run_slab_casts, refs[5:5 + n_cast], refs[6 + n_cast:-5],
                                 cast_moves)
    i = pl.program_id(2)
    heads = range(hb)
    lv = lamv_ref[...]
    lam = (jnp.exp(jnp.sum(lv[0:1] * lv[1:2], axis=1, keepdims=True))
           - jnp.exp(jnp.sum(lv[2:3] * lv[3:4], axis=1, keepdims=True)) + lam_init)

    lane = lax.broadcasted_iota(jnp.int32, (tq, LANES), 1)
    qs = []
    for h in heads:
        q = q_ref[:, h * LANES:(h + 1) * LANES].astype(F32)
        qs.append(jnp.concatenate([jnp.where(lane < DIFF_QK_DIM, q, 0.0),
                                   jnp.where(lane >= DIFF_QK_DIM, q, 0.0)], axis=0).astype(BF16))
    zero = jnp.zeros((1, 2 * tq), F32)

    def logits(h, j):
        start = pl.multiple_of(j * tq, tq)
        return _dot_nt(k_ref[pl.ds(start, tq), h * LANES:(h + 1) * LANES], qs[h])

    def values(h, j):
        return v_ref[pl.ds(pl.multiple_of(j * tq, tq), tq), h * LANES:(h + 1) * LANES]

    def finish(h, o_t):
        yd = (o_t[:, :tq] - lam * o_t[:, tq:]).T
        o_ref[:, h * LANES:(h + 1) * LANES] = (
            _rms(yd, g_ref[...]) * (1.0 - lam_init)).astype(BF16)

    krow = lax.broadcasted_iota(jnp.int32, (tq, 2 * tq), 0)
    qcol = lax.broadcasted_iota(jnp.int32, (tq, 2 * tq), 1)
    qcol = jnp.where(qcol >= tq, qcol - tq, qcol)
    visible = krow // CHUNK <= qcol // CHUNK
    _flash_attend(i, heads, logits, values, [zero] * hb, visible, finish, side_job, *scratch)


def _diff_attn(proj, lamv, g, *, batch, seq, lam_init, cast=(), cast_moves=None, tq=256, hb=4):
    proj3 = proj.reshape(batch, seq, PROJ_W)
    n_q = seq // tq
    w = hb * LANES
    groups = DIFF_HEADS // hb
    assert groups == 1 or not cast
    dq, dk, dv = (DQ_BLK * LANES) // w, (DK_BLK * LANES) // w, (DV_BLK * LANES) // w
    cast_in, cast_out, cast_shape = _slab_cast_specs(cast, batch * n_q, lambda b, g, i: b * n_q + i)
    outs = pl.pallas_call(
        functools.partial(_diff_kernel, tq=tq, hb=hb, lam_init=lam_init, n_cast=len(cast),
                          cast_moves=cast_moves),
        out_shape=[jax.ShapeDtypeStruct((batch, seq, DIFF_V_W), BF16)] + cast_shape,
        grid=(batch, groups, n_q),
        in_specs=[
            pl.BlockSpec((4, DIFF_QK_DIM), lambda b, g, i: (0, 0)),
            pl.BlockSpec((1, DIFF_V_DIM), lambda b, g, i: (0, 0)),
            pl.BlockSpec((None, tq, w), lambda b, g, i: (b, i, dq + g)),
            pl.BlockSpec((None, seq, w), lambda b, g, i: (b, 0, dk + g)),
            pl.BlockSpec((None, seq, w), lambda b, g, i: (b, 0, dv + g)),
        ] + cast_in,
        out_specs=[pl.BlockSpec((None, tq, w), lambda b, g, i: (b, i, g))] + cast_out,
        scratch_shapes=_flash_scratch(hb, DIFF_V_DIM, tq, 2 * tq),
        compiler_params=_params("parallel", "parallel", "arbitrary"),
        name="diff_attn",
    )(lamv, g, proj3, proj3, proj3, *[arr for arr, _ in cast])
    return outs[0].reshape(batch * seq, DIFF_V_W), outs[1:]


def _mem_kv_kernel(mem_ref, g_ref, w_ref, o_ref):
    o_ref[...] = _dot(_rms(mem_ref[...], g_ref[...]).astype(BF16), w_ref[...]).astype(BF16)


def _mem_kv(mem, g, w):
    batch, n_mem, d = mem.shape
    return pl.pallas_call(
        _mem_kv_kernel,
        out_shape=jax.ShapeDtypeStruct((batch, n_mem, 2 * MEM_W), BF16),
        grid=(batch,),
        in_specs=[pl.BlockSpec((None, n_mem, d), lambda b: (b, 0, 0)),
                  pl.BlockSpec((1, d), lambda b: (0, 0)),
                  pl.BlockSpec((d, 2 * MEM_W), lambda b: (0, 0))],
        out_specs=pl.BlockSpec((None, n_mem, 2 * MEM_W), lambda b: (b, 0, 0)),
        compiler_params=_params("parallel"),
        name="mem_kv",
    )(mem, g, w)


def _mem_attn_kernel(q_ref, kv_ref, o_ref):
    logits = [_dot_nt(kv_ref[:, h * MEM_DIM:(h + 1) * MEM_DIM],
                      q_ref[:, h * MEM_DIM:(h + 1) * MEM_DIM]) for h in range(MEM_HEADS)]
    for h in range(MEM_HEADS):
        v = kv_ref[:, MEM_W + h * MEM_DIM:MEM_W + (h + 1) * MEM_DIM]
        s = logits[h]
        p = jnp.exp2(s - jnp.max(s, axis=0, keepdims=True))
        l = jnp.sum(p, axis=0, keepdims=True)
        o_t = _dot_tn(v, p.astype(BF16)) / l
        o_ref[:, h * MEM_DIM:(h + 1) * MEM_DIM] = o_t.T.astype(BF16)


def _mem_attn(proj, mkv, *, batch, seq, tq=512):
    proj3 = proj.reshape(batch, seq, PROJ_W)
    n_mem = mkv.shape[1]
    out = pl.pallas_call(
        _mem_attn_kernel,
        out_shape=jax.ShapeDtypeStruct((batch, seq, MEM_W), BF16),
        grid=(batch, seq // tq),
        in_specs=[pl.BlockSpec((None, tq, MEM_W), lambda b, i: (b, i, MQ_COL // MEM_W)),
                  pl.BlockSpec((None, n_mem, 2 * MEM_W), lambda b, i: (b, 0, 0))],
        out_specs=pl.BlockSpec((None, tq, MEM_W), lambda b, i: (b, i, 0)),
        compiler_params=_params("parallel", "arbitrary"),
        name="mem_attn",
    )(proj3, mkv)
    return out.reshape(batch * seq, MEM_W)


def _merge_kernel(x_ref, h_ref, yf_ref, yd_ref, ym_ref, wg_ref, bg_ref, wf_ref, wd_ref, wm_ref,
                  wo_ref, gpost_ref, o_ref, acc_ref):
    n = pl.program_id(1)
    tn = wo_ref.shape[0]

    @pl.when(n == 0)
    def _():
        acc_ref[...] = jnp.zeros_like(acc_ref)

    z = _dot(h_ref[...], wg_ref[...]) + bg_ref[...]
    gates = 1.0 / (1.0 + jnp.exp(-z))
    merged = (gates[:, :tn] * _dot(yf_ref[...], wf_ref[...])
              + gates[:, tn:2 * tn] * _dot(yd_ref[...], wd_ref[...])
              + gates[:, 2 * tn:] * _dot(ym_ref[...], wm_ref[...]))
    acc_ref[...] += _dot(merged.astype(BF16), wo_ref[...])

    @pl.when(n == pl.num_programs(1) - 1)
    def _():
        def emit(sl, y):
            o_ref[sl, :] = x_ref[sl, :] + y

        _rms_rows(acc_ref, gpost_ref[...], emit, F32_ROWS)


def _gate_tile_order(d):
    tn = MERGE_TN
    nt = d // tn
    return tuple(((n * 3 + b) * tn, (b * nt + n) * tn, tn) for n in range(nt) for b in range(3))


def _merge(x, h, y_fox, y_diff, y_mem, w_gate, b_gate, w_fox, w_diff, w_mem, w_out, g_post,
           *, tm=512):
    t, d = x.shape
    tn = MERGE_TN
    nt = d // tn
    row = lambda m, n: (m, 0)
    return pl.pallas_call(
        _merge_kernel,
        out_shape=jax.ShapeDtypeStruct((t, d), F32),
        grid=(t // tm, nt),
        in_specs=[
            pl.BlockSpec((tm, d), row),
            pl.BlockSpec((tm, d), row),
            pl.BlockSpec((tm, FOX_W), row),
            pl.BlockSpec((tm, DIFF_V_W), row),
            pl.BlockSpec((tm, MEM_W), row),
            pl.BlockSpec((d, 3 * tn), lambda m, n: (0, n)),
            pl.BlockSpec((1, 3 * tn), lambda m, n: (0, n)),
            pl.BlockSpec((FOX_W, tn), lambda m, n: (0, n)),
            pl.BlockSpec((DIFF_V_W, tn), lambda m, n: (0, n)),
            pl.BlockSpec((MEM_W, tn), lambda m, n: (0, n)),
            pl.BlockSpec((tn, d), lambda m, n: (n, 0)),
            pl.BlockSpec((1, d), lambda m, n: (0, 0)),
        ],
        out_specs=pl.BlockSpec((tm, d), row),
        scratch_shapes=[pltpu.VMEM((tm, d), F32)],
        compiler_params=_params("parallel", "arbitrary"),
        name="merge",
    )(x, h, y_fox, y_diff, y_mem, w_gate, b_gate, w_fox, w_diff, w_mem, w_out, g_post)


def kernel(x, mem, ffn1_pre_g, ffn1_w_gate, ffn1_w_up, ffn1_w_down, ffn1_post_g, mix_pre_g, w_in, fox_f_bias, diff_lambda_q1, diff_lambda_k1, diff_lambda_q2, diff_lambda_k2, diff_head_g, mem_norm_g, w_mem_kv, w_branch_fox, w_branch_diff, w_branch_mem, w_merge_gate, b_merge_gate, w_out, mix_post_g, ffn2_pre_g, ffn2_w_gate, ffn2_w_up, ffn2_w_down, ffn2_post_g):
    batch, seq, d = x.shape
    depth = w_in.shape[0]
    xt = x.reshape(batch * seq, d)
    cos_t, sa_t, sb_t = _rope_tables(seq)

    def row(v):
        return v.reshape(1, -1).astype(F32)

    for l in range(depth):
        bf = lambda w: w[l].astype(BF16)
        w_in_t = w_in[l].T
        xt, (w2_gu, w_head, w_tail) = _ffn(
            xt, row(ffn1_pre_g[l]), (bf(ffn1_w_gate), bf(ffn1_w_up)), bf(ffn1_w_down),
            row(ffn1_post_g[l]), cast_pair=[(ffn2_w_gate, ffn2_w_up, l)],
            cast_t=[(w_in_t, 0, FF_COL + LANES), (w_in_t, FF_COL + FOX_HEADS, w_in.shape[2])])

        small = [w_out, w_branch_fox, w_branch_diff, w_branch_mem]
        proj, ff, h_mix, (wb_out, wb_fox, wb_diff, wb_mem) = _mix_proj(
            xt, row(mix_pre_g[l]), w_head, w_tail, cos_t, sa_t, sb_t, seq=seq,
            cast=[(w, l) for w in small])

        bias = jnp.pad(fox_f_bias[l].astype(F32), (0, LANES - FOX_HEADS)).reshape(1, LANES)
        fp, frow = _fox_gate(ff, bias, batch=batch, seq=seq)
        y_fox, (w2_down,) = _fox_attn(proj, fp, frow, batch=batch, seq=seq,
                                      cast=[(ffn2_w_down, l)])

        lam_init = 0.8 - 0.6 * math.exp(-0.3 * l)
        lamv = jnp.stack([diff_lambda_q1[l], diff_lambda_k1[l], diff_lambda_q2[l],
                          diff_lambda_k2[l]]).astype(F32)
        gate_order = _gate_tile_order(d)
        y_diff, (wb_merge,) = _diff_attn(proj, lamv, row(diff_head_g[l]), batch=batch, seq=seq,
                                         lam_init=lam_init, cast=[(w_merge_gate, l)],
                                         cast_moves=gate_order)
        b_gate = row(b_merge_gate[l])
        b_gate = jnp.concatenate([b_gate[:, src:src + w] for _, src, w in gate_order], axis=1)

        mkv = _mem_kv(mem, row(mem_norm_g[l]), bf(w_mem_kv))
        y_mem = _mem_attn(proj, mkv, batch=batch, seq=seq)

        xt = _merge(xt, h_mix, y_fox, y_diff, y_mem, wb_merge, b_gate, wb_fox, wb_diff, wb_mem,
                    wb_out, row(mix_post_g[l]))

        xt, _ = _ffn(xt, row(ffn2_pre_g[l]), w2_gu, w2_down, row(ffn2_post_g[l]))

    return xt.reshape(batch, seq, d)
```

```python
import functools
import math

import jax
import jax.numpy as jnp
from jax import lax
from jax.experimental import pallas as pl
from jax.experimental.pallas import tpu as pltpu

D_MODEL = 2048
CHUNK = 64
EPS = 1e-6
ROPE_THETA = 500000.0

FOX_HEADS = 8
FOX_DIM = 128
FOX_W = FOX_HEADS * FOX_DIM

DIFF_HEADS = 4
DIFF_QK_DIM = 64
DIFF_V_DIM = 2 * DIFF_QK_DIM
DIFF_QK_W = DIFF_HEADS * 2 * DIFF_QK_DIM
DIFF_V_W = DIFF_HEADS * DIFF_V_DIM
ROPE_DIM = DIFF_QK_DIM // 4

MEM_HEADS = 4
MEM_DIM = 128
MEM_W = MEM_HEADS * MEM_DIM

LANES = 128
F32_ROWS = 8
BF16_ROWS = 16
RMS_ROWS_IN_FLIGHT = 128
MERGE_TN = 512
PROJ_W = 3 * FOX_W + 2 * DIFF_QK_W + DIFF_V_W + MEM_W
FQ_BLK, FK_BLK, FV_BLK = 0, FOX_HEADS, 2 * FOX_HEADS
DQ_BLK = 3 * FOX_HEADS
DK_BLK = DQ_BLK + DIFF_HEADS
DV_BLK = DK_BLK + DIFF_HEADS
MQ_COL = 3 * FOX_W + 2 * DIFF_QK_W + DIFF_V_W
FF_COL = 3 * FOX_W

VMEM_LIMIT = 56 * 1024 * 1024
BF16 = jnp.bfloat16
F32 = jnp.float32
NEG_INF = float("-inf")
LOG2E = math.log2(math.e)
FOX_QSCALE = FOX_DIM ** -0.5 * LOG2E
DIFF_QSCALE = DIFF_QK_DIM ** -0.5 * LOG2E
MEM_QSCALE = MEM_DIM ** -0.5 * LOG2E


def _dot(a, b):
    return jnp.dot(a, b, preferred_element_type=F32)


def _dot_nt(a, b):
    return lax.dot_general(a, b, (((1,), (1,)), ((), ())), preferred_element_type=F32)


def _rms(x, g):
    return x * lax.rsqrt(jnp.mean(x * x, axis=-1, keepdims=True) + EPS) * g


def _rms_rows(src_ref, g, emit, rows):
    def body(r, carry):
        sl = pl.ds(pl.multiple_of(r * rows, rows), rows)
        emit(sl, _rms(src_ref[sl, :], g))
        return carry

    lax.fori_loop(0, src_ref.shape[0] // rows, body, 0, unroll=RMS_ROWS_IN_FLIGHT // rows)


def _params(*sem):
    return pltpu.CompilerParams(dimension_semantics=sem, vmem_limit_bytes=VMEM_LIMIT)


def _slab_cast_specs(cast, n_steps, step_index):
    in_specs, out_specs, out_shape = [], [], []
    for arr, layer in cast:
        _, rows, cols = arr.shape
        br = rows // n_steps
        assert br * n_steps == rows and br % BF16_ROWS == 0
        in_specs.append(pl.BlockSpec((None, br, cols),
                                     lambda *g, layer=layer: (layer, step_index(*g), 0)))
        out_specs.append(pl.BlockSpec((br, cols), lambda *g: (step_index(*g), 0)))
        out_shape.append(jax.ShapeDtypeStruct((rows, cols), BF16))
    return in_specs, out_specs, out_shape


def _run_slab_casts(src_refs, dst_refs, moves=None):
    for src_ref, dst_ref in zip(src_refs, dst_refs):
        if moves is None:
            dst_ref[...] = src_ref[...].astype(BF16)
        else:
            for dst_lo, src_lo, width in moves:
                dst_ref[:, dst_lo:dst_lo + width] = src_ref[:, src_lo:src_lo + width].astype(BF16)


def _ffn_kernel(*refs, transposed):
    n_cast = len(transposed)
    x_ref, gpre_ref, wg_ref, wu_ref, wd_ref, gpost_ref = refs[:6]
    o_ref = refs[6 + n_cast]
    h_ref, acc_ref = refs[-2:]
    f = pl.program_id(1)

    @pl.when(f == 0)
    def _():
        def emit(sl, y):
            h_ref[sl, :] = y.astype(BF16)
            acc_ref[sl, :] = jnp.zeros_like(y)

        _rms_rows(x_ref, gpre_ref[...], emit, BF16_ROWS)

    jobs = list(zip(refs[6:6 + n_cast], refs[7 + n_cast:7 + 2 * n_cast], transposed))

    def run_casts(part, parts=3):
        for src_ref, dst_ref, flip in jobs[part::parts]:
            blk = src_ref[...]
            dst_ref[...] = (blk.T if flip else blk).astype(BF16)

    h = h_ref[...]
    run_casts(0)
    g = _dot(h, wg_ref[...])
    run_casts(1)
    u = _dot(h, wu_ref[...])
    run_casts(2)
    a = (g * (1.0 / (1.0 + jnp.exp(-g)))) * u
    acc_ref[...] += _dot(a.astype(BF16), wd_ref[...])

    @pl.when(f == pl.num_programs(1) - 1)
    def _():
        def emit(sl, y):
            o_ref[sl, :] = x_ref[sl, :] + y

        _rms_rows(acc_ref, 0.5 * gpost_ref[...], emit, F32_ROWS)


def _ffn(x, g_pre, w_gate, w_up, w_down, g_post, *, cast=(), cast_t=(), tm=512, tf=512):
    t, d = x.shape
    d_ff = w_down.shape[0]
    m_tiles, f_steps = t // tm, d_ff // tf

    def up_spec(w):
        if w.ndim == 3:
            assert w.shape == (f_steps, d, tf)
            return pl.BlockSpec((None, d, tf), lambda m, f: (f, 0, 0))
        return pl.BlockSpec((d, tf), lambda m, f: (0, f))

    in_specs = [
        pl.BlockSpec((tm, d), lambda m, f: (m, 0)),
        pl.BlockSpec((1, d), lambda m, f: (0, 0)),
        up_spec(w_gate),
        up_spec(w_up),
        pl.BlockSpec((tf, d), lambda m, f: (f, 0)),
        pl.BlockSpec((1, d), lambda m, f: (0, 0)),
    ]
    out_specs = [pl.BlockSpec((tm, d), lambda m, f: (m, 0))]
    out_shape = [jax.ShapeDtypeStruct((t, d), F32)]
    for arr, layer in cast:
        _, rows, cols = arr.shape
        br = rows // m_tiles
        assert br * m_tiles == rows and cols == d_ff
        in_specs.append(pl.BlockSpec((None, br, tf), lambda m, f, layer=layer: (layer, m, f)))
        out_specs.append(pl.BlockSpec((None, br, tf), lambda m, f: (f, m, 0)))
        out_shape.append(jax.ShapeDtypeStruct((f_steps, rows, tf), BF16))
    for arr, lo, hi in cast_t:
        _, cols = arr.shape
        n_blk = (hi - lo) // LANES
        assert n_blk * LANES == hi - lo and n_blk <= m_tiles * f_steps

        def block(m, f, n_blk=n_blk):
            return jnp.minimum(m * f_steps + f, n_blk - 1)

        in_specs.append(pl.BlockSpec(
            (pl.Element(LANES), pl.Element(cols)),
            lambda m, f, block=block, lo=lo: (
                pl.multiple_of(lo + LANES * block(m, f), math.gcd(lo, LANES)), 0)))
        out_specs.append(pl.BlockSpec((cols, LANES), lambda m, f, block=block: (0, block(m, f))))
        out_shape.append(jax.ShapeDtypeStruct((cols, hi - lo), BF16))
    outs = pl.pallas_call(
        functools.partial(_ffn_kernel, transposed=(False,) * len(cast) + (True,) * len(cast_t)),
        out_shape=out_shape,
        grid=(m_tiles, f_steps),
        in_specs=in_specs,
        out_specs=out_specs,
        scratch_shapes=[pltpu.VMEM((tm, d), BF16), pltpu.VMEM((tm, d), F32)],
        compiler_params=_params("parallel", "arbitrary"),
        name="ffn",
    )(x, g_pre, w_gate, w_up, w_down, g_post, *[c[0] for c in cast], *[c[0] for c in cast_t])
    return outs[0], outs[1:]


def _proj_kernel(*refs, tn, n_cast):
    x_ref, g_ref, wa_ref, wff_ref, wb_ref, cos_ref, sa_ref, sb_ref = refs[:8]
    o_ref, ff_ref, h_ref = refs[8 + n_cast:11 + n_cast]
    _run_slab_casts(refs[8:8 + n_cast], refs[11 + n_cast:])
    h_ref[...] = _rms(x_ref[...], g_ref[...]).astype(BF16)
    ff_ref[...] = _dot(h_ref[...], wff_ref[...])
    half = ROPE_DIM // 2
    na = wa_ref.shape[1] // tn
    for n in range(na + wb_ref.shape[1] // tn):
        w_tile = (wa_ref[:, n * tn:(n + 1) * tn] if n < na
                  else wb_ref[:, (n - na) * tn:(n - na + 1) * tn])
        y = _dot(h_ref[...], w_tile)
        first_blk = n * tn // LANES
        if first_blk < FK_BLK:
            y = y * FOX_QSCALE
        if first_blk >= MQ_COL // LANES:
            y = y * MEM_QSCALE
        if DQ_BLK <= first_blk < DV_BLK:
            qscale = DIFF_QSCALE if first_blk < DK_BLK else 1.0
            for j in range(tn // LANES):
                blk = y[:, j * LANES:(j + 1) * LANES]
                rot = (blk * cos_ref[...] + pltpu.roll(blk, LANES - half, axis=1) * sa_ref[...]
                       + pltpu.roll(blk, half, axis=1) * sb_ref[...])
                o_ref[:, n * tn + j * LANES:n * tn + (j + 1) * LANES] = (rot * qscale).astype(BF16)
        else:
            o_ref[:, n * tn:(n + 1) * tn] = y.astype(BF16)


def _mix_proj(x, g, w_head, w_tail, cos_t, sa_t, sb_t, *, seq, cast=(), tm=512, tn=512):
    t, d = x.shape
    n_out = FF_COL + w_tail.shape[1]
    assert all((blk * LANES) % tn == 0 for blk in (FK_BLK, DQ_BLK, DK_BLK, DV_BLK))
    assert w_head.shape[1] == FF_COL + LANES and n_out == PROJ_W
    s_tiles = seq // tm
    m_tiles = t // tm
    resident = dict(pipeline_mode=pl.Buffered(1))
    in_specs = [
        pl.BlockSpec((tm, d), lambda m: (m, 0)),
        pl.BlockSpec((1, d), lambda m: (0, 0)),
        pl.BlockSpec((d, FF_COL), lambda m: (0, 0), **resident),
        pl.BlockSpec((d, LANES), lambda m: (0, FF_COL // LANES), **resident),
        pl.BlockSpec((d, w_tail.shape[1]), lambda m: (0, 0), **resident),
        pl.BlockSpec((tm, LANES), lambda m: (m % s_tiles, 0)),
        pl.BlockSpec((tm, LANES), lambda m: (m % s_tiles, 0)),
        pl.BlockSpec((tm, LANES), lambda m: (m % s_tiles, 0)),
    ]
    out_specs = [pl.BlockSpec((tm, n_out), lambda m: (m, 0)),
                 pl.BlockSpec((tm, LANES), lambda m: (m, 0)),
                 pl.BlockSpec((tm, d), lambda m: (m, 0))]
    out_shape = [jax.ShapeDtypeStruct((t, n_out), BF16), jax.ShapeDtypeStruct((t, LANES), F32),
                 jax.ShapeDtypeStruct((t, d), BF16)]
    cast_in, cast_out, cast_shape = _slab_cast_specs(cast, m_tiles, lambda m: m)
    in_specs, out_specs, out_shape = in_specs + cast_in, out_specs + cast_out, out_shape + cast_shape
    outs = pl.pallas_call(
        functools.partial(_proj_kernel, tn=tn, n_cast=len(cast)),
        out_shape=out_shape,
        grid=(m_tiles,),
        in_specs=in_specs,
        out_specs=out_specs,
        compiler_params=_params("parallel"),
        name="mix_proj",
    )(x, g, w_head, w_head, w_tail, cos_t, sa_t, sb_t, *[arr for arr, _ in cast])
    return outs[0], outs[1], outs[2], outs[3:]


def _rope_tables(seq):
    half = ROPE_DIM // 2
    pos = jnp.arange(seq, dtype=F32)
    inv_freq = ROPE_THETA ** (-jnp.arange(0, ROPE_DIM, 2, dtype=F32) / ROPE_DIM)
    ang = pos[:, None] * inv_freq[None, :]
    cos, sin = jnp.cos(ang), jnp.sin(ang)
    ones = jnp.ones((seq, DIFF_QK_DIM - ROPE_DIM), F32)
    zeros_h = jnp.zeros((seq, half), F32)
    zeros_r = jnp.zeros((seq, DIFF_QK_DIM - ROPE_DIM), F32)
    cos_m = jnp.concatenate([cos, cos, ones], axis=1)
    sa_m = jnp.concatenate([-sin, zeros_h, zeros_r], axis=1)
    sb_m = jnp.concatenate([zeros_h, sin, zeros_r], axis=1)
    rep = LANES // DIFF_QK_DIM
    return (jnp.tile(cos_m, (1, rep)), jnp.tile(sa_m, (1, rep)), jnp.tile(sb_m, (1, rep)))


def _fgate_kernel(ff_ref, bias_ref, fp_ref, frow_ref, fcol_ref, *, cb):
    seq = ff_ref.shape[0]
    head_lane = lax.broadcasted_iota(jnp.int32, (cb, LANES), 1) < FOX_HEADS
    z = ff_ref[...] + bias_ref[...]
    lf = jnp.minimum(z, 0.0) - jnp.log1p(jnp.exp(-jnp.abs(z)))
    r = lax.broadcasted_iota(jnp.int32, (cb, cb), 0)
    c = lax.broadcasted_iota(jnp.int32, (cb, cb), 1)
    tri = (r >= c).astype(F32)
    carry = jnp.zeros((1, LANES), F32)
    for i in range(seq // cb):
        cs = jnp.dot(tri, lf[i * cb:(i + 1) * cb], precision=lax.Precision.HIGHEST,
                     preferred_element_type=F32) + carry
        carry = cs[cb - 1:cb, :]
        cs2 = jnp.where(head_lane, cs * LOG2E, 0.0)
        fcol_ref[i * cb:(i + 1) * cb, :] = cs2
        hi = cs2.astype(BF16).astype(F32)
        mid = (cs2 - hi).astype(BF16).astype(F32)
        lo = (cs2 - hi - mid).astype(BF16).astype(F32)
        pieces = hi + pltpu.roll(mid, FOX_HEADS, axis=1) + pltpu.roll(lo, 2 * FOX_HEADS, axis=1)
        fp_ref[i * cb:(i + 1) * cb, :] = pieces.astype(BF16)
    frow_ref[...] = fcol_ref[...].T[:FOX_HEADS]


def _fox_gate(ff, bias, *, batch, seq, cb=256):
    ff = ff.reshape(batch, seq, LANES)
    return pl.pallas_call(
        functools.partial(_fgate_kernel, cb=cb),
        out_shape=(jax.ShapeDtypeStruct((batch, seq, LANES), BF16),
                   jax.ShapeDtypeStruct((batch, FOX_HEADS, seq), F32)),
        grid=(batch,),
        in_specs=[pl.BlockSpec((None, seq, LANES), lambda b: (b, 0, 0)),
                  pl.BlockSpec((1, LANES), lambda b: (0, 0))],
        out_specs=(pl.BlockSpec((None, seq, LANES), lambda b: (b, 0, 0)),
                   pl.BlockSpec((None, FOX_HEADS, seq), lambda b: (b, 0, 0))),
        scratch_shapes=[pltpu.VMEM((seq, LANES), F32)],
        compiler_params=_params("parallel"),
        name="fox_gate",
    )(ff, bias)


def _dot_tn(a, b):
    return lax.dot_general(a, b, (((0,), (0,)), ((), ())), preferred_element_type=F32)


def _flash_attend(i, heads, logits, values, ft2, visible, finish, side_job,
                  sa_ref, sb_ref, m_ref, l_ref, acc_ref):
    def step(h, s, j):
        m = m_ref[h]
        m_new = jnp.maximum(m, jnp.max(s, axis=0, keepdims=True) + ft2[h])
        alpha = jnp.exp2(m - m_new)
        p = jnp.exp2(s - (m_new - ft2[h]))
        m_ref[h] = m_new
        l_ref[h] = alpha * l_ref[h] + jnp.sum(p, axis=0, keepdims=True)
        acc_ref[h] = alpha * acc_ref[h] + _dot_tn(values(h, j), p.astype(BF16))

    for h in heads:
        m_ref[h] = jnp.full(m_ref.shape[1:], NEG_INF, F32)
        l_ref[h] = jnp.zeros(l_ref.shape[1:], F32)
        acc_ref[h] = jnp.zeros(acc_ref.shape[1:], F32)
        sa_ref[h] = logits(h, 0)
    side_job()

    def pair(jj, carry):
        j = 2 * jj
        for h in heads:
            sb_ref[h] = logits(h, j + 1)
            step(h, sa_ref[h], j)
        for h in heads:
            sa_ref[h] = logits(h, j + 2)
            step(h, sb_ref[h], j + 1)
        return carry

    lax.fori_loop(0, i // 2, pair, 0)

    def last(s_ref):
        for h in heads:
            step(h, jnp.where(visible, s_ref[h], NEG_INF), i)
            finish(h, acc_ref[h] / l_ref[h])

    @pl.when(i % 2 == 0)
    def _():
        last(sa_ref)

    @pl.when(i % 2 == 1)
    def _():
        for h in heads:
            sb_ref[h] = logits(h, i)
            step(h, sa_ref[h], i - 1)
        last(sb_ref)


def _flash_scratch(hb, width, tq, cols):
    return [pltpu.VMEM((hb, tq, cols), F32), pltpu.VMEM((hb, tq, cols), F32),
            pltpu.VMEM((hb, 1, cols), F32), pltpu.VMEM((hb, 1, cols), F32),
            pltpu.VMEM((hb, width, cols), F32)]


def _fox_kernel(*refs, tq, hb, n_cast):
    q_ref, k_ref, v_ref, fp_ref, fr_ref = refs[:5]
    o_ref = refs[5 + n_cast]
    scratch = refs[-5:]
    side_job = functools.partial(_run_slab_casts, refs[5:5 + n_cast], refs[6 + n_cast:-5])
    i = pl.program_id(2)
    h0 = pl.program_id(1) * hb
    heads = range(hb)
    ft2 = [fr_ref[h, pl.ds(i, 1), :] for h in heads]

    lane = lax.broadcasted_iota(jnp.int32, (tq, LANES), 1)
    q_ext = []
    for h in heads:
        mine = (lane % FOX_HEADS == h0 + h) & (lane < 3 * FOX_HEADS)
        sel = jnp.where(mine, -1.0, 0.0).astype(BF16)
        q_ext.append(jnp.concatenate([q_ref[:, h * FOX_DIM:(h + 1) * FOX_DIM], sel], axis=1))

    def logits(h, j):
        start = pl.multiple_of(j * tq, tq)
        k_ext = jnp.concatenate([k_ref[pl.ds(start, tq), h * FOX_DIM:(h + 1) * FOX_DIM],
                                 fp_ref[pl.ds(start, tq), :]], axis=1)
        return _dot_nt(k_ext, q_ext[h])

    def values(h, j):
        return v_ref[pl.ds(pl.multiple_of(j * tq, tq), tq), h * FOX_DIM:(h + 1) * FOX_DIM]

    def finish(h, o_t):
        o_ref[:, h * FOX_DIM:(h + 1) * FOX_DIM] = o_t.T.astype(BF16)

    krow = lax.broadcasted_iota(jnp.int32, (tq, tq), 0)
    qcol = lax.broadcasted_iota(jnp.int32, (tq, tq), 1)
    _flash_attend(i, heads, logits, values, ft2, krow <= qcol, finish, side_job, *scratch)


def _fox_attn(proj, fp, frow, *, batch, seq, cast=(), tq=256, hb=8):
    proj3 = proj.reshape(batch, seq, PROJ_W)
    n_q = seq // tq
    frow4 = frow.reshape(batch, FOX_HEADS, n_q, tq)
    w = hb * FOX_DIM
    groups = FOX_HEADS // hb
    assert groups == 1 or not cast
    cast_in, cast_out, cast_shape = _slab_cast_specs(cast, batch * n_q, lambda b, g, i: b * n_q + i)
    outs = pl.pallas_call(
        functools.partial(_fox_kernel, tq=tq, hb=hb, n_cast=len(cast)),
        out_shape=[jax.ShapeDtypeStruct((batch, seq, FOX_W), BF16)] + cast_shape,
        grid=(batch, groups, n_q),
        in_specs=[
            pl.BlockSpec((None, tq, w), lambda b, g, i: (b, i, g)),
            pl.BlockSpec((None, seq, w), lambda b, g, i: (b, 0, groups + g)),
            pl.BlockSpec((None, seq, w), lambda b, g, i: (b, 0, 2 * groups + g)),
            pl.BlockSpec((None, seq, LANES), lambda b, g, i: (b, 0, 0)),
            pl.BlockSpec((None, hb, n_q, tq), lambda b, g, i: (b, g, 0, 0)),
        ] + cast_in,
        out_specs=[pl.BlockSpec((None, tq, w), lambda b, g, i: (b, i, g))] + cast_out,
        scratch_shapes=_flash_scratch(hb, FOX_DIM, tq, tq),
        compiler_params=_params("parallel", "parallel", "arbitrary"),
        name="fox_attn",
    )(proj3, proj3, proj3, fp, frow4, *[arr for arr, _ in cast])
    return outs[0].reshape(batch * seq, FOX_W), outs[1:]


def _diff_kernel(*refs, tq, hb, lam_init, n_cast, cast_moves):
    lamv_ref, g_ref, q_ref, k_ref, v_ref = refs[:5]
    o_ref = refs[5 + n_cast]
    scratch = refs[-5:]
    side_job = functools.partial(_run_slab_casts, refs[5:5 + n_cast], refs[6 + n_cast:-5],
                                 cast_moves)
    i = pl.program_id(2)
    heads = range(hb)
    lv = lamv_ref[...]
    lam = (jnp.exp(jnp.sum(lv[0:1] * lv[1:2], axis=1, keepdims=True))
           - jnp.exp(jnp.sum(lv[2:3] * lv[3:4], axis=1, keepdims=True)) + lam_init)

    lane = lax.broadcasted_iota(jnp.int32, (tq, LANES), 1)
    qs = []
    for h in heads:
        q = q_ref[:, h * LANES:(h + 1) * LANES].astype(F32)
        qs.append(jnp.concatenate([jnp.where(lane < DIFF_QK_DIM, q, 0.0),
                                   jnp.where(lane >= DIFF_QK_DIM, q, 0.0)], axis=0).astype(BF16))
    zero = jnp.zeros((1, 2 * tq), F32)

    def logits(h, j):
        start = pl.multiple_of(j * tq, tq)
        return _dot_nt(k_ref[pl.ds(start, tq), h * LANES:(h + 1) * LANES], qs[h])

    def values(h, j):
        return v_ref[pl.ds(pl.multiple_of(j * tq, tq), tq), h * LANES:(h + 1) * LANES]

    def finish(h, o_t):
        yd = (o_t[:, :tq] - lam * o_t[:, tq:]).T
        o_ref[:, h * LANES:(h + 1) * LANES] = (
            _rms(yd, g_ref[...]) * (1.0 - lam_init)).astype(BF16)

    krow = lax.broadcasted_iota(jnp.int32, (tq, 2 * tq), 0)
    qcol = lax.broadcasted_iota(jnp.int32, (tq, 2 * tq), 1)
    qcol = jnp.where(qcol >= tq, qcol - tq, qcol)
    visible = krow // CHUNK <= qcol // CHUNK
    _flash_attend(i, heads, logits, values, [zero] * hb, visible, finish, side_job, *scratch)


def _diff_attn(proj, lamv, g, *, batch, seq, lam_init, cast=(), cast_moves=None, tq=256, hb=4):
    proj3 = proj.reshape(batch, seq, PROJ_W)
    n_q = seq // tq
    w = hb * LANES
    groups = DIFF_HEADS // hb
    assert groups == 1 or not cast
    dq, dk, dv = (DQ_BLK * LANES) // w, (DK_BLK * LANES) // w, (DV_BLK * LANES) // w
    cast_in, cast_out, cast_shape = _slab_cast_specs(cast, batch * n_q, lambda b, g, i: b * n_q + i)
    outs = pl.pallas_call(
        functools.partial(_diff_kernel, tq=tq, hb=hb, lam_init=lam_init, n_cast=len(cast),
                          cast_moves=cast_moves),
        out_shape=[jax.ShapeDtypeStruct((batch, seq, DIFF_V_W), BF16)] + cast_shape,
        grid=(batch, groups, n_q),
        in_specs=[
            pl.BlockSpec((4, DIFF_QK_DIM), lambda b, g, i: (0, 0)),
            pl.BlockSpec((1, DIFF_V_DIM), lambda b, g, i: (0, 0)),
            pl.BlockSpec((None, tq, w), lambda b, g, i: (b, i, dq + g)),
            pl.BlockSpec((None, seq, w), lambda b, g, i: (b, 0, dk + g)),
            pl.BlockSpec((None, seq, w), lambda b, g, i: (b, 0, dv + g)),
        ] + cast_in,
        out_specs=[pl.BlockSpec((None, tq, w), lambda b, g, i: (b, i, g))] + cast_out,
        scratch_shapes=_flash_scratch(hb, DIFF_V_DIM, tq, 2 * tq),
        compiler_params=_params("parallel", "parallel", "arbitrary"),
        name="diff_attn",
    )(lamv, g, proj3, proj3, proj3, *[arr for arr, _ in cast])
    return outs[0].reshape(batch * seq, DIFF_V_W), outs[1:]


def _mem_kv_kernel(mem_ref, g_ref, w_ref, o_ref):
    o_ref[...] = _dot(_rms(mem_ref[...], g_ref[...]).astype(BF16), w_ref[...]).astype(BF16)


def _mem_kv(mem, g, w):
    batch, n_mem, d = mem.shape
    return pl.pallas_call(
        _mem_kv_kernel,
        out_shape=jax.ShapeDtypeStruct((batch, n_mem, 2 * MEM_W), BF16),
        grid=(batch,),
        in_specs=[pl.BlockSpec((None, n_mem, d), lambda b: (b, 0, 0)),
                  pl.BlockSpec((1, d), lambda b: (0, 0)),
                  pl.BlockSpec((d, 2 * MEM_W), lambda b: (0, 0))],
        out_specs=pl.BlockSpec((None, n_mem, 2 * MEM_W), lambda b: (b, 0, 0)),
        compiler_params=_params("parallel"),
        name="mem_kv",
    )(mem, g, w)


def _mem_attn_kernel(q_ref, kv_ref, o_ref):
    logits = [_dot_nt(kv_ref[:, h * MEM_DIM:(h + 1) * MEM_DIM],
                      q_ref[:, h * MEM_DIM:(h + 1) * MEM_DIM]) for h in range(MEM_HEADS)]
    for h in range(MEM_HEADS):
        v = kv_ref[:, MEM_W + h * MEM_DIM:MEM_W + (h + 1) * MEM_DIM]
        s = logits[h]
        p = jnp.exp2(s - jnp.max(s, axis=0, keepdims=True))
        l = jnp.sum(p, axis=0, keepdims=True)
        o_t = _dot_tn(v, p.astype(BF16)) / l
        o_ref[:, h * MEM_DIM:(h + 1) * MEM_DIM] = o_t.T.astype(BF16)


def _mem_attn(proj, mkv, *, batch, seq, tq=512):
    proj3 = proj.reshape(batch, seq, PROJ_W)
    n_mem = mkv.shape[1]
    out = pl.pallas_call(
        _mem_attn_kernel,
        out_shape=jax.ShapeDtypeStruct((batch, seq, MEM_W), BF16),
        grid=(batch, seq // tq),
        in_specs=[pl.BlockSpec((None, tq, MEM_W), lambda b, i: (b, i, MQ_COL // MEM_W)),
                  pl.BlockSpec((None, n_mem, 2 * MEM_W), lambda b, i: (b, 0, 0))],
        out_specs=pl.BlockSpec((None, tq, MEM_W), lambda b, i: (b, i, 0)),
        compiler_params=_params("parallel", "arbitrary"),
        name="mem_attn",
    )(proj3, mkv)
    return out.reshape(batch * seq, MEM_W)


def _merge_kernel(x_ref, h_ref, yf_ref, yd_ref, ym_ref, wg_ref, bg_ref, wf_ref, wd_ref, wm_ref,
                  wo_ref, gpost_ref, o_ref, acc_ref):
    n = pl.program_id(1)
    tn = wo_ref.shape[0]

    @pl.when(n == 0)
    def _():
        acc_ref[...] = jnp.zeros_like(acc_ref)

    z = _dot(h_ref[...], wg_ref[...]) + bg_ref[...]
    gates = 1.0 / (1.0 + jnp.exp(-z))
    merged = (gates[:, :tn] * _dot(yf_ref[...], wf_ref[...])
              + gates[:, tn:2 * tn] * _dot(yd_ref[...], wd_ref[...])
              + gates[:, 2 * tn:] * _dot(ym_ref[...], wm_ref[...]))
    acc_ref[...] += _dot(merged.astype(BF16), wo_ref[...])

    @pl.when(n == pl.num_programs(1) - 1)
    def _():
        def emit(sl, y):
            o_ref[sl, :] = x_ref[sl, :] + y

        _rms_rows(acc_ref, gpost_ref[...], emit, F32_ROWS)


def _gate_tile_order(d):
    tn = MERGE_TN
    nt = d // tn
    return tuple(((n * 3 + b) * tn, (b * nt + n) * tn, tn) for n in range(nt) for b in range(3))


def _merge(x, h, y_fox, y_diff, y_mem, w_gate, b_gate, w_fox, w_diff, w_mem, w_out, g_post,
           *, tm=512):
    t, d = x.shape
    tn = MERGE_TN
    nt = d // tn
    row = lambda m, n: (m, 0)
    return pl.pallas_call(
        _merge_kernel,
        out_shape=jax.ShapeDtypeStruct((t, d), F32),
        grid=(t // tm, nt),
        in_specs=[
            pl.BlockSpec((tm, d), row),
            pl.BlockSpec((tm, d), row),
            pl.BlockSpec((tm, FOX_W), row),
            pl.BlockSpec((tm, DIFF_V_W), row),
            pl.BlockSpec((tm, MEM_W), row),
            pl.BlockSpec((d, 3 * tn), lambda m, n: (0, n)),
            pl.BlockSpec((1, 3 * tn), lambda m, n: (0, n)),
            pl.BlockSpec((FOX_W, tn), lambda m, n: (0, n)),
            pl.BlockSpec((DIFF_V_W, tn), lambda m, n: (0, n)),
            pl.BlockSpec((MEM_W, tn), lambda m, n: (0, n)),
            pl.BlockSpec((tn, d), lambda m, n: (n, 0)),
            pl.BlockSpec((1, d), lambda m, n: (0, 0)),
        ],
        out_specs=pl.BlockSpec((tm, d), row),
        scratch_shapes=[pltpu.VMEM((tm, d), F32)],
        compiler_params=_params("parallel", "arbitrary"),
        name="merge",
    )(x, h, y_fox, y_diff, y_mem, w_gate, b_gate, w_fox, w_diff, w_mem, w_out, g_post)


def kernel(x, mem, ffn1_pre_g, ffn1_w_gate, ffn1_w_up, ffn1_w_down, ffn1_post_g, mix_pre_g, w_in, fox_f_bias, diff_lambda_q1, diff_lambda_k1, diff_lambda_q2, diff_lambda_k2, diff_head_g, mem_norm_g, w_mem_kv, w_branch_fox, w_branch_diff, w_branch_mem, w_merge_gate, b_merge_gate, w_out, mix_post_g, ffn2_pre_g, ffn2_w_gate, ffn2_w_up, ffn2_w_down, ffn2_post_g):
    batch, seq, d = x.shape
    depth = w_in.shape[0]
    xt = x.reshape(batch * seq, d)
    cos_t, sa_t, sb_t = _rope_tables(seq)

    def row(v):
        return v.reshape(1, -1).astype(F32)

    for l in range(depth):
        bf = lambda w: w[l].astype(BF16)
        w_in_t = w_in[l].T
        xt, (w2_gate, w2_up, w_head, w_tail) = _ffn(
            xt, row(ffn1_pre_g[l]), bf(ffn1_w_gate), bf(ffn1_w_up), bf(ffn1_w_down),
            row(ffn1_post_g[l]), cast=[(ffn2_w_gate, l), (ffn2_w_up, l)],
            cast_t=[(w_in_t, 0, FF_COL + LANES), (w_in_t, FF_COL + FOX_HEADS, w_in.shape[2])])

        small = [w_out, w_branch_fox, w_branch_diff, w_branch_mem]
        proj, ff, h_mix, (wb_out, wb_fox, wb_diff, wb_mem) = _mix_proj(
            xt, row(mix_pre_g[l]), w_head, w_tail, cos_t, sa_t, sb_t, seq=seq,
            cast=[(w, l) for w in small])

        bias = jnp.pad(fox_f_bias[l].astype(F32), (0, LANES - FOX_HEADS)).reshape(1, LANES)
        fp, frow = _fox_gate(ff, bias, batch=batch, seq=seq)
        y_fox, (w2_down,) = _fox_attn(proj, fp, frow, batch=batch, seq=seq,
                                      cast=[(ffn2_w_down, l)])

        lam_init = 0.8 - 0.6 * math.exp(-0.3 * l)
        lamv = jnp.stack([diff_lambda_q1[l], diff_lambda_k1[l], diff_lambda_q2[l],
                          diff_lambda_k2[l]]).astype(F32)
        gate_order = _gate_tile_order(d)
        y_diff, (wb_merge,) = _diff_attn(proj, lamv, row(diff_head_g[l]), batch=batch, seq=seq,
                                         lam_init=lam_init, cast=[(w_merge_gate, l)],
                                         cast_moves=gate_order)
        b_gate = row(b_merge_gate[l])
        b_gate = jnp.concatenate([b_gate[:, src:src + w] for _, src, w in gate_order], axis=1)

        mkv = _mem_kv(mem, row(mem_norm_g[l]), bf(w_mem_kv))
        y_mem = _mem_attn(proj, mkv, batch=batch, seq=seq)

        xt = _merge(xt, h_mix, y_fox, y_diff, y_mem, wb_merge, b_gate, wb_fox, wb_diff, wb_mem,
                    wb_out, row(mix_post_g[l]))

        xt, _ = _ffn(xt, row(ffn2_pre_g[l]), w2_gate, w2_up, w2_down, row(ffn2_post_g[l]))

    return xt.reshape(batch, seq, d)
```

```python
import functools
import math

import jax
import jax.numpy as jnp
from jax import lax
from jax.experimental import pallas as pl
from jax.experimental.pallas import tpu as pltpu

D_MODEL = 2048
CHUNK = 64
EPS = 1e-6
ROPE_THETA = 500000.0

FOX_HEADS = 8
FOX_DIM = 128
FOX_W = FOX_HEADS * FOX_DIM

DIFF_HEADS = 4
DIFF_QK_DIM = 64
DIFF_V_DIM = 2 * DIFF_QK_DIM
DIFF_QK_W = DIFF_HEADS * 2 * DIFF_QK_DIM
DIFF_V_W = DIFF_HEADS * DIFF_V_DIM
ROPE_DIM = DIFF_QK_DIM // 4

MEM_HEADS = 4
MEM_DIM = 128
MEM_W = MEM_HEADS * MEM_DIM

LANES = 128
F32_ROWS = 8
BF16_ROWS = 16
RMS_ROWS_IN_FLIGHT = 128
MERGE_TN = 512
FIRST_ROWS = 1024
PROJ_W = 3 * FOX_W + 2 * DIFF_QK_W + DIFF_V_W + MEM_W
FQ_BLK, FK_BLK, FV_BLK = 0, FOX_HEADS, 2 * FOX_HEADS
DQ_BLK = 3 * FOX_HEADS
DK_BLK = DQ_BLK + DIFF_HEADS
DV_BLK = DK_BLK + DIFF_HEADS
MQ_COL = 3 * FOX_W + 2 * DIFF_QK_W + DIFF_V_W
FF_COL = 3 * FOX_W

VMEM_LIMIT = 56 * 1024 * 1024
BF16 = jnp.bfloat16
F32 = jnp.float32
NEG_INF = float("-inf")
LOG2E = math.log2(math.e)
FOX_QSCALE = FOX_DIM ** -0.5 * LOG2E
DIFF_QSCALE = DIFF_QK_DIM ** -0.5 * LOG2E
MEM_QSCALE = MEM_DIM ** -0.5 * LOG2E


def _dot(a, b):
    return jnp.dot(a, b, preferred_element_type=F32)


def _dot_nt(a, b):
    return lax.dot_general(a, b, (((1,), (1,)), ((), ())), preferred_element_type=F32)


def _rms(x, g):
    return x * lax.rsqrt(jnp.mean(x * x, axis=-1, keepdims=True) + EPS) * g


def _rms_rows(src_ref, g, emit, rows):
    def body(r, carry):
        sl = pl.ds(pl.multiple_of(r * rows, rows), rows)
        emit(sl, _rms(src_ref[sl, :], g))
        return carry

    lax.fori_loop(0, src_ref.shape[0] // rows, body, 0, unroll=RMS_ROWS_IN_FLIGHT // rows)


def _params(*sem):
    return pltpu.CompilerParams(dimension_semantics=sem, vmem_limit_bytes=VMEM_LIMIT)


def _slab_cast_specs(cast, n_steps, step_index):
    in_specs, out_specs, out_shape = [], [], []
    for arr, layer in cast:
        _, rows, cols = arr.shape
        br = rows // n_steps
        assert br * n_steps == rows and br % BF16_ROWS == 0
        in_specs.append(pl.BlockSpec((None, br, cols),
                                     lambda *g, layer=layer: (layer, step_index(*g), 0)))
        out_specs.append(pl.BlockSpec((br, cols), lambda *g: (step_index(*g), 0)))
        out_shape.append(jax.ShapeDtypeStruct((rows, cols), BF16))
    return in_specs, out_specs, out_shape


def _run_slab_casts(src_refs, dst_refs, moves=None):
    for src_ref, dst_ref in zip(src_refs, dst_refs):
        if moves is None:
            dst_ref[...] = src_ref[...].astype(BF16)
        else:
            for dst_lo, src_lo, width in moves:
                dst_ref[:, dst_lo:dst_lo + width] = src_ref[:, src_lo:src_lo + width].astype(BF16)


def _ffn_step(f, last, x_ref, gpre_ref, gpost_ref, o_ref, h_ref, acc_ref, w_gate, w_up, w_down,
              side_jobs):
    @pl.when(f == 0)
    def _():
        def emit(sl, y):
            h_ref[sl, :] = y.astype(BF16)
            acc_ref[sl, :] = jnp.zeros_like(y)

        _rms_rows(x_ref, gpre_ref[...], emit, BF16_ROWS)

    h = h_ref[...]
    side_jobs(0)
    g = _dot(h, w_gate())
    side_jobs(1)
    u = _dot(h, w_up())
    side_jobs(2)
    a = (g * (1.0 / (1.0 + jnp.exp(-g)))) * u
    acc_ref[...] += _dot(a.astype(BF16), w_down())

    @pl.when(f == last)
    def _():
        def emit(sl, y):
            o_ref[sl, :] = x_ref[sl, :] + y

        _rms_rows(acc_ref, 0.5 * gpost_ref[...], emit, F32_ROWS)


def _ffn_first_kernel(x_ref, gpre_ref, wg_ref, wu_ref, wd_ref, gpost_ref,
                      o_ref, wgb_ref, wub_ref, wdb_ref, h_ref):
    for src_ref, dst_ref in ((wg_ref, wgb_ref), (wu_ref, wub_ref), (wd_ref, wdb_ref)):
        dst_ref[...] = src_ref[...].astype(BF16)
    _ffn_step(pl.program_id(0), pl.num_programs(0) - 1, x_ref, gpre_ref, gpost_ref, o_ref, h_ref,
              o_ref, lambda: wgb_ref[...], lambda: wub_ref[...], lambda: wdb_ref[...],
              lambda part: None)


def _ffn_first(x, g_pre, w_gate, w_up, w_down, g_post, layer, *, tm, tf=256):
    _, d = x.shape
    d_ff = w_down.shape[1]
    return pl.pallas_call(
        _ffn_first_kernel,
        out_shape=(jax.ShapeDtypeStruct((tm, d), F32),
                   jax.ShapeDtypeStruct((d, d_ff), BF16), jax.ShapeDtypeStruct((d, d_ff), BF16),
                   jax.ShapeDtypeStruct((d_ff, d), BF16)),
        grid=(d_ff // tf,),
        in_specs=[
            pl.BlockSpec((tm, d), lambda f: (0, 0), pipeline_mode=pl.Buffered(1)),
            pl.BlockSpec((1, d), lambda f: (0, 0)),
            pl.BlockSpec((None, d, tf), lambda f: (layer, 0, f)),
            pl.BlockSpec((None, d, tf), lambda f: (layer, 0, f)),
            pl.BlockSpec((None, tf, d), lambda f: (layer, f, 0)),
            pl.BlockSpec((1, d), lambda f: (0, 0)),
        ],
        out_specs=(pl.BlockSpec((tm, d), lambda f: (0, 0)),
                   pl.BlockSpec((d, tf), lambda f: (0, f)), pl.BlockSpec((d, tf), lambda f: (0, f)),
                   pl.BlockSpec((tf, d), lambda f: (f, 0))),
        scratch_shapes=[pltpu.VMEM((tm, d), BF16)],
        compiler_params=_params("arbitrary"),
        name="ffn_first",
    )(x, g_pre, w_gate, w_up, w_down, g_post)


def _ffn_kernel(*refs, n_flip, skip):
    x_ref, gpre_ref, wg_ref, wu_ref, wd_ref, gpost_ref = refs[:6]
    n_in = 6 + (skip > 0)
    o_ref = refs[n_in + n_flip]
    h_ref, acc_ref = refs[-2:]
    m, f = pl.program_id(0), pl.program_id(1)
    last = pl.num_programs(1) - 1

    jobs = list(zip(refs[n_in:n_in + n_flip], refs[n_in + n_flip + 1:n_in + 2 * n_flip + 1]))

    def side_jobs(part, parts=3):
        for src_ref, dst_ref in jobs[part::parts]:
            dst_ref[...] = src_ref[...].T.astype(BF16)

    def compute():
        _ffn_step(f, last, x_ref, gpre_ref, gpost_ref, o_ref, h_ref, acc_ref,
                  lambda: wg_ref[...], lambda: wu_ref[...], lambda: wd_ref[...], side_jobs)

    if skip:
        pl.when(m >= skip)(compute)

        @pl.when(jnp.logical_and(m < skip, f == last))
        def _():
            o_ref[...] = refs[6][...]
    else:
        compute()


def _ffn(x, g_pre, w_gate, w_up, w_down, g_post, *, first_rows=None, cast_t=(), tm=512, tf=512):
    t, d = x.shape
    d_ff = w_down.shape[0]
    m_tiles, f_steps = t // tm, d_ff // tf
    skip = 0 if first_rows is None else first_rows.shape[0] // tm

    def tile(f, m):
        return f if skip == 0 else jnp.where(m < skip, 0, f)

    in_specs = [
        pl.BlockSpec((tm, d), lambda m, f: (m, 0)),
        pl.BlockSpec((1, d), lambda m, f: (0, 0)),
        pl.BlockSpec((d, tf), lambda m, f: (0, tile(f, m))),
        pl.BlockSpec((d, tf), lambda m, f: (0, tile(f, m))),
        pl.BlockSpec((tf, d), lambda m, f: (tile(f, m), 0)),
        pl.BlockSpec((1, d), lambda m, f: (0, 0)),
    ]
    extra = []
    if first_rows is not None:
        assert first_rows.shape == (skip * tm, d)
        in_specs.append(pl.BlockSpec((tm, d), lambda m, f: (jnp.minimum(m, skip - 1), 0),
                                     pipeline_mode=pl.Buffered(1)))
        extra.append(first_rows)
    out_specs = [pl.BlockSpec((tm, d), lambda m, f: (m, 0))]
    out_shape = [jax.ShapeDtypeStruct((t, d), F32)]
    for arr, lo, hi in cast_t:
        _, cols = arr.shape
        n_blk = (hi - lo) // LANES
        assert n_blk * LANES == hi - lo and n_blk <= (m_tiles - skip) * f_steps

        def block(m, f, n_blk=n_blk):
            return jnp.clip((m - skip) * f_steps + f, 0, n_blk - 1)

        in_specs.append(pl.BlockSpec(
            (pl.Element(LANES), pl.Element(cols)),
            lambda m, f, block=block, lo=lo: (
                pl.multiple_of(lo + LANES * block(m, f), math.gcd(lo, LANES)), 0)))
        out_specs.append(pl.BlockSpec((cols, LANES), lambda m, f, block=block: (0, block(m, f))))
        out_shape.append(jax.ShapeDtypeStruct((cols, hi - lo), BF16))
    outs = pl.pallas_call(
        functools.partial(_ffn_kernel, n_flip=len(cast_t), skip=skip),
        out_shape=out_shape,
        grid=(m_tiles, f_steps),
        in_specs=in_specs,
        out_specs=out_specs,
        scratch_shapes=[pltpu.VMEM((tm, d), BF16), pltpu.VMEM((tm, d), F32)],
        compiler_params=_params("parallel", "arbitrary"),
        name="ffn",
    )(x, g_pre, w_gate, w_up, w_down, g_post, *extra, *[c[0] for c in cast_t])
    return outs[0], outs[1:]


def _proj_kernel(*refs, tn, n_cast):
    x_ref, g_ref, wa_ref, wff_ref, wb_ref, cos_ref, sa_ref, sb_ref = refs[:8]
    o_ref, ff_ref, h_ref = refs[8 + n_cast:11 + n_cast]
    _run_slab_casts(refs[8:8 + n_cast], refs[11 + n_cast:])
    h_ref[...] = _rms(x_ref[...], g_ref[...]).astype(BF16)
    ff_ref[...] = _dot(h_ref[...], wff_ref[...])
    half = ROPE_DIM // 2
    na = wa_ref.shape[1] // tn
    for n in range(na + wb_ref.shape[1] // tn):
        w_tile = (wa_ref[:, n * tn:(n + 1) * tn] if n < na
                  else wb_ref[:, (n - na) * tn:(n - na + 1) * tn])
        y = _dot(h_ref[...], w_tile)
        first_blk = n * tn // LANES
        if first_blk < FK_BLK:
            y = y * FOX_QSCALE
        if first_blk >= MQ_COL // LANES:
            y = y * MEM_QSCALE
        if DQ_BLK <= first_blk < DV_BLK:
            qscale = DIFF_QSCALE if first_blk < DK_BLK else 1.0
            for j in range(tn // LANES):
                blk = y[:, j * LANES:(j + 1) * LANES]
                rot = (blk * cos_ref[...] + pltpu.roll(blk, LANES - half, axis=1) * sa_ref[...]
                       + pltpu.roll(blk, half, axis=1) * sb_ref[...])
                o_ref[:, n * tn + j * LANES:n * tn + (j + 1) * LANES] = (rot * qscale).astype(BF16)
        else:
            o_ref[:, n * tn:(n + 1) * tn] = y.astype(BF16)


def _mix_proj(x, g, w_head, w_tail, cos_t, sa_t, sb_t, *, seq, cast=(), tm=512, tn=512):
    t, d = x.shape
    n_out = FF_COL + w_tail.shape[1]
    assert all((blk * LANES) % tn == 0 for blk in (FK_BLK, DQ_BLK, DK_BLK, DV_BLK))
    assert w_head.shape[1] == FF_COL + LANES and n_out == PROJ_W
    s_tiles = seq // tm
    m_tiles = t // tm
    resident = dict(pipeline_mode=pl.Buffered(1))
    in_specs = [
        pl.BlockSpec((tm, d), lambda m: (m, 0)),
        pl.BlockSpec((1, d), lambda m: (0, 0)),
        pl.BlockSpec((d, FF_COL), lambda m: (0, 0), **resident),
        pl.BlockSpec((d, LANES), lambda m: (0, FF_COL // LANES), **resident),
        pl.BlockSpec((d, w_tail.shape[1]), lambda m: (0, 0), **resident),
        pl.BlockSpec((tm, LANES), lambda m: (m % s_tiles, 0)),
        pl.BlockSpec((tm, LANES), lambda m: (m % s_tiles, 0)),
        pl.BlockSpec((tm, LANES), lambda m: (m % s_tiles, 0)),
    ]
    out_specs = [pl.BlockSpec((tm, n_out), lambda m: (m, 0)),
                 pl.BlockSpec((tm, LANES), lambda m: (m, 0)),
                 pl.BlockSpec((tm, d), lambda m: (m, 0))]
    out_shape = [jax.ShapeDtypeStruct((t, n_out), BF16), jax.ShapeDtypeStruct((t, LANES), F32),
                 jax.ShapeDtypeStruct((t, d), BF16)]
    cast_in, cast_out, cast_shape = _slab_cast_specs(cast, m_tiles, lambda m: m)
    in_specs, out_specs, out_shape = in_specs + cast_in, out_specs + cast_out, out_shape + cast_shape
    outs = pl.pallas_call(
        functools.partial(_proj_kernel, tn=tn, n_cast=len(cast)),
        out_shape=out_shape,
        grid=(m_tiles,),
        in_specs=in_specs,
        out_specs=out_specs,
        compiler_params=_params("parallel"),
        name="mix_proj",
    )(x, g, w_head, w_head, w_tail, cos_t, sa_t, sb_t, *[arr for arr, _ in cast])
    return outs[0], outs[1], outs[2], outs[3:]


def _rope_tables(seq):
    half = ROPE_DIM // 2
    pos = jnp.arange(seq, dtype=F32)
    inv_freq = ROPE_THETA ** (-jnp.arange(0, ROPE_DIM, 2, dtype=F32) / ROPE_DIM)
    ang = pos[:, None] * inv_freq[None, :]
    cos, sin = jnp.cos(ang), jnp.sin(ang)
    ones = jnp.ones((seq, DIFF_QK_DIM - ROPE_DIM), F32)
    zeros_h = jnp.zeros((seq, half), F32)
    zeros_r = jnp.zeros((seq, DIFF_QK_DIM - ROPE_DIM), F32)
    cos_m = jnp.concatenate([cos, cos, ones], axis=1)
    sa_m = jnp.concatenate([-sin, zeros_h, zeros_r], axis=1)
    sb_m = jnp.concatenate([zeros_h, sin, zeros_r], axis=1)
    rep = LANES // DIFF_QK_DIM
    return (jnp.tile(cos_m, (1, rep)), jnp.tile(sa_m, (1, rep)), jnp.tile(sb_m, (1, rep)))


def _fgate_kernel(ff_ref, bias_ref, fp_ref, frow_ref, fcol_ref, *, cb):
    seq = ff_ref.shape[0]
    head_lane = lax.broadcasted_iota(jnp.int32, (cb, LANES), 1) < FOX_HEADS
    z = ff_ref[...] + bias_ref[...]
    lf = jnp.minimum(z, 0.0) - jnp.log1p(jnp.exp(-jnp.abs(z)))
    r = lax.broadcasted_iota(jnp.int32, (cb, cb), 0)
    c = lax.broadcasted_iota(jnp.int32, (cb, cb), 1)
    tri = (r >= c).astype(F32)
    carry = jnp.zeros((1, LANES), F32)
    for i in range(seq // cb):
        cs = jnp.dot(tri, lf[i * cb:(i + 1) * cb], precision=lax.Precision.HIGHEST,
                     preferred_element_type=F32) + carry
        carry = cs[cb - 1:cb, :]
        cs2 = jnp.where(head_lane, cs * LOG2E, 0.0)
        fcol_ref[i * cb:(i + 1) * cb, :] = cs2
        hi = cs2.astype(BF16).astype(F32)
        mid = (cs2 - hi).astype(BF16).astype(F32)
        lo = (cs2 - hi - mid).astype(BF16).astype(F32)
        pieces = hi + pltpu.roll(mid, FOX_HEADS, axis=1) + pltpu.roll(lo, 2 * FOX_HEADS, axis=1)
        fp_ref[i * cb:(i + 1) * cb, :] = pieces.astype(BF16)
    frow_ref[...] = fcol_ref[...].T[:FOX_HEADS]


def _fox_gate(ff, bias, *, batch, seq, cb=256):
    ff = ff.reshape(batch, seq, LANES)
    return pl.pallas_call(
        functools.partial(_fgate_kernel, cb=cb),
        out_shape=(jax.ShapeDtypeStruct((batch, seq, LANES), BF16),
                   jax.ShapeDtypeStruct((batch, FOX_HEADS, seq), F32)),
        grid=(batch,),
        in_specs=[pl.BlockSpec((None, seq, LANES), lambda b: (b, 0, 0)),
                  pl.BlockSpec((1, LANES), lambda b: (0, 0))],
        out_specs=(pl.BlockSpec((None, seq, LANES), lambda b: (b, 0, 0)),
                   pl.BlockSpec((None, FOX_HEADS, seq), lambda b: (b, 0, 0))),
        scratch_shapes=[pltpu.VMEM((seq, LANES), F32)],
        compiler_params=_params("parallel"),
        name="fox_gate",
    )(ff, bias)


def _dot_tn(a, b):
    return lax.dot_general(a, b, (((0,), (0,)), ((), ())), preferred_element_type=F32)


def _flash_attend(i, heads, logits, values, ft2, visible, finish, side_job,
                  sa_ref, sb_ref, m_ref, l_ref, acc_ref):
    def step(h, s, j):
        m = m_ref[h]
        m_new = jnp.maximum(m, jnp.max(s, axis=0, keepdims=True) + ft2[h])
        alpha = jnp.exp2(m - m_new)
        p = jnp.exp2(s - (m_new - ft2[h]))
        m_ref[h] = m_new
        l_ref[h] = alpha * l_ref[h] + jnp.sum(p, axis=0, keepdims=True)
        acc_ref[h] = alpha * acc_ref[h] + _dot_tn(values(h, j), p.astype(BF16))

    for h in heads:
        m_ref[h] = jnp.full(m_ref.shape[1:], NEG_INF, F32)
        l_ref[h] = jnp.zeros(l_ref.shape[1:], F32)
        acc_ref[h] = jnp.zeros(acc_ref.shape[1:], F32)
        sa_ref[h] = logits(h, 0)
    side_job()

    def pair(jj, carry):
        j = 2 * jj
        for h in heads:
            sb_ref[h] = logits(h, j + 1)
            step(h, sa_ref[h], j)
        for h in heads:
            sa_ref[h] = logits(h, j + 2)
            step(h, sb_ref[h], j + 1)
        return carry

    lax.fori_loop(0, i // 2, pair, 0)

    def last(s_ref):
        for h in heads:
            step(h, jnp.where(visible, s_ref[h], NEG_INF), i)
            finish(h, acc_ref[h] / l_ref[h])

    @pl.when(i % 2 == 0)
    def _():
        last(sa_ref)

    @pl.when(i % 2 == 1)
    def _():
        for h in heads:
            sb_ref[h] = logits(h, i)
            step(h, sa_ref[h], i - 1)
        last(sb_ref)


def _flash_scratch(hb, width, tq, cols):
    return [pltpu.VMEM((hb, tq, cols), F32), pltpu.VMEM((hb, tq, cols), F32),
            pltpu.VMEM((hb, 1, cols), F32), pltpu.VMEM((hb, 1, cols), F32),
            pltpu.VMEM((hb, width, cols), F32)]


def _fox_kernel(*refs, tq, hb, n_cast):
    q_ref, k_ref, v_ref, fp_ref, fr_ref = refs[:5]
    o_ref = refs[5 + n_cast]
    scratch = refs[-5:]
    side_job = functools.partial(_run_slab_casts, refs[5:5 + n_cast], refs[6 + n_cast:-5])
    i = pl.program_id(2)
    h0 = pl.program_id(1) * hb
    heads = range(hb)
    ft2 = [fr_ref[h, pl.ds(i, 1), :] for h in heads]

    lane = lax.broadcasted_iota(jnp.int32, (tq, LANES), 1)
    q_ext = []
    for h in heads:
        mine = (lane % FOX_HEADS == h0 + h) & (lane < 3 * FOX_HEADS)
        sel = jnp.where(mine, -1.0, 0.0).astype(BF16)
        q_ext.append(jnp.concatenate([q_ref[:, h * FOX_DIM:(h + 1) * FOX_DIM], sel], axis=1))

    def logits(h, j):
        start = pl.multiple_of(j * tq, tq)
        k_ext = jnp.concatenate([k_ref[pl.ds(start, tq), h * FOX_DIM:(h + 1) * FOX_DIM],
                                 fp_ref[pl.ds(start, tq), :]], axis=1)
        return _dot_nt(k_ext, q_ext[h])

    def values(h, j):
        return v_ref[pl.ds(pl.multiple_of(j * tq, tq), tq), h * FOX_DIM:(h + 1) * FOX_DIM]

    def finish(h, o_t):
        o_ref[:, h * FOX_DIM:(h + 1) * FOX_DIM] = o_t.T.astype(BF16)

    krow = lax.broadcasted_iota(jnp.int32, (tq, tq), 0)
    qcol = lax.broadcasted_iota(jnp.int32, (tq, tq), 1)
    _flash_attend(i, heads, logits, values, ft2, krow <= qcol, finish, side_job, *scratch)


def _fox_attn(proj, fp, frow, *, batch, seq, cast=(), tq=256, hb=8):
    proj3 = proj.reshape(batch, seq, PROJ_W)
    n_q = seq // tq
    frow4 = frow.reshape(batch, FOX_HEADS, n_q, tq)
    w = hb * FOX_DIM
    groups = FOX_HEADS // hb
    assert groups == 1 or not cast
    cast_in, cast_out, cast_shape = _slab_cast_specs(cast, batch * n_q, lambda b, g, i: b * n_q + i)
    outs = pl.pallas_call(
        functools.partial(_fox_kernel, tq=tq, hb=hb, n_cast=len(cast)),
        out_shape=[jax.ShapeDtypeStruct((batch, seq, FOX_W), BF16)] + cast_shape,
        grid=(batch, groups, n_q),
        in_specs=[
            pl.BlockSpec((None, tq, w), lambda b, g, i: (b, i, g)),
            pl.BlockSpec((None, seq, w), lambda b, g, i: (b, 0, groups + g)),
            pl.BlockSpec((None, seq, w), lambda b, g, i: (b, 0, 2 * groups + g)),
            pl.BlockSpec((None, seq, LANES), lambda b, g, i: (b, 0, 0)),
            pl.BlockSpec((None, hb, n_q, tq), lambda b, g, i: (b, g, 0, 0)),
        ] + cast_in,
        out_specs=[pl.BlockSpec((None, tq, w), lambda b, g, i: (b, i, g))] + cast_out,
        scratch_shapes=_flash_scratch(hb, FOX_DIM, tq, tq),
        compiler_params=_params("parallel", "parallel", "arbitrary"),
        name="fox_attn",
    )(proj3, proj3, proj3, fp, frow4, *[arr for arr, _ in cast])
    return outs[0].reshape(batch * seq, FOX_W), outs[1:]


def _diff_kernel(*refs, tq, hb, lam_init, n_cast, cast_moves):
    lamv_ref, g_ref, q_ref, k_ref, v_ref = refs[:5]
    o_ref = refs[5 + n_cast]
    scratch = refs[-5:]
    side_job = functools.partial(_run_slab_casts, refs[5:5 + n_cast], refs[6 + n_cast:-5],
                                 cast_moves)
    i = pl.program_id(2)
    heads = range(hb)
    lv = lamv_ref[...]
    lam = (jnp.exp(jnp.sum(lv[0:1] * lv[1:2], axis=1, keepdims=True))
           - jnp.exp(jnp.sum(lv[2:3] * lv[3:4], axis=1, keepdims=True)) + lam_init)

    lane = lax.broadcasted_iota(jnp.int32, (tq, LANES), 1)
    qs = []
    for h in heads:
        q = q_ref[:, h * LANES:(h + 1) * LANES].astype(F32)
        qs.append(jnp.concatenate([jnp.where(lane < DIFF_QK_DIM, q, 0.0),
                                   jnp.where(lane >= DIFF_QK_DIM, q, 0.0)], axis=0).astype(BF16))
    zero = jnp.zeros((1, 2 * tq), F32)

    def logits(h, j):
        start = pl.multiple_of(j * tq, tq)
        return _dot_nt(k_ref[pl.ds(start, tq), h * LANES:(h + 1) * LANES], qs[h])

    def values(h, j):
        return v_ref[pl.ds(pl.multiple_of(j * tq, tq), tq), h * LANES:(h + 1) * LANES]

    def finish(h, o_t):
        yd = (o_t[:, :tq] - lam * o_t[:, tq:]).T
        o_ref[:, h * LANES:(h + 1) * LANES] = (
            _rms(yd, g_ref[...]) * (1.0 - lam_init)).astype(BF16)

    krow = lax.broadcasted_iota(jnp.int32, (tq, 2 * tq), 0)
    qcol = lax.broadcasted_iota(jnp.int32, (tq, 2 * tq), 1)
    qcol = jnp.where(qcol >= tq, qcol - tq, qcol)
    visible = krow // CHUNK <= qcol // CHUNK
    _flash_attend(i, heads, logits, values, [zero] * hb, visible, finish, side_job, *scratch)


def _diff_attn(proj, lamv, g, *, batch, seq, lam_init, cast=(), cast_moves=None, tq=256, hb=4):
    proj3 = proj.reshape(batch, seq, PROJ_W)
    n_q = seq // tq
    w = hb * LANES
    groups = DIFF_HEADS // hb
    assert groups == 1 or not cast
    dq, dk, dv = (DQ_BLK * LANES) // w, (DK_BLK * LANES) // w, (DV_BLK * LANES) // w
    cast_in, cast_out, cast_shape = _slab_cast_specs(cast, batch * n_q, lambda b, g, i: b * n_q + i)
    outs = pl.pallas_call(
        functools.partial(_diff_kernel, tq=tq, hb=hb, lam_init=lam_init, n_cast=len(cast),
                          cast_moves=cast_moves),
        out_shape=[jax.ShapeDtypeStruct((batch, seq, DIFF_V_W), BF16)] + cast_shape,
        grid=(batch, groups, n_q),
        in_specs=[
            pl.BlockSpec((4, DIFF_QK_DIM), lambda b, g, i: (0, 0)),
            pl.BlockSpec((1, DIFF_V_DIM), lambda b, g, i: (0, 0)),
            pl.BlockSpec((None, tq, w), lambda b, g, i: (b, i, dq + g)),
            pl.BlockSpec((None, seq, w), lambda b, g, i: (b, 0, dk + g)),
            pl.BlockSpec((None, seq, w), lambda b, g, i: (b, 0, dv + g)),
        ] + cast_in,
        out_specs=[pl.BlockSpec((None, tq, w), lambda b, g, i: (b, i, g))] + cast_out,
        scratch_shapes=_flash_scratch(hb, DIFF_V_DIM, tq, 2 * tq),
        compiler_params=_params("parallel", "parallel", "arbitrary"),
        name="diff_attn",
    )(lamv, g, proj3, proj3, proj3, *[arr for arr, _ in cast])
    return outs[0].reshape(batch * seq, DIFF_V_W), outs[1:]


def _mem_kv_kernel(mem_ref, g_ref, w_ref, o_ref):
    o_ref[...] = _dot(_rms(mem_ref[...], g_ref[...]).astype(BF16), w_ref[...]).astype(BF16)


def _mem_kv(mem, g, w):
    batch, n_mem, d = mem.shape
    return pl.pallas_call(
        _mem_kv_kernel,
        out_shape=jax.ShapeDtypeStruct((batch, n_mem, 2 * MEM_W), BF16),
        grid=(batch,),
        in_specs=[pl.BlockSpec((None, n_mem, d), lambda b: (b, 0, 0)),
                  pl.BlockSpec((1, d), lambda b: (0, 0)),
                  pl.BlockSpec((d, 2 * MEM_W), lambda b: (0, 0))],
        out_specs=pl.BlockSpec((None, n_mem, 2 * MEM_W), lambda b: (b, 0, 0)),
        compiler_params=_params("parallel"),
        name="mem_kv",
    )(mem, g, w)


def _mem_attn_kernel(q_ref, kv_ref, o_ref):
    logits = [_dot_nt(kv_ref[:, h * MEM_DIM:(h + 1) * MEM_DIM],
                      q_ref[:, h * MEM_DIM:(h + 1) * MEM_DIM]) for h in range(MEM_HEADS)]
    for h in range(MEM_HEADS):
        v = kv_ref[:, MEM_W + h * MEM_DIM:MEM_W + (h + 1) * MEM_DIM]
        s = logits[h]
        p = jnp.exp2(s - jnp.max(s, axis=0, keepdims=True))
        l = jnp.sum(p, axis=0, keepdims=True)
        o_t = _dot_tn(v, p.astype(BF16)) / l
        o_ref[:, h * MEM_DIM:(h + 1) * MEM_DIM] = o_t.T.astype(BF16)


def _mem_attn(proj, mkv, *, batch, seq, tq=512):
    proj3 = proj.reshape(batch, seq, PROJ_W)
    n_mem = mkv.shape[1]
    out = pl.pallas_call(
        _mem_attn_kernel,
        out_shape=jax.ShapeDtypeStruct((batch, seq, MEM_W), BF16),
        grid=(batch, seq // tq),
        in_specs=[pl.BlockSpec((None, tq, MEM_W), lambda b, i: (b, i, MQ_COL // MEM_W)),
                  pl.BlockSpec((None, n_mem, 2 * MEM_W), lambda b, i: (b, 0, 0))],
        out_specs=pl.BlockSpec((None, tq, MEM_W), lambda b, i: (b, i, 0)),
        compiler_params=_params("parallel", "arbitrary"),
        name="mem_attn",
    )(proj3, mkv)
    return out.reshape(batch * seq, MEM_W)


def _merge_kernel(x_ref, h_ref, yf_ref, yd_ref, ym_ref, wg_ref, bg_ref, wf_ref, wd_ref, wm_ref,
                  wo_ref, gpost_ref, o_ref, acc_ref):
    n = pl.program_id(1)
    tn = wo_ref.shape[0]

    @pl.when(n == 0)
    def _():
        acc_ref[...] = jnp.zeros_like(acc_ref)

    z = _dot(h_ref[...], wg_ref[...]) + bg_ref[...]
    gates = 1.0 / (1.0 + jnp.exp(-z))
    merged = (gates[:, :tn] * _dot(yf_ref[...], wf_ref[...])
              + gates[:, tn:2 * tn] * _dot(yd_ref[...], wd_ref[...])
              + gates[:, 2 * tn:] * _dot(ym_ref[...], wm_ref[...]))
    acc_ref[...] += _dot(merged.astype(BF16), wo_ref[...])

    @pl.when(n == pl.num_programs(1) - 1)
    def _():
        def emit(sl, y):
            o_ref[sl, :] = x_ref[sl, :] + y

        _rms_rows(acc_ref, gpost_ref[...], emit, F32_ROWS)


def _gate_tile_order(d):
    tn = MERGE_TN
    nt = d // tn
    return tuple(((n * 3 + b) * tn, (b * nt + n) * tn, tn) for n in range(nt) for b in range(3))


def _merge(x, h, y_fox, y_diff, y_mem, w_gate, b_gate, w_fox, w_diff, w_mem, w_out, g_post,
           *, tm=512):
    t, d = x.shape
    tn = MERGE_TN
    nt = d // tn
    row = lambda m, n: (m, 0)
    return pl.pallas_call(
        _merge_kernel,
        out_shape=jax.ShapeDtypeStruct((t, d), F32),
        grid=(t // tm, nt),
        in_specs=[
            pl.BlockSpec((tm, d), row),
            pl.BlockSpec((tm, d), row),
            pl.BlockSpec((tm, FOX_W), row),
            pl.BlockSpec((tm, DIFF_V_W), row),
            pl.BlockSpec((tm, MEM_W), row),
            pl.BlockSpec((d, 3 * tn), lambda m, n: (0, n)),
            pl.BlockSpec((1, 3 * tn), lambda m, n: (0, n)),
            pl.BlockSpec((FOX_W, tn), lambda m, n: (0, n)),
            pl.BlockSpec((DIFF_V_W, tn), lambda m, n: (0, n)),
            pl.BlockSpec((MEM_W, tn), lambda m, n: (0, n)),
            pl.BlockSpec((tn, d), lambda m, n: (n, 0)),
            pl.BlockSpec((1, d), lambda m, n: (0, 0)),
        ],
        out_specs=pl.BlockSpec((tm, d), row),
        scratch_shapes=[pltpu.VMEM((tm, d), F32)],
        compiler_params=_params("parallel", "arbitrary"),
        name="merge",
    )(x, h, y_fox, y_diff, y_mem, w_gate, b_gate, w_fox, w_diff, w_mem, w_out, g_post)


def kernel(x, mem, ffn1_pre_g, ffn1_w_gate, ffn1_w_up, ffn1_w_down, ffn1_post_g, mix_pre_g, w_in, fox_f_bias, diff_lambda_q1, diff_lambda_k1, diff_lambda_q2, diff_lambda_k2, diff_head_g, mem_norm_g, w_mem_kv, w_branch_fox, w_branch_diff, w_branch_mem, w_merge_gate, b_merge_gate, w_out, mix_post_g, ffn2_pre_g, ffn2_w_gate, ffn2_w_up, ffn2_w_down, ffn2_post_g):
    batch, seq, d = x.shape
    depth = w_in.shape[0]
    xt = x.reshape(batch * seq, d)
    cos_t, sa_t, sb_t = _rope_tables(seq)

    def row(v):
        return v.reshape(1, -1).astype(F32)

    for l in range(depth):
        first, w1_gate, w1_up, w1_down = _ffn_first(
            xt, row(ffn1_pre_g[l]), ffn1_w_gate, ffn1_w_up, ffn1_w_down, row(ffn1_post_g[l]), l,
            tm=FIRST_ROWS)
        w_in_t = w_in[l].T
        xt, (w_head, w_tail) = _ffn(
            xt, row(ffn1_pre_g[l]), w1_gate, w1_up, w1_down, row(ffn1_post_g[l]), first_rows=first,
            cast_t=[(w_in_t, 0, FF_COL + LANES), (w_in_t, FF_COL + FOX_HEADS, w_in.shape[2])])

        small = [w_out, w_branch_fox, w_branch_diff, w_branch_mem]
        proj, ff, h_mix, (wb_out, wb_fox, wb_diff, wb_mem) = _mix_proj(
            xt, row(mix_pre_g[l]), w_head, w_tail, cos_t, sa_t, sb_t, seq=seq,
            cast=[(w, l) for w in small])

        bias = jnp.pad(fox_f_bias[l].astype(F32), (0, LANES - FOX_HEADS)).reshape(1, LANES)
        fp, frow = _fox_gate(ff, bias, batch=batch, seq=seq)
        y_fox, (w2_gate, w2_up, w2_down) = _fox_attn(
            proj, fp, frow, batch=batch, seq=seq,
            cast=[(w, l) for w in (ffn2_w_gate, ffn2_w_up, ffn2_w_down)])

        lam_init = 0.8 - 0.6 * math.exp(-0.3 * l)
        lamv = jnp.stack([diff_lambda_q1[l], diff_lambda_k1[l], diff_lambda_q2[l],
                          diff_lambda_k2[l]]).astype(F32)
        gate_order = _gate_tile_order(d)
        y_diff, (wb_merge,) = _diff_attn(proj, lamv, row(diff_head_g[l]), batch=batch, seq=seq,
                                         lam_init=lam_init, cast=[(w_merge_gate, l)],
                                         cast_moves=gate_order)
        b_gate = row(b_merge_gate[l])
        b_gate = jnp.concatenate([b_gate[:, src:src + w] for _, src, w in gate_order], axis=1)

        mkv = _mem_kv(mem, row(mem_norm_g[l]), w_mem_kv[l].astype(BF16))
        y_mem = _mem_attn(proj, mkv, batch=batch, seq=seq)

        xt = _merge(xt, h_mix, y_fox, y_diff, y_mem, wb_merge, b_gate, wb_fox, wb_diff, wb_mem,
                    wb_out, row(mix_post_g[l]))

        xt, _ = _ffn(xt, row(ffn2_pre_g[l]), w2_gate, w2_up, w2_down, row(ffn2_post_g[l]))

    return xt.reshape(batch, seq, d)
```

```python
import functools
import math

import jax
import jax.numpy as jnp
from jax import lax
from jax.experimental import pallas as pl
from jax.experimental.pallas import tpu as pltpu

D_MODEL = 2048
CHUNK = 64
EPS = 1e-6
ROPE_THETA = 500000.0

FOX_HEADS = 8
FOX_DIM = 128
FOX_W = FOX_HEADS * FOX_DIM

DIFF_HEADS = 4
DIFF_QK_DIM = 64
DIFF_V_DIM = 2 * DIFF_QK_DIM
DIFF_QK_W = DIFF_HEADS * 2 * DIFF_QK_DIM
DIFF_V_W = DIFF_HEADS * DIFF_V_DIM
ROPE_DIM = DIFF_QK_DIM // 4

MEM_HEADS = 4
MEM_DIM = 128
MEM_W = MEM_HEADS * MEM_DIM

LANES = 128
F32_ROWS = 8
BF16_ROWS = 16
RMS_ROWS_IN_FLIGHT = 256
MERGE_TN = 512
FIRST_ROWS = 1024
IN_PLACE_ROWS = 64
PROJ_W = 3 * FOX_W + 2 * DIFF_QK_W + DIFF_V_W + MEM_W
FQ_BLK, FK_BLK, FV_BLK = 0, FOX_HEADS, 2 * FOX_HEADS
DQ_BLK = 3 * FOX_HEADS
DK_BLK = DQ_BLK + DIFF_HEADS
DV_BLK = DK_BLK + DIFF_HEADS
MQ_COL = 3 * FOX_W + 2 * DIFF_QK_W + DIFF_V_W
FF_COL = 3 * FOX_W

VMEM_LIMIT = 56 * 1024 * 1024
BF16 = jnp.bfloat16
F32 = jnp.float32
NEG_INF = float("-inf")
LOG2E = math.log2(math.e)
FOX_QSCALE = FOX_DIM ** -0.5 * LOG2E
DIFF_QSCALE = DIFF_QK_DIM ** -0.5 * LOG2E
MEM_QSCALE = MEM_DIM ** -0.5 * LOG2E


def _dot(a, b):
    return jnp.dot(a, b, preferred_element_type=F32)


def _dot_nt(a, b):
    return lax.dot_general(a, b, (((1,), (1,)), ((), ())), preferred_element_type=F32)


def _rms(x, g):
    return x * lax.rsqrt(jnp.mean(x * x, axis=-1, keepdims=True) + EPS) * g


def _rms_rows(src_ref, g, emit, rows):
    def body(r, carry):
        sl = pl.ds(pl.multiple_of(r * rows, rows), rows)
        emit(sl, _rms(src_ref[sl, :], g))
        return carry

    lax.fori_loop(0, src_ref.shape[0] // rows, body, 0, unroll=RMS_ROWS_IN_FLIGHT // rows)


def _params(*sem):
    return pltpu.CompilerParams(dimension_semantics=sem, vmem_limit_bytes=VMEM_LIMIT)


def _slab_cast_specs(cast, n_steps, step_index):
    in_specs, out_specs, out_shape = [], [], []
    for arr, layer in cast:
        _, rows, cols = arr.shape
        br = rows // n_steps
        assert br * n_steps == rows and br % BF16_ROWS == 0
        in_specs.append(pl.BlockSpec((None, br, cols),
                                     lambda *g, layer=layer: (layer, step_index(*g), 0)))
        out_specs.append(pl.BlockSpec((br, cols), lambda *g: (step_index(*g), 0)))
        out_shape.append(jax.ShapeDtypeStruct((rows, cols), BF16))
    return in_specs, out_specs, out_shape


def _run_slab_casts(src_refs, dst_refs, moves=None):
    for src_ref, dst_ref in zip(src_refs, dst_refs):
        if moves is None:
            dst_ref[...] = src_ref[...].astype(BF16)
        else:
            for dst_lo, src_lo, width in moves:
                dst_ref[:, dst_lo:dst_lo + width] = src_ref[:, src_lo:src_lo + width].astype(BF16)


def _ffn_step(f, last, x_ref, gpre_ref, gpost_ref, o_ref, h_ref, acc_ref, w_gate, w_up, w_down,
              side_jobs):
    @pl.when(f == 0)
    def _():
        def emit(sl, y):
            h_ref[sl, :] = y.astype(BF16)
            acc_ref[sl, :] = jnp.zeros_like(y)

        _rms_rows(x_ref, gpre_ref[...], emit, BF16_ROWS)

    h = h_ref[...]
    side_jobs(0)
    g = _dot(h, w_gate())
    side_jobs(1)
    u = _dot(h, w_up())
    side_jobs(2)
    a = (g * (1.0 / (1.0 + jnp.exp(-g)))) * u
    acc_ref[...] += _dot(a.astype(BF16), w_down())

    @pl.when(f == last)
    def _():
        def emit(sl, y):
            o_ref[sl, :] = x_ref[sl, :] + y

        g_half = 0.5 * gpost_ref[...]
        if acc_ref is o_ref:
            for r in range(0, o_ref.shape[0], IN_PLACE_ROWS):
                sl = slice(r, r + IN_PLACE_ROWS)
                emit(sl, _rms(acc_ref[sl, :], g_half))
        else:
            _rms_rows(acc_ref, g_half, emit, F32_ROWS)


def _ffn_first_kernel(x_ref, gpre_ref, wg_ref, wu_ref, wd_ref, gpost_ref,
                      o_ref, wgb_ref, wub_ref, wdb_ref, h_ref):
    for src_ref, dst_ref in ((wg_ref, wgb_ref), (wu_ref, wub_ref), (wd_ref, wdb_ref)):
        dst_ref[...] = src_ref[...].astype(BF16)
    _ffn_step(pl.program_id(0), pl.num_programs(0) - 1, x_ref, gpre_ref, gpost_ref, o_ref, h_ref,
              o_ref, lambda: wgb_ref[...], lambda: wub_ref[...], lambda: wdb_ref[...],
              lambda part: None)


def _ffn_first(x, g_pre, w_gate, w_up, w_down, g_post, layer, *, tm, tf=256):
    _, d = x.shape
    d_ff = w_down.shape[1]
    return pl.pallas_call(
        _ffn_first_kernel,
        out_shape=(jax.ShapeDtypeStruct((tm, d), F32),
                   jax.ShapeDtypeStruct((d, d_ff), BF16), jax.ShapeDtypeStruct((d, d_ff), BF16),
                   jax.ShapeDtypeStruct((d_ff, d), BF16)),
        grid=(d_ff // tf,),
        in_specs=[
            pl.BlockSpec((tm, d), lambda f: (0, 0), pipeline_mode=pl.Buffered(1)),
            pl.BlockSpec((1, d), lambda f: (0, 0)),
            pl.BlockSpec((None, d, tf), lambda f: (layer, 0, f)),
            pl.BlockSpec((None, d, tf), lambda f: (layer, 0, f)),
            pl.BlockSpec((None, tf, d), lambda f: (layer, f, 0)),
            pl.BlockSpec((1, d), lambda f: (0, 0)),
        ],
        out_specs=(pl.BlockSpec((tm, d), lambda f: (0, 0)),
                   pl.BlockSpec((d, tf), lambda f: (0, f)), pl.BlockSpec((d, tf), lambda f: (0, f)),
                   pl.BlockSpec((tf, d), lambda f: (f, 0))),
        scratch_shapes=[pltpu.VMEM((tm, d), BF16)],
        compiler_params=_params("arbitrary"),
        name="ffn_first",
    )(x, g_pre, w_gate, w_up, w_down, g_post)


def _ffn_kernel(*refs, n_flip, skip):
    x_ref, gpre_ref, wg_ref, wu_ref, wd_ref, gpost_ref = refs[:6]
    n_in = 6 + (skip > 0)
    o_ref = refs[n_in + n_flip]
    h_ref, acc_ref = refs[-2:]
    m, f = pl.program_id(0), pl.program_id(1)
    last = pl.num_programs(1) - 1

    jobs = list(zip(refs[n_in:n_in + n_flip], refs[n_in + n_flip + 1:n_in + 2 * n_flip + 1]))

    def side_jobs(part, parts=3):
        for src_ref, dst_ref in jobs[part::parts]:
            dst_ref[...] = src_ref[...].T.astype(BF16)

    def compute():
        _ffn_step(f, last, x_ref, gpre_ref, gpost_ref, o_ref, h_ref, acc_ref,
                  lambda: wg_ref[...], lambda: wu_ref[...], lambda: wd_ref[...], side_jobs)

    if skip:
        pl.when(m >= skip)(compute)

        @pl.when(jnp.logical_and(m < skip, f == last))
        def _():
            o_ref[...] = refs[6][...]
    else:
        compute()


def _ffn(x, g_pre, w_gate, w_up, w_down, g_post, *, first_rows=None, cast_t=(), tm=512, tf=512):
    t, d = x.shape
    d_ff = w_down.shape[0]
    m_tiles, f_steps = t // tm, d_ff // tf
    skip = 0 if first_rows is None else first_rows.shape[0] // tm

    def tile(f, m):
        return f if skip == 0 else jnp.where(m < skip, 0, f)

    in_specs = [
        pl.BlockSpec((tm, d), lambda m, f: (m, 0)),
        pl.BlockSpec((1, d), lambda m, f: (0, 0)),
        pl.BlockSpec((d, tf), lambda m, f: (0, tile(f, m))),
        pl.BlockSpec((d, tf), lambda m, f: (0, tile(f, m))),
        pl.BlockSpec((tf, d), lambda m, f: (tile(f, m), 0)),
        pl.BlockSpec((1, d), lambda m, f: (0, 0)),
    ]
    extra = []
    if first_rows is not None:
        assert first_rows.shape == (skip * tm, d)
        in_specs.append(pl.BlockSpec((tm, d), lambda m, f: (jnp.minimum(m, skip - 1), 0),
                                     pipeline_mode=pl.Buffered(1)))
        extra.append(first_rows)
    out_specs = [pl.BlockSpec((tm, d), lambda m, f: (m, 0))]
    out_shape = [jax.ShapeDtypeStruct((t, d), F32)]
    for arr, lo, hi in cast_t:
        _, cols = arr.shape
        n_blk = (hi - lo) // LANES
        assert n_blk * LANES == hi - lo and n_blk <= (m_tiles - skip) * f_steps

        def block(m, f, n_blk=n_blk):
            return jnp.clip((m - skip) * f_steps + f, 0, n_blk - 1)

        in_specs.append(pl.BlockSpec(
            (pl.Element(LANES), pl.Element(cols)),
            lambda m, f, block=block, lo=lo: (
                pl.multiple_of(lo + LANES * block(m, f), math.gcd(lo, LANES)), 0)))
        out_specs.append(pl.BlockSpec((cols, LANES), lambda m, f, block=block: (0, block(m, f))))
        out_shape.append(jax.ShapeDtypeStruct((cols, hi - lo), BF16))
    outs = pl.pallas_call(
        functools.partial(_ffn_kernel, n_flip=len(cast_t), skip=skip),
        out_shape=out_shape,
        grid=(m_tiles, f_steps),
        in_specs=in_specs,
        out_specs=out_specs,
        scratch_shapes=[pltpu.VMEM((tm, d), BF16), pltpu.VMEM((tm, d), F32)],
        compiler_params=_params("parallel", "arbitrary"),
        name="ffn",
    )(x, g_pre, w_gate, w_up, w_down, g_post, *extra, *[c[0] for c in cast_t])
    return outs[0], outs[1:]


def _proj_kernel(*refs, tn, n_cast):
    x_ref, g_ref, wa_ref, wff_ref, wb_ref, cos_ref, sa_ref, sb_ref = refs[:8]
    o_ref, ff_ref, h_ref = refs[8 + n_cast:11 + n_cast]
    _run_slab_casts(refs[8:8 + n_cast], refs[11 + n_cast:])
    h_ref[...] = _rms(x_ref[...], g_ref[...]).astype(BF16)
    ff_ref[...] = _dot(h_ref[...], wff_ref[...])
    half = ROPE_DIM // 2
    na = wa_ref.shape[1] // tn
    for n in range(na + wb_ref.shape[1] // tn):
        w_tile = (wa_ref[:, n * tn:(n + 1) * tn] if n < na
                  else wb_ref[:, (n - na) * tn:(n - na + 1) * tn])
        y = _dot(h_ref[...], w_tile)
        first_blk = n * tn // LANES
        if first_blk < FK_BLK:
            y = y * FOX_QSCALE
        if first_blk >= MQ_COL // LANES:
            y = y * MEM_QSCALE
        if DQ_BLK <= first_blk < DV_BLK:
            qscale = DIFF_QSCALE if first_blk < DK_BLK else 1.0
            for j in range(tn // LANES):
                blk = y[:, j * LANES:(j + 1) * LANES]
                rot = (blk * cos_ref[...] + pltpu.roll(blk, LANES - half, axis=1) * sa_ref[...]
                       + pltpu.roll(blk, half, axis=1) * sb_ref[...])
                o_ref[:, n * tn + j * LANES:n * tn + (j + 1) * LANES] = (rot * qscale).astype(BF16)
        else:
            o_ref[:, n * tn:(n + 1) * tn] = y.astype(BF16)


def _mix_proj(x, g, w_head, w_tail, cos_t, sa_t, sb_t, *, seq, cast=(), tm=512, tn=512):
    t, d = x.shape
    n_out = FF_COL + w_tail.shape[1]
    assert all((blk * LANES) % tn == 0 for blk in (FK_BLK, DQ_BLK, DK_BLK, DV_BLK))
    assert w_head.shape[1] == FF_COL + LANES and n_out == PROJ_W
    s_tiles = seq // tm
    m_tiles = t // tm
    resident = dict(pipeline_mode=pl.Buffered(1))
    in_specs = [
        pl.BlockSpec((tm, d), lambda m: (m, 0)),
        pl.BlockSpec((1, d), lambda m: (0, 0)),
        pl.BlockSpec((d, FF_COL), lambda m: (0, 0), **resident),
        pl.BlockSpec((d, LANES), lambda m: (0, FF_COL // LANES), **resident),
        pl.BlockSpec((d, w_tail.shape[1]), lambda m: (0, 0), **resident),
        pl.BlockSpec((tm, LANES), lambda m: (m % s_tiles, 0)),
        pl.BlockSpec((tm, LANES), lambda m: (m % s_tiles, 0)),
        pl.BlockSpec((tm, LANES), lambda m: (m % s_tiles, 0)),
    ]
    out_specs = [pl.BlockSpec((tm, n_out), lambda m: (m, 0)),
                 pl.BlockSpec((tm, LANES), lambda m: (m, 0)),
                 pl.BlockSpec((tm, d), lambda m: (m, 0))]
    out_shape = [jax.ShapeDtypeStruct((t, n_out), BF16), jax.ShapeDtypeStruct((t, LANES), F32),
                 jax.ShapeDtypeStruct((t, d), BF16)]
    cast_in, cast_out, cast_shape = _slab_cast_specs(cast, m_tiles, lambda m: m)
    in_specs, out_specs, out_shape = in_specs + cast_in, out_specs + cast_out, out_shape + cast_shape
    outs = pl.pallas_call(
        functools.partial(_proj_kernel, tn=tn, n_cast=len(cast)),
        out_shape=out_shape,
        grid=(m_tiles,),
        in_specs=in_specs,
        out_specs=out_specs,
        compiler_params=_params("parallel"),
        name="mix_proj",
    )(x, g, w_head, w_head, w_tail, cos_t, sa_t, sb_t, *[arr for arr, _ in cast])
    return outs[0], outs[1], outs[2], outs[3:]


def _rope_tables(seq):
    half = ROPE_DIM // 2
    pos = jnp.arange(seq, dtype=F32)
    inv_freq = ROPE_THETA ** (-jnp.arange(0, ROPE_DIM, 2, dtype=F32) / ROPE_DIM)
    ang = pos[:, None] * inv_freq[None, :]
    cos, sin = jnp.cos(ang), jnp.sin(ang)
    ones = jnp.ones((seq, DIFF_QK_DIM - ROPE_DIM), F32)
    zeros_h = jnp.zeros((seq, half), F32)
    zeros_r = jnp.zeros((seq, DIFF_QK_DIM - ROPE_DIM), F32)
    cos_m = jnp.concatenate([cos, cos, ones], axis=1)
    sa_m = jnp.concatenate([-sin, zeros_h, zeros_r], axis=1)
    sb_m = jnp.concatenate([zeros_h, sin, zeros_r], axis=1)
    rep = LANES // DIFF_QK_DIM
    return (jnp.tile(cos_m, (1, rep)), jnp.tile(sa_m, (1, rep)), jnp.tile(sb_m, (1, rep)))


def _fgate_kernel(ff_ref, bias_ref, fp_ref, frow_ref, fcol_ref, *, cb):
    seq = ff_ref.shape[0]
    head_lane = lax.broadcasted_iota(jnp.int32, (cb, LANES), 1) < FOX_HEADS
    z = ff_ref[...] + bias_ref[...]
    lf = jnp.minimum(z, 0.0) - jnp.log1p(jnp.exp(-jnp.abs(z)))
    r = lax.broadcasted_iota(jnp.int32, (cb, cb), 0)
    c = lax.broadcasted_iota(jnp.int32, (cb, cb), 1)
    tri = (r >= c).astype(F32)
    carry = jnp.zeros((1, LANES), F32)
    for i in range(seq // cb):
        cs = jnp.dot(tri, lf[i * cb:(i + 1) * cb], precision=lax.Precision.HIGHEST,
                     preferred_element_type=F32) + carry
        carry = cs[cb - 1:cb, :]
        cs2 = jnp.where(head_lane, cs * LOG2E, 0.0)
        fcol_ref[i * cb:(i + 1) * cb, :] = cs2
        hi = cs2.astype(BF16).astype(F32)
        mid = (cs2 - hi).astype(BF16).astype(F32)
        lo = (cs2 - hi - mid).astype(BF16).astype(F32)
        pieces = hi + pltpu.roll(mid, FOX_HEADS, axis=1) + pltpu.roll(lo, 2 * FOX_HEADS, axis=1)
        fp_ref[i * cb:(i + 1) * cb, :] = pieces.astype(BF16)
    frow_ref[...] = fcol_ref[...].T[:FOX_HEADS]


def _fox_gate(ff, bias, *, batch, seq, cb=256):
    ff = ff.reshape(batch, seq, LANES)
    return pl.pallas_call(
        functools.partial(_fgate_kernel, cb=cb),
        out_shape=(jax.ShapeDtypeStruct((batch, seq, LANES), BF16),
                   jax.ShapeDtypeStruct((batch, FOX_HEADS, seq), F32)),
        grid=(batch,),
        in_specs=[pl.BlockSpec((None, seq, LANES), lambda b: (b, 0, 0)),
                  pl.BlockSpec((1, LANES), lambda b: (0, 0))],
        out_specs=(pl.BlockSpec((None, seq, LANES), lambda b: (b, 0, 0)),
                   pl.BlockSpec((None, FOX_HEADS, seq), lambda b: (b, 0, 0))),
        scratch_shapes=[pltpu.VMEM((seq, LANES), F32)],
        compiler_params=_params("parallel"),
        name="fox_gate",
    )(ff, bias)


def _dot_tn(a, b):
    return lax.dot_general(a, b, (((0,), (0,)), ((), ())), preferred_element_type=F32)


def _flash_attend(i, heads, logits, values, ft2, visible, finish, side_job,
                  sa_ref, sb_ref, m_ref, l_ref, acc_ref):
    def step(h, s, j):
        m = m_ref[h]
        m_new = jnp.maximum(m, jnp.max(s, axis=0, keepdims=True) + ft2[h])
        alpha = jnp.exp2(m - m_new)
        p = jnp.exp2(s - (m_new - ft2[h]))
        m_ref[h] = m_new
        l_ref[h] = alpha * l_ref[h] + jnp.sum(p, axis=0, keepdims=True)
        acc_ref[h] = alpha * acc_ref[h] + _dot_tn(values(h, j), p.astype(BF16))

    for h in heads:
        m_ref[h] = jnp.full(m_ref.shape[1:], NEG_INF, F32)
        l_ref[h] = jnp.zeros(l_ref.shape[1:], F32)
        acc_ref[h] = jnp.zeros(acc_ref.shape[1:], F32)
        sa_ref[h] = logits(h, 0)
    side_job()

    def pair(jj, carry):
        j = 2 * jj
        for h in heads:
            sb_ref[h] = logits(h, j + 1)
            step(h, sa_ref[h], j)
        for h in heads:
            sa_ref[h] = logits(h, j + 2)
            step(h, sb_ref[h], j + 1)
        return carry

    lax.fori_loop(0, i // 2, pair, 0)

    def last(s_ref):
        for h in heads:
            step(h, jnp.where(visible, s_ref[h], NEG_INF), i)
            finish(h, acc_ref[h] / l_ref[h])

    @pl.when(i % 2 == 0)
    def _():
        last(sa_ref)

    @pl.when(i % 2 == 1)
    def _():
        for h in heads:
            sb_ref[h] = logits(h, i)
            step(h, sa_ref[h], i - 1)
        last(sb_ref)


def _flash_scratch(hb, width, tq, cols):
    return [pltpu.VMEM((hb, tq, cols), F32), pltpu.VMEM((hb, tq, cols), F32),
            pltpu.VMEM((hb, 1, cols), F32), pltpu.VMEM((hb, 1, cols), F32),
            pltpu.VMEM((hb, width, cols), F32)]


def _fox_kernel(*refs, tq, hb, n_cast):
    q_ref, k_ref, v_ref, fp_ref, fr_ref = refs[:5]
    o_ref = refs[5 + n_cast]
    scratch = refs[-5:]
    side_job = functools.partial(_run_slab_casts, refs[5:5 + n_cast], refs[6 + n_cast:-5])
    i = pl.program_id(2)
    h0 = pl.program_id(1) * hb
    heads = range(hb)
    ft2 = [fr_ref[h, pl.ds(i, 1), :] for h in heads]

    lane = lax.broadcasted_iota(jnp.int32, (tq, LANES), 1)
    q_ext = []
    for h in heads:
        mine = (lane % FOX_HEADS == h0 + h) & (lane < 3 * FOX_HEADS)
        sel = jnp.where(mine, -1.0, 0.0).astype(BF16)
        q_ext.append(jnp.concatenate([q_ref[:, h * FOX_DIM:(h + 1) * FOX_DIM], sel], axis=1))

    def logits(h, j):
        start = pl.multiple_of(j * tq, tq)
        k_ext = jnp.concatenate([k_ref[pl.ds(start, tq), h * FOX_DIM:(h + 1) * FOX_DIM],
                                 fp_ref[pl.ds(start, tq), :]], axis=1)
        return _dot_nt(k_ext, q_ext[h])

    def values(h, j):
        return v_ref[pl.ds(pl.multiple_of(j * tq, tq), tq), h * FOX_DIM:(h + 1) * FOX_DIM]

    def finish(h, o_t):
        o_ref[:, h * FOX_DIM:(h + 1) * FOX_DIM] = o_t.T.astype(BF16)

    krow = lax.broadcasted_iota(jnp.int32, (tq, tq), 0)
    qcol = lax.broadcasted_iota(jnp.int32, (tq, tq), 1)
    _flash_attend(i, heads, logits, values, ft2, krow <= qcol, finish, side_job, *scratch)


def _fox_attn(proj, fp, frow, *, batch, seq, cast=(), tq=256, hb=8):
    proj3 = proj.reshape(batch, seq, PROJ_W)
    n_q = seq // tq
    frow4 = frow.reshape(batch, FOX_HEADS, n_q, tq)
    w = hb * FOX_DIM
    groups = FOX_HEADS // hb
    assert groups == 1 or not cast
    cast_in, cast_out, cast_shape = _slab_cast_specs(cast, batch * n_q, lambda b, g, i: b * n_q + i)
    outs = pl.pallas_call(
        functools.partial(_fox_kernel, tq=tq, hb=hb, n_cast=len(cast)),
        out_shape=[jax.ShapeDtypeStruct((batch, seq, FOX_W), BF16)] + cast_shape,
        grid=(batch, groups, n_q),
        in_specs=[
            pl.BlockSpec((None, tq, w), lambda b, g, i: (b, i, g)),
            pl.BlockSpec((None, seq, w), lambda b, g, i: (b, 0, groups + g)),
            pl.BlockSpec((None, seq, w), lambda b, g, i: (b, 0, 2 * groups + g)),
            pl.BlockSpec((None, seq, LANES), lambda b, g, i: (b, 0, 0)),
            pl.BlockSpec((None, hb, n_q, tq), lambda b, g, i: (b, g, 0, 0)),
        ] + cast_in,
        out_specs=[pl.BlockSpec((None, tq, w), lambda b, g, i: (b, i, g))] + cast_out,
        scratch_shapes=_flash_scratch(hb, FOX_DIM, tq, tq),
        compiler_params=_params("parallel", "parallel", "arbitrary"),
        name="fox_attn",
    )(proj3, proj3, proj3, fp, frow4, *[arr for arr, _ in cast])
    return outs[0].reshape(batch * seq, FOX_W), outs[1:]


def _diff_kernel(*refs, tq, hb, lam_init, n_cast, cast_moves):
    lamv_ref, g_ref, q_ref, k_ref, v_ref = refs[:5]
    o_ref = refs[5 + n_cast]
    scratch = refs[-5:]
    side_job = functools.partial(_run_slab_casts, refs[5:5 + n_cast], refs[6 + n_cast:-5],
                                 cast_moves)
    i = pl.program_id(2)
    heads = range(hb)
    lv = lamv_ref[...]
    lam = (jnp.exp(jnp.sum(lv[0:1] * lv[1:2], axis=1, keepdims=True))
           - jnp.exp(jnp.sum(lv[2:3] * lv[3:4], axis=1, keepdims=True)) + lam_init)

    lane = lax.broadcasted_iota(jnp.int32, (tq, LANES), 1)
    qs = []
    for h in heads:
        q = q_ref[:, h * LANES:(h + 1) * LANES].astype(F32)
        qs.append(jnp.concatenate([jnp.where(lane < DIFF_QK_DIM, q, 0.0),
                                   jnp.where(lane >= DIFF_QK_DIM, q, 0.0)], axis=0).astype(BF16))
    zero = jnp.zeros((1, 2 * tq), F32)

    def logits(h, j):
        start = pl.multiple_of(j * tq, tq)
        return _dot_nt(k_ref[pl.ds(start, tq), h * LANES:(h + 1) * LANES], qs[h])

    def values(h, j):
        return v_ref[pl.ds(pl.multiple_of(j * tq, tq), tq), h * LANES:(h + 1) * LANES]

    def finish(h, o_t):
        yd = (o_t[:, :tq] - lam * o_t[:, tq:]).T
        o_ref[:, h * LANES:(h + 1) * LANES] = (
            _rms(yd, g_ref[...]) * (1.0 - lam_init)).astype(BF16)

    krow = lax.broadcasted_iota(jnp.int32, (tq, 2 * tq), 0)
    qcol = lax.broadcasted_iota(jnp.int32, (tq, 2 * tq), 1)
    qcol = jnp.where(qcol >= tq, qcol - tq, qcol)
    visible = krow // CHUNK <= qcol // CHUNK
    _flash_attend(i, heads, logits, values, [zero] * hb, visible, finish, side_job, *scratch)


def _diff_attn(proj, lamv, g, *, batch, seq, lam_init, cast=(), cast_moves=None, tq=256, hb=4):
    proj3 = proj.reshape(batch, seq, PROJ_W)
    n_q = seq // tq
    w = hb * LANES
    groups = DIFF_HEADS // hb
    assert groups == 1 or not cast
    dq, dk, dv = (DQ_BLK * LANES) // w, (DK_BLK * LANES) // w, (DV_BLK * LANES) // w
    cast_in, cast_out, cast_shape = _slab_cast_specs(cast, batch * n_q, lambda b, g, i: b * n_q + i)
    outs = pl.pallas_call(
        functools.partial(_diff_kernel, tq=tq, hb=hb, lam_init=lam_init, n_cast=len(cast),
                          cast_moves=cast_moves),
        out_shape=[jax.ShapeDtypeStruct((batch, seq, DIFF_V_W), BF16)] + cast_shape,
        grid=(batch, groups, n_q),
        in_specs=[
            pl.BlockSpec((4, DIFF_QK_DIM), lambda b, g, i: (0, 0)),
            pl.BlockSpec((1, DIFF_V_DIM), lambda b, g, i: (0, 0)),
            pl.BlockSpec((None, tq, w), lambda b, g, i: (b, i, dq + g)),
            pl.BlockSpec((None, seq, w), lambda b, g, i: (b, 0, dk + g)),
            pl.BlockSpec((None, seq, w), lambda b, g, i: (b, 0, dv + g)),
        ] + cast_in,
        out_specs=[pl.BlockSpec((None, tq, w), lambda b, g, i: (b, i, g))] + cast_out,
        scratch_shapes=_flash_scratch(hb, DIFF_V_DIM, tq, 2 * tq),
        compiler_params=_params("parallel", "parallel", "arbitrary"),
        name="diff_attn",
    )(lamv, g, proj3, proj3, proj3, *[arr for arr, _ in cast])
    return outs[0].reshape(batch * seq, DIFF_V_W), outs[1:]


def _mem_kv_kernel(mem_ref, g_ref, w_ref, o_ref):
    o_ref[...] = _dot(_rms(mem_ref[...], g_ref[...]).astype(BF16), w_ref[...]).astype(BF16)


def _mem_kv(mem, g, w):
    batch, n_mem, d = mem.shape
    return pl.pallas_call(
        _mem_kv_kernel,
        out_shape=jax.ShapeDtypeStruct((batch, n_mem, 2 * MEM_W), BF16),
        grid=(batch,),
        in_specs=[pl.BlockSpec((None, n_mem, d), lambda b: (b, 0, 0)),
                  pl.BlockSpec((1, d), lambda b: (0, 0)),
                  pl.BlockSpec((d, 2 * MEM_W), lambda b: (0, 0))],
        out_specs=pl.BlockSpec((None, n_mem, 2 * MEM_W), lambda b: (b, 0, 0)),
        compiler_params=_params("parallel"),
        name="mem_kv",
    )(mem, g, w)


def _mem_attn_kernel(q_ref, kv_ref, o_ref):
    logits = [_dot_nt(kv_ref[:, h * MEM_DIM:(h + 1) * MEM_DIM],
                      q_ref[:, h * MEM_DIM:(h + 1) * MEM_DIM]) for h in range(MEM_HEADS)]
    for h in range(MEM_HEADS):
        v = kv_ref[:, MEM_W + h * MEM_DIM:MEM_W + (h + 1) * MEM_DIM]
        s = logits[h]
        p = jnp.exp2(s - jnp.max(s, axis=0, keepdims=True))
        l = jnp.sum(p, axis=0, keepdims=True)
        o_t = _dot_tn(v, p.astype(BF16)) / l
        o_ref[:, h * MEM_DIM:(h + 1) * MEM_DIM] = o_t.T.astype(BF16)


def _mem_attn(proj, mkv, *, batch, seq, tq=512):
    proj3 = proj.reshape(batch, seq, PROJ_W)
    n_mem = mkv.shape[1]
    out = pl.pallas_call(
        _mem_attn_kernel,
        out_shape=jax.ShapeDtypeStruct((batch, seq, MEM_W), BF16),
        grid=(batch, seq // tq),
        in_specs=[pl.BlockSpec((None, tq, MEM_W), lambda b, i: (b, i, MQ_COL // MEM_W)),
                  pl.BlockSpec((None, n_mem, 2 * MEM_W), lambda b, i: (b, 0, 0))],
        out_specs=pl.BlockSpec((None, tq, MEM_W), lambda b, i: (b, i, 0)),
        compiler_params=_params("parallel", "arbitrary"),
        name="mem_attn",
    )(proj3, mkv)
    return out.reshape(batch * seq, MEM_W)


def _merge_kernel(x_ref, h_ref, yf_ref, yd_ref, ym_ref, wg_ref, bg_ref, wf_ref, wd_ref, wm_ref,
                  wo_ref, gpost_ref, o_ref, acc_ref):
    n = pl.program_id(1)
    tn = wo_ref.shape[0]

    @pl.when(n == 0)
    def _():
        acc_ref[...] = jnp.zeros_like(acc_ref)

    z = _dot(h_ref[...], wg_ref[...]) + bg_ref[...]
    gates = 1.0 / (1.0 + jnp.exp(-z))
    merged = (gates[:, :tn] * _dot(yf_ref[...], wf_ref[...])
              + gates[:, tn:2 * tn] * _dot(yd_ref[...], wd_ref[...])
              + gates[:, 2 * tn:] * _dot(ym_ref[...], wm_ref[...]))
    acc_ref[...] += _dot(merged.astype(BF16), wo_ref[...])

    @pl.when(n == pl.num_programs(1) - 1)
    def _():
        def emit(sl, y):
            o_ref[sl, :] = x_ref[sl, :] + y

        _rms_rows(acc_ref, gpost_ref[...], emit, F32_ROWS)


def _gate_tile_order(d):
    tn = MERGE_TN
    nt = d // tn
    return tuple(((n * 3 + b) * tn, (b * nt + n) * tn, tn) for n in range(nt) for b in range(3))


def _merge(x, h, y_fox, y_diff, y_mem, w_gate, b_gate, w_fox, w_diff, w_mem, w_out, g_post,
           *, tm=512):
    t, d = x.shape
    tn = MERGE_TN
    nt = d // tn
    row = lambda m, n: (m, 0)
    return pl.pallas_call(
        _merge_kernel,
        out_shape=jax.ShapeDtypeStruct((t, d), F32),
        grid=(t // tm, nt),
        in_specs=[
            pl.BlockSpec((tm, d), row),
            pl.BlockSpec((tm, d), row),
            pl.BlockSpec((tm, FOX_W), row),
            pl.BlockSpec((tm, DIFF_V_W), row),
            pl.BlockSpec((tm, MEM_W), row),
            pl.BlockSpec((d, 3 * tn), lambda m, n: (0, n)),
            pl.BlockSpec((1, 3 * tn), lambda m, n: (0, n)),
            pl.BlockSpec((FOX_W, tn), lambda m, n: (0, n)),
            pl.BlockSpec((DIFF_V_W, tn), lambda m, n: (0, n)),
            pl.BlockSpec((MEM_W, tn), lambda m, n: (0, n)),
            pl.BlockSpec((tn, d), lambda m, n: (n, 0)),
            pl.BlockSpec((1, d), lambda m, n: (0, 0)),
        ],
        out_specs=pl.BlockSpec((tm, d), row),
        scratch_shapes=[pltpu.VMEM((tm, d), F32)],
        compiler_params=_params("parallel", "arbitrary"),
        name="merge",
    )(x, h, y_fox, y_diff, y_mem, w_gate, b_gate, w_fox, w_diff, w_mem, w_out, g_post)


def kernel(x, mem, ffn1_pre_g, ffn1_w_gate, ffn1_w_up, ffn1_w_down, ffn1_post_g, mix_pre_g, w_in, fox_f_bias, diff_lambda_q1, diff_lambda_k1, diff_lambda_q2, diff_lambda_k2, diff_head_g, mem_norm_g, w_mem_kv, w_branch_fox, w_branch_diff, w_branch_mem, w_merge_gate, b_merge_gate, w_out, mix_post_g, ffn2_pre_g, ffn2_w_gate, ffn2_w_up, ffn2_w_down, ffn2_post_g):
    batch, seq, d = x.shape
    depth = w_in.shape[0]
    xt = x.reshape(batch * seq, d)
    cos_t, sa_t, sb_t = _rope_tables(seq)

    def row(v):
        return v.reshape(1, -1).astype(F32)

    for l in range(depth):
        first, w1_gate, w1_up, w1_down = _ffn_first(
            xt, row(ffn1_pre_g[l]), ffn1_w_gate, ffn1_w_up, ffn1_w_down, row(ffn1_post_g[l]), l,
            tm=FIRST_ROWS)
        w_in_t = w_in[l].T
        xt, (w_head, w_tail) = _ffn(
            xt, row(ffn1_pre_g[l]), w1_gate, w1_up, w1_down, row(ffn1_post_g[l]), first_rows=first,
            cast_t=[(w_in_t, 0, FF_COL + LANES), (w_in_t, FF_COL + FOX_HEADS, w_in.shape[2])])

        small = [w_out, w_branch_fox, w_branch_diff, w_branch_mem]
        proj, ff, h_mix, (wb_out, wb_fox, wb_diff, wb_mem) = _mix_proj(
            xt, row(mix_pre_g[l]), w_head, w_tail, cos_t, sa_t, sb_t, seq=seq,
            cast=[(w, l) for w in small])

        bias = jnp.pad(fox_f_bias[l].astype(F32), (0, LANES - FOX_HEADS)).reshape(1, LANES)
        fp, frow = _fox_gate(ff, bias, batch=batch, seq=seq)
        y_fox, (w2_gate, w2_up, w2_down) = _fox_attn(
            proj, fp, frow, batch=batch, seq=seq,
            cast=[(w, l) for w in (ffn2_w_gate, ffn2_w_up, ffn2_w_down)])

        lam_init = 0.8 - 0.6 * math.exp(-0.3 * l)
        lamv = jnp.stack([diff_lambda_q1[l], diff_lambda_k1[l], diff_lambda_q2[l],
                          diff_lambda_k2[l]]).astype(F32)
        gate_order = _gate_tile_order(d)
        y_diff, (wb_merge,) = _diff_attn(proj, lamv, row(diff_head_g[l]), batch=batch, seq=seq,
                                         lam_init=lam_init, cast=[(w_merge_gate, l)],
                                         cast_moves=gate_order)
        b_gate = row(b_merge_gate[l])
        b_gate = jnp.concatenate([b_gate[:, src:src + w] for _, src, w in gate_order], axis=1)

        mkv = _mem_kv(mem, row(mem_norm_g[l]), w_mem_kv[l].astype(BF16))
        y_mem = _mem_attn(proj, mkv, batch=batch, seq=seq)

        xt = _merge(xt, h_mix, y_fox, y_diff, y_mem, wb_merge, b_gate, wb_fox, wb_diff, wb_mem,
                    wb_out, row(mix_post_g[l]))

        xt, _ = _ffn(xt, row(ffn2_pre_g[l]), w2_gate, w2_up, w2_down, row(ffn2_post_g[l]))

    return xt.reshape(batch, seq, d)
```

```python
import functools
import math

import jax
import jax.numpy as jnp
from jax import lax
from jax.experimental import pallas as pl
from jax.experimental.pallas import tpu as pltpu

CHUNK = 64
EPS = 1e-6
ROPE_THETA = 500000.0

FOX_HEADS = 8
FOX_DIM = 128
FOX_W = FOX_HEADS * FOX_DIM

DIFF_HEADS = 4
DIFF_QK_DIM = 64
DIFF_V_DIM = 2 * DIFF_QK_DIM
DIFF_QK_W = DIFF_HEADS * 2 * DIFF_QK_DIM
DIFF_V_W = DIFF_HEADS * DIFF_V_DIM
ROPE_DIM = DIFF_QK_DIM // 4

MEM_HEADS = 4
MEM_DIM = 128
MEM_W = MEM_HEADS * MEM_DIM

LANES = 128
F32_ROWS = 8
BF16_ROWS = 16
RMS_ROWS_IN_FLIGHT = 128
MERGE_TN = 512
FIRST_ROWS = 1024
IN_PLACE_ROWS = 64
PROJ_W = 3 * FOX_W + 2 * DIFF_QK_W + DIFF_V_W + MEM_W
FQ_BLK, FK_BLK, FV_BLK = 0, FOX_HEADS, 2 * FOX_HEADS
DQ_BLK = 3 * FOX_HEADS
DK_BLK = DQ_BLK + DIFF_HEADS
DV_BLK = DK_BLK + DIFF_HEADS
MQ_COL = 3 * FOX_W + 2 * DIFF_QK_W + DIFF_V_W
FF_COL = 3 * FOX_W

VMEM_LIMIT = 56 * 1024 * 1024
BF16 = jnp.bfloat16
F32 = jnp.float32
NEG_INF = float("-inf")
LOG2E = math.log2(math.e)
FOX_QSCALE = FOX_DIM ** -0.5 * LOG2E
DIFF_QSCALE = DIFF_QK_DIM ** -0.5 * LOG2E
MEM_QSCALE = MEM_DIM ** -0.5 * LOG2E


def _dot(a, b):
    return jnp.dot(a, b, preferred_element_type=F32)


def _dot_nt(a, b):
    return lax.dot_general(a, b, (((1,), (1,)), ((), ())), preferred_element_type=F32)


def _rms(x, g):
    return x * lax.rsqrt(jnp.mean(x * x, axis=-1, keepdims=True) + EPS) * g


def _rms_rows(src_ref, g, emit, rows):
    def body(r, carry):
        sl = pl.ds(pl.multiple_of(r * rows, rows), rows)
        emit(sl, _rms(src_ref[sl, :], g))
        return carry

    lax.fori_loop(0, src_ref.shape[0] // rows, body, 0, unroll=RMS_ROWS_IN_FLIGHT // rows)


def _params(*sem):
    return pltpu.CompilerParams(dimension_semantics=sem, vmem_limit_bytes=VMEM_LIMIT)


def _slab_cast_specs(cast, n_steps, step_index):
    in_specs, out_specs, out_shape = [], [], []
    for arr, layer in cast:
        _, rows, cols = arr.shape
        br = rows // n_steps
        assert br * n_steps == rows and br % BF16_ROWS == 0
        in_specs.append(pl.BlockSpec((None, br, cols),
                                     lambda *g, layer=layer: (layer, step_index(*g), 0)))
        out_specs.append(pl.BlockSpec((br, cols), lambda *g: (step_index(*g), 0)))
        out_shape.append(jax.ShapeDtypeStruct((rows, cols), BF16))
    return in_specs, out_specs, out_shape


def _run_slab_casts(src_refs, dst_refs, moves=None):
    for src_ref, dst_ref in zip(src_refs, dst_refs):
        if moves is None:
            dst_ref[...] = src_ref[...].astype(BF16)
        else:
            for dst_lo, src_lo, width in moves:
                dst_ref[:, dst_lo:dst_lo + width] = src_ref[:, src_lo:src_lo + width].astype(BF16)


def _ffn_step(f, last, x_ref, gpre_ref, gpost_ref, o_ref, h_ref, acc_ref, w_gate, w_up, w_down,
              side_jobs):
    @pl.when(f == 0)
    def _():
        def emit(sl, y):
            h_ref[sl, :] = y.astype(BF16)
            acc_ref[sl, :] = jnp.zeros_like(y)

        _rms_rows(x_ref, gpre_ref[...], emit, BF16_ROWS)

    h = h_ref[...]
    side_jobs(0)
    g = _dot(h, w_gate())
    side_jobs(1)
    u = _dot(h, w_up())
    side_jobs(2)
    a = (g * (1.0 / (1.0 + jnp.exp(-g)))) * u
    acc_ref[...] += _dot(a.astype(BF16), w_down())

    @pl.when(f == last)
    def _():
        def emit(sl, y):
            o_ref[sl, :] = x_ref[sl, :] + y

        g_half = 0.5 * gpost_ref[...]
        if acc_ref is o_ref:
            for r in range(0, o_ref.shape[0], IN_PLACE_ROWS):
                sl = slice(r, r + IN_PLACE_ROWS)
                emit(sl, _rms(acc_ref[sl, :], g_half))
        else:
            _rms_rows(acc_ref, g_half, emit, F32_ROWS)


def _ffn_first_kernel(x_ref, gpre_ref, wg_ref, wu_ref, wd_ref, gpost_ref,
                      o_ref, wgb_ref, wub_ref, wdb_ref, h_ref):
    for src_ref, dst_ref in ((wg_ref, wgb_ref), (wu_ref, wub_ref), (wd_ref, wdb_ref)):
        dst_ref[...] = src_ref[...].astype(BF16)
    _ffn_step(pl.program_id(0), pl.num_programs(0) - 1, x_ref, gpre_ref, gpost_ref, o_ref, h_ref,
              o_ref, lambda: wgb_ref[...], lambda: wub_ref[...], lambda: wdb_ref[...],
              lambda part: None)


def _ffn_first(x, g_pre, w_gate, w_up, w_down, g_post, layer, *, tm, tf=256):
    _, d = x.shape
    d_ff = w_down.shape[1]
    return pl.pallas_call(
        _ffn_first_kernel,
        out_shape=(jax.ShapeDtypeStruct((tm, d), F32),
                   jax.ShapeDtypeStruct((d, d_ff), BF16), jax.ShapeDtypeStruct((d, d_ff), BF16),
                   jax.ShapeDtypeStruct((d_ff, d), BF16)),
        grid=(d_ff // tf,),
        in_specs=[
            pl.BlockSpec((tm, d), lambda f: (0, 0), pipeline_mode=pl.Buffered(1)),
            pl.BlockSpec((1, d), lambda f: (0, 0)),
            pl.BlockSpec((None, d, tf), lambda f: (layer, 0, f)),
            pl.BlockSpec((None, d, tf), lambda f: (layer, 0, f)),
            pl.BlockSpec((None, tf, d), lambda f: (layer, f, 0)),
            pl.BlockSpec((1, d), lambda f: (0, 0)),
        ],
        out_specs=(pl.BlockSpec((tm, d), lambda f: (0, 0)),
                   pl.BlockSpec((d, tf), lambda f: (0, f)), pl.BlockSpec((d, tf), lambda f: (0, f)),
                   pl.BlockSpec((tf, d), lambda f: (f, 0))),
        scratch_shapes=[pltpu.VMEM((tm, d), BF16)],
        compiler_params=_params("arbitrary"),
        name="ffn_first",
    )(x, g_pre, w_gate, w_up, w_down, g_post)


def _ffn_kernel(*refs, n_flip, skip):
    x_ref, gpre_ref, wg_ref, wu_ref, wd_ref, gpost_ref = refs[:6]
    n_in = 6 + (skip > 0)
    o_ref = refs[n_in + n_flip]
    h_ref, acc_ref = refs[-2:]
    m, f = pl.program_id(0), pl.program_id(1)
    last = pl.num_programs(1) - 1

    jobs = list(zip(refs[n_in:n_in + n_flip], refs[n_in + n_flip + 1:n_in + 2 * n_flip + 1]))

    def side_jobs(part, parts=3):
        for src_ref, dst_ref in jobs[part::parts]:
            dst_ref[...] = src_ref[...].T.astype(BF16)

    def compute():
        _ffn_step(f, last, x_ref, gpre_ref, gpost_ref, o_ref, h_ref, acc_ref,
                  lambda: wg_ref[...], lambda: wu_ref[...], lambda: wd_ref[...], side_jobs)

    if skip:
        pl.when(m >= skip)(compute)

        @pl.when(jnp.logical_and(m < skip, f == last))
        def _():
            o_ref[...] = refs[6][...]
    else:
        compute()


def _ffn(x, g_pre, w_gate, w_up, w_down, g_post, *, first_rows=None, cast_t=(), tm=512, tf=512):
    t, d = x.shape
    d_ff = w_down.shape[0]
    m_tiles, f_steps = t // tm, d_ff // tf
    skip = 0 if first_rows is None else first_rows.shape[0] // tm

    def tile(f, m):
        return f if skip == 0 else jnp.where(m < skip, 0, f)

    in_specs = [
        pl.BlockSpec((tm, d), lambda m, f: (m, 0)),
        pl.BlockSpec((1, d), lambda m, f: (0, 0)),
        pl.BlockSpec((d, tf), lambda m, f: (0, tile(f, m))),
        pl.BlockSpec((d, tf), lambda m, f: (0, tile(f, m))),
        pl.BlockSpec((tf, d), lambda m, f: (tile(f, m), 0)),
        pl.BlockSpec((1, d), lambda m, f: (0, 0)),
    ]
    extra = []
    if first_rows is not None:
        assert first_rows.shape == (skip * tm, d)
        in_specs.append(pl.BlockSpec((tm, d), lambda m, f: (jnp.minimum(m, skip - 1), 0),
                                     pipeline_mode=pl.Buffered(1)))
        extra.append(first_rows)
    out_specs = [pl.BlockSpec((tm, d), lambda m, f: (m, 0))]
    out_shape = [jax.ShapeDtypeStruct((t, d), F32)]
    for arr, lo, hi in cast_t:
        _, cols = arr.shape
        n_blk = (hi - lo) // LANES
        assert n_blk * LANES == hi - lo and n_blk <= (m_tiles - skip) * f_steps

        def block(m, f, n_blk=n_blk):
            return jnp.clip((m - skip) * f_steps + f, 0, n_blk - 1)

        in_specs.append(pl.BlockSpec(
            (pl.Element(LANES), pl.Element(cols)),
            lambda m, f, block=block, lo=lo: (
                pl.multiple_of(lo + LANES * block(m, f), math.gcd(lo, LANES)), 0)))
        out_specs.append(pl.BlockSpec((cols, LANES), lambda m, f, block=block: (0, block(m, f))))
        out_shape.append(jax.ShapeDtypeStruct((cols, hi - lo), BF16))
    outs = pl.pallas_call(
        functools.partial(_ffn_kernel, n_flip=len(cast_t), skip=skip),
        out_shape=out_shape,
        grid=(m_tiles, f_steps),
        in_specs=in_specs,
        out_specs=out_specs,
        scratch_shapes=[pltpu.VMEM((tm, d), BF16), pltpu.VMEM((tm, d), F32)],
        compiler_params=_params("parallel", "arbitrary"),
        name="ffn",
    )(x, g_pre, w_gate, w_up, w_down, g_post, *extra, *[c[0] for c in cast_t])
    return outs[0], outs[1:]


def _proj_kernel(*refs, tn, n_cast):
    x_ref, g_ref, wa_ref, wff_ref, wb_ref, cos_ref, sa_ref, sb_ref = refs[:8]
    o_ref, ff_ref, h_ref = refs[8 + n_cast:11 + n_cast]
    _run_slab_casts(refs[8:8 + n_cast], refs[11 + n_cast:])
    h_ref[...] = _rms(x_ref[...], g_ref[...]).astype(BF16)
    ff_ref[...] = _dot(h_ref[...], wff_ref[...])
    half = ROPE_DIM // 2
    na = wa_ref.shape[1] // tn
    for n in range(na + wb_ref.shape[1] // tn):
        w_tile = (wa_ref[:, n * tn:(n + 1) * tn] if n < na
                  else wb_ref[:, (n - na) * tn:(n - na + 1) * tn])
        y = _dot(h_ref[...], w_tile)
        first_blk = n * tn // LANES
        if first_blk < FK_BLK:
            y = y * FOX_QSCALE
        if first_blk >= MQ_COL // LANES:
            y = y * MEM_QSCALE
        if DQ_BLK <= first_blk < DV_BLK:
            qscale = DIFF_QSCALE if first_blk < DK_BLK else 1.0
            for j in range(tn // LANES):
                blk = y[:, j * LANES:(j + 1) * LANES]
                rot = (blk * cos_ref[...] + pltpu.roll(blk, LANES - half, axis=1) * sa_ref[...]
                       + pltpu.roll(blk, half, axis=1) * sb_ref[...])
                o_ref[:, n * tn + j * LANES:n * tn + (j + 1) * LANES] = (rot * qscale).astype(BF16)
        else:
            o_ref[:, n * tn:(n + 1) * tn] = y.astype(BF16)


def _mix_proj(x, g, w_head, w_tail, cos_t, sa_t, sb_t, *, seq, cast=(), tm=512, tn=512):
    t, d = x.shape
    n_out = FF_COL + w_tail.shape[1]
    assert all((blk * LANES) % tn == 0 for blk in (FK_BLK, DQ_BLK, DK_BLK, DV_BLK))
    assert w_head.shape[1] == FF_COL + LANES and n_out == PROJ_W
    s_tiles = seq // tm
    m_tiles = t // tm
    resident = dict(pipeline_mode=pl.Buffered(1))
    in_specs = [
        pl.BlockSpec((tm, d), lambda m: (m, 0)),
        pl.BlockSpec((1, d), lambda m: (0, 0)),
        pl.BlockSpec((d, FF_COL), lambda m: (0, 0), **resident),
        pl.BlockSpec((d, LANES), lambda m: (0, FF_COL // LANES), **resident),
        pl.BlockSpec((d, w_tail.shape[1]), lambda m: (0, 0), **resident),
        pl.BlockSpec((tm, LANES), lambda m: (m % s_tiles, 0)),
        pl.BlockSpec((tm, LANES), lambda m: (m % s_tiles, 0)),
        pl.BlockSpec((tm, LANES), lambda m: (m % s_tiles, 0)),
    ]
    out_specs = [pl.BlockSpec((tm, n_out), lambda m: (m, 0)),
                 pl.BlockSpec((tm, LANES), lambda m: (m, 0)),
                 pl.BlockSpec((tm, d), lambda m: (m, 0))]
    out_shape = [jax.ShapeDtypeStruct((t, n_out), BF16), jax.ShapeDtypeStruct((t, LANES), F32),
                 jax.ShapeDtypeStruct((t, d), BF16)]
    cast_in, cast_out, cast_shape = _slab_cast_specs(cast, m_tiles, lambda m: m)
    in_specs, out_specs, out_shape = in_specs + cast_in, out_specs + cast_out, out_shape + cast_shape
    outs = pl.pallas_call(
        functools.partial(_proj_kernel, tn=tn, n_cast=len(cast)),
        out_shape=out_shape,
        grid=(m_tiles,),
        in_specs=in_specs,
        out_specs=out_specs,
        compiler_params=_params("parallel"),
        name="mix_proj",
    )(x, g, w_head, w_head, w_tail, cos_t, sa_t, sb_t, *[arr for arr, _ in cast])
    return outs[0], outs[1], outs[2], outs[3:]


def _rope_tables(seq):
    half = ROPE_DIM // 2
    pos = jnp.arange(seq, dtype=F32)
    inv_freq = ROPE_THETA ** (-jnp.arange(0, ROPE_DIM, 2, dtype=F32) / ROPE_DIM)
    ang = pos[:, None] * inv_freq[None, :]
    cos, sin = jnp.cos(ang), jnp.sin(ang)
    ones = jnp.ones((seq, DIFF_QK_DIM - ROPE_DIM), F32)
    zeros_h = jnp.zeros((seq, half), F32)
    zeros_r = jnp.zeros((seq, DIFF_QK_DIM - ROPE_DIM), F32)
    cos_m = jnp.concatenate([cos, cos, ones], axis=1)
    sa_m = jnp.concatenate([-sin, zeros_h, zeros_r], axis=1)
    sb_m = jnp.concatenate([zeros_h, sin, zeros_r], axis=1)
    rep = LANES // DIFF_QK_DIM
    return (jnp.tile(cos_m, (1, rep)), jnp.tile(sa_m, (1, rep)), jnp.tile(sb_m, (1, rep)))


def _fgate_kernel(ff_ref, bias_ref, fp_ref, frow_ref, fcol_ref, *, cb):
    seq = ff_ref.shape[0]
    head_lane = lax.broadcasted_iota(jnp.int32, (cb, LANES), 1) < FOX_HEADS
    z = ff_ref[...] + bias_ref[...]
    lf = jnp.minimum(z, 0.0) - jnp.log1p(jnp.exp(-jnp.abs(z)))
    r = lax.broadcasted_iota(jnp.int32, (cb, cb), 0)
    c = lax.broadcasted_iota(jnp.int32, (cb, cb), 1)
    tri = (r >= c).astype(F32)
    carry = jnp.zeros((1, LANES), F32)
    for i in range(seq // cb):
        cs = jnp.dot(tri, lf[i * cb:(i + 1) * cb], precision=lax.Precision.HIGHEST,
                     preferred_element_type=F32) + carry
        carry = cs[cb - 1:cb, :]
        cs2 = jnp.where(head_lane, cs * LOG2E, 0.0)
        fcol_ref[i * cb:(i + 1) * cb, :] = cs2
        hi = cs2.astype(BF16).astype(F32)
        mid = (cs2 - hi).astype(BF16).astype(F32)
        lo = (cs2 - hi - mid).astype(BF16).astype(F32)
        pieces = hi + pltpu.roll(mid, FOX_HEADS, axis=1) + pltpu.roll(lo, 2 * FOX_HEADS, axis=1)
        fp_ref[i * cb:(i + 1) * cb, :] = pieces.astype(BF16)
    frow_ref[...] = fcol_ref[...].T[:FOX_HEADS]


def _fox_gate(ff, bias, *, batch, seq, cb=256):
    ff = ff.reshape(batch, seq, LANES)
    return pl.pallas_call(
        functools.partial(_fgate_kernel, cb=cb),
        out_shape=(jax.ShapeDtypeStruct((batch, seq, LANES), BF16),
                   jax.ShapeDtypeStruct((batch, FOX_HEADS, seq), F32)),
        grid=(batch,),
        in_specs=[pl.BlockSpec((None, seq, LANES), lambda b: (b, 0, 0)),
                  pl.BlockSpec((1, LANES), lambda b: (0, 0))],
        out_specs=(pl.BlockSpec((None, seq, LANES), lambda b: (b, 0, 0)),
                   pl.BlockSpec((None, FOX_HEADS, seq), lambda b: (b, 0, 0))),
        scratch_shapes=[pltpu.VMEM((seq, LANES), F32)],
        compiler_params=_params("parallel"),
        name="fox_gate",
    )(ff, bias)


def _dot_tn(a, b):
    return lax.dot_general(a, b, (((0,), (0,)), ((), ())), preferred_element_type=F32)


def _flash_attend(i, heads, logits, values, ft2, visible, finish, side_job,
                  sa_ref, sb_ref, m_ref, l_ref, acc_ref):
    def step(h, s, j):
        m = m_ref[h]
        m_new = jnp.maximum(m, jnp.max(s, axis=0, keepdims=True) + ft2[h])
        alpha = jnp.exp2(m - m_new)
        p = jnp.exp2(s - (m_new - ft2[h]))
        m_ref[h] = m_new
        l_ref[h] = alpha * l_ref[h] + jnp.sum(p, axis=0, keepdims=True)
        acc_ref[h] = alpha * acc_ref[h] + _dot_tn(values(h, j), p.astype(BF16))

    for h in heads:
        m_ref[h] = jnp.full(m_ref.shape[1:], NEG_INF, F32)
        l_ref[h] = jnp.zeros(l_ref.shape[1:], F32)
        acc_ref[h] = jnp.zeros(acc_ref.shape[1:], F32)
        sa_ref[h] = logits(h, 0)
    side_job()

    def pair(jj, carry):
        j = 2 * jj
        for h in heads:
            sb_ref[h] = logits(h, j + 1)
            step(h, sa_ref[h], j)
        for h in heads:
            sa_ref[h] = logits(h, j + 2)
            step(h, sb_ref[h], j + 1)
        return carry

    lax.fori_loop(0, i // 2, pair, 0)

    def last(s_ref):
        for h in heads:
            step(h, jnp.where(visible, s_ref[h], NEG_INF), i)
            finish(h, acc_ref[h] / l_ref[h])

    @pl.when(i % 2 == 0)
    def _():
        last(sa_ref)

    @pl.when(i % 2 == 1)
    def _():
        for h in heads:
            sb_ref[h] = logits(h, i)
            step(h, sa_ref[h], i - 1)
        last(sb_ref)


def _flash_scratch(hb, width, tq, cols):
    return [pltpu.VMEM((hb, tq, cols), F32), pltpu.VMEM((hb, tq, cols), F32),
            pltpu.VMEM((hb, 1, cols), F32), pltpu.VMEM((hb, 1, cols), F32),
            pltpu.VMEM((hb, width, cols), F32)]


def _fox_kernel(*refs, tq, hb, n_cast):
    q_ref, k_ref, v_ref, fp_ref, fr_ref = refs[:5]
    o_ref = refs[5 + n_cast]
    scratch = refs[-5:]
    side_job = functools.partial(_run_slab_casts, refs[5:5 + n_cast], refs[6 + n_cast:-5])
    i = pl.program_id(2)
    h0 = pl.program_id(1) * hb
    heads = range(hb)
    ft2 = [fr_ref[h, pl.ds(i, 1), :] for h in heads]

    lane = lax.broadcasted_iota(jnp.int32, (tq, LANES), 1)
    q_ext = []
    for h in heads:
        mine = (lane % FOX_HEADS == h0 + h) & (lane < 3 * FOX_HEADS)
        sel = jnp.where(mine, -1.0, 0.0).astype(BF16)
        q_ext.append(jnp.concatenate([q_ref[:, h * FOX_DIM:(h + 1) * FOX_DIM], sel], axis=1))

    def logits(h, j):
        start = pl.multiple_of(j * tq, tq)
        k_ext = jnp.concatenate([k_ref[pl.ds(start, tq), h * FOX_DIM:(h + 1) * FOX_DIM],
                                 fp_ref[pl.ds(start, tq), :]], axis=1)
        return _dot_nt(k_ext, q_ext[h])

    def values(h, j):
        return v_ref[pl.ds(pl.multiple_of(j * tq, tq), tq), h * FOX_DIM:(h + 1) * FOX_DIM]

    def finish(h, o_t):
        o_ref[:, h * FOX_DIM:(h + 1) * FOX_DIM] = o_t.T.astype(BF16)

    krow = lax.broadcasted_iota(jnp.int32, (tq, tq), 0)
    qcol = lax.broadcasted_iota(jnp.int32, (tq, tq), 1)
    _flash_attend(i, heads, logits, values, ft2, krow <= qcol, finish, side_job, *scratch)


def _fox_attn(proj, fp, frow, *, batch, seq, cast=(), tq=256, hb=8):
    proj3 = proj.reshape(batch, seq, PROJ_W)
    n_q = seq // tq
    frow4 = frow.reshape(batch, FOX_HEADS, n_q, tq)
    w = hb * FOX_DIM
    groups = FOX_HEADS // hb
    assert groups == 1 or not cast
    cast_in, cast_out, cast_shape = _slab_cast_specs(cast, batch * n_q, lambda b, g, i: b * n_q + i)
    outs = pl.pallas_call(
        functools.partial(_fox_kernel, tq=tq, hb=hb, n_cast=len(cast)),
        out_shape=[jax.ShapeDtypeStruct((batch, seq, FOX_W), BF16)] + cast_shape,
        grid=(batch, groups, n_q),
        in_specs=[
            pl.BlockSpec((None, tq, w), lambda b, g, i: (b, i, g)),
            pl.BlockSpec((None, seq, w), lambda b, g, i: (b, 0, groups + g)),
            pl.BlockSpec((None, seq, w), lambda b, g, i: (b, 0, 2 * groups + g)),
            pl.BlockSpec((None, seq, LANES), lambda b, g, i: (b, 0, 0)),
            pl.BlockSpec((None, hb, n_q, tq), lambda b, g, i: (b, g, 0, 0)),
        ] + cast_in,
        out_specs=[pl.BlockSpec((None, tq, w), lambda b, g, i: (b, i, g))] + cast_out,
        scratch_shapes=_flash_scratch(hb, FOX_DIM, tq, tq),
        compiler_params=_params("parallel", "parallel", "arbitrary"),
        name="fox_attn",
    )(proj3, proj3, proj3, fp, frow4, *[arr for arr, _ in cast])
    return outs[0].reshape(batch * seq, FOX_W), outs[1:]


def _diff_kernel(*refs, tq, hb, lam_init, n_cast, cast_moves):
    lamv_ref, g_ref, q_ref, k_ref, v_ref = refs[:5]
    o_ref = refs[5 + n_cast]
    scratch = refs[-5:]
    side_job = functools.partial(_run_slab_casts, refs[5:5 + n_cast], refs[6 + n_cast:-5],
                                 cast_moves)
    i = pl.program_id(2)
    heads = range(hb)
    lv = lamv_ref[...]
    lam = (jnp.exp(jnp.sum(lv[0:1] * lv[1:2], axis=1, keepdims=True))
           - jnp.exp(jnp.sum(lv[2:3] * lv[3:4], axis=1, keepdims=True)) + lam_init)

    lane = lax.broadcasted_iota(jnp.int32, (tq, LANES), 1)
    qs = []
    for h in heads:
        q = q_ref[:, h * LANES:(h + 1) * LANES].astype(F32)
        qs.append(jnp.concatenate([jnp.where(lane < DIFF_QK_DIM, q, 0.0),
                                   jnp.where(lane >= DIFF_QK_DIM, q, 0.0)], axis=0).astype(BF16))
    zero = jnp.zeros((1, 2 * tq), F32)

    def logits(h, j):
        start = pl.multiple_of(j * tq, tq)
        return _dot_nt(k_ref[pl.ds(start, tq), h * LANES:(h + 1) * LANES], qs[h])

    def values(h, j):
        return v_ref[pl.ds(pl.multiple_of(j * tq, tq), tq), h * LANES:(h + 1) * LANES]

    def finish(h, o_t):
        yd = (o_t[:, :tq] - lam * o_t[:, tq:]).T
        o_ref[:, h * LANES:(h + 1) * LANES] = (
            _rms(yd, g_ref[...]) * (1.0 - lam_init)).astype(BF16)

    krow = lax.broadcasted_iota(jnp.int32, (tq, 2 * tq), 0)
    qcol = lax.broadcasted_iota(jnp.int32, (tq, 2 * tq), 1)
    qcol = jnp.where(qcol >= tq, qcol - tq, qcol)
    visible = krow // CHUNK <= qcol // CHUNK
    _flash_attend(i, heads, logits, values, [zero] * hb, visible, finish, side_job, *scratch)


def _diff_attn(proj, lamv, g, *, batch, seq, lam_init, cast=(), cast_moves=None, tq=256, hb=4):
    proj3 = proj.reshape(batch, seq, PROJ_W)
    n_q = seq // tq
    w = hb * LANES
    groups = DIFF_HEADS // hb
    assert groups == 1 or not cast
    dq, dk, dv = (DQ_BLK * LANES) // w, (DK_BLK * LANES) // w, (DV_BLK * LANES) // w
    cast_in, cast_out, cast_shape = _slab_cast_specs(cast, batch * n_q, lambda b, g, i: b * n_q + i)
    outs = pl.pallas_call(
        functools.partial(_diff_kernel, tq=tq, hb=hb, lam_init=lam_init, n_cast=len(cast),
                          cast_moves=cast_moves),
        out_shape=[jax.ShapeDtypeStruct((batch, seq, DIFF_V_W), BF16)] + cast_shape,
        grid=(batch, groups, n_q),
        in_specs=[
            pl.BlockSpec((4, DIFF_QK_DIM), lambda b, g, i: (0, 0)),
            pl.BlockSpec((1, DIFF_V_DIM), lambda b, g, i: (0, 0)),
            pl.BlockSpec((None, tq, w), lambda b, g, i: (b, i, dq + g)),
            pl.BlockSpec((None, seq, w), lambda b, g, i: (b, 0, dk + g)),
            pl.BlockSpec((None, seq, w), lambda b, g, i: (b, 0, dv + g)),
        ] + cast_in,
        out_specs=[pl.BlockSpec((None, tq, w), lambda b, g, i: (b, i, g))] + cast_out,
        scratch_shapes=_flash_scratch(hb, DIFF_V_DIM, tq, 2 * tq),
        compiler_params=_params("parallel", "parallel", "arbitrary"),
        name="diff_attn",
    )(lamv, g, proj3, proj3, proj3, *[arr for arr, _ in cast])
    return outs[0].reshape(batch * seq, DIFF_V_W), outs[1:]


def _mem_kv_kernel(mem_ref, g_ref, w_ref, o_ref):
    o_ref[...] = _dot(_rms(mem_ref[...], g_ref[...]).astype(BF16), w_ref[...]).astype(BF16)


def _mem_kv(mem, g, w):
    batch, n_mem, d = mem.shape
    return pl.pallas_call(
        _mem_kv_kernel,
        out_shape=jax.ShapeDtypeStruct((batch, n_mem, 2 * MEM_W), BF16),
        grid=(batch,),
        in_specs=[pl.BlockSpec((None, n_mem, d), lambda b: (b, 0, 0)),
                  pl.BlockSpec((1, d), lambda b: (0, 0)),
                  pl.BlockSpec((d, 2 * MEM_W), lambda b: (0, 0))],
        out_specs=pl.BlockSpec((None, n_mem, 2 * MEM_W), lambda b: (b, 0, 0)),
        compiler_params=_params("parallel"),
        name="mem_kv",
    )(mem, g, w)


def _mem_attn_kernel(q_ref, kv_ref, o_ref):
    logits = [_dot_nt(kv_ref[:, h * MEM_DIM:(h + 1) * MEM_DIM],
                      q_ref[:, h * MEM_DIM:(h + 1) * MEM_DIM]) for h in range(MEM_HEADS)]
    for h in range(MEM_HEADS):
        v = kv_ref[:, MEM_W + h * MEM_DIM:MEM_W + (h + 1) * MEM_DIM]
        s = logits[h]
        p = jnp.exp2(s - jnp.max(s, axis=0, keepdims=True))
        l = jnp.sum(p, axis=0, keepdims=True)
        o_t = _dot_tn(v, p.astype(BF16)) / l
        o_ref[:, h * MEM_DIM:(h + 1) * MEM_DIM] = o_t.T.astype(BF16)


def _mem_attn(proj, mkv, *, batch, seq, tq=512):
    proj3 = proj.reshape(batch, seq, PROJ_W)
    n_mem = mkv.shape[1]
    out = pl.pallas_call(
        _mem_attn_kernel,
        out_shape=jax.ShapeDtypeStruct((batch, seq, MEM_W), BF16),
        grid=(batch, seq // tq),
        in_specs=[pl.BlockSpec((None, tq, MEM_W), lambda b, i: (b, i, MQ_COL // MEM_W)),
                  pl.BlockSpec((None, n_mem, 2 * MEM_W), lambda b, i: (b, 0, 0))],
        out_specs=pl.BlockSpec((None, tq, MEM_W), lambda b, i: (b, i, 0)),
        compiler_params=_params("parallel", "arbitrary"),
        name="mem_attn",
    )(proj3, mkv)
    return out.reshape(batch * seq, MEM_W)


def _merge_kernel(x_ref, h_ref, yf_ref, yd_ref, ym_ref, wg_ref, bg_ref, wf_ref, wd_ref, wm_ref,
                  wo_ref, gpost_ref, o_ref, acc_ref):
    n = pl.program_id(1)
    tn = wo_ref.shape[0]

    @pl.when(n == 0)
    def _():
        acc_ref[...] = jnp.zeros_like(acc_ref)

    z = _dot(h_ref[...], wg_ref[...]) + bg_ref[...]
    gates = 1.0 / (1.0 + jnp.exp(-z))
    merged = (gates[:, :tn] * _dot(yf_ref[...], wf_ref[...])
              + gates[:, tn:2 * tn] * _dot(yd_ref[...], wd_ref[...])
              + gates[:, 2 * tn:] * _dot(ym_ref[...], wm_ref[...]))
    acc_ref[...] += _dot(merged.astype(BF16), wo_ref[...])

    @pl.when(n == pl.num_programs(1) - 1)
    def _():
        def emit(sl, y):
            o_ref[sl, :] = x_ref[sl, :] + y

        _rms_rows(acc_ref, gpost_ref[...], emit, F32_ROWS)


def _gate_tile_order(d):
    tn = MERGE_TN
    nt = d // tn
    return tuple(((n * 3 + b) * tn, (b * nt + n) * tn, tn) for n in range(nt) for b in range(3))


def _merge(x, h, y_fox, y_diff, y_mem, w_gate, b_gate, w_fox, w_diff, w_mem, w_out, g_post,
           *, tm=512):
    t, d = x.shape
    tn = MERGE_TN
    nt = d // tn
    row = lambda m, n: (m, 0)
    return pl.pallas_call(
        _merge_kernel,
        out_shape=jax.ShapeDtypeStruct((t, d), F32),
        grid=(t // tm, nt),
        in_specs=[
            pl.BlockSpec((tm, d), row),
            pl.BlockSpec((tm, d), row),
            pl.BlockSpec((tm, FOX_W), row),
            pl.BlockSpec((tm, DIFF_V_W), row),
            pl.BlockSpec((tm, MEM_W), row),
            pl.BlockSpec((d, 3 * tn), lambda m, n: (0, n)),
            pl.BlockSpec((1, 3 * tn), lambda m, n: (0, n)),
            pl.BlockSpec((FOX_W, tn), lambda m, n: (0, n)),
            pl.BlockSpec((DIFF_V_W, tn), lambda m, n: (0, n)),
            pl.BlockSpec((MEM_W, tn), lambda m, n: (0, n)),
            pl.BlockSpec((tn, d), lambda m, n: (n, 0)),
            pl.BlockSpec((1, d), lambda m, n: (0, 0)),
        ],
        out_specs=pl.BlockSpec((tm, d), row),
        scratch_shapes=[pltpu.VMEM((tm, d), F32)],
        compiler_params=_params("parallel", "arbitrary"),
        name="merge",
    )(x, h, y_fox, y_diff, y_mem, w_gate, b_gate, w_fox, w_diff, w_mem, w_out, g_post)


def kernel(x, mem, ffn1_pre_g, ffn1_w_gate, ffn1_w_up, ffn1_w_down, ffn1_post_g, mix_pre_g, w_in, fox_f_bias, diff_lambda_q1, diff_lambda_k1, diff_lambda_q2, diff_lambda_k2, diff_head_g, mem_norm_g, w_mem_kv, w_branch_fox, w_branch_diff, w_branch_mem, w_merge_gate, b_merge_gate, w_out, mix_post_g, ffn2_pre_g, ffn2_w_gate, ffn2_w_up, ffn2_w_down, ffn2_post_g):
    batch, seq, d = x.shape
    depth = w_in.shape[0]
    xt = x.reshape(batch * seq, d)
    cos_t, sa_t, sb_t = _rope_tables(seq)

    def row(v):
        return v.reshape(1, -1).astype(F32)

    for l in range(depth):
        first, w1_gate, w1_up, w1_down = _ffn_first(
            xt, row(ffn1_pre_g[l]), ffn1_w_gate, ffn1_w_up, ffn1_w_down, row(ffn1_post_g[l]), l,
            tm=FIRST_ROWS)
        w_in_t = w_in[l].T
        xt, (w_head, w_tail) = _ffn(
            xt, row(ffn1_pre_g[l]), w1_gate, w1_up, w1_down, row(ffn1_post_g[l]), first_rows=first,
            cast_t=[(w_in_t, 0, FF_COL + LANES), (w_in_t, FF_COL + FOX_HEADS, w_in.shape[2])])

        small = [w_out, w_branch_fox, w_branch_diff, w_branch_mem]
        proj, ff, h_mix, (wb_out, wb_fox, wb_diff, wb_mem) = _mix_proj(
            xt, row(mix_pre_g[l]), w_head, w_tail, cos_t, sa_t, sb_t, seq=seq,
            cast=[(w, l) for w in small])

        bias = jnp.pad(fox_f_bias[l].astype(F32), (0, LANES - FOX_HEADS)).reshape(1, LANES)
        fp, frow = _fox_gate(ff, bias, batch=batch, seq=seq)
        y_fox, (w2_gate, w2_up, w2_down) = _fox_attn(
            proj, fp, frow, batch=batch, seq=seq,
            cast=[(w, l) for w in (ffn2_w_gate, ffn2_w_up, ffn2_w_down)])

        lam_init = 0.8 - 0.6 * math.exp(-0.3 * l)
        lamv = jnp.stack([diff_lambda_q1[l], diff_lambda_k1[l], diff_lambda_q2[l],
                          diff_lambda_k2[l]]).astype(F32)
        gate_order = _gate_tile_order(d)
        y_diff, (wb_merge,) = _diff_attn(proj, lamv, row(diff_head_g[l]), batch=batch, seq=seq,
                                         lam_init=lam_init, cast=[(w_merge_gate, l)],
                                         cast_moves=gate_order)
        b_gate = row(b_merge_gate[l])
        b_gate = jnp.concatenate([b_gate[:, src:src + w] for _, src, w in gate_order], axis=1)

        mkv = _mem_kv(mem, row(mem_norm_g[l]), w_mem_kv[l].astype(BF16))
        y_mem = _mem_attn(proj, mkv, batch=batch, seq=seq)

        xt = _merge(xt, h_mix, y_fox, y_diff, y_mem, wb_merge, b_gate, wb_fox, wb_diff, wb_mem,
                    wb_out, row(mix_post_g[l]))

        xt, _ = _ffn(xt, row(ffn2_pre_g[l]), w2_gate, w2_up, w2_down, row(ffn2_post_g[l]))

    return xt.reshape(batch, seq, d)
```

```python
import functools
import math

import jax
import jax.numpy as jnp
from jax import lax
from jax.experimental import pallas as pl
from jax.experimental.pallas import tpu as pltpu

D_MODEL = 2048
CHUNK = 64
EPS = 1e-6
ROPE_THETA = 500000.0

FOX_HEADS = 8
FOX_DIM = 128
FOX_W = FOX_HEADS * FOX_DIM

DIFF_HEADS = 4
DIFF_QK_DIM = 64
DIFF_V_DIM = 2 * DIFF_QK_DIM
DIFF_QK_W = DIFF_HEADS * 2 * DIFF_QK_DIM
DIFF_V_W = DIFF_HEADS * DIFF_V_DIM
ROPE_DIM = DIFF_QK_DIM // 4

MEM_HEADS = 4
MEM_DIM = 128
MEM_W = MEM_HEADS * MEM_DIM

LANES = 128
F32_ROWS = 8
BF16_ROWS = 16
RMS_ROWS_IN_FLIGHT = 256
MERGE_TN = 512
FIRST_ROWS = 1024
IN_PLACE_ROWS = 64
PROJ_W = 3 * FOX_W + 2 * DIFF_QK_W + DIFF_V_W + MEM_W
FQ_BLK, FK_BLK, FV_BLK = 0, FOX_HEADS, 2 * FOX_HEADS
DQ_BLK = 3 * FOX_HEADS
DK_BLK = DQ_BLK + DIFF_HEADS
DV_BLK = DK_BLK + DIFF_HEADS
MQ_COL = 3 * FOX_W + 2 * DIFF_QK_W + DIFF_V_W
FF_COL = 3 * FOX_W

VMEM_LIMIT = 56 * 1024 * 1024
BF16 = jnp.bfloat16
F32 = jnp.float32
NEG_INF = float("-inf")
LOG2E = math.log2(math.e)
FOX_QSCALE = FOX_DIM ** -0.5 * LOG2E
DIFF_QSCALE = DIFF_QK_DIM ** -0.5 * LOG2E
MEM_QSCALE = MEM_DIM ** -0.5 * LOG2E


def _dot(a, b):
    return jnp.dot(a, b, preferred_element_type=F32)


def _dot_nt(a, b):
    return lax.dot_general(a, b, (((1,), (1,)), ((), ())), preferred_element_type=F32)


def _rms(x, g):
    return x * lax.rsqrt(jnp.mean(x * x, axis=-1, keepdims=True) + EPS) * g


def _rms_rows(src_ref, g, emit, rows):
    def body(r, carry):
        sl = pl.ds(pl.multiple_of(r * rows, rows), rows)
        emit(sl, _rms(src_ref[sl, :], g))
        return carry

    lax.fori_loop(0, src_ref.shape[0] // rows, body, 0, unroll=RMS_ROWS_IN_FLIGHT // rows)


def _params(*sem):
    return pltpu.CompilerParams(dimension_semantics=sem, vmem_limit_bytes=VMEM_LIMIT)


def _slab_cast_specs(cast, n_steps, step_index):
    in_specs, out_specs, out_shape = [], [], []
    for arr, layer in cast:
        _, rows, cols = arr.shape
        br = rows // n_steps
        assert br * n_steps == rows and br % BF16_ROWS == 0
        in_specs.append(pl.BlockSpec((None, br, cols),
                                     lambda *g, layer=layer: (layer, step_index(*g), 0)))
        out_specs.append(pl.BlockSpec((br, cols), lambda *g: (step_index(*g), 0)))
        out_shape.append(jax.ShapeDtypeStruct((rows, cols), BF16))
    return in_specs, out_specs, out_shape


def _run_slab_casts(src_refs, dst_refs, moves=None):
    for src_ref, dst_ref in zip(src_refs, dst_refs):
        if moves is None:
            dst_ref[...] = src_ref[...].astype(BF16)
        else:
            for dst_lo, src_lo, width in moves:
                dst_ref[:, dst_lo:dst_lo + width] = src_ref[:, src_lo:src_lo + width].astype(BF16)


def _ffn_step(f, last, x_ref, gpre_ref, gpost_ref, o_ref, h_ref, acc_ref, w_gate, w_up, w_down,
              side_jobs):
    @pl.when(f == 0)
    def _():
        def emit(sl, y):
            h_ref[sl, :] = y.astype(BF16)
            acc_ref[sl, :] = jnp.zeros_like(y)

        _rms_rows(x_ref, gpre_ref[...], emit, BF16_ROWS)

    h = h_ref[...]
    side_jobs(0)
    g = _dot(h, w_gate())
    side_jobs(1)
    u = _dot(h, w_up())
    side_jobs(2)
    a = (g * (1.0 / (1.0 + jnp.exp(-g)))) * u
    acc_ref[...] += _dot(a.astype(BF16), w_down())

    @pl.when(f == last)
    def _():
        def emit(sl, y):
            o_ref[sl, :] = x_ref[sl, :] + y

        g_half = 0.5 * gpost_ref[...]
        if acc_ref is o_ref:
            for r in range(0, o_ref.shape[0], IN_PLACE_ROWS):
                sl = slice(r, r + IN_PLACE_ROWS)
                emit(sl, _rms(acc_ref[sl, :], g_half))
        else:
            _rms_rows(acc_ref, g_half, emit, F32_ROWS)


def _ffn_first_kernel(x_ref, gpre_ref, wg_ref, wu_ref, wd_ref, gpost_ref,
                      o_ref, wgb_ref, wub_ref, wdb_ref, h_ref):
    for src_ref, dst_ref in ((wg_ref, wgb_ref), (wu_ref, wub_ref), (wd_ref, wdb_ref)):
        dst_ref[...] = src_ref[...].astype(BF16)
    _ffn_step(pl.program_id(0), pl.num_programs(0) - 1, x_ref, gpre_ref, gpost_ref, o_ref, h_ref,
              o_ref, lambda: wgb_ref[...], lambda: wub_ref[...], lambda: wdb_ref[...],
              lambda part: None)


def _ffn_first(x, g_pre, w_gate, w_up, w_down, g_post, layer, *, tm, tf=256):
    _, d = x.shape
    d_ff = w_down.shape[1]
    return pl.pallas_call(
        _ffn_first_kernel,
        out_shape=(jax.ShapeDtypeStruct((tm, d), F32),
                   jax.ShapeDtypeStruct((d, d_ff), BF16), jax.ShapeDtypeStruct((d, d_ff), BF16),
                   jax.ShapeDtypeStruct((d_ff, d), BF16)),
        grid=(d_ff // tf,),
        in_specs=[
            pl.BlockSpec((tm, d), lambda f: (0, 0), pipeline_mode=pl.Buffered(1)),
            pl.BlockSpec((1, d), lambda f: (0, 0)),
            pl.BlockSpec((None, d, tf), lambda f: (layer, 0, f)),
            pl.BlockSpec((None, d, tf), lambda f: (layer, 0, f)),
            pl.BlockSpec((None, tf, d), lambda f: (layer, f, 0)),
            pl.BlockSpec((1, d), lambda f: (0, 0)),
        ],
        out_specs=(pl.BlockSpec((tm, d), lambda f: (0, 0)),
                   pl.BlockSpec((d, tf), lambda f: (0, f)), pl.BlockSpec((d, tf), lambda f: (0, f)),
                   pl.BlockSpec((tf, d), lambda f: (f, 0))),
        scratch_shapes=[pltpu.VMEM((tm, d), BF16)],
        compiler_params=_params("arbitrary"),
        name="ffn_first",
    )(x, g_pre, w_gate, w_up, w_down, g_post)


def _ffn_kernel(*refs, n_flip, skip):
    x_ref, gpre_ref, wg_ref, wu_ref, wd_ref, gpost_ref = refs[:6]
    n_in = 6 + (skip > 0)
    o_ref = refs[n_in + n_flip]
    h_ref, acc_ref = refs[-2:]
    m, f = pl.program_id(0), pl.program_id(1)
    last = pl.num_programs(1) - 1

    jobs = list(zip(refs[n_in:n_in + n_flip], refs[n_in + n_flip + 1:n_in + 2 * n_flip + 1]))

    def side_jobs(part, parts=3):
        for src_ref, dst_ref in jobs[part::parts]:
            dst_ref[...] = src_ref[...].T.astype(BF16)

    def compute():
        _ffn_step(f, last, x_ref, gpre_ref, gpost_ref, o_ref, h_ref, acc_ref,
                  lambda: wg_ref[...], lambda: wu_ref[...], lambda: wd_ref[...], side_jobs)

    if skip:
        pl.when(m >= skip)(compute)

        @pl.when(jnp.logical_and(m < skip, f == last))
        def _():
            o_ref[...] = refs[6][...]
    else:
        compute()


def _ffn(x, g_pre, w_gate, w_up, w_down, g_post, *, first_rows=None, cast_t=(), tm=512, tf=512):
    t, d = x.shape
    d_ff = w_down.shape[0]
    m_tiles, f_steps = t // tm, d_ff // tf
    skip = 0 if first_rows is None else first_rows.shape[0] // tm

    def tile(f, m):
        return f if skip == 0 else jnp.where(m < skip, 0, f)

    in_specs = [
        pl.BlockSpec((tm, d), lambda m, f: (jnp.maximum(m, skip), 0)),
        pl.BlockSpec((1, d), lambda m, f: (0, 0)),
        pl.BlockSpec((d, tf), lambda m, f: (0, tile(f, m))),
        pl.BlockSpec((d, tf), lambda m, f: (0, tile(f, m))),
        pl.BlockSpec((tf, d), lambda m, f: (tile(f, m), 0)),
        pl.BlockSpec((1, d), lambda m, f: (0, 0)),
    ]
    extra = []
    if first_rows is not None:
        assert first_rows.shape == (skip * tm, d)
        in_specs.append(pl.BlockSpec((tm, d), lambda m, f: (jnp.minimum(m, skip - 1), 0),
                                     pipeline_mode=pl.Buffered(1)))
        extra.append(first_rows)
    out_specs = [pl.BlockSpec((tm, d), lambda m, f: (m, 0))]
    out_shape = [jax.ShapeDtypeStruct((t, d), F32)]
    for arr, lo, hi in cast_t:
        _, cols = arr.shape
        n_blk = (hi - lo) // LANES
        assert n_blk * LANES == hi - lo and n_blk <= (m_tiles - skip) * f_steps

        def block(m, f, n_blk=n_blk):
            return jnp.clip((m - skip) * f_steps + f, 0, n_blk - 1)

        in_specs.append(pl.BlockSpec(
            (pl.Element(LANES), pl.Element(cols)),
            lambda m, f, block=block, lo=lo: (
                pl.multiple_of(lo + LANES * block(m, f), math.gcd(lo, LANES)), 0)))
        out_specs.append(pl.BlockSpec((cols, LANES), lambda m, f, block=block: (0, block(m, f))))
        out_shape.append(jax.ShapeDtypeStruct((cols, hi - lo), BF16))
    outs = pl.pallas_call(
        functools.partial(_ffn_kernel, n_flip=len(cast_t), skip=skip),
        out_shape=out_shape,
        grid=(m_tiles, f_steps),
        in_specs=in_specs,
        out_specs=out_specs,
        scratch_shapes=[pltpu.VMEM((tm, d), BF16), pltpu.VMEM((tm, d), F32)],
        compiler_params=_params("parallel", "arbitrary"),
        name="ffn",
    )(x, g_pre, w_gate, w_up, w_down, g_post, *extra, *[c[0] for c in cast_t])
    return outs[0], outs[1:]


def _proj_kernel(*refs, tn, n_cast):
    x_ref, g_ref, wa_ref, wff_ref, wb_ref, cos_ref, sa_ref, sb_ref = refs[:8]
    o_ref, ff_ref, h_ref = refs[8 + n_cast:11 + n_cast]
    _run_slab_casts(refs[8:8 + n_cast], refs[11 + n_cast:])
    h_ref[...] = _rms(x_ref[...], g_ref[...]).astype(BF16)
    ff_ref[...] = _dot(h_ref[...], wff_ref[...])
    half = ROPE_DIM // 2
    na = wa_ref.shape[1] // tn
    for n in range(na + wb_ref.shape[1] // tn):
        w_tile = (wa_ref[:, n * tn:(n + 1) * tn] if n < na
                  else wb_ref[:, (n - na) * tn:(n - na + 1) * tn])
        y = _dot(h_ref[...], w_tile)
        first_blk = n * tn // LANES
        if first_blk < FK_BLK:
            y = y * FOX_QSCALE
        if first_blk >= MQ_COL // LANES:
            y = y * MEM_QSCALE
        if DQ_BLK <= first_blk < DV_BLK:
            qscale = DIFF_QSCALE if first_blk < DK_BLK else 1.0
            for j in range(tn // LANES):
                blk = y[:, j * LANES:(j + 1) * LANES]
                rot = (blk * cos_ref[...] + pltpu.roll(blk, LANES - half, axis=1) * sa_ref[...]
                       + pltpu.roll(blk, half, axis=1) * sb_ref[...])
                o_ref[:, n * tn + j * LANES:n * tn + (j + 1) * LANES] = (rot * qscale).astype(BF16)
        else:
            o_ref[:, n * tn:(n + 1) * tn] = y.astype(BF16)


def _mix_proj(x, g, w_head, w_tail, cos_t, sa_t, sb_t, *, seq, cast=(), tm=512, tn=512):
    t, d = x.shape
    n_out = FF_COL + w_tail.shape[1]
    assert all((blk * LANES) % tn == 0 for blk in (FK_BLK, DQ_BLK, DK_BLK, DV_BLK))
    assert w_head.shape[1] == FF_COL + LANES and n_out == PROJ_W
    s_tiles = seq // tm
    m_tiles = t // tm
    resident = dict(pipeline_mode=pl.Buffered(1))
    in_specs = [
        pl.BlockSpec((tm, d), lambda m: (m, 0)),
        pl.BlockSpec((1, d), lambda m: (0, 0)),
        pl.BlockSpec((d, FF_COL), lambda m: (0, 0), **resident),
        pl.BlockSpec((d, LANES), lambda m: (0, FF_COL // LANES), **resident),
        pl.BlockSpec((d, w_tail.shape[1]), lambda m: (0, 0), **resident),
        pl.BlockSpec((tm, LANES), lambda m: (m % s_tiles, 0)),
        pl.BlockSpec((tm, LANES), lambda m: (m % s_tiles, 0)),
        pl.BlockSpec((tm, LANES), lambda m: (m % s_tiles, 0)),
    ]
    out_specs = [pl.BlockSpec((tm, n_out), lambda m: (m, 0)),
                 pl.BlockSpec((tm, LANES), lambda m: (m, 0)),
                 pl.BlockSpec((tm, d), lambda m: (m, 0))]
    out_shape = [jax.ShapeDtypeStruct((t, n_out), BF16), jax.ShapeDtypeStruct((t, LANES), F32),
                 jax.ShapeDtypeStruct((t, d), BF16)]
    cast_in, cast_out, cast_shape = _slab_cast_specs(cast, m_tiles, lambda m: m)
    in_specs, out_specs, out_shape = in_specs + cast_in, out_specs + cast_out, out_shape + cast_shape
    outs = pl.pallas_call(
        functools.partial(_proj_kernel, tn=tn, n_cast=len(cast)),
        out_shape=out_shape,
        grid=(m_tiles,),
        in_specs=in_specs,
        out_specs=out_specs,
        compiler_params=_params("parallel"),
        name="mix_proj",
    )(x, g, w_head, w_head, w_tail, cos_t, sa_t, sb_t, *[arr for arr, _ in cast])
    return outs[0], outs[1], outs[2], outs[3:]


def _rope_tables(seq):
    half = ROPE_DIM // 2
    pos = jnp.arange(seq, dtype=F32)
    inv_freq = ROPE_THETA ** (-jnp.arange(0, ROPE_DIM, 2, dtype=F32) / ROPE_DIM)
    ang = pos[:, None] * inv_freq[None, :]
    cos, sin = jnp.cos(ang), jnp.sin(ang)
    ones = jnp.ones((seq, DIFF_QK_DIM - ROPE_DIM), F32)
    zeros_h = jnp.zeros((seq, half), F32)
    zeros_r = jnp.zeros((seq, DIFF_QK_DIM - ROPE_DIM), F32)
    cos_m = jnp.concatenate([cos, cos, ones], axis=1)
    sa_m = jnp.concatenate([-sin, zeros_h, zeros_r], axis=1)
    sb_m = jnp.concatenate([zeros_h, sin, zeros_r], axis=1)
    rep = LANES // DIFF_QK_DIM
    return (jnp.tile(cos_m, (1, rep)), jnp.tile(sa_m, (1, rep)), jnp.tile(sb_m, (1, rep)))


def _fgate_kernel(ff_ref, bias_ref, fp_ref, frow_ref, fcol_ref, *, cb):
    seq = ff_ref.shape[0]
    head_lane = lax.broadcasted_iota(jnp.int32, (cb, LANES), 1) < FOX_HEADS
    z = ff_ref[...] + bias_ref[...]
    lf = jnp.minimum(z, 0.0) - jnp.log1p(jnp.exp(-jnp.abs(z)))
    r = lax.broadcasted_iota(jnp.int32, (cb, cb), 0)
    c = lax.broadcasted_iota(jnp.int32, (cb, cb), 1)
    tri = (r >= c).astype(F32)
    carry = jnp.zeros((1, LANES), F32)
    for i in range(seq // cb):
        cs = jnp.dot(tri, lf[i * cb:(i + 1) * cb], precision=lax.Precision.HIGHEST,
                     preferred_element_type=F32) + carry
        carry = cs[cb - 1:cb, :]
        cs2 = jnp.where(head_lane, cs * LOG2E, 0.0)
        fcol_ref[i * cb:(i + 1) * cb, :] = cs2
        hi = cs2.astype(BF16).astype(F32)
        mid = (cs2 - hi).astype(BF16).astype(F32)
        lo = (cs2 - hi - mid).astype(BF16).astype(F32)
        pieces = hi + pltpu.roll(mid, FOX_HEADS, axis=1) + pltpu.roll(lo, 2 * FOX_HEADS, axis=1)
        fp_ref[i * cb:(i + 1) * cb, :] = pieces.astype(BF16)
    frow_ref[...] = fcol_ref[...].T[:FOX_HEADS]


def _fox_gate(ff, bias, *, batch, seq, cb=256):
    ff = ff.reshape(batch, seq, LANES)
    return pl.pallas_call(
        functools.partial(_fgate_kernel, cb=cb),
        out_shape=(jax.ShapeDtypeStruct((batch, seq, LANES), BF16),
                   jax.ShapeDtypeStruct((batch, FOX_HEADS, seq), F32)),
        grid=(batch,),
        in_specs=[pl.BlockSpec((None, seq, LANES), lambda b: (b, 0, 0)),
                  pl.BlockSpec((1, LANES), lambda b: (0, 0))],
        out_specs=(pl.BlockSpec((None, seq, LANES), lambda b: (b, 0, 0)),
                   pl.BlockSpec((None, FOX_HEADS, seq), lambda b: (b, 0, 0))),
        scratch_shapes=[pltpu.VMEM((seq, LANES), F32)],
        compiler_params=_params("parallel"),
        name="fox_gate",
    )(ff, bias)


def _dot_tn(a, b):
    return lax.dot_general(a, b, (((0,), (0,)), ((), ())), preferred_element_type=F32)


def _flash_attend(i, heads, logits, values, ft2, visible, finish, side_job,
                  sa_ref, sb_ref, m_ref, l_ref, acc_ref):
    def step(h, s, j):
        m = m_ref[h]
        m_new = jnp.maximum(m, jnp.max(s, axis=0, keepdims=True) + ft2[h])
        alpha = jnp.exp2(m - m_new)
        p = jnp.exp2(s - (m_new - ft2[h]))
        m_ref[h] = m_new
        l_ref[h] = alpha * l_ref[h] + jnp.sum(p, axis=0, keepdims=True)
        acc_ref[h] = alpha * acc_ref[h] + _dot_tn(values(h, j), p.astype(BF16))

    for h in heads:
        m_ref[h] = jnp.full(m_ref.shape[1:], NEG_INF, F32)
        l_ref[h] = jnp.zeros(l_ref.shape[1:], F32)
        acc_ref[h] = jnp.zeros(acc_ref.shape[1:], F32)
        sa_ref[h] = logits(h, 0)
    side_job()

    def pair(jj, carry):
        j = 2 * jj
        for h in heads:
            sb_ref[h] = logits(h, j + 1)
            step(h, sa_ref[h], j)
        for h in heads:
            sa_ref[h] = logits(h, j + 2)
            step(h, sb_ref[h], j + 1)
        return carry

    lax.fori_loop(0, i // 2, pair, 0)

    def last(s_ref):
        for h in heads:
            step(h, jnp.where(visible, s_ref[h], NEG_INF), i)
            finish(h, acc_ref[h] / l_ref[h])

    @pl.when(i % 2 == 0)
    def _():
        last(sa_ref)

    @pl.when(i % 2 == 1)
    def _():
        for h in heads:
            sb_ref[h] = logits(h, i)
            step(h, sa_ref[h], i - 1)
        last(sb_ref)


def _flash_scratch(hb, width, tq, cols):
    return [pltpu.VMEM((hb, tq, cols), F32), pltpu.VMEM((hb, tq, cols), F32),
            pltpu.VMEM((hb, 1, cols), F32), pltpu.VMEM((hb, 1, cols), F32),
            pltpu.VMEM((hb, width, cols), F32)]


def _fox_kernel(*refs, tq, hb, n_cast):
    q_ref, k_ref, v_ref, fp_ref, fr_ref = refs[:5]
    o_ref = refs[5 + n_cast]
    scratch = refs[-5:]
    side_job = functools.partial(_run_slab_casts, refs[5:5 + n_cast], refs[6 + n_cast:-5])
    i = pl.program_id(2)
    h0 = pl.program_id(1) * hb
    heads = range(hb)
    ft2 = [fr_ref[h, pl.ds(i, 1), :] for h in heads]

    lane = lax.broadcasted_iota(jnp.int32, (tq, LANES), 1)
    q_ext = []
    for h in heads:
        mine = (lane % FOX_HEADS == h0 + h) & (lane < 3 * FOX_HEADS)
        sel = jnp.where(mine, -1.0, 0.0).astype(BF16)
        q_ext.append(jnp.concatenate([q_ref[:, h * FOX_DIM:(h + 1) * FOX_DIM], sel], axis=1))

    def logits(h, j):
        start = pl.multiple_of(j * tq, tq)
        k_ext = jnp.concatenate([k_ref[pl.ds(start, tq), h * FOX_DIM:(h + 1) * FOX_DIM],
                                 fp_ref[pl.ds(start, tq), :]], axis=1)
        return _dot_nt(k_ext, q_ext[h])

    def values(h, j):
        return v_ref[pl.ds(pl.multiple_of(j * tq, tq), tq), h * FOX_DIM:(h + 1) * FOX_DIM]

    def finish(h, o_t):
        o_ref[:, h * FOX_DIM:(h + 1) * FOX_DIM] = o_t.T.astype(BF16)

    krow = lax.broadcasted_iota(jnp.int32, (tq, tq), 0)
    qcol = lax.broadcasted_iota(jnp.int32, (tq, tq), 1)
    _flash_attend(i, heads, logits, values, ft2, krow <= qcol, finish, side_job, *scratch)


def _fox_attn(proj, fp, frow, *, batch, seq, cast=(), tq=256, hb=8):
    proj3 = proj.reshape(batch, seq, PROJ_W)
    n_q = seq // tq
    frow4 = frow.reshape(batch, FOX_HEADS, n_q, tq)
    w = hb * FOX_DIM
    groups = FOX_HEADS // hb
    assert groups == 1 or not cast
    cast_in, cast_out, cast_shape = _slab_cast_specs(cast, batch * n_q, lambda b, g, i: b * n_q + i)
    outs = pl.pallas_call(
        functools.partial(_fox_kernel, tq=tq, hb=hb, n_cast=len(cast)),
        out_shape=[jax.ShapeDtypeStruct((batch, seq, FOX_W), BF16)] + cast_shape,
        grid=(batch, groups, n_q),
        in_specs=[
            pl.BlockSpec((None, tq, w), lambda b, g, i: (b, i, g)),
            pl.BlockSpec((None, seq, w), lambda b, g, i: (b, 0, groups + g)),
            pl.BlockSpec((None, seq, w), lambda b, g, i: (b, 0, 2 * groups + g)),
            pl.BlockSpec((None, seq, LANES), lambda b, g, i: (b, 0, 0)),
            pl.BlockSpec((None, hb, n_q, tq), lambda b, g, i: (b, g, 0, 0)),
        ] + cast_in,
        out_specs=[pl.BlockSpec((None, tq, w), lambda b, g, i: (b, i, g))] + cast_out,
        scratch_shapes=_flash_scratch(hb, FOX_DIM, tq, tq),
        compiler_params=_params("parallel", "parallel", "arbitrary"),
        name="fox_attn",
    )(proj3, proj3, proj3, fp, frow4, *[arr for arr, _ in cast])
    return outs[0].reshape(batch * seq, FOX_W), outs[1:]


def _diff_kernel(*refs, tq, hb, lam_init, n_cast, cast_moves):
    lamv_ref, g_ref, q_ref, k_ref, v_ref = refs[:5]
    o_ref = refs[5 + n_cast]
    scratch = refs[-5:]
    side_job = functools.partial(_run_slab_casts, refs[5:5 + n_cast], refs[6 + n_cast:-5],
                                 cast_moves)
    i = pl.program_id(2)
    heads = range(hb)
    lv = lamv_ref[...]
    lam = (jnp.exp(jnp.sum(lv[0:1] * lv[1:2], axis=1, keepdims=True))
           - jnp.exp(jnp.sum(lv[2:3] * lv[3:4], axis=1, keepdims=True)) + lam_init)

    lane = lax.broadcasted_iota(jnp.int32, (tq, LANES), 1)
    qs = []
    for h in heads:
        q = q_ref[:, h * LANES:(h + 1) * LANES].astype(F32)
        qs.append(jnp.concatenate([jnp.where(lane < DIFF_QK_DIM, q, 0.0),
                                   jnp.where(lane >= DIFF_QK_DIM, q, 0.0)], axis=0).astype(BF16))
    zero = jnp.zeros((1, 2 * tq), F32)

    def logits(h, j):
        start = pl.multiple_of(j * tq, tq)
        return _dot_nt(k_ref[pl.ds(start, tq), h * LANES:(h + 1) * LANES], qs[h])

    def values(h, j):
        return v_ref[pl.ds(pl.multiple_of(j * tq, tq), tq), h * LANES:(h + 1) * LANES]

    def finish(h, o_t):
        yd = (o_t[:, :tq] - lam * o_t[:, tq:]).T
        o_ref[:, h * LANES:(h + 1) * LANES] = (
            _rms(yd, g_ref[...]) * (1.0 - lam_init)).astype(BF16)

    krow = lax.broadcasted_iota(jnp.int32, (tq, 2 * tq), 0)
    qcol = lax.broadcasted_iota(jnp.int32, (tq, 2 * tq), 1)
    qcol = jnp.where(qcol >= tq, qcol - tq, qcol)
    visible = krow // CHUNK <= qcol // CHUNK
    _flash_attend(i, heads, logits, values, [zero] * hb, visible, finish, side_job, *scratch)


def _diff_attn(proj, lamv, g, *, batch, seq, lam_init, cast=(), cast_moves=None, tq=256, hb=4):
    proj3 = proj.reshape(batch, seq, PROJ_W)
    n_q = seq // tq
    w = hb * LANES
    groups = DIFF_HEADS // hb
    assert groups == 1 or not cast
    dq, dk, dv = (DQ_BLK * LANES) // w, (DK_BLK * LANES) // w, (DV_BLK * LANES) // w
    cast_in, cast_out, cast_shape = _slab_cast_specs(cast, batch * n_q, lambda b, g, i: b * n_q + i)
    outs = pl.pallas_call(
        functools.partial(_diff_kernel, tq=tq, hb=hb, lam_init=lam_init, n_cast=len(cast),
                          cast_moves=cast_moves),
        out_shape=[jax.ShapeDtypeStruct((batch, seq, DIFF_V_W), BF16)] + cast_shape,
        grid=(batch, groups, n_q),
        in_specs=[
            pl.BlockSpec((4, DIFF_QK_DIM), lambda b, g, i: (0, 0)),
            pl.BlockSpec((1, DIFF_V_DIM), lambda b, g, i: (0, 0)),
            pl.BlockSpec((None, tq, w), lambda b, g, i: (b, i, dq + g)),
            pl.BlockSpec((None, seq, w), lambda b, g, i: (b, 0, dk + g)),
            pl.BlockSpec((None, seq, w), lambda b, g, i: (b, 0, dv + g)),
        ] + cast_in,
        out_specs=[pl.BlockSpec((None, tq, w), lambda b, g, i: (b, i, g))] + cast_out,
        scratch_shapes=_flash_scratch(hb, DIFF_V_DIM, tq, 2 * tq),
        compiler_params=_params("parallel", "parallel", "arbitrary"),
        name="diff_attn",
    )(lamv, g, proj3, proj3, proj3, *[arr for arr, _ in cast])
    return outs[0].reshape(batch * seq, DIFF_V_W), outs[1:]


def _mem_kv_kernel(mem_ref, g_ref, w_ref, o_ref):
    o_ref[...] = _dot(_rms(mem_ref[...], g_ref[...]).astype(BF16), w_ref[...]).astype(BF16)


def _mem_kv(mem, g, w):
    batch, n_mem, d = mem.shape
    return pl.pallas_call(
        _mem_kv_kernel,
        out_shape=jax.ShapeDtypeStruct((batch, n_mem, 2 * MEM_W), BF16),
        grid=(batch,),
        in_specs=[pl.BlockSpec((None, n_mem, d), lambda b: (b, 0, 0)),
                  pl.BlockSpec((1, d), lambda b: (0, 0)),
                  pl.BlockSpec((d, 2 * MEM_W), lambda b: (0, 0))],
        out_specs=pl.BlockSpec((None, n_mem, 2 * MEM_W), lambda b: (b, 0, 0)),
        compiler_params=_params("parallel"),
        name="mem_kv",
    )(mem, g, w)


def _mem_attn_kernel(q_ref, kv_ref, o_ref):
    logits = [_dot_nt(kv_ref[:, h * MEM_DIM:(h + 1) * MEM_DIM],
                      q_ref[:, h * MEM_DIM:(h + 1) * MEM_DIM]) for h in range(MEM_HEADS)]
    for h in range(MEM_HEADS):
        v = kv_ref[:, MEM_W + h * MEM_DIM:MEM_W + (h + 1) * MEM_DIM]
        s = logits[h]
        p = jnp.exp2(s - jnp.max(s, axis=0, keepdims=True))
        l = jnp.sum(p, axis=0, keepdims=True)
        o_t = _dot_tn(v, p.astype(BF16)) / l
        o_ref[:, h * MEM_DIM:(h + 1) * MEM_DIM] = o_t.T.astype(BF16)


def _mem_attn(proj, mkv, *, batch, seq, tq=1024):
    proj3 = proj.reshape(batch, seq, PROJ_W)
    n_mem = mkv.shape[1]
    out = pl.pallas_call(
        _mem_attn_kernel,
        out_shape=jax.ShapeDtypeStruct((batch, seq, MEM_W), BF16),
        grid=(batch, seq // tq),
        in_specs=[pl.BlockSpec((None, tq, MEM_W), lambda b, i: (b, i, MQ_COL // MEM_W)),
                  pl.BlockSpec((None, n_mem, 2 * MEM_W), lambda b, i: (b, 0, 0))],
        out_specs=pl.BlockSpec((None, tq, MEM_W), lambda b, i: (b, i, 0)),
        compiler_params=_params("parallel", "arbitrary"),
        name="mem_attn",
    )(proj3, mkv)
    return out.reshape(batch * seq, MEM_W)


def _merge_kernel(x_ref, h_ref, yf_ref, yd_ref, ym_ref, wg_ref, bg_ref, wf_ref, wd_ref, wm_ref,
                  wo_ref, gpost_ref, o_ref, acc_ref):
    n = pl.program_id(1)
    tn = wo_ref.shape[0]

    @pl.when(n == 0)
    def _():
        acc_ref[...] = jnp.zeros_like(acc_ref)

    z = _dot(h_ref[...], wg_ref[...]) + bg_ref[...]
    gates = 1.0 / (1.0 + jnp.exp(-z))
    merged = (gates[:, :tn] * _dot(yf_ref[...], wf_ref[...])
              + gates[:, tn:2 * tn] * _dot(yd_ref[...], wd_ref[...])
              + gates[:, 2 * tn:] * _dot(ym_ref[...], wm_ref[...]))
    acc_ref[...] += _dot(merged.astype(BF16), wo_ref[...])

    @pl.when(n == pl.num_programs(1) - 1)
    def _():
        def emit(sl, y):
            o_ref[sl, :] = x_ref[sl, :] + y

        _rms_rows(acc_ref, gpost_ref[...], emit, F32_ROWS)


def _gate_tile_order(d):
    tn = MERGE_TN
    nt = d // tn
    return tuple(((n * 3 + b) * tn, (b * nt + n) * tn, tn) for n in range(nt) for b in range(3))


def _merge(x, h, y_fox, y_diff, y_mem, w_gate, b_gate, w_fox, w_diff, w_mem, w_out, g_post,
           *, tm=512):
    t, d = x.shape
    tn = MERGE_TN
    nt = d // tn
    row = lambda m, n: (m, 0)
    return pl.pallas_call(
        _merge_kernel,
        out_shape=jax.ShapeDtypeStruct((t, d), F32),
        grid=(t // tm, nt),
        in_specs=[
            pl.BlockSpec((tm, d), row),
            pl.BlockSpec((tm, d), row),
            pl.BlockSpec((tm, FOX_W), row),
            pl.BlockSpec((tm, DIFF_V_W), row),
            pl.BlockSpec((tm, MEM_W), row),
            pl.BlockSpec((d, 3 * tn), lambda m, n: (0, n)),
            pl.BlockSpec((1, 3 * tn), lambda m, n: (0, n)),
            pl.BlockSpec((FOX_W, tn), lambda m, n: (0, n)),
            pl.BlockSpec((DIFF_V_W, tn), lambda m, n: (0, n)),
            pl.BlockSpec((MEM_W, tn), lambda m, n: (0, n)),
            pl.BlockSpec((tn, d), lambda m, n: (n, 0)),
            pl.BlockSpec((1, d), lambda m, n: (0, 0)),
        ],
        out_specs=pl.BlockSpec((tm, d), row),
        scratch_shapes=[pltpu.VMEM((tm, d), F32)],
        compiler_params=_params("parallel", "arbitrary"),
        name="merge",
    )(x, h, y_fox, y_diff, y_mem, w_gate, b_gate, w_fox, w_diff, w_mem, w_out, g_post)


def kernel(x, mem, ffn1_pre_g, ffn1_w_gate, ffn1_w_up, ffn1_w_down, ffn1_post_g, mix_pre_g, w_in, fox_f_bias, diff_lambda_q1, diff_lambda_k1, diff_lambda_q2, diff_lambda_k2, diff_head_g, mem_norm_g, w_mem_kv, w_branch_fox, w_branch_diff, w_branch_mem, w_merge_gate, b_merge_gate, w_out, mix_post_g, ffn2_pre_g, ffn2_w_gate, ffn2_w_up, ffn2_w_down, ffn2_post_g):
    batch, seq, d = x.shape
    depth = w_in.shape[0]
    xt = x.reshape(batch * seq, d)
    cos_t, sa_t, sb_t = _rope_tables(seq)

    def row(v):
        return v.reshape(1, -1).astype(F32)

    for l in range(depth):
        first, w1_gate, w1_up, w1_down = _ffn_first(
            xt, row(ffn1_pre_g[l]), ffn1_w_gate, ffn1_w_up, ffn1_w_down, row(ffn1_post_g[l]), l,
            tm=FIRST_ROWS)
        w_in_t = w_in[l].T
        xt, (w_head, w_tail) = _ffn(
            xt, row(ffn1_pre_g[l]), w1_gate, w1_up, w1_down, row(ffn1_post_g[l]), first_rows=first,
            cast_t=[(w_in_t, 0, FF_COL + LANES), (w_in_t, FF_COL + FOX_HEADS, w_in.shape[2])])

        small = [w_out, w_branch_fox, w_branch_diff, w_branch_mem]
        proj, ff, h_mix, (wb_out, wb_fox, wb_diff, wb_mem) = _mix_proj(
            xt, row(mix_pre_g[l]), w_head, w_tail, cos_t, sa_t, sb_t, seq=seq,
            cast=[(w, l) for w in small])

        bias = jnp.pad(fox_f_bias[l].astype(F32), (0, LANES - FOX_HEADS)).reshape(1, LANES)
        fp, frow = _fox_gate(ff, bias, batch=batch, seq=seq)
        y_fox, (w2_gate, w2_up, w2_down) = _fox_attn(
            proj, fp, frow, batch=batch, seq=seq,
            cast=[(w, l) for w in (ffn2_w_gate, ffn2_w_up, ffn2_w_down)])

        lam_init = 0.8 - 0.6 * math.exp(-0.3 * l)
        lamv = jnp.stack([diff_lambda_q1[l], diff_lambda_k1[l], diff_lambda_q2[l],
                          diff_lambda_k2[l]]).astype(F32)
        gate_order = _gate_tile_order(d)
        y_diff, (wb_merge,) = _diff_attn(proj, lamv, row(diff_head_g[l]), batch=batch, seq=seq,
                                         lam_init=lam_init, cast=[(w_merge_gate, l)],
                                         cast_moves=gate_order)
        b_gate = row(b_merge_gate[l])
        b_gate = jnp.concatenate([b_gate[:, src:src + w] for _, src, w in gate_order], axis=1)

        mkv = _mem_kv(mem, row(mem_norm_g[l]), w_mem_kv[l].astype(BF16))
        y_mem = _mem_attn(proj, mkv, batch=batch, seq=seq)

        xt = _merge(xt, h_mix, y_fox, y_diff, y_mem, wb_merge, b_gate, wb_fox, wb_diff, wb_mem,
                    wb_out, row(mix_post_g[l]))

        xt, _ = _ffn(xt, row(ffn2_pre_g[l]), w2_gate, w2_up, w2_down, row(ffn2_post_g[l]))

    return xt.reshape(batch, seq, d)
```

```python
import functools
import math

import jax
import jax.numpy as jnp
from jax import lax
from jax.experimental import pallas as pl
from jax.experimental.pallas import tpu as pltpu

CHUNK = 64
EPS = 1e-6
ROPE_THETA = 500000.0

FOX_HEADS = 8
FOX_DIM = 128
FOX_W = FOX_HEADS * FOX_DIM

DIFF_HEADS = 4
DIFF_QK_DIM = 64
DIFF_V_DIM = 2 * DIFF_QK_DIM
DIFF_QK_W = DIFF_HEADS * 2 * DIFF_QK_DIM
DIFF_V_W = DIFF_HEADS * DIFF_V_DIM
ROPE_DIM = DIFF_QK_DIM // 4

MEM_HEADS = 4
MEM_DIM = 128
MEM_W = MEM_HEADS * MEM_DIM

LANES = 128
F32_ROWS = 8
BF16_ROWS = 16
RMS_ROWS_IN_FLIGHT = 256
MERGE_TN = 512
FIRST_ROWS = 1024
IN_PLACE_ROWS = 64
PROJ_W = 3 * FOX_W + 2 * DIFF_QK_W + DIFF_V_W + MEM_W
FQ_BLK, FK_BLK, FV_BLK = 0, FOX_HEADS, 2 * FOX_HEADS
DQ_BLK = 3 * FOX_HEADS
DK_BLK = DQ_BLK + DIFF_HEADS
DV_BLK = DK_BLK + DIFF_HEADS
MQ_COL = 3 * FOX_W + 2 * DIFF_QK_W + DIFF_V_W
FF_COL = 3 * FOX_W

VMEM_LIMIT = 56 * 1024 * 1024
BF16 = jnp.bfloat16
F32 = jnp.float32
NEG_INF = float("-inf")
LOG2E = math.log2(math.e)
FOX_QSCALE = FOX_DIM ** -0.5 * LOG2E
DIFF_QSCALE = DIFF_QK_DIM ** -0.5 * LOG2E
MEM_QSCALE = MEM_DIM ** -0.5 * LOG2E


def _dot(a, b):
    return jnp.dot(a, b, preferred_element_type=F32)


def _dot_nt(a, b):
    return lax.dot_general(a, b, (((1,), (1,)), ((), ())), preferred_element_type=F32)


def _rms(x, g):
    return x * lax.rsqrt(jnp.mean(x * x, axis=-1, keepdims=True) + EPS) * g


def _rms_rows(src_ref, g, emit, rows):
    def body(r, carry):
        sl = pl.ds(pl.multiple_of(r * rows, rows), rows)
        emit(sl, _rms(src_ref[sl, :], g))
        return carry

    lax.fori_loop(0, src_ref.shape[0] // rows, body, 0, unroll=RMS_ROWS_IN_FLIGHT // rows)


def _params(*sem):
    return pltpu.CompilerParams(dimension_semantics=sem, vmem_limit_bytes=VMEM_LIMIT)


def _slab_cast_specs(cast, n_steps, step_index):
    in_specs, out_specs, out_shape = [], [], []
    for arr, layer in cast:
        _, rows, cols = arr.shape
        br = rows // n_steps
        assert br * n_steps == rows and br % BF16_ROWS == 0
        in_specs.append(pl.BlockSpec((None, br, cols),
                                     lambda *g, layer=layer: (layer, step_index(*g), 0)))
        out_specs.append(pl.BlockSpec((br, cols), lambda *g: (step_index(*g), 0)))
        out_shape.append(jax.ShapeDtypeStruct((rows, cols), BF16))
    return in_specs, out_specs, out_shape


def _run_slab_casts(src_refs, dst_refs, moves=None):
    for src_ref, dst_ref in zip(src_refs, dst_refs):
        if moves is None:
            dst_ref[...] = src_ref[...].astype(BF16)
        else:
            for dst_lo, src_lo, width in moves:
                dst_ref[:, dst_lo:dst_lo + width] = src_ref[:, src_lo:src_lo + width].astype(BF16)


def _ffn_step(f, last, x_ref, gpre_ref, gpost_ref, o_ref, h_ref, acc_ref, w_gate, w_up, w_down,
              side_jobs):
    @pl.when(f == 0)
    def _():
        def emit(sl, y):
            h_ref[sl, :] = y.astype(BF16)
            acc_ref[sl, :] = jnp.zeros_like(y)

        _rms_rows(x_ref, gpre_ref[...], emit, BF16_ROWS)

    h = h_ref[...]
    side_jobs(0)
    g = _dot(h, w_gate())
    side_jobs(1)
    u = _dot(h, w_up())
    side_jobs(2)
    a = (g * (1.0 / (1.0 + jnp.exp(-g)))) * u
    acc_ref[...] += _dot(a.astype(BF16), w_down())

    @pl.when(f == last)
    def _():
        def emit(sl, y):
            o_ref[sl, :] = x_ref[sl, :] + y

        g_half = 0.5 * gpost_ref[...]
        if acc_ref is o_ref:
            for r in range(0, o_ref.shape[0], IN_PLACE_ROWS):
                sl = slice(r, r + IN_PLACE_ROWS)
                emit(sl, _rms(acc_ref[sl, :], g_half))
        else:
            _rms_rows(acc_ref, g_half, emit, F32_ROWS)


def _ffn_first_kernel(x_ref, gpre_ref, wg_ref, wu_ref, wd_ref, gpost_ref,
                      o_ref, wgb_ref, wub_ref, wdb_ref, h_ref):
    for src_ref, dst_ref in ((wg_ref, wgb_ref), (wu_ref, wub_ref), (wd_ref, wdb_ref)):
        dst_ref[...] = src_ref[...].astype(BF16)
    _ffn_step(pl.program_id(0), pl.num_programs(0) - 1, x_ref, gpre_ref, gpost_ref, o_ref, h_ref,
              o_ref, lambda: wgb_ref[...], lambda: wub_ref[...], lambda: wdb_ref[...],
              lambda part: None)


def _ffn_first(x, g_pre, w_gate, w_up, w_down, g_post, layer, *, tm, tf=256):
    _, d = x.shape
    d_ff = w_down.shape[1]
    return pl.pallas_call(
        _ffn_first_kernel,
        out_shape=(jax.ShapeDtypeStruct((tm, d), F32),
                   jax.ShapeDtypeStruct((d, d_ff), BF16), jax.ShapeDtypeStruct((d, d_ff), BF16),
                   jax.ShapeDtypeStruct((d_ff, d), BF16)),
        grid=(d_ff // tf,),
        in_specs=[
            pl.BlockSpec((tm, d), lambda f: (0, 0), pipeline_mode=pl.Buffered(1)),
            pl.BlockSpec((1, d), lambda f: (0, 0)),
            pl.BlockSpec((None, d, tf), lambda f: (layer, 0, f)),
            pl.BlockSpec((None, d, tf), lambda f: (layer, 0, f)),
            pl.BlockSpec((None, tf, d), lambda f: (layer, f, 0)),
            pl.BlockSpec((1, d), lambda f: (0, 0)),
        ],
        out_specs=(pl.BlockSpec((tm, d), lambda f: (0, 0)),
                   pl.BlockSpec((d, tf), lambda f: (0, f)), pl.BlockSpec((d, tf), lambda f: (0, f)),
                   pl.BlockSpec((tf, d), lambda f: (f, 0))),
        scratch_shapes=[pltpu.VMEM((tm, d), BF16)],
        compiler_params=_params("arbitrary"),
        name="ffn_first",
    )(x, g_pre, w_gate, w_up, w_down, g_post)


def _ffn_kernel(*refs, n_flip, skip):
    x_ref, gpre_ref, wg_ref, wu_ref, wd_ref, gpost_ref = refs[:6]
    n_in = 6 + (skip > 0)
    o_ref = refs[n_in + n_flip]
    h_ref, acc_ref = refs[-2:]
    m, f = pl.program_id(0), pl.program_id(1)
    last = pl.num_programs(1) - 1

    jobs = list(zip(refs[n_in:n_in + n_flip], refs[n_in + n_flip + 1:n_in + 2 * n_flip + 1]))

    def side_jobs(part, parts=3):
        for src_ref, dst_ref in jobs[part::parts]:
            dst_ref[...] = src_ref[...].T.astype(BF16)

    def compute():
        _ffn_step(f, last, x_ref, gpre_ref, gpost_ref, o_ref, h_ref, acc_ref,
                  lambda: wg_ref[...], lambda: wu_ref[...], lambda: wd_ref[...], side_jobs)

    if skip:
        pl.when(m >= skip)(compute)

        @pl.when(jnp.logical_and(m < skip, f == last))
        def _():
            o_ref[...] = refs[6][...]
    else:
        compute()


def _ffn(x, g_pre, w_gate, w_up, w_down, g_post, *, first_rows=None, cast_t=(), tm=512, tf=512):
    t, d = x.shape
    d_ff = w_down.shape[0]
    m_tiles, f_steps = t // tm, d_ff // tf
    skip = 0 if first_rows is None else first_rows.shape[0] // tm

    def tile(f, m):
        return f if skip == 0 else jnp.where(m < skip, 0, f)

    in_specs = [
        pl.BlockSpec((tm, d), lambda m, f: (jnp.maximum(m, skip), 0)),
        pl.BlockSpec((1, d), lambda m, f: (0, 0)),
        pl.BlockSpec((d, tf), lambda m, f: (0, tile(f, m))),
        pl.BlockSpec((d, tf), lambda m, f: (0, tile(f, m))),
        pl.BlockSpec((tf, d), lambda m, f: (tile(f, m), 0)),
        pl.BlockSpec((1, d), lambda m, f: (0, 0)),
    ]
    extra = []
    if first_rows is not None:
        assert first_rows.shape == (skip * tm, d)
        in_specs.append(pl.BlockSpec((tm, d), lambda m, f: (jnp.minimum(m, skip - 1), 0),
                                     pipeline_mode=pl.Buffered(1)))
        extra.append(first_rows)
    out_specs = [pl.BlockSpec((tm, d), lambda m, f: (m, 0))]
    out_shape = [jax.ShapeDtypeStruct((t, d), F32)]
    for arr, lo, hi in cast_t:
        _, cols = arr.shape
        n_blk = (hi - lo) // LANES
        assert n_blk * LANES == hi - lo and n_blk <= (m_tiles - skip) * f_steps

        def block(m, f, n_blk=n_blk):
            return jnp.clip((m - skip) * f_steps + f, 0, n_blk - 1)

        in_specs.append(pl.BlockSpec(
            (pl.Element(LANES), pl.Element(cols)),
            lambda m, f, block=block, lo=lo: (
                pl.multiple_of(lo + LANES * block(m, f), math.gcd(lo, LANES)), 0)))
        out_specs.append(pl.BlockSpec((cols, LANES), lambda m, f, block=block: (0, block(m, f))))
        out_shape.append(jax.ShapeDtypeStruct((cols, hi - lo), BF16))
    outs = pl.pallas_call(
        functools.partial(_ffn_kernel, n_flip=len(cast_t), skip=skip),
        out_shape=out_shape,
        grid=(m_tiles, f_steps),
        in_specs=in_specs,
        out_specs=out_specs,
        scratch_shapes=[pltpu.VMEM((tm, d), BF16), pltpu.VMEM((tm, d), F32)],
        compiler_params=_params("parallel", "arbitrary"),
        name="ffn",
    )(x, g_pre, w_gate, w_up, w_down, g_post, *extra, *[c[0] for c in cast_t])
    return outs[0], outs[1:]


def _proj_kernel(*refs, tn, n_cast):
    x_ref, g_ref, wa_ref, wff_ref, wb_ref, cos_ref, sa_ref, sb_ref = refs[:8]
    o_ref, ff_ref, h_ref = refs[8 + n_cast:11 + n_cast]
    _run_slab_casts(refs[8:8 + n_cast], refs[11 + n_cast:])
    h_ref[...] = _rms(x_ref[...], g_ref[...]).astype(BF16)
    ff_ref[...] = _dot(h_ref[...], wff_ref[...])
    half = ROPE_DIM // 2
    na = wa_ref.shape[1] // tn
    for n in range(na + wb_ref.shape[1] // tn):
        w_tile = (wa_ref[:, n * tn:(n + 1) * tn] if n < na
                  else wb_ref[:, (n - na) * tn:(n - na + 1) * tn])
        y = _dot(h_ref[...], w_tile)
        first_blk = n * tn // LANES
        if first_blk < FK_BLK:
            y = y * FOX_QSCALE
        if first_blk >= MQ_COL // LANES:
            y = y * MEM_QSCALE
        if DQ_BLK <= first_blk < DV_BLK:
            qscale = DIFF_QSCALE if first_blk < DK_BLK else 1.0
            for j in range(tn // LANES):
                blk = y[:, j * LANES:(j + 1) * LANES]
                rot = (blk * cos_ref[...] + pltpu.roll(blk, LANES - half, axis=1) * sa_ref[...]
                       + pltpu.roll(blk, half, axis=1) * sb_ref[...])
                o_ref[:, n * tn + j * LANES:n * tn + (j + 1) * LANES] = (rot * qscale).astype(BF16)
        else:
            o_ref[:, n * tn:(n + 1) * tn] = y.astype(BF16)


def _mix_proj(x, g, w_head, w_tail, cos_t, sa_t, sb_t, *, seq, cast=(), tm=512, tn=512):
    t, d = x.shape
    n_out = FF_COL + w_tail.shape[1]
    assert all((blk * LANES) % tn == 0 for blk in (FK_BLK, DQ_BLK, DK_BLK, DV_BLK))
    assert w_head.shape[1] == FF_COL + LANES and n_out == PROJ_W
    s_tiles = seq // tm
    m_tiles = t // tm
    resident = dict(pipeline_mode=pl.Buffered(1))
    in_specs = [
        pl.BlockSpec((tm, d), lambda m: (m, 0)),
        pl.BlockSpec((1, d), lambda m: (0, 0)),
        pl.BlockSpec((d, FF_COL), lambda m: (0, 0), **resident),
        pl.BlockSpec((d, LANES), lambda m: (0, FF_COL // LANES), **resident),
        pl.BlockSpec((d, w_tail.shape[1]), lambda m: (0, 0), **resident),
        pl.BlockSpec((tm, LANES), lambda m: (m % s_tiles, 0)),
        pl.BlockSpec((tm, LANES), lambda m: (m % s_tiles, 0)),
        pl.BlockSpec((tm, LANES), lambda m: (m % s_tiles, 0)),
    ]
    out_specs = [pl.BlockSpec((tm, n_out), lambda m: (m, 0)),
                 pl.BlockSpec((tm, LANES), lambda m: (m, 0)),
                 pl.BlockSpec((tm, d), lambda m: (m, 0))]
    out_shape = [jax.ShapeDtypeStruct((t, n_out), BF16), jax.ShapeDtypeStruct((t, LANES), F32),
                 jax.ShapeDtypeStruct((t, d), BF16)]
    cast_in, cast_out, cast_shape = _slab_cast_specs(cast, m_tiles, lambda m: m)
    in_specs, out_specs, out_shape = in_specs + cast_in, out_specs + cast_out, out_shape + cast_shape
    outs = pl.pallas_call(
        functools.partial(_proj_kernel, tn=tn, n_cast=len(cast)),
        out_shape=out_shape,
        grid=(m_tiles,),
        in_specs=in_specs,
        out_specs=out_specs,
        compiler_params=_params("parallel"),
        name="mix_proj",
    )(x, g, w_head, w_head, w_tail, cos_t, sa_t, sb_t, *[arr for arr, _ in cast])
    return outs[0], outs[1], outs[2], outs[3:]


def _rope_tables(seq):
    half = ROPE_DIM // 2
    pos = jnp.arange(seq, dtype=F32)
    inv_freq = ROPE_THETA ** (-jnp.arange(0, ROPE_DIM, 2, dtype=F32) / ROPE_DIM)
    ang = pos[:, None] * inv_freq[None, :]
    cos, sin = jnp.cos(ang), jnp.sin(ang)
    ones = jnp.ones((seq, DIFF_QK_DIM - ROPE_DIM), F32)
    zeros_h = jnp.zeros((seq, half), F32)
    zeros_r = jnp.zeros((seq, DIFF_QK_DIM - ROPE_DIM), F32)
    cos_m = jnp.concatenate([cos, cos, ones], axis=1)
    sa_m = jnp.concatenate([-sin, zeros_h, zeros_r], axis=1)
    sb_m = jnp.concatenate([zeros_h, sin, zeros_r], axis=1)
    rep = LANES // DIFF_QK_DIM
    return (jnp.tile(cos_m, (1, rep)), jnp.tile(sa_m, (1, rep)), jnp.tile(sb_m, (1, rep)))


def _fgate_kernel(ff_ref, bias_ref, fp_ref, frow_ref, fcol_ref, *, cb):
    seq = ff_ref.shape[0]
    head_lane = lax.broadcasted_iota(jnp.int32, (cb, LANES), 1) < FOX_HEADS
    z = ff_ref[...] + bias_ref[...]
    lf = jnp.minimum(z, 0.0) - jnp.log1p(jnp.exp(-jnp.abs(z)))
    r = lax.broadcasted_iota(jnp.int32, (cb, cb), 0)
    c = lax.broadcasted_iota(jnp.int32, (cb, cb), 1)
    tri = (r >= c).astype(F32)
    carry = jnp.zeros((1, LANES), F32)
    for i in range(seq // cb):
        cs = jnp.dot(tri, lf[i * cb:(i + 1) * cb], precision=lax.Precision.HIGHEST,
                     preferred_element_type=F32) + carry
        carry = cs[cb - 1:cb, :]
        cs2 = jnp.where(head_lane, cs * LOG2E, 0.0)
        fcol_ref[i * cb:(i + 1) * cb, :] = cs2
        hi = cs2.astype(BF16).astype(F32)
        mid = (cs2 - hi).astype(BF16).astype(F32)
        lo = (cs2 - hi - mid).astype(BF16).astype(F32)
        pieces = hi + pltpu.roll(mid, FOX_HEADS, axis=1) + pltpu.roll(lo, 2 * FOX_HEADS, axis=1)
        fp_ref[i * cb:(i + 1) * cb, :] = pieces.astype(BF16)
    frow_ref[...] = fcol_ref[...].T[:FOX_HEADS]


def _fox_gate(ff, bias, *, batch, seq, cb=256):
    ff = ff.reshape(batch, seq, LANES)
    return pl.pallas_call(
        functools.partial(_fgate_kernel, cb=cb),
        out_shape=(jax.ShapeDtypeStruct((batch, seq, LANES), BF16),
                   jax.ShapeDtypeStruct((batch, FOX_HEADS, seq), F32)),
        grid=(batch,),
        in_specs=[pl.BlockSpec((None, seq, LANES), lambda b: (b, 0, 0)),
                  pl.BlockSpec((1, LANES), lambda b: (0, 0))],
        out_specs=(pl.BlockSpec((None, seq, LANES), lambda b: (b, 0, 0)),
                   pl.BlockSpec((None, FOX_HEADS, seq), lambda b: (b, 0, 0))),
        scratch_shapes=[pltpu.VMEM((seq, LANES), F32)],
        compiler_params=_params("parallel"),
        name="fox_gate",
    )(ff, bias)


def _dot_tn(a, b):
    return lax.dot_general(a, b, (((0,), (0,)), ((), ())), preferred_element_type=F32)


def _flash_attend(i, heads, logits, values, ft2, visible, finish, side_job,
                  sa_ref, sb_ref, m_ref, l_ref, acc_ref):
    def step(h, s, j):
        m = m_ref[h]
        m_new = jnp.maximum(m, jnp.max(s, axis=0, keepdims=True) + ft2[h])
        alpha = jnp.exp2(m - m_new)
        p = jnp.exp2(s - (m_new - ft2[h]))
        m_ref[h] = m_new
        l_ref[h] = alpha * l_ref[h] + jnp.sum(p, axis=0, keepdims=True)
        acc_ref[h] = alpha * acc_ref[h] + _dot_tn(values(h, j), p.astype(BF16))

    for h in heads:
        m_ref[h] = jnp.full(m_ref.shape[1:], NEG_INF, F32)
        l_ref[h] = jnp.zeros(l_ref.shape[1:], F32)
        acc_ref[h] = jnp.zeros(acc_ref.shape[1:], F32)
        sa_ref[h] = logits(h, 0)
    side_job()

    def pair(jj, carry):
        j = 2 * jj
        for h in heads:
            sb_ref[h] = logits(h, j + 1)
            step(h, sa_ref[h], j)
        for h in heads:
            sa_ref[h] = logits(h, j + 2)
            step(h, sb_ref[h], j + 1)
        return carry

    lax.fori_loop(0, i // 2, pair, 0)

    def last(s_ref):
        for h in heads:
            step(h, jnp.where(visible, s_ref[h], NEG_INF), i)
            finish(h, acc_ref[h] / l_ref[h])

    @pl.when(i % 2 == 0)
    def _():
        last(sa_ref)

    @pl.when(i % 2 == 1)
    def _():
        for h in heads:
            sb_ref[h] = logits(h, i)
            step(h, sa_ref[h], i - 1)
        last(sb_ref)


def _flash_scratch(hb, width, tq, cols):
    return [pltpu.VMEM((hb, tq, cols), F32), pltpu.VMEM((hb, tq, cols), F32),
            pltpu.VMEM((hb, 1, cols), F32), pltpu.VMEM((hb, 1, cols), F32),
            pltpu.VMEM((hb, width, cols), F32)]


def _fox_kernel(*refs, tq, hb, n_cast):
    q_ref, k_ref, v_ref, fp_ref, fr_ref = refs[:5]
    o_ref = refs[5 + n_cast]
    scratch = refs[-5:]
    side_job = functools.partial(_run_slab_casts, refs[5:5 + n_cast], refs[6 + n_cast:-5])
    i = pl.program_id(2)
    h0 = pl.program_id(1) * hb
    heads = range(hb)
    ft2 = [fr_ref[h, pl.ds(i, 1), :] for h in heads]

    lane = lax.broadcasted_iota(jnp.int32, (tq, LANES), 1)
    q_ext = []
    for h in heads:
        mine = (lane % FOX_HEADS == h0 + h) & (lane < 3 * FOX_HEADS)
        sel = jnp.where(mine, -1.0, 0.0).astype(BF16)
        q_ext.append(jnp.concatenate([q_ref[:, h * FOX_DIM:(h + 1) * FOX_DIM], sel], axis=1))

    def logits(h, j):
        start = pl.multiple_of(j * tq, tq)
        k_ext = jnp.concatenate([k_ref[pl.ds(start, tq), h * FOX_DIM:(h + 1) * FOX_DIM],
                                 fp_ref[pl.ds(start, tq), :]], axis=1)
        return _dot_nt(k_ext, q_ext[h])

    def values(h, j):
        return v_ref[pl.ds(pl.multiple_of(j * tq, tq), tq), h * FOX_DIM:(h + 1) * FOX_DIM]

    def finish(h, o_t):
        o_ref[:, h * FOX_DIM:(h + 1) * FOX_DIM] = o_t.T.astype(BF16)

    krow = lax.broadcasted_iota(jnp.int32, (tq, tq), 0)
    qcol = lax.broadcasted_iota(jnp.int32, (tq, tq), 1)
    _flash_attend(i, heads, logits, values, ft2, krow <= qcol, finish, side_job, *scratch)


def _fox_attn(proj, fp, frow, *, batch, seq, cast=(), tq=256, hb=8):
    proj3 = proj.reshape(batch, seq, PROJ_W)
    n_q = seq // tq
    frow4 = frow.reshape(batch, FOX_HEADS, n_q, tq)
    w = hb * FOX_DIM
    groups = FOX_HEADS // hb
    assert groups == 1 or not cast
    cast_in, cast_out, cast_shape = _slab_cast_specs(cast, batch * n_q, lambda b, g, i: b * n_q + i)
    outs = pl.pallas_call(
        functools.partial(_fox_kernel, tq=tq, hb=hb, n_cast=len(cast)),
        out_shape=[jax.ShapeDtypeStruct((batch, seq, FOX_W), BF16)] + cast_shape,
        grid=(batch, groups, n_q),
        in_specs=[
            pl.BlockSpec((None, tq, w), lambda b, g, i: (b, i, g)),
            pl.BlockSpec((None, seq, w), lambda b, g, i: (b, 0, groups + g)),
            pl.BlockSpec((None, seq, w), lambda b, g, i: (b, 0, 2 * groups + g)),
            pl.BlockSpec((None, seq, LANES), lambda b, g, i: (b, 0, 0)),
            pl.BlockSpec((None, hb, n_q, tq), lambda b, g, i: (b, g, 0, 0)),
        ] + cast_in,
        out_specs=[pl.BlockSpec((None, tq, w), lambda b, g, i: (b, i, g))] + cast_out,
        scratch_shapes=_flash_scratch(hb, FOX_DIM, tq, tq),
        compiler_params=_params("parallel", "parallel", "arbitrary"),
        name="fox_attn",
    )(proj3, proj3, proj3, fp, frow4, *[arr for arr, _ in cast])
    return outs[0].reshape(batch * seq, FOX_W), outs[1:]


def _diff_kernel(*refs, tq, hb, lam_init, n_cast, cast_moves):
    lamv_ref, g_ref, q_ref, k_ref, v_ref = refs[:5]
    o_ref = refs[5 + n_cast]
    scratch = refs[-5:]
    side_job = functools.partial(_run_slab_casts, refs[5:5 + n_cast], refs[6 + n_cast:-5],
                                 cast_moves)
    i = pl.program_id(2)
    heads = range(hb)
    lv = lamv_ref[...]
    lam = (jnp.exp(jnp.sum(lv[0:1] * lv[1:2], axis=1, keepdims=True))
           - jnp.exp(jnp.sum(lv[2:3] * lv[3:4], axis=1, keepdims=True)) + lam_init)

    lane = lax.broadcasted_iota(jnp.int32, (tq, LANES), 1)
    qs = []
    for h in heads:
        q = q_ref[:, h * LANES:(h + 1) * LANES].astype(F32)
        qs.append(jnp.concatenate([jnp.where(lane < DIFF_QK_DIM, q, 0.0),
                                   jnp.where(lane >= DIFF_QK_DIM, q, 0.0)], axis=0).astype(BF16))
    zero = jnp.zeros((1, 2 * tq), F32)

    def logits(h, j):
        start = pl.multiple_of(j * tq, tq)
        return _dot_nt(k_ref[pl.ds(start, tq), h * LANES:(h + 1) * LANES], qs[h])

    def values(h, j):
        return v_ref[pl.ds(pl.multiple_of(j * tq, tq), tq), h * LANES:(h + 1) * LANES]

    def finish(h, o_t):
        yd = (o_t[:, :tq] - lam * o_t[:, tq:]).T
        o_ref[:, h * LANES:(h + 1) * LANES] = (
            _rms(yd, g_ref[...]) * (1.0 - lam_init)).astype(BF16)

    krow = lax.broadcasted_iota(jnp.int32, (tq, 2 * tq), 0)
    qcol = lax.broadcasted_iota(jnp.int32, (tq, 2 * tq), 1)
    qcol = jnp.where(qcol >= tq, qcol - tq, qcol)
    visible = krow // CHUNK <= qcol // CHUNK
    _flash_attend(i, heads, logits, values, [zero] * hb, visible, finish, side_job, *scratch)


def _diff_attn(proj, lamv, g, *, batch, seq, lam_init, cast=(), cast_moves=None, tq=256, hb=4):
    proj3 = proj.reshape(batch, seq, PROJ_W)
    n_q = seq // tq
    w = hb * LANES
    groups = DIFF_HEADS // hb
    assert groups == 1 or not cast
    dq, dk, dv = (DQ_BLK * LANES) // w, (DK_BLK * LANES) // w, (DV_BLK * LANES) // w
    cast_in, cast_out, cast_shape = _slab_cast_specs(cast, batch * n_q, lambda b, g, i: b * n_q + i)
    outs = pl.pallas_call(
        functools.partial(_diff_kernel, tq=tq, hb=hb, lam_init=lam_init, n_cast=len(cast),
                          cast_moves=cast_moves),
        out_shape=[jax.ShapeDtypeStruct((batch, seq, DIFF_V_W), BF16)] + cast_shape,
        grid=(batch, groups, n_q),
        in_specs=[
            pl.BlockSpec((4, DIFF_QK_DIM), lambda b, g, i: (0, 0)),
            pl.BlockSpec((1, DIFF_V_DIM), lambda b, g, i: (0, 0)),
            pl.BlockSpec((None, tq, w), lambda b, g, i: (b, i, dq + g)),
            pl.BlockSpec((None, seq, w), lambda b, g, i: (b, 0, dk + g)),
            pl.BlockSpec((None, seq, w), lambda b, g, i: (b, 0, dv + g)),
        ] + cast_in,
        out_specs=[pl.BlockSpec((None, tq, w), lambda b, g, i: (b, i, g))] + cast_out,
        scratch_shapes=_flash_scratch(hb, DIFF_V_DIM, tq, 2 * tq),
        compiler_params=_params("parallel", "parallel", "arbitrary"),
        name="diff_attn",
    )(lamv, g, proj3, proj3, proj3, *[arr for arr, _ in cast])
    return outs[0].reshape(batch * seq, DIFF_V_W), outs[1:]


def _mem_kv_kernel(mem_ref, g_ref, w_ref, o_ref):
    o_ref[...] = _dot(_rms(mem_ref[...], g_ref[...]).astype(BF16), w_ref[...]).astype(BF16)


def _mem_kv(mem, g, w):
    batch, n_mem, d = mem.shape
    return pl.pallas_call(
        _mem_kv_kernel,
        out_shape=jax.ShapeDtypeStruct((batch, n_mem, 2 * MEM_W), BF16),
        grid=(batch,),
        in_specs=[pl.BlockSpec((None, n_mem, d), lambda b: (b, 0, 0)),
                  pl.BlockSpec((1, d), lambda b: (0, 0)),
                  pl.BlockSpec((d, 2 * MEM_W), lambda b: (0, 0))],
        out_specs=pl.BlockSpec((None, n_mem, 2 * MEM_W), lambda b: (b, 0, 0)),
        compiler_params=_params("parallel"),
        name="mem_kv",
    )(mem, g, w)


def _mem_attn_kernel(q_ref, kv_ref, o_ref):
    logits = [_dot_nt(kv_ref[:, h * MEM_DIM:(h + 1) * MEM_DIM],
                      q_ref[:, h * MEM_DIM:(h + 1) * MEM_DIM]) for h in range(MEM_HEADS)]
    for h in range(MEM_HEADS):
        v = kv_ref[:, MEM_W + h * MEM_DIM:MEM_W + (h + 1) * MEM_DIM]
        s = logits[h]
        p = jnp.exp2(s - jnp.max(s, axis=0, keepdims=True))
        l = jnp.sum(p, axis=0, keepdims=True)
        o_t = _dot_tn(v, p.astype(BF16)) / l
        o_ref[:, h * MEM_DIM:(h + 1) * MEM_DIM] = o_t.T.astype(BF16)


def _mem_attn(proj, mkv, *, batch, seq, tq=2048):
    proj3 = proj.reshape(batch, seq, PROJ_W)
    n_mem = mkv.shape[1]
    out = pl.pallas_call(
        _mem_attn_kernel,
        out_shape=jax.ShapeDtypeStruct((batch, seq, MEM_W), BF16),
        grid=(batch, seq // tq),
        in_specs=[pl.BlockSpec((None, tq, MEM_W), lambda b, i: (b, i, MQ_COL // MEM_W)),
                  pl.BlockSpec((None, n_mem, 2 * MEM_W), lambda b, i: (b, 0, 0))],
        out_specs=pl.BlockSpec((None, tq, MEM_W), lambda b, i: (b, i, 0)),
        compiler_params=_params("parallel", "arbitrary"),
        name="mem_attn",
    )(proj3, mkv)
    return out.reshape(batch * seq, MEM_W)


def _merge_kernel(x_ref, h_ref, yf_ref, yd_ref, ym_ref, wg_ref, bg_ref, wf_ref, wd_ref, wm_ref,
                  wo_ref, gpost_ref, o_ref, acc_ref):
    n = pl.program_id(1)
    tn = wo_ref.shape[0]

    @pl.when(n == 0)
    def _():
        acc_ref[...] = jnp.zeros_like(acc_ref)

    z = _dot(h_ref[...], wg_ref[...]) + bg_ref[...]
    gates = 1.0 / (1.0 + jnp.exp(-z))
    merged = (gates[:, :tn] * _dot(yf_ref[...], wf_ref[...])
              + gates[:, tn:2 * tn] * _dot(yd_ref[...], wd_ref[...])
              + gates[:, 2 * tn:] * _dot(ym_ref[...], wm_ref[...]))
    acc_ref[...] += _dot(merged.astype(BF16), wo_ref[...])

    @pl.when(n == pl.num_programs(1) - 1)
    def _():
        def emit(sl, y):
            o_ref[sl, :] = x_ref[sl, :] + y

        _rms_rows(acc_ref, gpost_ref[...], emit, F32_ROWS)


def _gate_tile_order(d):
    tn = MERGE_TN
    nt = d // tn
    return tuple(((n * 3 + b) * tn, (b * nt + n) * tn, tn) for n in range(nt) for b in range(3))


def _merge(x, h, y_fox, y_diff, y_mem, w_gate, b_gate, w_fox, w_diff, w_mem, w_out, g_post,
           *, tm=512):
    t, d = x.shape
    tn = MERGE_TN
    nt = d // tn
    row = lambda m, n: (m, 0)
    return pl.pallas_call(
        _merge_kernel,
        out_shape=jax.ShapeDtypeStruct((t, d), F32),
        grid=(t // tm, nt),
        in_specs=[
            pl.BlockSpec((tm, d), row),
            pl.BlockSpec((tm, d), row),
            pl.BlockSpec((tm, FOX_W), row),
            pl.BlockSpec((tm, DIFF_V_W), row),
            pl.BlockSpec((tm, MEM_W), row),
            pl.BlockSpec((d, 3 * tn), lambda m, n: (0, n)),
            pl.BlockSpec((1, 3 * tn), lambda m, n: (0, n)),
            pl.BlockSpec((FOX_W, tn), lambda m, n: (0, n)),
            pl.BlockSpec((DIFF_V_W, tn), lambda m, n: (0, n)),
            pl.BlockSpec((MEM_W, tn), lambda m, n: (0, n)),
            pl.BlockSpec((tn, d), lambda m, n: (n, 0)),
            pl.BlockSpec((1, d), lambda m, n: (0, 0)),
        ],
        out_specs=pl.BlockSpec((tm, d), row),
        scratch_shapes=[pltpu.VMEM((tm, d), F32)],
        compiler_params=_params("parallel", "arbitrary"),
        name="merge",
    )(x, h, y_fox, y_diff, y_mem, w_gate, b_gate, w_fox, w_diff, w_mem, w_out, g_post)


def kernel(x, mem, ffn1_pre_g, ffn1_w_gate, ffn1_w_up, ffn1_w_down, ffn1_post_g, mix_pre_g, w_in, fox_f_bias, diff_lambda_q1, diff_lambda_k1, diff_lambda_q2, diff_lambda_k2, diff_head_g, mem_norm_g, w_mem_kv, w_branch_fox, w_branch_diff, w_branch_mem, w_merge_gate, b_merge_gate, w_out, mix_post_g, ffn2_pre_g, ffn2_w_gate, ffn2_w_up, ffn2_w_down, ffn2_post_g):
    batch, seq, d = x.shape
    depth = w_in.shape[0]
    xt = x.reshape(batch * seq, d)
    cos_t, sa_t, sb_t = _rope_tables(seq)

    def row(v):
        return v.reshape(1, -1).astype(F32)

    for l in range(depth):
        first, w1_gate, w1_up, w1_down = _ffn_first(
            xt, row(ffn1_pre_g[l]), ffn1_w_gate, ffn1_w_up, ffn1_w_down, row(ffn1_post_g[l]), l,
            tm=FIRST_ROWS)
        w_in_t = w_in[l].T
        xt, (w_head, w_tail) = _ffn(
            xt, row(ffn1_pre_g[l]), w1_gate, w1_up, w1_down, row(ffn1_post_g[l]), first_rows=first,
            cast_t=[(w_in_t, 0, FF_COL + LANES), (w_in_t, FF_COL + FOX_HEADS, w_in.shape[2])])

        small = [w_out, w_branch_fox, w_branch_diff, w_branch_mem]
        proj, ff, h_mix, (wb_out, wb_fox, wb_diff, wb_mem) = _mix_proj(
            xt, row(mix_pre_g[l]), w_head, w_tail, cos_t, sa_t, sb_t, seq=seq,
            cast=[(w, l) for w in small])

        bias = jnp.pad(fox_f_bias[l].astype(F32), (0, LANES - FOX_HEADS)).reshape(1, LANES)
        fp, frow = _fox_gate(ff, bias, batch=batch, seq=seq)
        y_fox, (w2_gate, w2_up, w2_down) = _fox_attn(
            proj, fp, frow, batch=batch, seq=seq,
            cast=[(w, l) for w in (ffn2_w_gate, ffn2_w_up, ffn2_w_down)])

        lam_init = 0.8 - 0.6 * math.exp(-0.3 * l)
        lamv = jnp.stack([diff_lambda_q1[l], diff_lambda_k1[l], diff_lambda_q2[l],
                          diff_lambda_k2[l]]).astype(F32)
        gate_order = _gate_tile_order(d)
        y_diff, (wb_merge,) = _diff_attn(proj, lamv, row(diff_head_g[l]), batch=batch, seq=seq,
                                         lam_init=lam_init, cast=[(w_merge_gate, l)],
                                         cast_moves=gate_order)
        b_gate = row(b_merge_gate[l])
        b_gate = jnp.concatenate([b_gate[:, src:src + w] for _, src, w in gate_order], axis=1)

        mkv = _mem_kv(mem, row(mem_norm_g[l]), w_mem_kv[l].astype(BF16))
        y_mem = _mem_attn(proj, mkv, batch=batch, seq=seq)

        xt = _merge(xt, h_mix, y_fox, y_diff, y_mem, wb_merge, b_gate, wb_fox, wb_diff, wb_mem,
                    wb_out, row(mix_post_g[l]))

        xt, _ = _ffn(xt, row(ffn2_pre_g[l]), w2_gate, w2_up, w2_down, row(ffn2_post_g[l]))

    return xt.reshape(batch, seq, d)
```
